```python
import math
import jax, jax.numpy as jnp
from jax import lax
import numpy as np

D_MODEL = 1024
BATCH = 2
SEQ = 8192
DEPTH = 4

CHUNK = 64
QBLK = 128
A_HEADS = 4
A_DQK = 64
A_DV = 2 * A_DQK
B_HEADS = 4
B_DH = 128
C_HEADS = 4
C_DK = 128
C_DV = 256
ROPE_BASE = 10000.0
N_BRANCH = 3
D_FF = 2816
CONV_W = 3
LN_EPS = 1e-5
ALPHA = (2 * DEPTH) ** 0.25
BETA = (8 * DEPTH) ** -0.25

SPLITS = (A_HEADS * 2 * A_DQK, A_HEADS * 2 * A_DQK, A_HEADS * A_DV,
          B_HEADS * B_DH, B_HEADS * B_DH, B_HEADS * B_DH, B_HEADS,
          C_HEADS * C_DK, C_HEADS * C_DK, C_HEADS * C_DV, C_HEADS * C_DV,
          N_BRANCH * D_MODEL)
D_IN = int(sum(SPLITS))
SPLIT_IDX = tuple(int(v) for v in np.cumsum(SPLITS)[:-1])

kernel_name = "hybrid_diff_fox_retention_convffn_deepnorm_adaln"


def _layernorm(x, g, b):
    xf = x.astype(jnp.float32)
    mu = jnp.mean(xf, -1, keepdims=True)
    var = jnp.mean(jnp.square(xf - mu), -1, keepdims=True)
    return ((xf - mu) * lax.rsqrt(var + LN_EPS) * g + b).astype(x.dtype)


def _head_rmsnorm(o, g):
    of = o.astype(jnp.float32)
    of = of * lax.rsqrt(jnp.mean(jnp.square(of), -1, keepdims=True) + LN_EPS)
    return of * g.reshape(o.shape[2], o.shape[3])


def _head_groupnorm(o, g):
    of = o.astype(jnp.float32)
    mu = jnp.mean(of, -1, keepdims=True)
    var = jnp.mean(jnp.square(of - mu), -1, keepdims=True)
    return (of - mu) * lax.rsqrt(var + LN_EPS) * g.reshape(o.shape[2], o.shape[3])


def _query_blocks(t):
    t = t.reshape((t.shape[0], t.shape[1] // QBLK, QBLK) + t.shape[2:])
    return jnp.moveaxis(t, 1, 0)


def _merge_blocks(o):
    o = jnp.moveaxis(o, 0, 1)
    return o.reshape((o.shape[0], o.shape[1] * o.shape[2]) + o.shape[3:])


def _diff_attention(q, k, v, lam):
    S_ = q.shape[1]
    scale = A_DQK ** -0.5
    k_chunk = jnp.arange(S_) // CHUNK

    def block(args):
        qb, i = args
        s = jnp.einsum('bqhmd,bkhmd->bhmqk', qb, k).astype(jnp.float32) * scale
        q_chunk = (i * QBLK + jnp.arange(QBLK)) // CHUNK
        mask = k_chunk[None, :] <= q_chunk[:, None]
        p = jax.nn.softmax(jnp.where(mask, s, -jnp.inf), axis=-1)
        a = p[:, :, 0] - lam * p[:, :, 1]
        return jnp.einsum('bhqk,bkhe->bqhe', a.astype(v.dtype), v)

    o = lax.map(block, (_query_blocks(q), jnp.arange(S_ // QBLK)))
    return _merge_blocks(o)


def _forgetting_attention(q, k, v, log_f):
    S_ = q.shape[1]
    scale = B_DH ** -0.5
    F = jnp.cumsum(log_f, axis=1)
    Fk = jnp.transpose(F, (0, 2, 1))[:, :, None, :]
    pos = jnp.arange(S_)

    def block(args):
        qb, Fq, i = args
        s = jnp.einsum('bqhd,bkhd->bhqk', qb, k).astype(jnp.float32) * scale
        s = s + jnp.transpose(Fq, (0, 2, 1))[..., None] - Fk
        q_pos = i * QBLK + jnp.arange(QBLK)
        mask = pos[None, :] <= q_pos[:, None]
        p = jax.nn.softmax(jnp.where(mask, s, -jnp.inf), axis=-1)
        return jnp.einsum('bhqk,bkhd->bqhd', p.astype(v.dtype), v)

    o = lax.map(block, (_query_blocks(q), _query_blocks(F), jnp.arange(S_ // QBLK)))
    return _merge_blocks(o)


def _rotary(t, pos):
    half = t.shape[-1] // 2
    inv = 1.0 / (ROPE_BASE ** jnp.linspace(0.0, 1.0, half, dtype=jnp.float32))
    ang = pos.astype(jnp.float32)[:, None] * inv[None, :]
    cos, sin = jnp.cos(ang)[None, :, None, :], jnp.sin(ang)[None, :, None, :]
    tf = t.astype(jnp.float32)
    t1, t2 = tf[..., :half], tf[..., half:]
    return jnp.concatenate([t1 * cos - t2 * sin, t1 * sin + t2 * cos], axis=-1)


def _retention(q, k, v):
    B_, S_ = q.shape[0], q.shape[1]
    nc = S_ // CHUNK
    log_gamma = jnp.log(1.0 - 2.0 ** (-5.0 - jnp.arange(C_HEADS, dtype=jnp.float32)))
    pos = jnp.arange(S_)
    qc = _rotary(q, pos).reshape(B_, nc, CHUNK, C_HEADS, C_DK)
    kc = (_rotary(k, pos) * C_DK ** -0.5).reshape(B_, nc, CHUNK, C_HEADS, C_DK)
    vc = v.astype(jnp.float32).reshape(B_, nc, CHUNK, C_HEADS, C_DV)
    i = jnp.arange(CHUNK, dtype=jnp.float32)
    diff = i[:, None] - i[None, :]
    dmask = jnp.where(diff >= 0, jnp.exp(log_gamma[:, None, None] * jnp.maximum(diff, 0.0)), 0.0)
    s = jnp.einsum('bnihd,bnjhd->bnhij', qc, kc) * dmask
    inner = jnp.einsum('bnhij,bnjhe->bnihe', s, vc)
    q_dec = qc * jnp.exp(log_gamma[None, :] * (i[:, None] + 1.0))[:, :, None]
    k_dec = kc * jnp.exp(log_gamma[None, :] * (CHUNK - 1.0 - i[:, None]))[:, :, None]
    chunk_decay = jnp.exp(log_gamma * CHUNK)[None, :, None, None]

    def step(R, xs):
        qd, kd, vv = xs
        cross = jnp.einsum('bihd,bhde->bihe', qd, R)
        R = chunk_decay * R + jnp.einsum('bjhd,bjhe->bhde', kd, vv)
        return R, cross

    R0 = jnp.zeros((B_, C_HEADS, C_DK, C_DV), jnp.float32)
    _, cross = lax.scan(step, R0, (jnp.moveaxis(q_dec, 1, 0), jnp.moveaxis(k_dec, 1, 0),
                                   jnp.moveaxis(vc, 1, 0)))
    o = inner + jnp.moveaxis(cross, 0, 1)
    return o.reshape(B_, S_, C_HEADS, C_DV)


def _token_mixer(h, w_in, b_in, lam, lam_init, g_diff, g_ret, w_pa, w_pb, w_pc, w_out):
    B_, S_, _ = h.shape
    z = h @ w_in + b_in
    aq, ak, av, bq, bk, bv, bf, cq, ck, cv, cg, gates = jnp.split(z, SPLIT_IDX, axis=-1)
    ya = _diff_attention(aq.reshape(B_, S_, A_HEADS, 2, A_DQK), ak.reshape(B_, S_, A_HEADS, 2, A_DQK),
                         av.reshape(B_, S_, A_HEADS, A_DV), lam)
    ya = (_head_rmsnorm(ya, g_diff) * (1.0 - lam_init)).astype(h.dtype).reshape(B_, S_, -1)
    yb = _forgetting_attention(bq.reshape(B_, S_, B_HEADS, B_DH), bk.reshape(B_, S_, B_HEADS, B_DH),
                               bv.reshape(B_, S_, B_HEADS, B_DH),
                               jax.nn.log_sigmoid(bf.astype(jnp.float32)))
    yb = yb.reshape(B_, S_, -1)
    yc = _retention(cq.reshape(B_, S_, C_HEADS, C_DK), ck.reshape(B_, S_, C_HEADS, C_DK),
                    cv.reshape(B_, S_, C_HEADS, C_DV))
    yc = _head_groupnorm(yc, g_ret).astype(h.dtype).reshape(B_, S_, -1) * jax.nn.silu(cg)
    ga, gb, gc = jnp.split(jax.nn.sigmoid(gates), N_BRANCH, axis=-1)
    m = ga * (ya @ w_pa) + gb * (yb @ w_pb) + gc * (yc @ w_pc)
    return m @ w_out


def _conv_ffn(h, w_up, w_conv, b_conv, w_down):
    S_ = h.shape[1]
    u, g = jnp.split(h @ w_up, 2, axis=-1)
    up = jnp.pad(u, ((0, 0), (CONV_W - 1, 0), (0, 0)))
    conv = b_conv + up[:, 0:S_] * w_conv[0]
    for j in range(1, CONV_W):
        conv = conv + up[:, j:j + S_] * w_conv[j]
    return (jax.nn.gelu(conv, approximate=False) * g) @ w_down


def setup_inputs(seed: int = 0) -> dict:
    key = jax.random.key(seed)
    ks = jax.random.split(key, 24)
    nrm = lambda k, shape, s: jax.random.normal(k, shape, jnp.float32) * s
    D = D_MODEL
    return {
        "x": nrm(ks[0], (BATCH, SEQ, D), 1.0),
        "c": nrm(ks[1], (BATCH, D), 1.0),
        "w_ada": nrm(ks[2], (DEPTH, D, 6 * D), D ** -0.5),
        "b_ada": nrm(ks[3], (DEPTH, 6 * D), 0.02),
        "w_in": nrm(ks[4], (DEPTH, D, D_IN), D ** -0.5),
        "b_in": nrm(ks[5], (DEPTH, D_IN), 0.02),
        "lam_q1": nrm(ks[6], (DEPTH, A_DQK), 0.1),
        "lam_k1": nrm(ks[7], (DEPTH, A_DQK), 0.1),
        "lam_q2": nrm(ks[8], (DEPTH, A_DQK), 0.1),
        "lam_k2": nrm(ks[9], (DEPTH, A_DQK), 0.1),
        "g_diff": 1.0 + nrm(ks[10], (DEPTH, A_HEADS * A_DV), 0.02),
        "g_ret": 1.0 + nrm(ks[11], (DEPTH, C_HEADS * C_DV), 0.02),
        "w_pa": nrm(ks[12], (DEPTH, A_HEADS * A_DV, D), (A_HEADS * A_DV) ** -0.5),
        "w_pb": nrm(ks[13], (DEPTH, B_HEADS * B_DH, D), (B_HEADS * B_DH) ** -0.5),
        "w_pc": nrm(ks[14], (DEPTH, C_HEADS * C_DV, D), (C_HEADS * C_DV) ** -0.5),
        "w_out": nrm(ks[15], (DEPTH, D, D), BETA * D ** -0.5),
        "ln_g": 1.0 + nrm(ks[16], (DEPTH, 2, D), 0.02),
        "ln_b": nrm(ks[17], (DEPTH, 2, D), 0.02),
        "w_up": nrm(ks[18], (DEPTH, D, 2 * D_FF), D ** -0.5),
        "w_conv": nrm(ks[19], (DEPTH, CONV_W, D_FF), CONV_W ** -0.5),
        "b_conv": nrm(ks[20], (DEPTH, D_FF), 0.02),
        "w_down": nrm(ks[21], (DEPTH, D_FF, D), BETA * D_FF ** -0.5),
    }


def reference(x, c, w_ada, b_ada, w_in, b_in, lam_q1, lam_k1, lam_q2, lam_k2, g_diff, g_ret,
              w_pa, w_pb, w_pc, w_out, ln_g, ln_b, w_up, w_conv, b_conv, w_down):
    c_act = jax.nn.silu(c)
    for l in range(DEPTH):
        mod = c_act @ w_ada[l] + b_ada[l]
        sh1, sc1, gt1, sh2, sc2, gt2 = [m[:, None, :] for m in jnp.split(mod, 6, axis=-1)]
        lam_init = 0.8 - 0.6 * math.exp(-0.3 * l)
        lam = (jnp.exp(jnp.sum(lam_q1[l].astype(jnp.float32) * lam_k1[l].astype(jnp.float32)))
               - jnp.exp(jnp.sum(lam_q2[l].astype(jnp.float32) * lam_k2[l].astype(jnp.float32)))
               + lam_init)
        h = x * (1.0 + sc1) + sh1
        y = _token_mixer(h, w_in[l], b_in[l], lam, lam_init, g_diff[l], g_ret[l],
                         w_pa[l], w_pb[l], w_pc[l], w_out[l])
        x = _layernorm(ALPHA * x + gt1 * y, ln_g[l, 0], ln_b[l, 0])
        h = x * (1.0 + sc2) + sh2
        y = _conv_ffn(h, w_up[l], w_conv[l], b_conv[l], w_down[l])
        x = _layernorm(ALPHA * x + gt2 * y, ln_g[l, 1], ln_b[l, 1])
    return x
```

```python
import functools
import math

import numpy as np
import jax
import jax.numpy as jnp
from jax import lax
from jax.experimental import pallas as pl
from jax.experimental.pallas import tpu as pltpu

F32 = jnp.float32
BF16 = jnp.bfloat16

HEADS = 4
A_DQK = 64
HEAD_W = 128
C_DV = 256
ROPE_BASE = 10000.0
LN_EPS = 1e-5
CONV_W = 3

Z_AQ, Z_AK, Z_AV = 0, 512, 1024
Z_BQ, Z_BK, Z_BV = 1536, 2048, 2560
Z_CQ, Z_CK, Z_CV, Z_CG = 3072, 3584, 4096, 5120
Z_GATES = 6144
Z_WIDTH = 9216
BF_OFF = 3072
BF_PAD = 128

NEG = -1e30

TN_ADA = 1024
TM_IN, TN_IN = 1024, 1024
TQ_A = 256
TQ_B = 256
L_RET = 256
TM_MERGE = 512
TM_FFN = 512
CW_FFN = 256

VMEM_LIMIT = 56 * 1024 * 1024


def _cparams(sem):
    return pltpu.CompilerParams(dimension_semantics=sem, vmem_limit_bytes=VMEM_LIMIT)


def _ada_kernel(ct_ref, w_ref, b_ref, o_ref):
    w = w_ref[0]
    rows = []
    for b in range(ct_ref.shape[0]):
        cb = ct_ref[b]
        ca = cb * jax.nn.sigmoid(cb)
        rows.append(jnp.sum(ca * w, axis=0, keepdims=True))
    o_ref[0] = jnp.concatenate(rows, axis=0) + b_ref[0]


def _ada(c, w_ada, b_ada):
    B, D = c.shape
    L, _, N = w_ada.shape
    tn = min(TN_ADA, N)
    return pl.pallas_call(
        _ada_kernel,
        grid=(L, N // tn),
        in_specs=[
            pl.BlockSpec((B, D, 1), lambda l, j: (0, 0, 0)),
            pl.BlockSpec((1, D, tn), lambda l, j: (l, 0, j)),
            pl.BlockSpec((1, 1, tn), lambda l, j: (l, 0, j)),
        ],
        out_specs=pl.BlockSpec((1, B, tn), lambda l, j: (l, 0, j)),
        out_shape=jax.ShapeDtypeStruct((L, B, N), F32),
        compiler_params=_cparams(("parallel", "parallel")),
        name="ada_mod",
    )(c.reshape(B, D, 1), w_ada, b_ada.reshape(L, 1, N))


def _inproj_kernel(x_ref, sh_ref, sc_ref, w_ref, b_ref, wf_ref, bf_ref, z_ref, lf_ref, h_scr):
    @pl.when(pl.program_id(2) == 0)
    def _():
        h = x_ref[0] * (1.0 + sc_ref[0]) + sh_ref[0]
        hb = h.astype(BF16)
        h_scr[...] = hb
        zf = jnp.dot(hb, wf_ref[...], preferred_element_type=F32) + bf_ref[...]
        lt = zf.T[:8]
        lf_ref[0] = jnp.minimum(lt, 0.0) - jnp.log1p(jnp.exp(-jnp.abs(lt)))

    acc = jnp.dot(h_scr[...], w_ref[...], preferred_element_type=F32)
    z_ref[0] = (acc + b_ref[...]).astype(BF16)


def _inproj(x, mod_l, w, b, wf, bf):
    B, S, D = x.shape
    N = w.shape[1]
    tm, tn = min(TM_IN, S), min(TN_IN, N)
    return pl.pallas_call(
        _inproj_kernel,
        grid=(B, S // tm, N // tn),
        in_specs=[
            pl.BlockSpec((1, tm, D), lambda b, i, j: (b, i, 0)),
            pl.BlockSpec((1, 1, D), lambda b, i, j: (b, 0, 0)),
            pl.BlockSpec((1, 1, D), lambda b, i, j: (b, 0, 1)),
            pl.BlockSpec((D, tn), lambda b, i, j: (0, j)),
            pl.BlockSpec((1, tn), lambda b, i, j: (0, j)),
            pl.BlockSpec((D, BF_PAD), lambda b, i, j: (0, 0)),
            pl.BlockSpec((1, BF_PAD), lambda b, i, j: (0, 0)),
        ],
        out_specs=[
            pl.BlockSpec((1, tm, tn), lambda b, i, j: (b, i, j)),
            pl.BlockSpec((1, 8, tm), lambda b, i, j: (b, 0, i)),
        ],
        out_shape=[
            jax.ShapeDtypeStruct((B, S, N), BF16),
            jax.ShapeDtypeStruct((B, 8, S), F32),
        ],
        scratch_shapes=[pltpu.VMEM((tm, D), BF16)],
        compiler_params=_cparams(("parallel", "parallel", "arbitrary")),
        name="in_proj",
    )(x, mod_l, mod_l, w, b, wf, bf)


def _cumsum_kernel(lf_ref, f_ref):
    x = lf_ref[0]
    lane = lax.broadcasted_iota(jnp.int32, x.shape, 1)
    k = 1
    while k < x.shape[1]:
        x = x + jnp.where(lane >= k, pltpu.roll(x, k, 1), 0.0)
        k *= 2
    f_ref[0] = x


def _cumsum(lf):
    B, R, S = lf.shape
    return pl.pallas_call(
        _cumsum_kernel,
        grid=(B,),
        in_specs=[pl.BlockSpec((1, R, S), lambda b: (b, 0, 0))],
        out_specs=pl.BlockSpec((1, R, S), lambda b: (b, 0, 0)),
        out_shape=jax.ShapeDtypeStruct((B, R, S), F32),
        compiler_params=_cparams(("parallel",)),
        name="forget_cumsum",
    )(lf)


def _softmax_step(s, v, m_scr, l_scr, acc_scr):
    m_prev = m_scr[...]
    m_new = jnp.maximum(m_prev, jnp.max(s, axis=1, keepdims=True))
    alpha = jnp.exp(m_prev - m_new)
    p = jnp.exp(s - m_new)
    l_scr[...] = alpha * l_scr[...] + jnp.sum(p, axis=1, keepdims=True)
    acc_scr[...] = alpha * acc_scr[...] + jnp.dot(p.astype(BF16), v, preferred_element_type=F32)
    m_scr[...] = m_new


def _init_softmax(m_scr, l_scr, acc_scr):
    m_scr[...] = jnp.full(m_scr.shape, NEG, F32)
    l_scr[...] = jnp.zeros(l_scr.shape, F32)
    acc_scr[...] = jnp.zeros(acc_scr.shape, F32)


_NT = (((1,), (1,)), ((), ()))


def _diff_attn_kernel(q_ref, k_ref, v_ref, lq1_ref, lk1_ref, lq2_ref, lk2_ref, li_ref, g_ref,
                      o_ref, m_scr, l_scr, acc_scr, *, tq, chunk):
    qi = pl.program_id(2)
    q = q_ref[0]
    lane = lax.broadcasted_iota(jnp.int32, q.shape, 1)
    zero = jnp.zeros_like(q)
    qs = jnp.concatenate([jnp.where(lane < A_DQK, q, zero),
                          jnp.where(lane >= A_DQK, q, zero)], axis=0)
    _init_softmax(m_scr, l_scr, acc_scr)

    def scores(j):
        off = pl.multiple_of(j * tq, tq)
        k = k_ref[0, pl.ds(off, tq), :]
        v = v_ref[0, pl.ds(off, tq), :]
        return lax.dot_general(qs, k, _NT, preferred_element_type=F32), v

    def body(j, carry):
        s, v = scores(j)
        _softmax_step(s, v, m_scr, l_scr, acc_scr)
        return carry

    lax.fori_loop(0, qi, body, 0)

    s, v = scores(qi)
    row = lax.broadcasted_iota(jnp.int32, s.shape, 0)
    col = lax.broadcasted_iota(jnp.int32, s.shape, 1)
    rq = jnp.where(row >= tq, row - tq, row)
    s = jnp.where(col // chunk <= rq // chunk, s, NEG)
    _softmax_step(s, v, m_scr, l_scr, acc_scr)

    o_all = acc_scr[...] / l_scr[...]
    lam_init = li_ref[...]
    lam = (jnp.exp(jnp.sum(lq1_ref[...] * lk1_ref[...], axis=1, keepdims=True))
           - jnp.exp(jnp.sum(lq2_ref[...] * lk2_ref[...], axis=1, keepdims=True)) + lam_init)
    o = o_all[:tq] - lam * o_all[tq:]
    o = o * lax.rsqrt(jnp.mean(o * o, axis=1, keepdims=True) + LN_EPS)
    o_ref[0] = (o * g_ref[...] * (1.0 - lam_init)).astype(BF16)


def _diff_attn(z, lq1, lk1, lq2, lk2, lam_init, g_diff, chunk):
    B, S, _ = z.shape
    tq = min(TQ_A, S)
    qb, kb, vb = Z_AQ // HEAD_W, Z_AK // HEAD_W, Z_AV // HEAD_W
    small = pl.BlockSpec((1, A_DQK), lambda b, h, i: (0, 0))
    return pl.pallas_call(
        functools.partial(_diff_attn_kernel, tq=tq, chunk=chunk),
        grid=(B, HEADS, S // tq),
        in_specs=[
            pl.BlockSpec((1, tq, HEAD_W), lambda b, h, i: (b, i, qb + h)),
            pl.BlockSpec((1, S, HEAD_W), lambda b, h, i: (b, 0, kb + h)),
            pl.BlockSpec((1, S, HEAD_W), lambda b, h, i: (b, 0, vb + h)),
            small, small, small, small,
            pl.BlockSpec((1, 1), lambda b, h, i: (0, 0)),
            pl.BlockSpec((1, HEAD_W), lambda b, h, i: (0, h)),
        ],
        out_specs=pl.BlockSpec((1, tq, HEAD_W), lambda b, h, i: (b, i, h)),
        out_shape=jax.ShapeDtypeStruct((B, S, HEADS * HEAD_W), BF16),
        scratch_shapes=[pltpu.VMEM((2 * tq, 1), F32), pltpu.VMEM((2 * tq, 1), F32),
                        pltpu.VMEM((2 * tq, HEAD_W), F32)],
        compiler_params=_cparams(("parallel", "parallel", "arbitrary")),
        name="diff_attn",
    )(z, z, z, lq1, lk1, lq2, lk2, lam_init, g_diff)


def _fox_attn_kernel(q_ref, k_ref, v_ref, f_ref, o_ref, m_scr, l_scr, acc_scr, *, tq):
    qi = pl.program_id(2)
    q = q_ref[0]
    _init_softmax(m_scr, l_scr, acc_scr)
    q_off = pl.multiple_of(qi * tq, tq)
    c = f_ref[0, :, pl.ds(q_off, HEAD_W)][:, 0:1]

    def scores(j):
        off = pl.multiple_of(j * tq, tq)
        k = k_ref[0, pl.ds(off, tq), :]
        v = v_ref[0, pl.ds(off, tq), :]
        bias = c - f_ref[0, :, pl.ds(off, tq)]
        return lax.dot_general(q, k, _NT, preferred_element_type=F32) + bias, v

    def body(j, carry):
        s, v = scores(j)
        _softmax_step(s, v, m_scr, l_scr, acc_scr)
        return carry

    lax.fori_loop(0, qi, body, 0)

    s, v = scores(qi)
    row = lax.broadcasted_iota(jnp.int32, s.shape, 0)
    col = lax.broadcasted_iota(jnp.int32, s.shape, 1)
    s = jnp.where(col <= row, s, NEG)
    _softmax_step(s, v, m_scr, l_scr, acc_scr)

    o_ref[0] = (acc_scr[...] / l_scr[...]).astype(BF16)


def _fox_attn(z, f3):
    B, S, _ = z.shape
    tq = min(TQ_B, S)
    rows = f3.shape[0] // B
    qb, kb, vb = Z_BQ // HEAD_W, Z_BK // HEAD_W, Z_BV // HEAD_W
    return pl.pallas_call(
        functools.partial(_fox_attn_kernel, tq=tq),
        grid=(B, HEADS, S // tq),
        in_specs=[
            pl.BlockSpec((1, tq, HEAD_W), lambda b, h, i: (b, i, qb + h)),
            pl.BlockSpec((1, S, HEAD_W), lambda b, h, i: (b, 0, kb + h)),
            pl.BlockSpec((1, S, HEAD_W), lambda b, h, i: (b, 0, vb + h)),
            pl.BlockSpec((1, 1, S), lambda b, h, i: (b * rows + h, 0, 0)),
        ],
        out_specs=pl.BlockSpec((1, tq, HEAD_W), lambda b, h, i: (b, i, h)),
        out_shape=jax.ShapeDtypeStruct((B, S, HEADS * HEAD_W), BF16),
        scratch_shapes=[pltpu.VMEM((tq, 1), F32), pltpu.VMEM((tq, 1), F32),
                        pltpu.VMEM((tq, HEAD_W), F32)],
        compiler_params=_cparams(("parallel", "parallel", "arbitrary")),
        name="fox_attn",
    )(z, z, z, f3)


def _retention_kernel(q_ref, k_ref, v_ref, cg_ref, cos_ref, sin_ref, g_ref, o_ref, r_scr, *, L):
    @pl.when(pl.program_id(1) == 0)
    def _():
        r_scr[...] = jnp.zeros(r_scr.shape, F32)

    cosf = cos_ref[...]
    sinf = sin_ref[...]
    ii = lax.broadcasted_iota(jnp.int32, (L, L), 0)
    jj = lax.broadcasted_iota(jnp.int32, (L, L), 1)
    dpos = jnp.maximum(ii - jj, 0).astype(F32)
    causal = ii >= jj
    pos = lax.broadcasted_iota(jnp.int32, (L, 1), 0).astype(F32)

    for h in range(HEADS):
        lg = math.log(1.0 - 2.0 ** (-5.0 - h))
        qh = q_ref[0, :, h * HEAD_W:(h + 1) * HEAD_W].astype(F32)
        kh = k_ref[0, :, h * HEAD_W:(h + 1) * HEAD_W].astype(F32)
        qr = qh * cosf + pltpu.roll(qh, HEAD_W // 2, 1) * sinf
        kr = kh * cosf + pltpu.roll(kh, HEAD_W // 2, 1) * sinf
        v = v_ref[0, :, h * C_DV:(h + 1) * C_DV]
        dmask = jnp.where(causal, jnp.exp(lg * dpos), 0.0)
        s = lax.dot_general(qr.astype(BF16), kr.astype(BF16), _NT, preferred_element_type=F32) * dmask
        inner = jnp.dot(s.astype(BF16), v, preferred_element_type=F32)
        r = r_scr[h]
        q_dec = qr * jnp.exp(lg * (pos + 1.0))
        cross = jnp.dot(q_dec.astype(BF16), r.astype(BF16), preferred_element_type=F32)
        k_dec = kr * jnp.exp(lg * (L - 1.0 - pos))
        r_scr[h] = math.exp(lg * L) * r + jnp.dot(k_dec.T.astype(BF16), v, preferred_element_type=F32)
        o = inner + cross
        mu = jnp.mean(o, axis=1, keepdims=True)
        var = jnp.mean(jnp.square(o - mu), axis=1, keepdims=True)
        y = (o - mu) * lax.rsqrt(var + LN_EPS) * g_ref[:, h * C_DV:(h + 1) * C_DV]
        cg = cg_ref[0, :, h * C_DV:(h + 1) * C_DV].astype(F32)
        o_ref[0, :, h * C_DV:(h + 1) * C_DV] = (y * (cg * jax.nn.sigmoid(cg))).astype(BF16)


def _retention(z, cos_t, sin_t, g_ret):
    B, S, _ = z.shape
    L = min(L_RET, S)
    qk_w, v_w = HEADS * HEAD_W, HEADS * C_DV
    return pl.pallas_call(
        functools.partial(_retention_kernel, L=L),
        grid=(B, S // L),
        in_specs=[
            pl.BlockSpec((1, L, qk_w), lambda b, n: (b, n, Z_CQ // qk_w)),
            pl.BlockSpec((1, L, qk_w), lambda b, n: (b, n, Z_CK // qk_w)),
            pl.BlockSpec((1, L, v_w), lambda b, n: (b, n, Z_CV // v_w)),
            pl.BlockSpec((1, L, v_w), lambda b, n: (b, n, Z_CG // v_w)),
            pl.BlockSpec((L, HEAD_W), lambda b, n: (n, 0)),
            pl.BlockSpec((L, HEAD_W), lambda b, n: (n, 0)),
            pl.BlockSpec((1, v_w), lambda b, n: (0, 0)),
        ],
        out_specs=pl.BlockSpec((1, L, v_w), lambda b, n: (b, n, 0)),
        out_shape=jax.ShapeDtypeStruct((B, S, v_w), BF16),
        scratch_shapes=[pltpu.VMEM((HEADS, HEAD_W, C_DV), F32)],
        compiler_params=_cparams(("parallel", "arbitrary")),
        name="retention",
    )(z, z, z, z, cos_t, sin_t, g_ret)


def _rope_tables(S):
    half = HEAD_W // 2
    inv = 1.0 / (ROPE_BASE ** np.linspace(0.0, 1.0, half))
    ang = np.arange(S, dtype=np.float64)[:, None] * inv[None, :]
    cos, sin = np.cos(ang), np.sin(ang)
    return (jnp.asarray(np.concatenate([cos, cos], axis=1), F32),
            jnp.asarray(np.concatenate([-sin, sin], axis=1), F32))


def _layernorm(r, g, b):
    mu = jnp.mean(r, axis=1, keepdims=True)
    var = jnp.mean(jnp.square(r - mu), axis=1, keepdims=True)
    return (r - mu) * lax.rsqrt(var + LN_EPS) * g + b


def _merge_kernel(ya_ref, yb_ref, yc_ref, ga_ref, gb_ref, gc_ref, x_ref, gt_ref, sh2_ref, sc2_ref,
                  wpa_ref, wpb_ref, wpc_ref, wout_ref, lng_ref, lnb_ref, xo_ref, ho_ref, *, alpha):
    def branch(y_ref, g_ref, w_ref):
        gate = jax.nn.sigmoid(g_ref[0].astype(F32))
        return gate * jnp.dot(y_ref[0], w_ref[...], preferred_element_type=F32)

    m = branch(ya_ref, ga_ref, wpa_ref) + branch(yb_ref, gb_ref, wpb_ref) + branch(yc_ref, gc_ref, wpc_ref)
    y = jnp.dot(m.astype(BF16), wout_ref[...], preferred_element_type=F32)
    xn = _layernorm(alpha * x_ref[0] + gt_ref[0] * y, lng_ref[...], lnb_ref[...])
    xo_ref[0] = xn
    ho_ref[0] = (xn * (1.0 + sc2_ref[0]) + sh2_ref[0]).astype(BF16)


def _const_spec(shape):
    return pl.BlockSpec(shape, lambda b, i: (0,) * len(shape), pipeline_mode=pl.Buffered(1))


def _merge(ya, yb, yc, z, x, mod_l, wpa, wpb, wpc, wout, lng, lnb, alpha):
    B, S, D = x.shape
    tm = min(TM_MERGE, S)
    gblk = Z_GATES // D
    tok = lambda w, col: pl.BlockSpec((1, tm, w), lambda b, i: (b, i, col))
    modv = lambda col: pl.BlockSpec((1, 1, D), lambda b, i: (b, 0, col))
    return pl.pallas_call(
        functools.partial(_merge_kernel, alpha=alpha),
        grid=(B, S // tm),
        in_specs=[
            tok(ya.shape[2], 0), tok(yb.shape[2], 0), tok(yc.shape[2], 0),
            tok(D, gblk), tok(D, gblk + 1), tok(D, gblk + 2),
            tok(D, 0),
            modv(2), modv(3), modv(4),
            _const_spec(wpa.shape), _const_spec(wpb.shape), _const_spec(wpc.shape),
            _const_spec(wout.shape), _const_spec(lng.shape), _const_spec(lnb.shape),
        ],
        out_specs=[tok(D, 0), tok(D, 0)],
        out_shape=[jax.ShapeDtypeStruct((B, S, D), F32), jax.ShapeDtypeStruct((B, S, D), BF16)],
        compiler_params=_cparams(("parallel", "parallel")),
        name="merge_out_ln",
    )(ya, yb, yc, z, z, z, x, mod_l, mod_l, mod_l, wpa, wpb, wpc, wout, lng, lnb)


def _ffn_kernel(h_ref, x_ref, gt_ref, wup_ref, wconv_ref, bconv_ref, wdown_ref, lng_ref, lnb_ref,
                xo_ref, a_scr, tail_scr, *, alpha, cw):
    @pl.when(pl.program_id(1) == 0)
    def _():
        tail_scr[...] = jnp.zeros(tail_scr.shape, F32)

    h = h_ref[0]
    tm = h.shape[0]
    dff = a_scr.shape[1]
    row = lax.broadcasted_iota(jnp.int32, (tm, cw), 0)
    for ci in range(dff // cw):
        lo = ci * cw
        u = jnp.dot(h, wup_ref[:, lo:lo + cw], preferred_element_type=F32)
        g = jnp.dot(h, wup_ref[:, dff + lo:dff + lo + cw], preferred_element_type=F32)
        tail = tail_scr[:, lo:lo + cw]
        p1, p2 = tail[7:8], tail[6:7]
        um1 = jnp.where(row == 0, p1, pltpu.roll(u, 1, 0))
        um2 = jnp.where(row == 0, p2, jnp.where(row == 1, p1, pltpu.roll(u, 2, 0)))
        tail_scr[:, lo:lo + cw] = u[tm - 8:]
        conv = bconv_ref[:, lo:lo + cw] + um2 * wconv_ref[0:1, lo:lo + cw]
        conv = conv + um1 * wconv_ref[1:2, lo:lo + cw]
        conv = conv + u * wconv_ref[2:3, lo:lo + cw]
        act = 0.5 * conv * (1.0 + lax.erf(conv * (2.0 ** -0.5)))
        a_scr[:, lo:lo + cw] = (act * g).astype(BF16)
    y = jnp.dot(a_scr[...], wdown_ref[...], preferred_element_type=F32)
    xo_ref[0] = _layernorm(alpha * x_ref[0] + gt_ref[0] * y, lng_ref[...], lnb_ref[...])


def _ffn(h, x, mod_l, wup, wconv, bconv, wdown, lng, lnb, alpha):
    B, S, D = x.shape
    dff = wdown.shape[0]
    tm = min(TM_FFN, S)
    tok = pl.BlockSpec((1, tm, D), lambda b, i: (b, i, 0))
    return pl.pallas_call(
        functools.partial(_ffn_kernel, alpha=alpha, cw=CW_FFN),
        grid=(B, S // tm),
        in_specs=[
            tok, tok,
            pl.BlockSpec((1, 1, D), lambda b, i: (b, 0, 5)),
            _const_spec(wup.shape), _const_spec(wconv.shape), _const_spec(bconv.shape),
            _const_spec(wdown.shape), _const_spec(lng.shape), _const_spec(lnb.shape),
        ],
        out_specs=tok,
        out_shape=jax.ShapeDtypeStruct((B, S, D), F32),
        scratch_shapes=[pltpu.VMEM((tm, dff), BF16), pltpu.VMEM((8, dff), F32)],
        compiler_params=_cparams(("parallel", "arbitrary")),
        name="conv_ffn_ln",
    )(h, x, mod_l, wup, wconv, bconv, wdown, lng, lnb)


def _prep_in_proj(w, b):
    D = w.shape[0]
    scale = np.ones((Z_WIDTH,), np.float32)
    scale[Z_AQ:Z_AQ + 512] = A_DQK ** -0.5
    scale[Z_BQ:Z_BQ + 512] = HEAD_W ** -0.5
    scale[Z_CK:Z_CK + 512] = HEAD_W ** -0.5
    wm = jnp.concatenate([w[:, :BF_OFF], w[:, BF_OFF + HEADS:]], axis=1) * scale
    bm = jnp.concatenate([b[:BF_OFF], b[BF_OFF + HEADS:]]) * scale
    wf = jnp.pad(w[:, BF_OFF:BF_OFF + HEADS], ((0, 0), (0, BF_PAD - HEADS)))
    bf = jnp.pad(b[BF_OFF:BF_OFF + HEADS], (0, BF_PAD - HEADS))
    return wm.astype(BF16), bm.reshape(1, -1), wf.astype(BF16), bf.reshape(1, -1)


def kernel(x, c, w_ada, b_ada, w_in, b_in, lam_q1, lam_k1, lam_q2, lam_k2, g_diff, g_ret, w_pa, w_pb, w_pc, w_out, ln_g, ln_b, w_up, w_conv, b_conv, w_down):
    B, S, D = x.shape
    depth = w_ada.shape[0]
    alpha = (2 * depth) ** 0.25
    chunk = 64
    mod = _ada(c, w_ada, b_ada)
    cos_t, sin_t = _rope_tables(S)
    for l in range(depth):
        mod_l = mod[l].reshape(B, 1, 6 * D)
        lam_init = jnp.full((1, 1), 0.8 - 0.6 * math.exp(-0.3 * l), F32)
        wm, bm, wf, bf = _prep_in_proj(w_in[l], b_in[l])
        z, lf = _inproj(x, mod_l, wm, bm, wf, bf)
        f = _cumsum(lf)
        ya = _diff_attn(z, lam_q1[l][None], lam_k1[l][None], lam_q2[l][None], lam_k2[l][None],
                        lam_init, g_diff[l][None], chunk)
        yb = _fox_attn(z, f.reshape(B * f.shape[1], 1, S))
        yc = _retention(z, cos_t, sin_t, g_ret[l][None])
        x, h2 = _merge(ya, yb, yc, z, x, mod_l,
                       w_pa[l].astype(BF16), w_pb[l].astype(BF16), w_pc[l].astype(BF16),
                       w_out[l].astype(BF16), ln_g[l, 0][None], ln_b[l, 0][None], alpha)
        x = _ffn(h2, x, mod_l, w_up[l].astype(BF16), w_conv[l], b_conv[l][None],
                 w_down[l].astype(BF16), ln_g[l, 1][None], ln_b[l, 1][None], alpha)
    return x
```

```python
import functools
import math

import numpy as np
import jax
import jax.numpy as jnp
from jax import lax
from jax.experimental import pallas as pl
from jax.experimental.pallas import tpu as pltpu

F32 = jnp.float32
BF16 = jnp.bfloat16

HEADS = 4
A_DQK = 64
HEAD_W = 128
C_DV = 256
ROPE_BASE = 10000.0
LN_EPS = 1e-5
CONV_W = 3

Z_AQ, Z_AK, Z_AV = 0, 512, 1024
Z_BQ, Z_BK, Z_BV = 1536, 2048, 2560
Z_CQ, Z_CK, Z_CV, Z_CG = 3072, 3584, 4096, 5120
Z_GATES = 6144
Z_WIDTH = 9216
BF_OFF = 3072
BF_PAD = 128

NEG = -1e30
LOG2E = 1.4426950408889634
VT_ROWS = 144

TN_ADA = 1024
TM_IN, TN_IN = 1024, 1024
TQ_A = 512
TQ_B = 512
PROLOGUE_CHUNK = 512
L_RET = 256
TM_MERGE = 512
TM_FFN = 512
CW_FFN = 256

VMEM_LIMIT = 56 * 1024 * 1024


def _cparams(sem):
    return pltpu.CompilerParams(dimension_semantics=sem, vmem_limit_bytes=VMEM_LIMIT)


def _ada_kernel(ct_ref, w_ref, b_ref, o_ref):
    w = w_ref[0]
    rows = []
    for b in range(ct_ref.shape[0]):
        cb = ct_ref[b]
        ca = cb * jax.nn.sigmoid(cb)
        rows.append(jnp.sum(ca * w, axis=0, keepdims=True))
    o_ref[0] = jnp.concatenate(rows, axis=0) + b_ref[0]


def _ada(c, w_ada, b_ada):
    B, D = c.shape
    L, _, N = w_ada.shape
    tn = min(TN_ADA, N)
    return pl.pallas_call(
        _ada_kernel,
        grid=(L, N // tn),
        in_specs=[
            pl.BlockSpec((B, D, 1), lambda l, j: (0, 0, 0)),
            pl.BlockSpec((1, D, tn), lambda l, j: (l, 0, j)),
            pl.BlockSpec((1, 1, tn), lambda l, j: (l, 0, j)),
        ],
        out_specs=pl.BlockSpec((1, B, tn), lambda l, j: (l, 0, j)),
        out_shape=jax.ShapeDtypeStruct((L, B, N), F32),
        compiler_params=_cparams(("parallel", "parallel")),
        name="ada_mod",
    )(c.reshape(B, D, 1), w_ada, b_ada.reshape(L, 1, N))


def _inproj_kernel(x_ref, sh_ref, sc_ref, w_ref, b_ref, wf_ref, bf_ref, z_ref, lf_ref, h_scr):
    @pl.when(pl.program_id(2) == 0)
    def _():
        h = x_ref[0] * (1.0 + sc_ref[0]) + sh_ref[0]
        hb = h.astype(BF16)
        h_scr[...] = hb
        zf = jnp.dot(hb, wf_ref[...], preferred_element_type=F32) + bf_ref[...]
        lt = zf.T[:8]
        lf_ref[0] = jnp.minimum(lt, 0.0) - jnp.log1p(jnp.exp(-jnp.abs(lt)))

    acc = jnp.dot(h_scr[...], w_ref[...], preferred_element_type=F32)
    z_ref[0] = (acc + b_ref[...]).astype(BF16)


def _inproj(x, mod_l, w, b, wf, bf):
    B, S, D = x.shape
    N = w.shape[1]
    tm, tn = min(TM_IN, S), min(TN_IN, N)
    return pl.pallas_call(
        _inproj_kernel,
        grid=(B, S // tm, N // tn),
        in_specs=[
            pl.BlockSpec((1, tm, D), lambda b, i, j: (b, i, 0)),
            pl.BlockSpec((1, 1, D), lambda b, i, j: (b, 0, 0)),
            pl.BlockSpec((1, 1, D), lambda b, i, j: (b, 0, 1)),
            pl.BlockSpec((D, tn), lambda b, i, j: (0, j)),
            pl.BlockSpec((1, tn), lambda b, i, j: (0, j)),
            pl.BlockSpec((D, BF_PAD), lambda b, i, j: (0, 0)),
            pl.BlockSpec((1, BF_PAD), lambda b, i, j: (0, 0)),
        ],
        out_specs=[
            pl.BlockSpec((1, tm, tn), lambda b, i, j: (b, i, j)),
            pl.BlockSpec((1, 8, tm), lambda b, i, j: (b, 0, i)),
        ],
        out_shape=[
            jax.ShapeDtypeStruct((B, S, N), BF16),
            jax.ShapeDtypeStruct((B, 8, S), F32),
        ],
        scratch_shapes=[pltpu.VMEM((tm, D), BF16)],
        compiler_params=_cparams(("parallel", "parallel", "arbitrary")),
        name="in_proj",
    )(x, mod_l, mod_l, w, b, wf, bf)


def _cumsum_kernel(lf_ref, f_ref):
    x = lf_ref[0]
    lane = lax.broadcasted_iota(jnp.int32, x.shape, 1)
    k = 1
    while k < x.shape[1]:
        x = x + jnp.where(lane >= k, pltpu.roll(x, k, 1), 0.0)
        k *= 2
    f_ref[0] = x


def _cumsum(lf):
    B, R, S = lf.shape
    return pl.pallas_call(
        _cumsum_kernel,
        grid=(B,),
        in_specs=[pl.BlockSpec((1, R, S), lambda b: (b, 0, 0))],
        out_specs=pl.BlockSpec((1, R, S), lambda b: (b, 0, 0)),
        out_shape=jax.ShapeDtypeStruct((B, R, S), F32),
        compiler_params=_cparams(("parallel",)),
        name="forget_cumsum",
    )(lf)


def _build_vt(v_ref, vt_scr, tk):
    ones_row = (lax.broadcasted_iota(jnp.int32, (VT_ROWS - HEAD_W, tk), 0) == 0).astype(BF16)
    for c in range(v_ref.shape[1] // tk):
        vb = v_ref[0, c * tk:(c + 1) * tk, :].astype(F32)
        vt_scr[c, 0:HEAD_W, :] = vb.T.astype(BF16)
        vt_scr[c, HEAD_W:VT_ROWS, :] = ones_row


def _softmax_step_t(s, vt, m_scr, acc_scr):
    m_prev = m_scr[...]
    m_new = jnp.maximum(m_prev, jnp.max(s, axis=0, keepdims=True))
    p = jnp.exp2(s - m_new).astype(BF16)
    alpha = jnp.exp2(m_prev - m_new)
    acc_scr[...] = alpha * acc_scr[...] + jnp.dot(vt, p, preferred_element_type=F32)
    m_scr[...] = m_new


def _sweep(qi, scores, mask_fn, vt_scr, m_scr, acc_scr, sa_scr, sb_scr):
    m_scr[...] = jnp.full(m_scr.shape, NEG, F32)
    acc_scr[...] = jnp.zeros(acc_scr.shape, F32)
    sa_scr[...] = scores(0)

    def pair(t, carry):
        j = 2 * t
        sb_scr[...] = scores(j + 1)
        _softmax_step_t(sa_scr[...], vt_scr[j], m_scr, acc_scr)
        sa_scr[...] = scores(j + 2)
        _softmax_step_t(sb_scr[...], vt_scr[j + 1], m_scr, acc_scr)
        return carry

    lax.fori_loop(0, qi // 2, pair, 0)

    @pl.when(qi % 2 == 0)
    def _():
        _softmax_step_t(mask_fn(sa_scr[...]), vt_scr[qi], m_scr, acc_scr)

    @pl.when(qi % 2 == 1)
    def _():
        sb_scr[...] = scores(qi)
        _softmax_step_t(sa_scr[...], vt_scr[qi - 1], m_scr, acc_scr)
        _softmax_step_t(mask_fn(sb_scr[...]), vt_scr[qi], m_scr, acc_scr)


_NT = (((1,), (1,)), ((), ()))


def _attn_scratch(S, tq, nq):
    return [pltpu.VMEM((S // tq, VT_ROWS, tq), BF16),
            pltpu.VMEM((1, nq), F32),
            pltpu.VMEM((VT_ROWS, nq), F32),
            pltpu.VMEM((tq, nq), F32), pltpu.VMEM((tq, nq), F32)]


def _diff_attn_kernel(q_ref, k_ref, v_ref, lq1_ref, lk1_ref, lq2_ref, lk2_ref, li_ref, g_ref,
                      o_ref, vt_scr, m_scr, acc_scr, sa_scr, sb_scr, *, tq, chunk):
    qi = pl.program_id(2)

    @pl.when(qi == 0)
    def _():
        _build_vt(v_ref, vt_scr, tq)

    q = q_ref[0]
    lane = lax.broadcasted_iota(jnp.int32, q.shape, 1)
    zero = jnp.zeros_like(q)
    qs = jnp.concatenate([jnp.where(lane < A_DQK, q, zero),
                          jnp.where(lane >= A_DQK, q, zero)], axis=0)

    def scores(j):
        off = pl.multiple_of(j * tq, tq)
        return lax.dot_general(k_ref[0, pl.ds(off, tq), :], qs, _NT,
                               preferred_element_type=F32)

    def mask_fn(s):
        key = lax.broadcasted_iota(jnp.int32, s.shape, 0)
        col = lax.broadcasted_iota(jnp.int32, s.shape, 1)
        qry = jnp.where(col >= tq, col - tq, col)
        return jnp.where(key // chunk <= qry // chunk, s, NEG)

    _sweep(qi, scores, mask_fn, vt_scr, m_scr, acc_scr, sa_scr, sb_scr)

    acc = acc_scr[...]
    ot_all = acc[0:HEAD_W] / acc[HEAD_W:HEAD_W + 1]
    lam_init = li_ref[...]
    lam = (jnp.exp(jnp.sum(lq1_ref[...] * lk1_ref[...], axis=1, keepdims=True))
           - jnp.exp(jnp.sum(lq2_ref[...] * lk2_ref[...], axis=1, keepdims=True)) + lam_init)
    o = (ot_all[:, :tq] - lam * ot_all[:, tq:]).T
    o = o * lax.rsqrt(jnp.mean(o * o, axis=1, keepdims=True) + LN_EPS)
    o_ref[0] = (o * g_ref[...] * (1.0 - lam_init)).astype(BF16)


def _diff_attn(z, lq1, lk1, lq2, lk2, lam_init, g_diff, chunk):
    B, S, _ = z.shape
    tq = min(TQ_A, S)
    qb, kb, vb = Z_AQ // HEAD_W, Z_AK // HEAD_W, Z_AV // HEAD_W
    small = pl.BlockSpec((1, A_DQK), lambda b, h, i: (0, 0))
    return pl.pallas_call(
        functools.partial(_diff_attn_kernel, tq=tq, chunk=chunk),
        grid=(B, HEADS, S // tq),
        in_specs=[
            pl.BlockSpec((1, tq, HEAD_W), lambda b, h, i: (b, i, qb + h)),
            pl.BlockSpec((1, S, HEAD_W), lambda b, h, i: (b, 0, kb + h)),
            pl.BlockSpec((1, S, HEAD_W), lambda b, h, i: (b, 0, vb + h)),
            small, small, small, small,
            pl.BlockSpec((1, 1), lambda b, h, i: (0, 0)),
            pl.BlockSpec((1, HEAD_W), lambda b, h, i: (0, h)),
        ],
        out_specs=pl.BlockSpec((1, tq, HEAD_W), lambda b, h, i: (b, i, h)),
        out_shape=jax.ShapeDtypeStruct((B, S, HEADS * HEAD_W), BF16),
        scratch_shapes=_attn_scratch(S, tq, 2 * tq),
        compiler_params=_cparams(("parallel", "parallel", "arbitrary")),
        name="diff_attn",
    )(z, z, z, lq1, lk1, lq2, lk2, lam_init, g_diff)


def _fox_attn_kernel(q_ref, k_ref, v_ref, f_ref, o_ref, frep_scr, vt_scr, m_scr, acc_scr,
                     sa_scr, sb_scr, *, tq):
    qi = pl.program_id(2)
    S = k_ref.shape[1]
    pc = min(PROLOGUE_CHUNK, S)

    @pl.when(qi == 0)
    def _():
        _build_vt(v_ref, vt_scr, tq)
        for c in range(S // pc):
            fb = jnp.broadcast_to(f_ref[0, :, c * pc:(c + 1) * pc] * LOG2E, (HEAD_W, pc))
            frep_scr[c * pc:(c + 1) * pc, :] = fb.T

    q = q_ref[0]
    q_off = pl.multiple_of(qi * tq, tq)
    c0 = frep_scr[pl.ds(q_off, 8), :][0:1, 0:1]

    def scores(j):
        off = pl.multiple_of(j * tq, tq)
        fsub = frep_scr[pl.ds(off, tq), :] - c0
        s = lax.dot_general(k_ref[0, pl.ds(off, tq), :], q, _NT, preferred_element_type=F32)
        return s - jnp.tile(fsub, (1, tq // HEAD_W))

    def mask_fn(s):
        key = lax.broadcasted_iota(jnp.int32, s.shape, 0)
        qry = lax.broadcasted_iota(jnp.int32, s.shape, 1)
        return jnp.where(key <= qry, s, NEG)

    _sweep(qi, scores, mask_fn, vt_scr, m_scr, acc_scr, sa_scr, sb_scr)

    acc = acc_scr[...]
    o_ref[0] = (acc[0:HEAD_W] / acc[HEAD_W:HEAD_W + 1]).T.astype(BF16)


def _fox_attn(z, f3):
    B, S, _ = z.shape
    tq = min(TQ_B, S)
    rows = f3.shape[0] // B
    qb, kb, vb = Z_BQ // HEAD_W, Z_BK // HEAD_W, Z_BV // HEAD_W
    return pl.pallas_call(
        functools.partial(_fox_attn_kernel, tq=tq),
        grid=(B, HEADS, S // tq),
        in_specs=[
            pl.BlockSpec((1, tq, HEAD_W), lambda b, h, i: (b, i, qb + h)),
            pl.BlockSpec((1, S, HEAD_W), lambda b, h, i: (b, 0, kb + h)),
            pl.BlockSpec((1, S, HEAD_W), lambda b, h, i: (b, 0, vb + h)),
            pl.BlockSpec((1, 1, S), lambda b, h, i: (b * rows + h, 0, 0)),
        ],
        out_specs=pl.BlockSpec((1, tq, HEAD_W), lambda b, h, i: (b, i, h)),
        out_shape=jax.ShapeDtypeStruct((B, S, HEADS * HEAD_W), BF16),
        scratch_shapes=[pltpu.VMEM((S, HEAD_W), F32)] + _attn_scratch(S, tq, tq),
        compiler_params=_cparams(("parallel", "parallel", "arbitrary")),
        name="fox_attn",
    )(z, z, z, f3)


def _retention_kernel(q_ref, k_ref, v_ref, cg_ref, cos_ref, sin_ref, g_ref, o_ref, r_scr, *, L):
    @pl.when(pl.program_id(1) == 0)
    def _():
        r_scr[...] = jnp.zeros(r_scr.shape, F32)

    cosf = cos_ref[...]
    sinf = sin_ref[...]
    ii = lax.broadcasted_iota(jnp.int32, (L, L), 0)
    jj = lax.broadcasted_iota(jnp.int32, (L, L), 1)
    dpos = jnp.maximum(ii - jj, 0).astype(F32)
    causal = ii >= jj
    pos = lax.broadcasted_iota(jnp.int32, (L, 1), 0).astype(F32)

    for h in range(HEADS):
        lg = math.log(1.0 - 2.0 ** (-5.0 - h))
        qh = q_ref[0, :, h * HEAD_W:(h + 1) * HEAD_W].astype(F32)
        kh = k_ref[0, :, h * HEAD_W:(h + 1) * HEAD_W].astype(F32)
        qr = qh * cosf + pltpu.roll(qh, HEAD_W // 2, 1) * sinf
        kr = kh * cosf + pltpu.roll(kh, HEAD_W // 2, 1) * sinf
        v = v_ref[0, :, h * C_DV:(h + 1) * C_DV]
        dmask = jnp.where(causal, jnp.exp(lg * dpos), 0.0)
        s = lax.dot_general(qr.astype(BF16), kr.astype(BF16), _NT, preferred_element_type=F32) * dmask
        inner = jnp.dot(s.astype(BF16), v, preferred_element_type=F32)
        r = r_scr[h]
        q_dec = qr * jnp.exp(lg * (pos + 1.0))
        cross = jnp.dot(q_dec.astype(BF16), r.astype(BF16), preferred_element_type=F32)
        k_dec = kr * jnp.exp(lg * (L - 1.0 - pos))
        r_scr[h] = math.exp(lg * L) * r + jnp.dot(k_dec.T.astype(BF16), v, preferred_element_type=F32)
        o = inner + cross
        mu = jnp.mean(o, axis=1, keepdims=True)
        var = jnp.mean(jnp.square(o - mu), axis=1, keepdims=True)
        y = (o - mu) * lax.rsqrt(var + LN_EPS) * g_ref[:, h * C_DV:(h + 1) * C_DV]
        cg = cg_ref[0, :, h * C_DV:(h + 1) * C_DV].astype(F32)
        o_ref[0, :, h * C_DV:(h + 1) * C_DV] = (y * (cg * jax.nn.sigmoid(cg))).astype(BF16)


def _retention(z, cos_t, sin_t, g_ret):
    B, S, _ = z.shape
    L = min(L_RET, S)
    qk_w, v_w = HEADS * HEAD_W, HEADS * C_DV
    return pl.pallas_call(
        functools.partial(_retention_kernel, L=L),
        grid=(B, S // L),
        in_specs=[
            pl.BlockSpec((1, L, qk_w), lambda b, n: (b, n, Z_CQ // qk_w)),
            pl.BlockSpec((1, L, qk_w), lambda b, n: (b, n, Z_CK // qk_w)),
            pl.BlockSpec((1, L, v_w), lambda b, n: (b, n, Z_CV // v_w)),
            pl.BlockSpec((1, L, v_w), lambda b, n: (b, n, Z_CG // v_w)),
            pl.BlockSpec((L, HEAD_W), lambda b, n: (n, 0)),
            pl.BlockSpec((L, HEAD_W), lambda b, n: (n, 0)),
            pl.BlockSpec((1, v_w), lambda b, n: (0, 0)),
        ],
        out_specs=pl.BlockSpec((1, L, v_w), lambda b, n: (b, n, 0)),
        out_shape=jax.ShapeDtypeStruct((B, S, v_w), BF16),
        scratch_shapes=[pltpu.VMEM((HEADS, HEAD_W, C_DV), F32)],
        compiler_params=_cparams(("parallel", "arbitrary")),
        name="retention",
    )(z, z, z, z, cos_t, sin_t, g_ret)


def _rope_tables(S):
    half = HEAD_W // 2
    inv = 1.0 / (ROPE_BASE ** np.linspace(0.0, 1.0, half))
    ang = np.arange(S, dtype=np.float64)[:, None] * inv[None, :]
    cos, sin = np.cos(ang), np.sin(ang)
    return (jnp.asarray(np.concatenate([cos, cos], axis=1), F32),
            jnp.asarray(np.concatenate([-sin, sin], axis=1), F32))


def _layernorm(r, g, b):
    mu = jnp.mean(r, axis=1, keepdims=True)
    var = jnp.mean(jnp.square(r - mu), axis=1, keepdims=True)
    return (r - mu) * lax.rsqrt(var + LN_EPS) * g + b


def _merge_kernel(ya_ref, yb_ref, yc_ref, ga_ref, gb_ref, gc_ref, x_ref, gt_ref, sh2_ref, sc2_ref,
                  wpa_ref, wpb_ref, wpc_ref, wout_ref, lng_ref, lnb_ref, xo_ref, ho_ref, *, alpha):
    def branch(y_ref, g_ref, w_ref):
        gate = jax.nn.sigmoid(g_ref[0].astype(F32))
        return gate * jnp.dot(y_ref[0], w_ref[...], preferred_element_type=F32)

    m = branch(ya_ref, ga_ref, wpa_ref) + branch(yb_ref, gb_ref, wpb_ref) + branch(yc_ref, gc_ref, wpc_ref)
    y = jnp.dot(m.astype(BF16), wout_ref[...], preferred_element_type=F32)
    xn = _layernorm(alpha * x_ref[0] + gt_ref[0] * y, lng_ref[...], lnb_ref[...])
    xo_ref[0] = xn
    ho_ref[0] = (xn * (1.0 + sc2_ref[0]) + sh2_ref[0]).astype(BF16)


def _const_spec(shape):
    return pl.BlockSpec(shape, lambda b, i: (0,) * len(shape), pipeline_mode=pl.Buffered(1))


def _merge(ya, yb, yc, z, x, mod_l, wpa, wpb, wpc, wout, lng, lnb, alpha):
    B, S, D = x.shape
    tm = min(TM_MERGE, S)
    gblk = Z_GATES // D
    tok = lambda w, col: pl.BlockSpec((1, tm, w), lambda b, i: (b, i, col))
    modv = lambda col: pl.BlockSpec((1, 1, D), lambda b, i: (b, 0, col))
    return pl.pallas_call(
        functools.partial(_merge_kernel, alpha=alpha),
        grid=(B, S // tm),
        in_specs=[
            tok(ya.shape[2], 0), tok(yb.shape[2], 0), tok(yc.shape[2], 0),
            tok(D, gblk), tok(D, gblk + 1), tok(D, gblk + 2),
            tok(D, 0),
            modv(2), modv(3), modv(4),
            _const_spec(wpa.shape), _const_spec(wpb.shape), _const_spec(wpc.shape),
            _const_spec(wout.shape), _const_spec(lng.shape), _const_spec(lnb.shape),
        ],
        out_specs=[tok(D, 0), tok(D, 0)],
        out_shape=[jax.ShapeDtypeStruct((B, S, D), F32), jax.ShapeDtypeStruct((B, S, D), BF16)],
        compiler_params=_cparams(("parallel", "parallel")),
        name="merge_out_ln",
    )(ya, yb, yc, z, z, z, x, mod_l, mod_l, mod_l, wpa, wpb, wpc, wout, lng, lnb)


def _ffn_kernel(h_ref, x_ref, gt_ref, wup_ref, wconv_ref, bconv_ref, wdown_ref, lng_ref, lnb_ref,
                xo_ref, a_scr, tail_scr, *, alpha, cw):
    @pl.when(pl.program_id(1) == 0)
    def _():
        tail_scr[...] = jnp.zeros(tail_scr.shape, F32)

    h = h_ref[0]
    tm = h.shape[0]
    dff = a_scr.shape[1]
    row = lax.broadcasted_iota(jnp.int32, (tm, cw), 0)
    for ci in range(dff // cw):
        lo = ci * cw
        u = jnp.dot(h, wup_ref[:, lo:lo + cw], preferred_element_type=F32)
        g = jnp.dot(h, wup_ref[:, dff + lo:dff + lo + cw], preferred_element_type=F32)
        tail = tail_scr[:, lo:lo + cw]
        p1, p2 = tail[7:8], tail[6:7]
        um1 = jnp.where(row == 0, p1, pltpu.roll(u, 1, 0))
        um2 = jnp.where(row == 0, p2, jnp.where(row == 1, p1, pltpu.roll(u, 2, 0)))
        tail_scr[:, lo:lo + cw] = u[tm - 8:]
        conv = bconv_ref[:, lo:lo + cw] + um2 * wconv_ref[0:1, lo:lo + cw]
        conv = conv + um1 * wconv_ref[1:2, lo:lo + cw]
        conv = conv + u * wconv_ref[2:3, lo:lo + cw]
        act = 0.5 * conv * (1.0 + lax.erf(conv * (2.0 ** -0.5)))
        a_scr[:, lo:lo + cw] = (act * g).astype(BF16)
    y = jnp.dot(a_scr[...], wdown_ref[...], preferred_element_type=F32)
    xo_ref[0] = _layernorm(alpha * x_ref[0] + gt_ref[0] * y, lng_ref[...], lnb_ref[...])


def _ffn(h, x, mod_l, wup, wconv, bconv, wdown, lng, lnb, alpha):
    B, S, D = x.shape
    dff = wdown.shape[0]
    tm = min(TM_FFN, S)
    tok = pl.BlockSpec((1, tm, D), lambda b, i: (b, i, 0))
    return pl.pallas_call(
        functools.partial(_ffn_kernel, alpha=alpha, cw=CW_FFN),
        grid=(B, S // tm),
        in_specs=[
            tok, tok,
            pl.BlockSpec((1, 1, D), lambda b, i: (b, 0, 5)),
            _const_spec(wup.shape), _const_spec(wconv.shape), _const_spec(bconv.shape),
            _const_spec(wdown.shape), _const_spec(lng.shape), _const_spec(lnb.shape),
        ],
        out_specs=tok,
        out_shape=jax.ShapeDtypeStruct((B, S, D), F32),
        scratch_shapes=[pltpu.VMEM((tm, dff), BF16), pltpu.VMEM((8, dff), F32)],
        compiler_params=_cparams(("parallel", "arbitrary")),
        name="conv_ffn_ln",
    )(h, x, mod_l, wup, wconv, bconv, wdown, lng, lnb)


def _prep_in_proj(w, b):
    D = w.shape[0]
    scale = np.ones((Z_WIDTH,), np.float32)
    scale[Z_AQ:Z_AQ + 512] = A_DQK ** -0.5 * LOG2E
    scale[Z_BQ:Z_BQ + 512] = HEAD_W ** -0.5 * LOG2E
    scale[Z_CK:Z_CK + 512] = HEAD_W ** -0.5
    wm = jnp.concatenate([w[:, :BF_OFF], w[:, BF_OFF + HEADS:]], axis=1) * scale
    bm = jnp.concatenate([b[:BF_OFF], b[BF_OFF + HEADS:]]) * scale
    wf = jnp.pad(w[:, BF_OFF:BF_OFF + HEADS], ((0, 0), (0, BF_PAD - HEADS)))
    bf = jnp.pad(b[BF_OFF:BF_OFF + HEADS], (0, BF_PAD - HEADS))
    return wm.astype(BF16), bm.reshape(1, -1), wf.astype(BF16), bf.reshape(1, -1)


def kernel(x, c, w_ada, b_ada, w_in, b_in, lam_q1, lam_k1, lam_q2, lam_k2, g_diff, g_ret, w_pa, w_pb, w_pc, w_out, ln_g, ln_b, w_up, w_conv, b_conv, w_down):
    B, S, D = x.shape
    depth = w_ada.shape[0]
    alpha = (2 * depth) ** 0.25
    chunk = 64
    mod = _ada(c, w_ada, b_ada)
    cos_t, sin_t = _rope_tables(S)
    for l in range(depth):
        mod_l = mod[l].reshape(B, 1, 6 * D)
        lam_init = jnp.full((1, 1), 0.8 - 0.6 * math.exp(-0.3 * l), F32)
        wm, bm, wf, bf = _prep_in_proj(w_in[l], b_in[l])
        z, lf = _inproj(x, mod_l, wm, bm, wf, bf)
        f = _cumsum(lf)
        ya = _diff_attn(z, lam_q1[l][None], lam_k1[l][None], lam_q2[l][None], lam_k2[l][None],
                        lam_init, g_diff[l][None], chunk)
        yb = _fox_attn(z, f.reshape(B * f.shape[1], 1, S))
        yc = _retention(z, cos_t, sin_t, g_ret[l][None])
        x, h2 = _merge(ya, yb, yc, z, x, mod_l,
                       w_pa[l].astype(BF16), w_pb[l].astype(BF16), w_pc[l].astype(BF16),
                       w_out[l].astype(BF16), ln_g[l, 0][None], ln_b[l, 0][None], alpha)
        x = _ffn(h2, x, mod_l, w_up[l].astype(BF16), w_conv[l], b_conv[l][None],
                 w_down[l].astype(BF16), ln_g[l, 1][None], ln_b[l, 1][None], alpha)
    return x
```

```python
import functools
import math

import numpy as np
import jax
import jax.numpy as jnp
from jax import lax
from jax.experimental import pallas as pl
from jax.experimental.pallas import tpu as pltpu

F32 = jnp.float32
BF16 = jnp.bfloat16

HEADS = 4
A_DQK = 64
HEAD_W = 128
C_DV = 256
ROPE_BASE = 10000.0
LN_EPS = 1e-5
CONV_W = 3

Z_AQ, Z_AK, Z_AV = 0, 512, 1024
Z_BQ, Z_BK, Z_BV = 1536, 2048, 2560
Z_CQ, Z_CK, Z_CV, Z_CG = 3072, 3584, 4096, 5120
Z_GATES = 6144
Z_WIDTH = 9216
BF_OFF = 3072
BF_PAD = 128

NEG = -1e30
LOG2E = 1.4426950408889634
VT_ROWS = 144

TN_ADA = 1024
TM_IN, TN_IN = 1024, 1024
TQ_A = 512
TK_B = 512
PROLOGUE_CHUNK = 512
L_RET = 256
TM_MERGE = 512
TM_FFN = 512
CW_FFN = 256

VMEM_LIMIT = 56 * 1024 * 1024


def _cparams(sem):
    return pltpu.CompilerParams(dimension_semantics=sem, vmem_limit_bytes=VMEM_LIMIT)


def _ada_kernel(ct_ref, w_ref, b_ref, o_ref):
    w = w_ref[0]
    rows = []
    for b in range(ct_ref.shape[0]):
        cb = ct_ref[b]
        ca = cb * jax.nn.sigmoid(cb)
        rows.append(jnp.sum(ca * w, axis=0, keepdims=True))
    o_ref[0] = jnp.concatenate(rows, axis=0) + b_ref[0]


def _ada(c, w_ada, b_ada):
    B, D = c.shape
    L, _, N = w_ada.shape
    tn = min(TN_ADA, N)
    return pl.pallas_call(
        _ada_kernel,
        grid=(L, N // tn),
        in_specs=[
            pl.BlockSpec((B, D, 1), lambda l, j: (0, 0, 0)),
            pl.BlockSpec((1, D, tn), lambda l, j: (l, 0, j)),
            pl.BlockSpec((1, 1, tn), lambda l, j: (l, 0, j)),
        ],
        out_specs=pl.BlockSpec((1, B, tn), lambda l, j: (l, 0, j)),
        out_shape=jax.ShapeDtypeStruct((L, B, N), F32),
        compiler_params=_cparams(("parallel", "parallel")),
        name="ada_mod",
    )(c.reshape(B, D, 1), w_ada, b_ada.reshape(L, 1, N))


def _inproj_kernel(x_ref, sh_ref, sc_ref, w_ref, b_ref, wf_ref, bf_ref, z_ref, lf_ref, h_scr):
    @pl.when(pl.program_id(2) == 0)
    def _():
        h = x_ref[0] * (1.0 + sc_ref[0]) + sh_ref[0]
        hb = h.astype(BF16)
        h_scr[...] = hb
        zf = jnp.dot(hb, wf_ref[...], preferred_element_type=F32) + bf_ref[...]
        lt = zf.T[:8]
        lf_ref[0] = jnp.minimum(lt, 0.0) - jnp.log1p(jnp.exp(-jnp.abs(lt)))

    acc = jnp.dot(h_scr[...], w_ref[...], preferred_element_type=F32)
    z_ref[0] = (acc + b_ref[...]).astype(BF16)


def _inproj(x, mod_l, w, b, wf, bf, l):
    B, S, D = x.shape
    N = w.shape[2]
    tm, tn = min(TM_IN, S), min(TN_IN, N)
    return pl.pallas_call(
        _inproj_kernel,
        grid=(B, S // tm, N // tn),
        in_specs=[
            pl.BlockSpec((1, tm, D), lambda b, i, j: (b, i, 0)),
            pl.BlockSpec((1, 1, D), lambda b, i, j: (b, 0, 0)),
            pl.BlockSpec((1, 1, D), lambda b, i, j: (b, 0, 1)),
            pl.BlockSpec((None, D, tn), lambda b, i, j: (l, 0, j)),
            pl.BlockSpec((None, 1, tn), lambda b, i, j: (l, 0, j)),
            pl.BlockSpec((None, D, BF_PAD), lambda b, i, j: (l, 0, 0)),
            pl.BlockSpec((None, 1, BF_PAD), lambda b, i, j: (l, 0, 0)),
        ],
        out_specs=[
            pl.BlockSpec((1, tm, tn), lambda b, i, j: (b, i, j)),
            pl.BlockSpec((1, 8, tm), lambda b, i, j: (b, 0, i)),
        ],
        out_shape=[
            jax.ShapeDtypeStruct((B, S, N), BF16),
            jax.ShapeDtypeStruct((B, 8, S), F32),
        ],
        scratch_shapes=[pltpu.VMEM((tm, D), BF16)],
        compiler_params=_cparams(("parallel", "parallel", "arbitrary")),
        name="in_proj",
    )(x, mod_l, mod_l, w, b, wf, bf)


def _cumsum_kernel(lf_ref, f_ref):
    x = lf_ref[0]
    lane = lax.broadcasted_iota(jnp.int32, x.shape, 1)
    k = 1
    while k < x.shape[1]:
        x = x + jnp.where(lane >= k, pltpu.roll(x, k, 1), 0.0)
        k *= 2
    f_ref[0] = x


def _cumsum(lf):
    B, R, S = lf.shape
    return pl.pallas_call(
        _cumsum_kernel,
        grid=(B,),
        in_specs=[pl.BlockSpec((1, R, S), lambda b: (b, 0, 0))],
        out_specs=pl.BlockSpec((1, R, S), lambda b: (b, 0, 0)),
        out_shape=jax.ShapeDtypeStruct((B, R, S), F32),
        compiler_params=_cparams(("parallel",)),
        name="forget_cumsum",
    )(lf)


def _build_vt(v_ref, vt_scr, tk):
    ones_row = (lax.broadcasted_iota(jnp.int32, (VT_ROWS - HEAD_W, tk), 0) == 0).astype(BF16)
    for c in range(v_ref.shape[1] // tk):
        vb = v_ref[0, c * tk:(c + 1) * tk, :].astype(F32)
        vt_scr[c, 0:HEAD_W, :] = vb.T.astype(BF16)
        vt_scr[c, HEAD_W:VT_ROWS, :] = ones_row


def _softmax_step_t(s, vt, m_scr, acc_scr):
    m_prev = m_scr[...]
    m_new = jnp.maximum(m_prev, jnp.max(s, axis=0, keepdims=True))
    p = jnp.exp2(s - m_new).astype(BF16)
    alpha = jnp.exp2(m_prev - m_new)
    acc_scr[...] = alpha * acc_scr[...] + jnp.dot(vt, p, preferred_element_type=F32)
    m_scr[...] = m_new


def _sweep(qi, scores, mask_fn, vt_scr, m_scr, acc_scr, sa_scr, sb_scr):
    m_scr[...] = jnp.full(m_scr.shape, NEG, F32)
    acc_scr[...] = jnp.zeros(acc_scr.shape, F32)
    sa_scr[...] = scores(0)

    def pair(t, carry):
        j = 2 * t
        sb_scr[...] = scores(j + 1)
        _softmax_step_t(sa_scr[...], vt_scr[j], m_scr, acc_scr)
        sa_scr[...] = scores(j + 2)
        _softmax_step_t(sb_scr[...], vt_scr[j + 1], m_scr, acc_scr)
        return carry

    lax.fori_loop(0, qi // 2, pair, 0)

    @pl.when(qi % 2 == 0)
    def _():
        _softmax_step_t(mask_fn(sa_scr[...]), vt_scr[qi], m_scr, acc_scr)

    @pl.when(qi % 2 == 1)
    def _():
        sb_scr[...] = scores(qi)
        _softmax_step_t(sa_scr[...], vt_scr[qi - 1], m_scr, acc_scr)
        _softmax_step_t(mask_fn(sb_scr[...]), vt_scr[qi], m_scr, acc_scr)


def _sweep_wide(qi, scores, scores_late, mask_fn, mask_late_fn, vt_scr, m_scr, acc_scr,
                sa_scr, sb_scr, sl_scr):
    nl = sl_scr.shape[1]
    m_scr[...] = jnp.full(m_scr.shape, NEG, F32)
    acc_scr[...] = jnp.zeros(acc_scr.shape, F32)
    sa_scr[...] = scores(0)

    def pair(t, carry):
        j = 2 * t
        sb_scr[...] = scores(j + 1)
        _softmax_step_t(sa_scr[...], vt_scr[j], m_scr, acc_scr)
        sa_scr[...] = scores(j + 2)
        _softmax_step_t(sb_scr[...], vt_scr[j + 1], m_scr, acc_scr)
        return carry

    lax.fori_loop(0, qi, pair, 0)
    sl_scr[...] = scores_late(2 * qi + 1)
    _softmax_step_t(mask_fn(sa_scr[...]), vt_scr[2 * qi], m_scr, acc_scr)
    _softmax_step_t(mask_late_fn(sl_scr[...]), vt_scr[2 * qi + 1],
                    m_scr.at[:, nl:], acc_scr.at[:, nl:])


_NT = (((1,), (1,)), ((), ()))


def _attn_scratch(S, tk, nq):
    return [pltpu.VMEM((S // tk, VT_ROWS, tk), BF16),
            pltpu.VMEM((1, nq), F32),
            pltpu.VMEM((VT_ROWS, nq), F32),
            pltpu.VMEM((tk, nq), F32), pltpu.VMEM((tk, nq), F32)]


def _diff_attn_kernel(q_ref, k_ref, v_ref, lq1_ref, lk1_ref, lq2_ref, lk2_ref, li_ref, g_ref,
                      o_ref, vt_scr, m_scr, acc_scr, sa_scr, sb_scr, *, tq, chunk):
    qi = pl.program_id(2)

    @pl.when(qi == 0)
    def _():
        _build_vt(v_ref, vt_scr, tq)

    q = q_ref[0]
    lane = lax.broadcasted_iota(jnp.int32, q.shape, 1)
    zero = jnp.zeros_like(q)
    qs = jnp.concatenate([jnp.where(lane < A_DQK, q, zero),
                          jnp.where(lane >= A_DQK, q, zero)], axis=0)

    def scores(j):
        off = pl.multiple_of(j * tq, tq)
        return lax.dot_general(k_ref[0, pl.ds(off, tq), :], qs, _NT,
                               preferred_element_type=F32)

    def mask_fn(s):
        key = lax.broadcasted_iota(jnp.int32, s.shape, 0)
        col = lax.broadcasted_iota(jnp.int32, s.shape, 1)
        qry = jnp.where(col >= tq, col - tq, col)
        return jnp.where(key // chunk <= qry // chunk, s, NEG)

    _sweep(qi, scores, mask_fn, vt_scr, m_scr, acc_scr, sa_scr, sb_scr)

    acc = acc_scr[...]
    ot_all = acc[0:HEAD_W] / acc[HEAD_W:HEAD_W + 1]
    lam_init = li_ref[...]
    lam = (jnp.exp(jnp.sum(lq1_ref[...] * lk1_ref[...], axis=1, keepdims=True))
           - jnp.exp(jnp.sum(lq2_ref[...] * lk2_ref[...], axis=1, keepdims=True)) + lam_init)
    o = (ot_all[:, :tq] - lam * ot_all[:, tq:]).T
    o = o * lax.rsqrt(jnp.mean(o * o, axis=1, keepdims=True) + LN_EPS)
    o_ref[0] = (o * g_ref[...] * (1.0 - lam_init)).astype(BF16)


def _diff_attn(z, lq1, lk1, lq2, lk2, lam_init, g_diff, chunk):
    B, S, _ = z.shape
    tq = min(TQ_A, S)
    qb, kb, vb = Z_AQ // HEAD_W, Z_AK // HEAD_W, Z_AV // HEAD_W
    small = pl.BlockSpec((1, A_DQK), lambda b, h, i: (0, 0))
    return pl.pallas_call(
        functools.partial(_diff_attn_kernel, tq=tq, chunk=chunk),
        grid=(B, HEADS, S // tq),
        in_specs=[
            pl.BlockSpec((1, tq, HEAD_W), lambda b, h, i: (b, i, qb + h)),
            pl.BlockSpec((1, S, HEAD_W), lambda b, h, i: (b, 0, kb + h)),
            pl.BlockSpec((1, S, HEAD_W), lambda b, h, i: (b, 0, vb + h)),
            small, small, small, small,
            pl.BlockSpec((1, 1), lambda b, h, i: (0, 0)),
            pl.BlockSpec((1, HEAD_W), lambda b, h, i: (0, h)),
        ],
        out_specs=pl.BlockSpec((1, tq, HEAD_W), lambda b, h, i: (b, i, h)),
        out_shape=jax.ShapeDtypeStruct((B, S, HEADS * HEAD_W), BF16),
        scratch_shapes=_attn_scratch(S, tq, 2 * tq),
        compiler_params=_cparams(("parallel", "parallel", "arbitrary")),
        name="diff_attn",
    )(z, z, z, lq1, lk1, lq2, lk2, lam_init, g_diff)


def _fox_attn_kernel(q_ref, k_ref, v_ref, f_ref, o_ref, frep_scr, vt_scr, m_scr, acc_scr,
                     sa_scr, sb_scr, sl_scr, *, tk):
    qi = pl.program_id(2)
    S = k_ref.shape[1]
    tq = 2 * tk
    pc = min(PROLOGUE_CHUNK, S)

    @pl.when(qi == 0)
    def _():
        _build_vt(v_ref, vt_scr, tk)
        for c in range(S // pc):
            fb = jnp.broadcast_to(f_ref[0, :, c * pc:(c + 1) * pc] * LOG2E, (HEAD_W, pc))
            frep_scr[c * pc:(c + 1) * pc, :] = fb.T

    q = q_ref[0]
    q_late = q[tk:]
    q_off = pl.multiple_of(qi * tq, tq)
    c0 = frep_scr[pl.ds(q_off, 8), :][0:1, 0:1]

    def biased(j, qq):
        off = pl.multiple_of(j * tk, tk)
        fsub = frep_scr[pl.ds(off, tk), :] - c0
        s = lax.dot_general(k_ref[0, pl.ds(off, tk), :], qq, _NT, preferred_element_type=F32)
        return s - jnp.tile(fsub, (1, qq.shape[0] // HEAD_W))

    def mask_fn(s):
        key = lax.broadcasted_iota(jnp.int32, s.shape, 0)
        qry = lax.broadcasted_iota(jnp.int32, s.shape, 1)
        return jnp.where(key <= qry, s, NEG)

    _sweep_wide(qi, lambda j: biased(j, q), lambda j: biased(j, q_late), mask_fn, mask_fn,
                vt_scr, m_scr, acc_scr, sa_scr, sb_scr, sl_scr)

    acc = acc_scr[...]
    o_ref[0] = (acc[0:HEAD_W] / acc[HEAD_W:HEAD_W + 1]).T.astype(BF16)


def _fox_attn(z, f3):
    B, S, _ = z.shape
    tk = min(TK_B, S // 2)
    tq = 2 * tk
    rows = f3.shape[0] // B
    qb, kb, vb = Z_BQ // HEAD_W, Z_BK // HEAD_W, Z_BV // HEAD_W
    return pl.pallas_call(
        functools.partial(_fox_attn_kernel, tk=tk),
        grid=(B, HEADS, S // tq),
        in_specs=[
            pl.BlockSpec((1, tq, HEAD_W), lambda b, h, i: (b, i, qb + h)),
            pl.BlockSpec((1, S, HEAD_W), lambda b, h, i: (b, 0, kb + h)),
            pl.BlockSpec((1, S, HEAD_W), lambda b, h, i: (b, 0, vb + h)),
            pl.BlockSpec((1, 1, S), lambda b, h, i: (b * rows + h, 0, 0)),
        ],
        out_specs=pl.BlockSpec((1, tq, HEAD_W), lambda b, h, i: (b, i, h)),
        out_shape=jax.ShapeDtypeStruct((B, S, HEADS * HEAD_W), BF16),
        scratch_shapes=([pltpu.VMEM((S, HEAD_W), F32)] + _attn_scratch(S, tk, tq)
                        + [pltpu.VMEM((tk, tq // 2), F32)]),
        compiler_params=_cparams(("parallel", "parallel", "arbitrary")),
        name="fox_attn",
    )(z, z, z, f3)


def _retention_kernel(q_ref, k_ref, v_ref, cg_ref, cos_ref, sin_ref, g_ref, o_ref, r_scr, *, L):
    @pl.when(pl.program_id(1) == 0)
    def _():
        r_scr[...] = jnp.zeros(r_scr.shape, F32)

    cosf = cos_ref[...]
    sinf = sin_ref[...]
    ii = lax.broadcasted_iota(jnp.int32, (L, L), 0)
    jj = lax.broadcasted_iota(jnp.int32, (L, L), 1)
    dpos = jnp.maximum(ii - jj, 0).astype(F32)
    causal = ii >= jj
    pos = lax.broadcasted_iota(jnp.int32, (L, 1), 0).astype(F32)

    for h in range(HEADS):
        lg = math.log(1.0 - 2.0 ** (-5.0 - h))
        qh = q_ref[0, :, h * HEAD_W:(h + 1) * HEAD_W].astype(F32)
        kh = k_ref[0, :, h * HEAD_W:(h + 1) * HEAD_W].astype(F32)
        qr = qh * cosf + pltpu.roll(qh, HEAD_W // 2, 1) * sinf
        kr = kh * cosf + pltpu.roll(kh, HEAD_W // 2, 1) * sinf
        v = v_ref[0, :, h * C_DV:(h + 1) * C_DV]
        dmask = jnp.where(causal, jnp.exp(lg * dpos), 0.0)
        s = lax.dot_general(qr.astype(BF16), kr.astype(BF16), _NT, preferred_element_type=F32) * dmask
        inner = jnp.dot(s.astype(BF16), v, preferred_element_type=F32)
        r = r_scr[h]
        q_dec = qr * jnp.exp(lg * (pos + 1.0))
        cross = jnp.dot(q_dec.astype(BF16), r.astype(BF16), preferred_element_type=F32)
        k_dec = kr * jnp.exp(lg * (L - 1.0 - pos))
        r_scr[h] = math.exp(lg * L) * r + jnp.dot(k_dec.T.astype(BF16), v, preferred_element_type=F32)
        o = inner + cross
        mu = jnp.mean(o, axis=1, keepdims=True)
        var = jnp.mean(jnp.square(o - mu), axis=1, keepdims=True)
        y = (o - mu) * lax.rsqrt(var + LN_EPS) * g_ref[:, h * C_DV:(h + 1) * C_DV]
        cg = cg_ref[0, :, h * C_DV:(h + 1) * C_DV].astype(F32)
        o_ref[0, :, h * C_DV:(h + 1) * C_DV] = (y * (cg * jax.nn.sigmoid(cg))).astype(BF16)


def _retention(z, cos_t, sin_t, g_ret):
    B, S, _ = z.shape
    L = min(L_RET, S)
    qk_w, v_w = HEADS * HEAD_W, HEADS * C_DV
    return pl.pallas_call(
        functools.partial(_retention_kernel, L=L),
        grid=(B, S // L),
        in_specs=[
            pl.BlockSpec((1, L, qk_w), lambda b, n: (b, n, Z_CQ // qk_w)),
            pl.BlockSpec((1, L, qk_w), lambda b, n: (b, n, Z_CK // qk_w)),
            pl.BlockSpec((1, L, v_w), lambda b, n: (b, n, Z_CV // v_w)),
            pl.BlockSpec((1, L, v_w), lambda b, n: (b, n, Z_CG // v_w)),
            pl.BlockSpec((L, HEAD_W), lambda b, n: (n, 0)),
            pl.BlockSpec((L, HEAD_W), lambda b, n: (n, 0)),
            pl.BlockSpec((1, v_w), lambda b, n: (0, 0)),
        ],
        out_specs=pl.BlockSpec((1, L, v_w), lambda b, n: (b, n, 0)),
        out_shape=jax.ShapeDtypeStruct((B, S, v_w), BF16),
        scratch_shapes=[pltpu.VMEM((HEADS, HEAD_W, C_DV), F32)],
        compiler_params=_cparams(("parallel", "arbitrary")),
        name="retention",
    )(z, z, z, z, cos_t, sin_t, g_ret)


def _rope_tables(S):
    half = HEAD_W // 2
    inv = 1.0 / (ROPE_BASE ** np.linspace(0.0, 1.0, half))
    ang = np.arange(S, dtype=np.float64)[:, None] * inv[None, :]
    cos, sin = np.cos(ang), np.sin(ang)
    return (jnp.asarray(np.concatenate([cos, cos], axis=1), F32),
            jnp.asarray(np.concatenate([-sin, sin], axis=1), F32))


def _layernorm(r, g, b):
    mu = jnp.mean(r, axis=1, keepdims=True)
    var = jnp.mean(jnp.square(r - mu), axis=1, keepdims=True)
    return (r - mu) * lax.rsqrt(var + LN_EPS) * g + b


def _merge_kernel(ya_ref, yb_ref, yc_ref, ga_ref, gb_ref, gc_ref, x_ref, gt_ref, sh2_ref, sc2_ref,
                  wpa_ref, wpb_ref, wpc_ref, wout_ref, lng_ref, lnb_ref, xo_ref, ho_ref, *, alpha):
    def branch(y_ref, g_ref, w_ref):
        gate = jax.nn.sigmoid(g_ref[0].astype(F32))
        return gate * jnp.dot(y_ref[0], w_ref[...], preferred_element_type=F32)

    m = branch(ya_ref, ga_ref, wpa_ref) + branch(yb_ref, gb_ref, wpb_ref) + branch(yc_ref, gc_ref, wpc_ref)
    y = jnp.dot(m.astype(BF16), wout_ref[...], preferred_element_type=F32)
    xn = _layernorm(alpha * x_ref[0] + gt_ref[0] * y, lng_ref[...], lnb_ref[...])
    xo_ref[0] = xn
    ho_ref[0] = (xn * (1.0 + sc2_ref[0]) + sh2_ref[0]).astype(BF16)


def _const_spec(shape):
    return pl.BlockSpec(shape, lambda b, i: (0,) * len(shape), pipeline_mode=pl.Buffered(1))


def _layer_spec(stacked, l):
    shape = stacked.shape[1:]
    return pl.BlockSpec((None,) + shape, lambda b, i: (l,) + (0,) * len(shape),
                        pipeline_mode=pl.Buffered(1))


def _merge(ya, yb, yc, z, x, mod_l, wpa, wpb, wpc, wout, lng, lnb, alpha, l):
    B, S, D = x.shape
    tm = min(TM_MERGE, S)
    gblk = Z_GATES // D
    tok = lambda w, col: pl.BlockSpec((1, tm, w), lambda b, i: (b, i, col))
    modv = lambda col: pl.BlockSpec((1, 1, D), lambda b, i: (b, 0, col))
    return pl.pallas_call(
        functools.partial(_merge_kernel, alpha=alpha),
        grid=(B, S // tm),
        in_specs=[
            tok(ya.shape[2], 0), tok(yb.shape[2], 0), tok(yc.shape[2], 0),
            tok(D, gblk), tok(D, gblk + 1), tok(D, gblk + 2),
            tok(D, 0),
            modv(2), modv(3), modv(4),
            _layer_spec(wpa, l), _layer_spec(wpb, l), _layer_spec(wpc, l),
            _layer_spec(wout, l), _const_spec(lng.shape), _const_spec(lnb.shape),
        ],
        out_specs=[tok(D, 0), tok(D, 0)],
        out_shape=[jax.ShapeDtypeStruct((B, S, D), F32), jax.ShapeDtypeStruct((B, S, D), BF16)],
        compiler_params=_cparams(("parallel", "parallel")),
        name="merge_out_ln",
    )(ya, yb, yc, z, z, z, x, mod_l, mod_l, mod_l, wpa, wpb, wpc, wout, lng, lnb)


def _ffn_kernel(h_ref, x_ref, gt_ref, wup_ref, wconv_ref, bconv_ref, wdown_ref, lng_ref, lnb_ref,
                xo_ref, a_scr, tail_scr, *, alpha, cw):
    @pl.when(pl.program_id(1) == 0)
    def _():
        tail_scr[...] = jnp.zeros(tail_scr.shape, F32)

    h = h_ref[0]
    tm = h.shape[0]
    dff = a_scr.shape[1]
    row = lax.broadcasted_iota(jnp.int32, (tm, cw), 0)
    for ci in range(dff // cw):
        lo = ci * cw
        u = jnp.dot(h, wup_ref[:, lo:lo + cw], preferred_element_type=F32)
        g = jnp.dot(h, wup_ref[:, dff + lo:dff + lo + cw], preferred_element_type=F32)
        tail = tail_scr[:, lo:lo + cw]
        p1, p2 = tail[7:8], tail[6:7]
        um1 = jnp.where(row == 0, p1, pltpu.roll(u, 1, 0))
        um2 = jnp.where(row == 0, p2, jnp.where(row == 1, p1, pltpu.roll(u, 2, 0)))
        tail_scr[:, lo:lo + cw] = u[tm - 8:]
        conv = bconv_ref[:, lo:lo + cw] + um2 * wconv_ref[0:1, lo:lo + cw]
        conv = conv + um1 * wconv_ref[1:2, lo:lo + cw]
        conv = conv + u * wconv_ref[2:3, lo:lo + cw]
        act = 0.5 * conv * (1.0 + lax.erf(conv * (2.0 ** -0.5)))
        a_scr[:, lo:lo + cw] = (act * g).astype(BF16)
    y = jnp.dot(a_scr[...], wdown_ref[...], preferred_element_type=F32)
    xo_ref[0] = _layernorm(alpha * x_ref[0] + gt_ref[0] * y, lng_ref[...], lnb_ref[...])


def _ffn(h, x, mod_l, wup, wconv, bconv, wdown, lng, lnb, alpha, l):
    B, S, D = x.shape
    dff = wdown.shape[1]
    tm = min(TM_FFN, S)
    tok = pl.BlockSpec((1, tm, D), lambda b, i: (b, i, 0))
    return pl.pallas_call(
        functools.partial(_ffn_kernel, alpha=alpha, cw=CW_FFN),
        grid=(B, S // tm),
        in_specs=[
            tok, tok,
            pl.BlockSpec((1, 1, D), lambda b, i: (b, 0, 5)),
            _layer_spec(wup, l), _const_spec(wconv.shape), _const_spec(bconv.shape),
            _layer_spec(wdown, l), _const_spec(lng.shape), _const_spec(lnb.shape),
        ],
        out_specs=tok,
        out_shape=jax.ShapeDtypeStruct((B, S, D), F32),
        scratch_shapes=[pltpu.VMEM((tm, dff), BF16), pltpu.VMEM((8, dff), F32)],
        compiler_params=_cparams(("parallel", "arbitrary")),
        name="conv_ffn_ln",
    )(h, x, mod_l, wup, wconv, bconv, wdown, lng, lnb)


def _prep_in_proj(w, b):
    scale = np.ones((Z_WIDTH,), np.float32)
    scale[Z_AQ:Z_AQ + 512] = A_DQK ** -0.5 * LOG2E
    scale[Z_BQ:Z_BQ + 512] = HEAD_W ** -0.5 * LOG2E
    scale[Z_CK:Z_CK + 512] = HEAD_W ** -0.5
    wm = jnp.concatenate([w[..., :BF_OFF], w[..., BF_OFF + HEADS:]], axis=-1) * scale
    bm = jnp.concatenate([b[..., :BF_OFF], b[..., BF_OFF + HEADS:]], axis=-1) * scale
    pad = BF_PAD - HEADS
    wf = jnp.pad(w[..., BF_OFF:BF_OFF + HEADS], ((0, 0), (0, 0), (0, pad)))
    bf = jnp.pad(b[..., BF_OFF:BF_OFF + HEADS], ((0, 0), (0, pad)))
    return wm.astype(BF16), bm[:, None, :], wf.astype(BF16), bf[:, None, :]


def kernel(x, c, w_ada, b_ada, w_in, b_in, lam_q1, lam_k1, lam_q2, lam_k2, g_diff, g_ret, w_pa, w_pb, w_pc, w_out, ln_g, ln_b, w_up, w_conv, b_conv, w_down):
    B, S, D = x.shape
    depth = w_ada.shape[0]
    alpha = (2 * depth) ** 0.25
    chunk = 64
    mod = _ada(c, w_ada, b_ada)
    cos_t, sin_t = _rope_tables(S)
    wm, bm, wf, bf = _prep_in_proj(w_in, b_in)
    wpa, wpb, wpc, wout = (w.astype(BF16) for w in (w_pa, w_pb, w_pc, w_out))
    wup, wdown = w_up.astype(BF16), w_down.astype(BF16)
    for l in range(depth):
        mod_l = mod[l].reshape(B, 1, 6 * D)
        lam_init = jnp.full((1, 1), 0.8 - 0.6 * math.exp(-0.3 * l), F32)
        z, lf = _inproj(x, mod_l, wm, bm, wf, bf, l)
        f = _cumsum(lf)
        ya = _diff_attn(z, lam_q1[l][None], lam_k1[l][None], lam_q2[l][None], lam_k2[l][None],
                        lam_init, g_diff[l][None], chunk)
        yb = _fox_attn(z, f.reshape(B * f.shape[1], 1, S))
        yc = _retention(z, cos_t, sin_t, g_ret[l][None])
        x, h2 = _merge(ya, yb, yc, z, x, mod_l, wpa, wpb, wpc, wout,
                       ln_g[l, 0][None], ln_b[l, 0][None], alpha, l)
        x = _ffn(h2, x, mod_l, wup, w_conv[l], b_conv[l][None], wdown,
                 ln_g[l, 1][None], ln_b[l, 1][None], alpha, l)
    return x
```

```python
import functools
import math

import numpy as np
import jax
import jax.numpy as jnp
from jax import lax
from jax.experimental import pallas as pl
from jax.experimental.pallas import tpu as pltpu

F32 = jnp.float32
BF16 = jnp.bfloat16

HEADS = 4
A_DQK = 64
HEAD_W = 128
C_DV = 256
ROPE_BASE = 10000.0
LN_EPS = 1e-5
CONV_W = 3

Z_AQ, Z_AK, Z_AV = 0, 512, 1024
Z_BQ, Z_BK, Z_BV = 1536, 2048, 2560
Z_CQ, Z_CK, Z_CV, Z_CG = 3072, 3584, 4096, 5120
Z_GATES = 6144
Z_WIDTH = 9216
BF_OFF = 3072
BF_PAD = 128

NEG = -1e30
LOG2E = 1.4426950408889634
VT_ROWS = 144

TN_ADA = 1024
TM_IN, TN_IN = 1024, 1024
TQ_A = 512
TK_B = 512
PROLOGUE_CHUNK = 512
SKIP_MARGIN = 160.0
L_RET = 256
TM_MERGE = 512
TM_FFN = 512
CW_FFN = 256

VMEM_LIMIT = 56 * 1024 * 1024


def _cparams(sem):
    return pltpu.CompilerParams(dimension_semantics=sem, vmem_limit_bytes=VMEM_LIMIT)


def _ada_kernel(ct_ref, w_ref, b_ref, o_ref):
    w = w_ref[0]
    rows = []
    for b in range(ct_ref.shape[0]):
        cb = ct_ref[b]
        ca = cb * jax.nn.sigmoid(cb)
        rows.append(jnp.sum(ca * w, axis=0, keepdims=True))
    o_ref[0] = jnp.concatenate(rows, axis=0) + b_ref[0]


def _ada(c, w_ada, b_ada):
    B, D = c.shape
    L, _, N = w_ada.shape
    tn = min(TN_ADA, N)
    return pl.pallas_call(
        _ada_kernel,
        grid=(L, N // tn),
        in_specs=[
            pl.BlockSpec((B, D, 1), lambda l, j: (0, 0, 0)),
            pl.BlockSpec((1, D, tn), lambda l, j: (l, 0, j)),
            pl.BlockSpec((1, 1, tn), lambda l, j: (l, 0, j)),
        ],
        out_specs=pl.BlockSpec((1, B, tn), lambda l, j: (l, 0, j)),
        out_shape=jax.ShapeDtypeStruct((L, B, N), F32),
        compiler_params=_cparams(("parallel", "parallel")),
        name="ada_mod",
    )(c.reshape(B, D, 1), w_ada, b_ada.reshape(L, 1, N))


def _inproj_kernel(x_ref, sh_ref, sc_ref, w_ref, b_ref, wf_ref, bf_ref, z_ref, lf_ref, h_scr):
    @pl.when(pl.program_id(2) == 0)
    def _():
        h = x_ref[0] * (1.0 + sc_ref[0]) + sh_ref[0]
        hb = h.astype(BF16)
        h_scr[...] = hb
        zf = jnp.dot(hb, wf_ref[...], preferred_element_type=F32) + bf_ref[...]
        lt = zf.T[:8]
        lf_ref[0] = jnp.minimum(lt, 0.0) - jnp.log1p(jnp.exp(-jnp.abs(lt)))

    acc = jnp.dot(h_scr[...], w_ref[...], preferred_element_type=F32)
    z_ref[0] = (acc + b_ref[...]).astype(BF16)


def _inproj(x, mod_l, w, b, wf, bf, l):
    B, S, D = x.shape
    N = w.shape[2]
    tm, tn = min(TM_IN, S), min(TN_IN, N)
    return pl.pallas_call(
        _inproj_kernel,
        grid=(B, S // tm, N // tn),
        in_specs=[
            pl.BlockSpec((1, tm, D), lambda b, i, j: (b, i, 0)),
            pl.BlockSpec((1, 1, D), lambda b, i, j: (b, 0, 0)),
            pl.BlockSpec((1, 1, D), lambda b, i, j: (b, 0, 1)),
            pl.BlockSpec((None, D, tn), lambda b, i, j: (l, 0, j)),
            pl.BlockSpec((None, 1, tn), lambda b, i, j: (l, 0, j)),
            pl.BlockSpec((None, D, BF_PAD), lambda b, i, j: (l, 0, 0)),
            pl.BlockSpec((None, 1, BF_PAD), lambda b, i, j: (l, 0, 0)),
        ],
        out_specs=[
            pl.BlockSpec((1, tm, tn), lambda b, i, j: (b, i, j)),
            pl.BlockSpec((1, 8, tm), lambda b, i, j: (b, 0, i)),
        ],
        out_shape=[
            jax.ShapeDtypeStruct((B, S, N), BF16),
            jax.ShapeDtypeStruct((B, 8, S), F32),
        ],
        scratch_shapes=[pltpu.VMEM((tm, D), BF16)],
        compiler_params=_cparams(("parallel", "parallel", "arbitrary")),
        name="in_proj",
    )(x, mod_l, mod_l, w, b, wf, bf)


def _cumsum_kernel(lf_ref, f_ref):
    x = lf_ref[0]
    lane = lax.broadcasted_iota(jnp.int32, x.shape, 1)
    k = 1
    while k < x.shape[1]:
        x = x + jnp.where(lane >= k, pltpu.roll(x, k, 1), 0.0)
        k *= 2
    f_ref[0] = x


def _cumsum(lf):
    B, R, S = lf.shape
    return pl.pallas_call(
        _cumsum_kernel,
        grid=(B,),
        in_specs=[pl.BlockSpec((1, R, S), lambda b: (b, 0, 0))],
        out_specs=pl.BlockSpec((1, R, S), lambda b: (b, 0, 0)),
        out_shape=jax.ShapeDtypeStruct((B, R, S), F32),
        compiler_params=_cparams(("parallel",)),
        name="forget_cumsum",
    )(lf)


def _build_vt(v_ref, vt_scr, tk):
    ones_row = (lax.broadcasted_iota(jnp.int32, (VT_ROWS - HEAD_W, tk), 0) == 0).astype(BF16)
    for c in range(v_ref.shape[1] // tk):
        vb = v_ref[0, c * tk:(c + 1) * tk, :].astype(F32)
        vt_scr[c, 0:HEAD_W, :] = vb.T.astype(BF16)
        vt_scr[c, HEAD_W:VT_ROWS, :] = ones_row


def _softmax_step_t(s, vt, m_scr, acc_scr):
    m_prev = m_scr[...]
    m_new = jnp.maximum(m_prev, jnp.max(s, axis=0, keepdims=True))
    p = jnp.exp2(s - m_new).astype(BF16)
    alpha = jnp.exp2(m_prev - m_new)
    acc_scr[...] = alpha * acc_scr[...] + jnp.dot(vt, p, preferred_element_type=F32)
    m_scr[...] = m_new


def _sweep(qi, scores, mask_fn, vt_scr, m_scr, acc_scr, sa_scr, sb_scr):
    m_scr[...] = jnp.full(m_scr.shape, NEG, F32)
    acc_scr[...] = jnp.zeros(acc_scr.shape, F32)
    sa_scr[...] = scores(0)

    def pair(t, carry):
        j = 2 * t
        sb_scr[...] = scores(j + 1)
        _softmax_step_t(sa_scr[...], vt_scr[j], m_scr, acc_scr)
        sa_scr[...] = scores(j + 2)
        _softmax_step_t(sb_scr[...], vt_scr[j + 1], m_scr, acc_scr)
        return carry

    lax.fori_loop(0, qi // 2, pair, 0)

    @pl.when(qi % 2 == 0)
    def _():
        _softmax_step_t(mask_fn(sa_scr[...]), vt_scr[qi], m_scr, acc_scr)

    @pl.when(qi % 2 == 1)
    def _():
        sb_scr[...] = scores(qi)
        _softmax_step_t(sa_scr[...], vt_scr[qi - 1], m_scr, acc_scr)
        _softmax_step_t(mask_fn(sb_scr[...]), vt_scr[qi], m_scr, acc_scr)


def _sweep_wide(qi, t0, scores, scores_late, mask_fn, mask_late_fn, vt_scr, m_scr, acc_scr,
                sa_scr, sb_scr, sl_scr):
    nl = sl_scr.shape[1]
    m_scr[...] = jnp.full(m_scr.shape, NEG, F32)
    acc_scr[...] = jnp.zeros(acc_scr.shape, F32)
    sa_scr[...] = scores(2 * t0)

    def pair(t, carry):
        j = 2 * t
        sb_scr[...] = scores(j + 1)
        _softmax_step_t(sa_scr[...], vt_scr[j], m_scr, acc_scr)
        sa_scr[...] = scores(j + 2)
        _softmax_step_t(sb_scr[...], vt_scr[j + 1], m_scr, acc_scr)
        return carry

    lax.fori_loop(t0, qi, pair, 0)
    sl_scr[...] = scores_late(2 * qi + 1)
    _softmax_step_t(mask_fn(sa_scr[...]), vt_scr[2 * qi], m_scr, acc_scr)
    _softmax_step_t(mask_late_fn(sl_scr[...]), vt_scr[2 * qi + 1],
                    m_scr.at[:, nl:], acc_scr.at[:, nl:])


_NT = (((1,), (1,)), ((), ()))


def _attn_scratch(S, tk, nq):
    return [pltpu.VMEM((S // tk, VT_ROWS, tk), BF16),
            pltpu.VMEM((1, nq), F32),
            pltpu.VMEM((VT_ROWS, nq), F32),
            pltpu.VMEM((tk, nq), F32), pltpu.VMEM((tk, nq), F32)]


def _diff_attn_kernel(q_ref, k_ref, v_ref, lq1_ref, lk1_ref, lq2_ref, lk2_ref, li_ref, g_ref,
                      o_ref, vt_scr, m_scr, acc_scr, sa_scr, sb_scr, *, tq, chunk):
    qi = pl.program_id(2)

    @pl.when(qi == 0)
    def _():
        _build_vt(v_ref, vt_scr, tq)

    q = q_ref[0]
    lane = lax.broadcasted_iota(jnp.int32, q.shape, 1)
    zero = jnp.zeros_like(q)
    qs = jnp.concatenate([jnp.where(lane < A_DQK, q, zero),
                          jnp.where(lane >= A_DQK, q, zero)], axis=0)

    def scores(j):
        off = pl.multiple_of(j * tq, tq)
        return lax.dot_general(k_ref[0, pl.ds(off, tq), :], qs, _NT,
                               preferred_element_type=F32)

    def mask_fn(s):
        key = lax.broadcasted_iota(jnp.int32, s.shape, 0)
        col = lax.broadcasted_iota(jnp.int32, s.shape, 1)
        qry = jnp.where(col >= tq, col - tq, col)
        return jnp.where(key // chunk <= qry // chunk, s, NEG)

    _sweep(qi, scores, mask_fn, vt_scr, m_scr, acc_scr, sa_scr, sb_scr)

    acc = acc_scr[...]
    ot_all = acc[0:HEAD_W] / acc[HEAD_W:HEAD_W + 1]
    lam_init = li_ref[...]
    lam = (jnp.exp(jnp.sum(lq1_ref[...] * lk1_ref[...], axis=1, keepdims=True))
           - jnp.exp(jnp.sum(lq2_ref[...] * lk2_ref[...], axis=1, keepdims=True)) + lam_init)
    o = (ot_all[:, :tq] - lam * ot_all[:, tq:]).T
    o = o * lax.rsqrt(jnp.mean(o * o, axis=1, keepdims=True) + LN_EPS)
    o_ref[0] = (o * g_ref[...] * (1.0 - lam_init)).astype(BF16)


def _diff_attn(z, lq1, lk1, lq2, lk2, lam_init, g_diff, chunk):
    B, S, _ = z.shape
    tq = min(TQ_A, S)
    qb, kb, vb = Z_AQ // HEAD_W, Z_AK // HEAD_W, Z_AV // HEAD_W
    small = pl.BlockSpec((1, A_DQK), lambda b, h, i: (0, 0))
    return pl.pallas_call(
        functools.partial(_diff_attn_kernel, tq=tq, chunk=chunk),
        grid=(B, HEADS, S // tq),
        in_specs=[
            pl.BlockSpec((1, tq, HEAD_W), lambda b, h, i: (b, i, qb + h)),
            pl.BlockSpec((1, S, HEAD_W), lambda b, h, i: (b, 0, kb + h)),
            pl.BlockSpec((1, S, HEAD_W), lambda b, h, i: (b, 0, vb + h)),
            small, small, small, small,
            pl.BlockSpec((1, 1), lambda b, h, i: (0, 0)),
            pl.BlockSpec((1, HEAD_W), lambda b, h, i: (0, h)),
        ],
        out_specs=pl.BlockSpec((1, tq, HEAD_W), lambda b, h, i: (b, i, h)),
        out_shape=jax.ShapeDtypeStruct((B, S, HEADS * HEAD_W), BF16),
        scratch_shapes=_attn_scratch(S, tq, 2 * tq),
        compiler_params=_cparams(("parallel", "parallel", "arbitrary")),
        name="diff_attn",
    )(z, z, z, lq1, lk1, lq2, lk2, lam_init, g_diff)


def _row_norm_max(x):
    xf = x.astype(F32)
    return jnp.sqrt(jnp.max(jnp.sum(xf * xf, axis=1, keepdims=True)))


def _fox_attn_kernel(q_ref, k_ref, v_ref, f_ref, o_ref, frep_scr, vt_scr, m_scr, acc_scr,
                     sa_scr, sb_scr, sl_scr, fmin_s, fmax_s, c0_s, kn_s, *, tk):
    qi = pl.program_id(2)
    S = k_ref.shape[1]
    tq = 2 * tk
    n_blk = S // tq
    pc = min(PROLOGUE_CHUNK, S)

    @pl.when(qi == 0)
    def _():
        _build_vt(v_ref, vt_scr, tk)
        kn = jnp.float32(0.0)
        for c in range(S // pc):
            fb = jnp.broadcast_to(f_ref[0, :, c * pc:(c + 1) * pc] * LOG2E, (HEAD_W, pc))
            frep_scr[c * pc:(c + 1) * pc, :] = fb.T
            kn = jnp.maximum(kn, _row_norm_max(k_ref[0, c * pc:(c + 1) * pc, :]))
        kn_s[0] = kn
        for t in range(n_blk):
            fb = f_ref[0, :, t * tq:(t + 1) * tq] * LOG2E
            fmin_s[t] = jnp.min(fb)
            fmax_s[t] = jnp.max(fb)
            c0_s[t] = jnp.max(fb[:, 0:1])

    q = q_ref[0]
    q_late = q[tk:]
    q_off = pl.multiple_of(qi * tq, tq)
    c0 = frep_scr[pl.ds(q_off, 8), :][0:1, 0:1]

    qk = _row_norm_max(q) * kn_s[0] * 1.01 + 1.0
    thr = 2.0 * qk + (fmax_s[qi] - c0_s[qi]) + SKIP_MARGIN
    t0 = jnp.int32(0)
    for t in range(n_blk - 1):
        skip = (t < qi) & (t0 == t) & (fmin_s[t] - c0_s[qi] >= thr)
        t0 = t0 + skip.astype(jnp.int32)

    def biased(j, qq):
        off = pl.multiple_of(j * tk, tk)
        fsub = frep_scr[pl.ds(off, tk), :] - c0
        s = lax.dot_general(k_ref[0, pl.ds(off, tk), :], qq, _NT, preferred_element_type=F32)
        return s - jnp.tile(fsub, (1, qq.shape[0] // HEAD_W))

    def mask_fn(s):
        key = lax.broadcasted_iota(jnp.int32, s.shape, 0)
        qry = lax.broadcasted_iota(jnp.int32, s.shape, 1)
        return jnp.where(key <= qry, s, NEG)

    _sweep_wide(qi, t0, lambda j: biased(j, q), lambda j: biased(j, q_late), mask_fn, mask_fn,
                vt_scr, m_scr, acc_scr, sa_scr, sb_scr, sl_scr)

    acc = acc_scr[...]
    o_ref[0] = (acc[0:HEAD_W] / acc[HEAD_W:HEAD_W + 1]).T.astype(BF16)


def _fox_attn(z, f3):
    B, S, _ = z.shape
    tk = min(TK_B, S // 2)
    tq = 2 * tk
    rows = f3.shape[0] // B
    qb, kb, vb = Z_BQ // HEAD_W, Z_BK // HEAD_W, Z_BV // HEAD_W
    return pl.pallas_call(
        functools.partial(_fox_attn_kernel, tk=tk),
        grid=(B, HEADS, S // tq),
        in_specs=[
            pl.BlockSpec((1, tq, HEAD_W), lambda b, h, i: (b, i, qb + h)),
            pl.BlockSpec((1, S, HEAD_W), lambda b, h, i: (b, 0, kb + h)),
            pl.BlockSpec((1, S, HEAD_W), lambda b, h, i: (b, 0, vb + h)),
            pl.BlockSpec((1, 1, S), lambda b, h, i: (b * rows + h, 0, 0)),
        ],
        out_specs=pl.BlockSpec((1, tq, HEAD_W), lambda b, h, i: (b, i, h)),
        out_shape=jax.ShapeDtypeStruct((B, S, HEADS * HEAD_W), BF16),
        scratch_shapes=([pltpu.VMEM((S, HEAD_W), F32)] + _attn_scratch(S, tk, tq)
                        + [pltpu.VMEM((tk, tq // 2), F32)]
                        + [pltpu.SMEM((S // tq,), F32)] * 3 + [pltpu.SMEM((1,), F32)]),
        compiler_params=_cparams(("parallel", "parallel", "arbitrary")),
        name="fox_attn",
    )(z, z, z, f3)


def _retention_kernel(q_ref, k_ref, v_ref, cg_ref, cos_ref, sin_ref, g_ref, o_ref, r_scr, *, L):
    @pl.when(pl.program_id(1) == 0)
    def _():
        r_scr[...] = jnp.zeros(r_scr.shape, F32)

    cosf = cos_ref[...]
    sinf = sin_ref[...]
    ii = lax.broadcasted_iota(jnp.int32, (L, L), 0)
    jj = lax.broadcasted_iota(jnp.int32, (L, L), 1)
    dpos = jnp.maximum(ii - jj, 0).astype(F32)
    causal = ii >= jj
    pos = lax.broadcasted_iota(jnp.int32, (L, 1), 0).astype(F32)

    for h in range(HEADS):
        lg = math.log(1.0 - 2.0 ** (-5.0 - h))
        qh = q_ref[0, :, h * HEAD_W:(h + 1) * HEAD_W].astype(F32)
        kh = k_ref[0, :, h * HEAD_W:(h + 1) * HEAD_W].astype(F32)
        qr = qh * cosf + pltpu.roll(qh, HEAD_W // 2, 1) * sinf
        kr = kh * cosf + pltpu.roll(kh, HEAD_W // 2, 1) * sinf
        v = v_ref[0, :, h * C_DV:(h + 1) * C_DV]
        dmask = jnp.where(causal, jnp.exp(lg * dpos), 0.0)
        s = lax.dot_general(qr.astype(BF16), kr.astype(BF16), _NT, preferred_element_type=F32) * dmask
        inner = jnp.dot(s.astype(BF16), v, preferred_element_type=F32)
        r = r_scr[h]
        q_dec = qr * jnp.exp(lg * (pos + 1.0))
        cross = jnp.dot(q_dec.astype(BF16), r.astype(BF16), preferred_element_type=F32)
        k_dec = kr * jnp.exp(lg * (L - 1.0 - pos))
        r_scr[h] = math.exp(lg * L) * r + jnp.dot(k_dec.T.astype(BF16), v, preferred_element_type=F32)
        o = inner + cross
        mu = jnp.mean(o, axis=1, keepdims=True)
        var = jnp.mean(jnp.square(o - mu), axis=1, keepdims=True)
        y = (o - mu) * lax.rsqrt(var + LN_EPS) * g_ref[:, h * C_DV:(h + 1) * C_DV]
        cg = cg_ref[0, :, h * C_DV:(h + 1) * C_DV].astype(F32)
        o_ref[0, :, h * C_DV:(h + 1) * C_DV] = (y * (cg * jax.nn.sigmoid(cg))).astype(BF16)


def _retention(z, cos_t, sin_t, g_ret):
    B, S, _ = z.shape
    L = min(L_RET, S)
    qk_w, v_w = HEADS * HEAD_W, HEADS * C_DV
    return pl.pallas_call(
        functools.partial(_retention_kernel, L=L),
        grid=(B, S // L),
        in_specs=[
            pl.BlockSpec((1, L, qk_w), lambda b, n: (b, n, Z_CQ // qk_w)),
            pl.BlockSpec((1, L, qk_w), lambda b, n: (b, n, Z_CK // qk_w)),
            pl.BlockSpec((1, L, v_w), lambda b, n: (b, n, Z_CV // v_w)),
            pl.BlockSpec((1, L, v_w), lambda b, n: (b, n, Z_CG // v_w)),
            pl.BlockSpec((L, HEAD_W), lambda b, n: (n, 0)),
            pl.BlockSpec((L, HEAD_W), lambda b, n: (n, 0)),
            pl.BlockSpec((1, v_w), lambda b, n: (0, 0)),
        ],
        out_specs=pl.BlockSpec((1, L, v_w), lambda b, n: (b, n, 0)),
        out_shape=jax.ShapeDtypeStruct((B, S, v_w), BF16),
        scratch_shapes=[pltpu.VMEM((HEADS, HEAD_W, C_DV), F32)],
        compiler_params=_cparams(("parallel", "arbitrary")),
        name="retention",
    )(z, z, z, z, cos_t, sin_t, g_ret)


def _rope_tables(S):
    half = HEAD_W // 2
    inv = 1.0 / (ROPE_BASE ** np.linspace(0.0, 1.0, half))
    ang = np.arange(S, dtype=np.float64)[:, None] * inv[None, :]
    cos, sin = np.cos(ang), np.sin(ang)
    return (jnp.asarray(np.concatenate([cos, cos], axis=1), F32),
            jnp.asarray(np.concatenate([-sin, sin], axis=1), F32))


def _layernorm(r, g, b):
    mu = jnp.mean(r, axis=1, keepdims=True)
    var = jnp.mean(jnp.square(r - mu), axis=1, keepdims=True)
    return (r - mu) * lax.rsqrt(var + LN_EPS) * g + b


def _merge_kernel(ya_ref, yb_ref, yc_ref, ga_ref, gb_ref, gc_ref, x_ref, gt_ref, sh2_ref, sc2_ref,
                  wpa_ref, wpb_ref, wpc_ref, wout_ref, lng_ref, lnb_ref, xo_ref, ho_ref, *, alpha):
    def branch(y_ref, g_ref, w_ref):
        gate = jax.nn.sigmoid(g_ref[0].astype(F32))
        return gate * jnp.dot(y_ref[0], w_ref[...], preferred_element_type=F32)

    m = branch(ya_ref, ga_ref, wpa_ref) + branch(yb_ref, gb_ref, wpb_ref) + branch(yc_ref, gc_ref, wpc_ref)
    y = jnp.dot(m.astype(BF16), wout_ref[...], preferred_element_type=F32)
    xn = _layernorm(alpha * x_ref[0] + gt_ref[0] * y, lng_ref[...], lnb_ref[...])
    xo_ref[0] = xn
    ho_ref[0] = (xn * (1.0 + sc2_ref[0]) + sh2_ref[0]).astype(BF16)


def _const_spec(shape):
    return pl.BlockSpec(shape, lambda b, i: (0,) * len(shape), pipeline_mode=pl.Buffered(1))


def _layer_spec(stacked, l):
    shape = stacked.shape[1:]
    return pl.BlockSpec((None,) + shape, lambda b, i: (l,) + (0,) * len(shape),
                        pipeline_mode=pl.Buffered(1))


def _merge(ya, yb, yc, z, x, mod_l, wpa, wpb, wpc, wout, lng, lnb, alpha, l):
    B, S, D = x.shape
    tm = min(TM_MERGE, S)
    gblk = Z_GATES // D
    tok = lambda w, col: pl.BlockSpec((1, tm, w), lambda b, i: (b, i, col))
    modv = lambda col: pl.BlockSpec((1, 1, D), lambda b, i: (b, 0, col))
    return pl.pallas_call(
        functools.partial(_merge_kernel, alpha=alpha),
        grid=(B, S // tm),
        in_specs=[
            tok(ya.shape[2], 0), tok(yb.shape[2], 0), tok(yc.shape[2], 0),
            tok(D, gblk), tok(D, gblk + 1), tok(D, gblk + 2),
            tok(D, 0),
            modv(2), modv(3), modv(4),
            _layer_spec(wpa, l), _layer_spec(wpb, l), _layer_spec(wpc, l),
            _layer_spec(wout, l), _const_spec(lng.shape), _const_spec(lnb.shape),
        ],
        out_specs=[tok(D, 0), tok(D, 0)],
        out_shape=[jax.ShapeDtypeStruct((B, S, D), F32), jax.ShapeDtypeStruct((B, S, D), BF16)],
        compiler_params=_cparams(("parallel", "parallel")),
        name="merge_out_ln",
    )(ya, yb, yc, z, z, z, x, mod_l, mod_l, mod_l, wpa, wpb, wpc, wout, lng, lnb)


def _ffn_kernel(h_ref, x_ref, gt_ref, wup_ref, wconv_ref, bconv_ref, wdown_ref, lng_ref, lnb_ref,
                xo_ref, a_scr, tail_scr, *, alpha, cw):
    @pl.when(pl.program_id(1) == 0)
    def _():
        tail_scr[...] = jnp.zeros(tail_scr.shape, F32)

    h = h_ref[0]
    tm = h.shape[0]
    dff = a_scr.shape[1]
    row = lax.broadcasted_iota(jnp.int32, (tm, cw), 0)
    for ci in range(dff // cw):
        lo = ci * cw
        u = jnp.dot(h, wup_ref[:, lo:lo + cw], preferred_element_type=F32)
        g = jnp.dot(h, wup_ref[:, dff + lo:dff + lo + cw], preferred_element_type=F32)
        tail = tail_scr[:, lo:lo + cw]
        p1, p2 = tail[7:8], tail[6:7]
        um1 = jnp.where(row == 0, p1, pltpu.roll(u, 1, 0))
        um2 = jnp.where(row == 0, p2, jnp.where(row == 1, p1, pltpu.roll(u, 2, 0)))
        tail_scr[:, lo:lo + cw] = u[tm - 8:]
        conv = bconv_ref[:, lo:lo + cw] + um2 * wconv_ref[0:1, lo:lo + cw]
        conv = conv + um1 * wconv_ref[1:2, lo:lo + cw]
        conv = conv + u * wconv_ref[2:3, lo:lo + cw]
        act = 0.5 * conv * (1.0 + lax.erf(conv * (2.0 ** -0.5)))
        a_scr[:, lo:lo + cw] = (act * g).astype(BF16)
    y = jnp.dot(a_scr[...], wdown_ref[...], preferred_element_type=F32)
    xo_ref[0] = _layernorm(alpha * x_ref[0] + gt_ref[0] * y, lng_ref[...], lnb_ref[...])


def _ffn(h, x, mod_l, wup, wconv, bconv, wdown, lng, lnb, alpha, l):
    B, S, D = x.shape
    dff = wdown.shape[1]
    tm = min(TM_FFN, S)
    tok = pl.BlockSpec((1, tm, D), lambda b, i: (b, i, 0))
    return pl.pallas_call(
        functools.partial(_ffn_kernel, alpha=alpha, cw=CW_FFN),
        grid=(B, S // tm),
        in_specs=[
            tok, tok,
            pl.BlockSpec((1, 1, D), lambda b, i: (b, 0, 5)),
            _layer_spec(wup, l), _const_spec(wconv.shape), _const_spec(bconv.shape),
            _layer_spec(wdown, l), _const_spec(lng.shape), _const_spec(lnb.shape),
        ],
        out_specs=tok,
        out_shape=jax.ShapeDtypeStruct((B, S, D), F32),
        scratch_shapes=[pltpu.VMEM((tm, dff), BF16), pltpu.VMEM((8, dff), F32)],
        compiler_params=_cparams(("parallel", "arbitrary")),
        name="conv_ffn_ln",
    )(h, x, mod_l, wup, wconv, bconv, wdown, lng, lnb)


def _prep_in_proj(w, b):
    scale = np.ones((Z_WIDTH,), np.float32)
    scale[Z_AQ:Z_AQ + 512] = A_DQK ** -0.5 * LOG2E
    scale[Z_BQ:Z_BQ + 512] = HEAD_W ** -0.5 * LOG2E
    scale[Z_CK:Z_CK + 512] = HEAD_W ** -0.5
    wm = jnp.concatenate([w[..., :BF_OFF], w[..., BF_OFF + HEADS:]], axis=-1) * scale
    bm = jnp.concatenate([b[..., :BF_OFF], b[..., BF_OFF + HEADS:]], axis=-1) * scale
    pad = BF_PAD - HEADS
    wf = jnp.pad(w[..., BF_OFF:BF_OFF + HEADS], ((0, 0), (0, 0), (0, pad)))
    bf = jnp.pad(b[..., BF_OFF:BF_OFF + HEADS], ((0, 0), (0, pad)))
    return wm.astype(BF16), bm[:, None, :], wf.astype(BF16), bf[:, None, :]


def kernel(x, c, w_ada, b_ada, w_in, b_in, lam_q1, lam_k1, lam_q2, lam_k2, g_diff, g_ret, w_pa, w_pb, w_pc, w_out, ln_g, ln_b, w_up, w_conv, b_conv, w_down):
    B, S, D = x.shape
    depth = w_ada.shape[0]
    alpha = (2 * depth) ** 0.25
    chunk = 64
    mod = _ada(c, w_ada, b_ada)
    cos_t, sin_t = _rope_tables(S)
    wm, bm, wf, bf = _prep_in_proj(w_in, b_in)
    wpa, wpb, wpc, wout = (w.astype(BF16) for w in (w_pa, w_pb, w_pc, w_out))
    wup, wdown = w_up.astype(BF16), w_down.astype(BF16)
    for l in range(depth):
        mod_l = mod[l].reshape(B, 1, 6 * D)
        lam_init = jnp.full((1, 1), 0.8 - 0.6 * math.exp(-0.3 * l), F32)
        z, lf = _inproj(x, mod_l, wm, bm, wf, bf, l)
        f = _cumsum(lf)
        ya = _diff_attn(z, lam_q1[l][None], lam_k1[l][None], lam_q2[l][None], lam_k2[l][None],
                        lam_init, g_diff[l][None], chunk)
        yb = _fox_attn(z, f.reshape(B * f.shape[1], 1, S))
        yc = _retention(z, cos_t, sin_t, g_ret[l][None])
        x, h2 = _merge(ya, yb, yc, z, x, mod_l, wpa, wpb, wpc, wout,
                       ln_g[l, 0][None], ln_b[l, 0][None], alpha, l)
        x = _ffn(h2, x, mod_l, wup, w_conv[l], b_conv[l][None], wdown,
                 ln_g[l, 1][None], ln_b[l, 1][None], alpha, l)
    return x
```

```python
import functools
import math

import numpy as np
import jax
import jax.numpy as jnp
from jax import lax
from jax.experimental import pallas as pl
from jax.experimental.pallas import tpu as pltpu

F32 = jnp.float32
BF16 = jnp.bfloat16

HEADS = 4
A_DQK = 64
HEAD_W = 128
C_DV = 256
ROPE_BASE = 10000.0
LN_EPS = 1e-5
CONV_W = 3

Z_AQ, Z_AK, Z_AV = 0, 512, 1024
Z_BQ, Z_BK, Z_BV = 1536, 2048, 2560
Z_CQ, Z_CK, Z_CV, Z_CG = 3072, 3584, 4096, 5120
Z_GATES = 6144
Z_WIDTH = 9216
BF_OFF = 3072
BF_PAD = 128

NEG = -1e30
LOG2E = 1.4426950408889634
VT_ROWS = 144

TN_ADA = 1024
TM_IN, TN_IN = 1024, 1024
TQ_A = 512
TK_B = 512
NH_A = 4
NH_B = 2
PROLOGUE_CHUNK = 512
SKIP_MARGIN = 160.0
L_RET = 256
TM_MERGE = 512
TM_FFN = 512
CW_FFN = 256

VMEM_LIMIT = 56 * 1024 * 1024


def _cparams(sem):
    return pltpu.CompilerParams(dimension_semantics=sem, vmem_limit_bytes=VMEM_LIMIT)


def _ada_kernel(ct_ref, w_ref, b_ref, o_ref):
    w = w_ref[0]
    rows = []
    for b in range(ct_ref.shape[0]):
        cb = ct_ref[b]
        ca = cb * jax.nn.sigmoid(cb)
        rows.append(jnp.sum(ca * w, axis=0, keepdims=True))
    o_ref[0] = jnp.concatenate(rows, axis=0) + b_ref[0]


def _ada(c, w_ada, b_ada):
    B, D = c.shape
    L, _, N = w_ada.shape
    tn = min(TN_ADA, N)
    return pl.pallas_call(
        _ada_kernel,
        grid=(L, N // tn),
        in_specs=[
            pl.BlockSpec((B, D, 1), lambda l, j: (0, 0, 0)),
            pl.BlockSpec((1, D, tn), lambda l, j: (l, 0, j)),
            pl.BlockSpec((1, 1, tn), lambda l, j: (l, 0, j)),
        ],
        out_specs=pl.BlockSpec((1, B, tn), lambda l, j: (l, 0, j)),
        out_shape=jax.ShapeDtypeStruct((L, B, N), F32),
        compiler_params=_cparams(("parallel", "parallel")),
        name="ada_mod",
    )(c.reshape(B, D, 1), w_ada, b_ada.reshape(L, 1, N))


def _inproj_kernel(x_ref, sh_ref, sc_ref, w_ref, b_ref, wf_ref, bf_ref, z_ref, lf_ref, h_scr):
    @pl.when(pl.program_id(2) == 0)
    def _():
        h = x_ref[0] * (1.0 + sc_ref[0]) + sh_ref[0]
        hb = h.astype(BF16)
        h_scr[...] = hb
        zf = jnp.dot(hb, wf_ref[...], preferred_element_type=F32) + bf_ref[...]
        lt = zf.T[:8]
        lf_ref[0] = jnp.minimum(lt, 0.0) - jnp.log1p(jnp.exp(-jnp.abs(lt)))

    acc = jnp.dot(h_scr[...], w_ref[...], preferred_element_type=F32)
    z_ref[0] = (acc + b_ref[...]).astype(BF16)


def _inproj(x, mod_l, w, b, wf, bf, l):
    B, S, D = x.shape
    N = w.shape[2]
    tm, tn = min(TM_IN, S), min(TN_IN, N)
    return pl.pallas_call(
        _inproj_kernel,
        grid=(B, S // tm, N // tn),
        in_specs=[
            pl.BlockSpec((1, tm, D), lambda b, i, j: (b, i, 0)),
            pl.BlockSpec((1, 1, D), lambda b, i, j: (b, 0, 0)),
            pl.BlockSpec((1, 1, D), lambda b, i, j: (b, 0, 1)),
            pl.BlockSpec((None, D, tn), lambda b, i, j: (l, 0, j)),
            pl.BlockSpec((None, 1, tn), lambda b, i, j: (l, 0, j)),
            pl.BlockSpec((None, D, BF_PAD), lambda b, i, j: (l, 0, 0)),
            pl.BlockSpec((None, 1, BF_PAD), lambda b, i, j: (l, 0, 0)),
        ],
        out_specs=[
            pl.BlockSpec((1, tm, tn), lambda b, i, j: (b, i, j)),
            pl.BlockSpec((1, 8, tm), lambda b, i, j: (b, 0, i)),
        ],
        out_shape=[
            jax.ShapeDtypeStruct((B, S, N), BF16),
            jax.ShapeDtypeStruct((B, 8, S), F32),
        ],
        scratch_shapes=[pltpu.VMEM((tm, D), BF16)],
        compiler_params=_cparams(("parallel", "parallel", "arbitrary")),
        name="in_proj",
    )(x, mod_l, mod_l, w, b, wf, bf)


def _cumsum_kernel(lf_ref, f_ref):
    x = lf_ref[0]
    lane = lax.broadcasted_iota(jnp.int32, x.shape, 1)
    k = 1
    while k < x.shape[1]:
        x = x + jnp.where(lane >= k, pltpu.roll(x, k, 1), 0.0)
        k *= 2
    f_ref[0] = x


def _cumsum(lf):
    B, R, S = lf.shape
    return pl.pallas_call(
        _cumsum_kernel,
        grid=(B,),
        in_specs=[pl.BlockSpec((1, R, S), lambda b: (b, 0, 0))],
        out_specs=pl.BlockSpec((1, R, S), lambda b: (b, 0, 0)),
        out_shape=jax.ShapeDtypeStruct((B, R, S), F32),
        compiler_params=_cparams(("parallel",)),
        name="forget_cumsum",
    )(lf)


def _build_vt(v_ref, vt_scr, tk):
    ones_row = (lax.broadcasted_iota(jnp.int32, (VT_ROWS - HEAD_W, tk), 0) == 0).astype(BF16)
    for c in range(v_ref.shape[1] // tk):
        vb = v_ref[0, c * tk:(c + 1) * tk, :].astype(F32)
        vt_scr[c, 0:HEAD_W, :] = vb.T.astype(BF16)
        vt_scr[c, HEAD_W:VT_ROWS, :] = ones_row


def _softmax_step_t(s, vt, m_scr, acc_scr):
    m_prev = m_scr[...]
    m_new = jnp.maximum(m_prev, jnp.max(s, axis=0, keepdims=True))
    p = jnp.exp2(s - m_new).astype(BF16)
    alpha = jnp.exp2(m_prev - m_new)
    acc_scr[...] = alpha * acc_scr[...] + jnp.dot(vt, p, preferred_element_type=F32)
    m_scr[...] = m_new


def _init_streams(streams, first_tile):
    for st in streams:
        st["m"][...] = jnp.full(st["m"].shape, NEG, F32)
        st["acc"][...] = jnp.zeros(st["acc"].shape, F32)
        st["sa"][...] = st["scores"](first_tile)


def _pair_body(streams):
    def pair(t, carry):
        j = 2 * t
        for st in streams:
            st["sb"][...] = st["scores"](j + 1)
            _softmax_step_t(st["sa"][...], st["vt"][j], st["m"], st["acc"])
        for st in streams:
            st["sa"][...] = st["scores"](j + 2)
            _softmax_step_t(st["sb"][...], st["vt"][j + 1], st["m"], st["acc"])
        return carry
    return pair


def _sweep(qi, streams):
    _init_streams(streams, 0)
    lax.fori_loop(0, qi // 2, _pair_body(streams), 0)

    @pl.when(qi % 2 == 0)
    def _():
        for st in streams:
            _softmax_step_t(st["mask"](st["sa"][...]), st["vt"][qi], st["m"], st["acc"])

    @pl.when(qi % 2 == 1)
    def _():
        for st in streams:
            st["sb"][...] = st["scores"](qi)
            _softmax_step_t(st["sa"][...], st["vt"][qi - 1], st["m"], st["acc"])
        for st in streams:
            _softmax_step_t(st["mask"](st["sb"][...]), st["vt"][qi], st["m"], st["acc"])


def _sweep_wide(qi, t0, streams):
    _init_streams(streams, 2 * t0)
    lax.fori_loop(t0, qi, _pair_body(streams), 0)
    for st in streams:
        st["sl"][...] = st["scores_late"](2 * qi + 1)
        _softmax_step_t(st["mask"](st["sa"][...]), st["vt"][2 * qi], st["m"], st["acc"])
    for st in streams:
        nl = st["sl"].shape[1]
        _softmax_step_t(st["mask"](st["sl"][...]), st["vt"][2 * qi + 1],
                        st["m"].at[:, nl:], st["acc"].at[:, nl:])


_NT = (((1,), (1,)), ((), ()))


def _attn_scratch(S, tk, nq):
    return [pltpu.VMEM((S // tk, VT_ROWS, tk), BF16),
            pltpu.VMEM((1, nq), F32),
            pltpu.VMEM((VT_ROWS, nq), F32),
            pltpu.VMEM((tk, nq), F32), pltpu.VMEM((tk, nq), F32)]


def _head_cols(hh):
    return slice(hh * HEAD_W, (hh + 1) * HEAD_W)


def _diff_attn_kernel(q_ref, k_ref, v_ref, lq1_ref, lk1_ref, lq2_ref, lk2_ref, li_ref, g_ref,
                      o_ref, *scr, tq, chunk, nh):
    qi = pl.program_id(2)
    per = len(scr) // nh
    names = ("vt", "m", "acc", "sa", "sb")
    streams = [dict(zip(names, scr[hh * per:(hh + 1) * per])) for hh in range(nh)]

    @pl.when(qi == 0)
    def _():
        for hh, st in enumerate(streams):
            _build_vt(v_ref.at[:, :, _head_cols(hh)], st["vt"], tq)

    def mask_fn(s):
        key = lax.broadcasted_iota(jnp.int32, s.shape, 0)
        col = lax.broadcasted_iota(jnp.int32, s.shape, 1)
        qry = jnp.where(col >= tq, col - tq, col)
        return jnp.where(key // chunk <= qry // chunk, s, NEG)

    for hh, st in enumerate(streams):
        q = q_ref[0, :, _head_cols(hh)]
        lane = lax.broadcasted_iota(jnp.int32, q.shape, 1)
        zero = jnp.zeros_like(q)
        qs = jnp.concatenate([jnp.where(lane < A_DQK, q, zero),
                              jnp.where(lane >= A_DQK, q, zero)], axis=0)

        def scores(j, qs=qs, hh=hh):
            off = pl.multiple_of(j * tq, tq)
            return lax.dot_general(k_ref[0, pl.ds(off, tq), _head_cols(hh)], qs, _NT,
                                   preferred_element_type=F32)

        st["scores"], st["mask"] = scores, mask_fn

    _sweep(qi, streams)

    lam_init = li_ref[...]
    lam = (jnp.exp(jnp.sum(lq1_ref[...] * lk1_ref[...], axis=1, keepdims=True))
           - jnp.exp(jnp.sum(lq2_ref[...] * lk2_ref[...], axis=1, keepdims=True)) + lam_init)
    for hh, st in enumerate(streams):
        acc = st["acc"][...]
        ot_all = acc[0:HEAD_W] / acc[HEAD_W:HEAD_W + 1]
        o = (ot_all[:, :tq] - lam * ot_all[:, tq:]).T
        o = o * lax.rsqrt(jnp.mean(o * o, axis=1, keepdims=True) + LN_EPS)
        o_ref[0, :, _head_cols(hh)] = (o * g_ref[:, _head_cols(hh)] * (1.0 - lam_init)).astype(BF16)


def _resident(shape, index_map):
    return pl.BlockSpec(shape, index_map, pipeline_mode=pl.Buffered(1))


def _diff_attn(z, lq1, lk1, lq2, lk2, lam_init, g_diff, chunk):
    B, S, _ = z.shape
    tq = min(TQ_A, S)
    nh = NH_A
    w = nh * HEAD_W
    qb, kb, vb = Z_AQ // w, Z_AK // w, Z_AV // w
    small = pl.BlockSpec((1, A_DQK), lambda b, h, i: (0, 0))
    return pl.pallas_call(
        functools.partial(_diff_attn_kernel, tq=tq, chunk=chunk, nh=nh),
        grid=(B, HEADS // nh, S // tq),
        in_specs=[
            pl.BlockSpec((1, tq, w), lambda b, h, i: (b, i, qb + h)),
            _resident((1, S, w), lambda b, h, i: (b, 0, kb + h)),
            _resident((1, S, w), lambda b, h, i: (b, 0, vb + h)),
            small, small, small, small,
            pl.BlockSpec((1, 1), lambda b, h, i: (0, 0)),
            pl.BlockSpec((1, w), lambda b, h, i: (0, h)),
        ],
        out_specs=pl.BlockSpec((1, tq, w), lambda b, h, i: (b, i, h)),
        out_shape=jax.ShapeDtypeStruct((B, S, HEADS * HEAD_W), BF16),
        scratch_shapes=_attn_scratch(S, tq, 2 * tq) * nh,
        compiler_params=_cparams(("parallel", "parallel", "arbitrary")),
        name="diff_attn",
    )(z, z, z, lq1, lk1, lq2, lk2, lam_init, g_diff)


def _row_norm_max(x):
    xf = x.astype(F32)
    return jnp.sqrt(jnp.max(jnp.sum(xf * xf, axis=1, keepdims=True)))


def _fox_attn_kernel(q_ref, k_ref, v_ref, f_ref, o_ref, fmin_s, fmax_s, c0_s, kn_s, *scr, tk, nh):
    qi = pl.program_id(2)
    S = k_ref.shape[1]
    tq = 2 * tk
    n_blk = S // tq
    pc = min(PROLOGUE_CHUNK, S)
    per = len(scr) // nh
    names = ("frep", "vt", "m", "acc", "sa", "sb", "sl")
    streams = [dict(zip(names, scr[hh * per:(hh + 1) * per])) for hh in range(nh)]

    @pl.when(qi == 0)
    def _():
        for hh, st in enumerate(streams):
            _build_vt(v_ref.at[:, :, _head_cols(hh)], st["vt"], tk)
            kn = jnp.float32(0.0)
            for c in range(S // pc):
                fb = jnp.broadcast_to(f_ref[hh, :, c * pc:(c + 1) * pc] * LOG2E, (HEAD_W, pc))
                st["frep"][c * pc:(c + 1) * pc, :] = fb.T
                kn = jnp.maximum(kn, _row_norm_max(k_ref[0, c * pc:(c + 1) * pc, _head_cols(hh)]))
            kn_s[hh] = kn
            for t in range(n_blk):
                fb = f_ref[hh, :, t * tq:(t + 1) * tq] * LOG2E
                fmin_s[hh, t] = jnp.min(fb)
                fmax_s[hh, t] = jnp.max(fb)
                c0_s[hh, t] = jnp.max(fb[:, 0:1])

    def mask_fn(s):
        key = lax.broadcasted_iota(jnp.int32, s.shape, 0)
        qry = lax.broadcasted_iota(jnp.int32, s.shape, 1)
        return jnp.where(key <= qry, s, NEG)

    q_off = pl.multiple_of(qi * tq, tq)
    t0 = qi
    for hh, st in enumerate(streams):
        q = q_ref[0, :, _head_cols(hh)]
        c0 = st["frep"][pl.ds(q_off, 8), :][0:1, 0:1]

        qk = _row_norm_max(q) * kn_s[hh] * 1.01 + 1.0
        thr = 2.0 * qk + (fmax_s[hh, qi] - c0_s[hh, qi]) + SKIP_MARGIN
        t0h = jnp.int32(0)
        for t in range(n_blk - 1):
            skip = (t < qi) & (t0h == t) & (fmin_s[hh, t] - c0_s[hh, qi] >= thr)
            t0h = t0h + skip.astype(jnp.int32)
        t0 = jnp.minimum(t0, t0h)

        def biased(j, qq, st=st, hh=hh, c0=c0):
            off = pl.multiple_of(j * tk, tk)
            fsub = st["frep"][pl.ds(off, tk), :] - c0
            s = lax.dot_general(k_ref[0, pl.ds(off, tk), _head_cols(hh)], qq, _NT,
                                preferred_element_type=F32)
            return s - jnp.tile(fsub, (1, qq.shape[0] // HEAD_W))

        st["scores"] = functools.partial(biased, qq=q)
        st["scores_late"] = functools.partial(biased, qq=q[tk:])
        st["mask"] = mask_fn

    _sweep_wide(qi, t0, streams)

    for hh, st in enumerate(streams):
        acc = st["acc"][...]
        o_ref[0, :, _head_cols(hh)] = (acc[0:HEAD_W] / acc[HEAD_W:HEAD_W + 1]).T.astype(BF16)


def _fox_attn(z, f3):
    B, S, _ = z.shape
    tk = min(TK_B, S // 2)
    tq = 2 * tk
    nh = NH_B
    w = nh * HEAD_W
    rows = f3.shape[0] // B
    qb, kb, vb = Z_BQ // w, Z_BK // w, Z_BV // w
    per_head = ([pltpu.VMEM((S, HEAD_W), F32)] + _attn_scratch(S, tk, tq)
                + [pltpu.VMEM((tk, tq // 2), F32)])
    return pl.pallas_call(
        functools.partial(_fox_attn_kernel, tk=tk, nh=nh),
        grid=(B, HEADS // nh, S // tq),
        in_specs=[
            pl.BlockSpec((1, tq, w), lambda b, h, i: (b, i, qb + h)),
            _resident((1, S, w), lambda b, h, i: (b, 0, kb + h)),
            _resident((1, S, w), lambda b, h, i: (b, 0, vb + h)),
            pl.BlockSpec((nh, 1, S), lambda b, h, i: (b * (rows // nh) + h, 0, 0)),
        ],
        out_specs=pl.BlockSpec((1, tq, w), lambda b, h, i: (b, i, h)),
        out_shape=jax.ShapeDtypeStruct((B, S, HEADS * HEAD_W), BF16),
        scratch_shapes=([pltpu.SMEM((nh, S // tq), F32)] * 3 + [pltpu.SMEM((nh,), F32)]
                        + per_head * nh),
        compiler_params=_cparams(("parallel", "parallel", "arbitrary")),
        name="fox_attn",
    )(z, z, z, f3)


def _retention_kernel(q_ref, k_ref, v_ref, cg_ref, cos_ref, sin_ref, g_ref, o_ref, r_scr, *, L):
    @pl.when(pl.program_id(1) == 0)
    def _():
        r_scr[...] = jnp.zeros(r_scr.shape, F32)

    cosf = cos_ref[...]
    sinf = sin_ref[...]
    ii = lax.broadcasted_iota(jnp.int32, (L, L), 0)
    jj = lax.broadcasted_iota(jnp.int32, (L, L), 1)
    dpos = jnp.maximum(ii - jj, 0).astype(F32)
    causal = ii >= jj
    pos = lax.broadcasted_iota(jnp.int32, (L, 1), 0).astype(F32)

    for h in range(HEADS):
        lg = math.log(1.0 - 2.0 ** (-5.0 - h))
        qh = q_ref[0, :, h * HEAD_W:(h + 1) * HEAD_W].astype(F32)
        kh = k_ref[0, :, h * HEAD_W:(h + 1) * HEAD_W].astype(F32)
        qr = qh * cosf + pltpu.roll(qh, HEAD_W // 2, 1) * sinf
        kr = kh * cosf + pltpu.roll(kh, HEAD_W // 2, 1) * sinf
        v = v_ref[0, :, h * C_DV:(h + 1) * C_DV]
        dmask = jnp.where(causal, jnp.exp(lg * dpos), 0.0)
        s = lax.dot_general(qr.astype(BF16), kr.astype(BF16), _NT, preferred_element_type=F32) * dmask
        inner = jnp.dot(s.astype(BF16), v, preferred_element_type=F32)
        r = r_scr[h]
        q_dec = qr * jnp.exp(lg * (pos + 1.0))
        cross = jnp.dot(q_dec.astype(BF16), r.astype(BF16), preferred_element_type=F32)
        k_dec = kr * jnp.exp(lg * (L - 1.0 - pos))
        r_scr[h] = math.exp(lg * L) * r + jnp.dot(k_dec.T.astype(BF16), v, preferred_element_type=F32)
        o = inner + cross
        mu = jnp.mean(o, axis=1, keepdims=True)
        var = jnp.mean(jnp.square(o - mu), axis=1, keepdims=True)
        y = (o - mu) * lax.rsqrt(var + LN_EPS) * g_ref[:, h * C_DV:(h + 1) * C_DV]
        cg = cg_ref[0, :, h * C_DV:(h + 1) * C_DV].astype(F32)
        o_ref[0, :, h * C_DV:(h + 1) * C_DV] = (y * (cg * jax.nn.sigmoid(cg))).astype(BF16)


def _retention(z, cos_t, sin_t, g_ret):
    B, S, _ = z.shape
    L = min(L_RET, S)
    qk_w, v_w = HEADS * HEAD_W, HEADS * C_DV
    return pl.pallas_call(
        functools.partial(_retention_kernel, L=L),
        grid=(B, S // L),
        in_specs=[
            pl.BlockSpec((1, L, qk_w), lambda b, n: (b, n, Z_CQ // qk_w)),
            pl.BlockSpec((1, L, qk_w), lambda b, n: (b, n, Z_CK // qk_w)),
            pl.BlockSpec((1, L, v_w), lambda b, n: (b, n, Z_CV // v_w)),
            pl.BlockSpec((1, L, v_w), lambda b, n: (b, n, Z_CG // v_w)),
            pl.BlockSpec((L, HEAD_W), lambda b, n: (n, 0)),
            pl.BlockSpec((L, HEAD_W), lambda b, n: (n, 0)),
            pl.BlockSpec((1, v_w), lambda b, n: (0, 0)),
        ],
        out_specs=pl.BlockSpec((1, L, v_w), lambda b, n: (b, n, 0)),
        out_shape=jax.ShapeDtypeStruct((B, S, v_w), BF16),
        scratch_shapes=[pltpu.VMEM((HEADS, HEAD_W, C_DV), F32)],
        compiler_params=_cparams(("parallel", "arbitrary")),
        name="retention",
    )(z, z, z, z, cos_t, sin_t, g_ret)


def _rope_tables(S):
    half = HEAD_W // 2
    inv = 1.0 / (ROPE_BASE ** np.linspace(0.0, 1.0, half))
    ang = np.arange(S, dtype=np.float64)[:, None] * inv[None, :]
    cos, sin = np.cos(ang), np.sin(ang)
    return (jnp.asarray(np.concatenate([cos, cos], axis=1), F32),
            jnp.asarray(np.concatenate([-sin, sin], axis=1), F32))


def _layernorm(r, g, b):
    mu = jnp.mean(r, axis=1, keepdims=True)
    var = jnp.mean(jnp.square(r - mu), axis=1, keepdims=True)
    return (r - mu) * lax.rsqrt(var + LN_EPS) * g + b


def _merge_kernel(ya_ref, yb_ref, yc_ref, ga_ref, gb_ref, gc_ref, x_ref, gt_ref, sh2_ref, sc2_ref,
                  wpa_ref, wpb_ref, wpc_ref, wout_ref, lng_ref, lnb_ref, xo_ref, ho_ref, *, alpha):
    def branch(y_ref, g_ref, w_ref):
        gate = jax.nn.sigmoid(g_ref[0].astype(F32))
        return gate * jnp.dot(y_ref[0], w_ref[...], preferred_element_type=F32)

    m = branch(ya_ref, ga_ref, wpa_ref) + branch(yb_ref, gb_ref, wpb_ref) + branch(yc_ref, gc_ref, wpc_ref)
    y = jnp.dot(m.astype(BF16), wout_ref[...], preferred_element_type=F32)
    xn = _layernorm(alpha * x_ref[0] + gt_ref[0] * y, lng_ref[...], lnb_ref[...])
    xo_ref[0] = xn
    ho_ref[0] = (xn * (1.0 + sc2_ref[0]) + sh2_ref[0]).astype(BF16)


def _const_spec(shape):
    return pl.BlockSpec(shape, lambda b, i: (0,) * len(shape), pipeline_mode=pl.Buffered(1))


def _layer_spec(stacked, l):
    shape = stacked.shape[1:]
    return pl.BlockSpec((None,) + shape, lambda b, i: (l,) + (0,) * len(shape),
                        pipeline_mode=pl.Buffered(1))


def _merge(ya, yb, yc, z, x, mod_l, wpa, wpb, wpc, wout, lng, lnb, alpha, l):
    B, S, D = x.shape
    tm = min(TM_MERGE, S)
    gblk = Z_GATES // D
    tok = lambda w, col: pl.BlockSpec((1, tm, w), lambda b, i: (b, i, col))
    modv = lambda col: pl.BlockSpec((1, 1, D), lambda b, i: (b, 0, col))
    return pl.pallas_call(
        functools.partial(_merge_kernel, alpha=alpha),
        grid=(B, S // tm),
        in_specs=[
            tok(ya.shape[2], 0), tok(yb.shape[2], 0), tok(yc.shape[2], 0),
            tok(D, gblk), tok(D, gblk + 1), tok(D, gblk + 2),
            tok(D, 0),
            modv(2), modv(3), modv(4),
            _layer_spec(wpa, l), _layer_spec(wpb, l), _layer_spec(wpc, l),
            _layer_spec(wout, l), _const_spec(lng.shape), _const_spec(lnb.shape),
        ],
        out_specs=[tok(D, 0), tok(D, 0)],
        out_shape=[jax.ShapeDtypeStruct((B, S, D), F32), jax.ShapeDtypeStruct((B, S, D), BF16)],
        compiler_params=_cparams(("parallel", "parallel")),
        name="merge_out_ln",
    )(ya, yb, yc, z, z, z, x, mod_l, mod_l, mod_l, wpa, wpb, wpc, wout, lng, lnb)


def _ffn_kernel(h_ref, x_ref, gt_ref, wup_ref, wconv_ref, bconv_ref, wdown_ref, lng_ref, lnb_ref,
                xo_ref, a_scr, tail_scr, *, alpha, cw):
    @pl.when(pl.program_id(1) == 0)
    def _():
        tail_scr[...] = jnp.zeros(tail_scr.shape, F32)

    h = h_ref[0]
    tm = h.shape[0]
    dff = a_scr.shape[1]
    row = lax.broadcasted_iota(jnp.int32, (tm, cw), 0)
    for ci in range(dff // cw):
        lo = ci * cw
        u = jnp.dot(h, wup_ref[:, lo:lo + cw], preferred_element_type=F32)
        g = jnp.dot(h, wup_ref[:, dff + lo:dff + lo + cw], preferred_element_type=F32)
        tail = tail_scr[:, lo:lo + cw]
        p1, p2 = tail[7:8], tail[6:7]
        um1 = jnp.where(row == 0, p1, pltpu.roll(u, 1, 0))
        um2 = jnp.where(row == 0, p2, jnp.where(row == 1, p1, pltpu.roll(u, 2, 0)))
        tail_scr[:, lo:lo + cw] = u[tm - 8:]
        conv = bconv_ref[:, lo:lo + cw] + um2 * wconv_ref[0:1, lo:lo + cw]
        conv = conv + um1 * wconv_ref[1:2, lo:lo + cw]
        conv = conv + u * wconv_ref[2:3, lo:lo + cw]
        act = 0.5 * conv * (1.0 + lax.erf(conv * (2.0 ** -0.5)))
        a_scr[:, lo:lo + cw] = (act * g).astype(BF16)
    y = jnp.dot(a_scr[...], wdown_ref[...], preferred_element_type=F32)
    xo_ref[0] = _layernorm(alpha * x_ref[0] + gt_ref[0] * y, lng_ref[...], lnb_ref[...])


def _ffn(h, x, mod_l, wup, wconv, bconv, wdown, lng, lnb, alpha, l):
    B, S, D = x.shape
    dff = wdown.shape[1]
    tm = min(TM_FFN, S)
    tok = pl.BlockSpec((1, tm, D), lambda b, i: (b, i, 0))
    return pl.pallas_call(
        functools.partial(_ffn_kernel, alpha=alpha, cw=CW_FFN),
        grid=(B, S // tm),
        in_specs=[
            tok, tok,
            pl.BlockSpec((1, 1, D), lambda b, i: (b, 0, 5)),
            _layer_spec(wup, l), _const_spec(wconv.shape), _const_spec(bconv.shape),
            _layer_spec(wdown, l), _const_spec(lng.shape), _const_spec(lnb.shape),
        ],
        out_specs=tok,
        out_shape=jax.ShapeDtypeStruct((B, S, D), F32),
        scratch_shapes=[pltpu.VMEM((tm, dff), BF16), pltpu.VMEM((8, dff), F32)],
        compiler_params=_cparams(("parallel", "arbitrary")),
        name="conv_ffn_ln",
    )(h, x, mod_l, wup, wconv, bconv, wdown, lng, lnb)


def _prep_in_proj(w, b):
    scale = np.ones((Z_WIDTH,), np.float32)
    scale[Z_AQ:Z_AQ + 512] = A_DQK ** -0.5 * LOG2E
    scale[Z_BQ:Z_BQ + 512] = HEAD_W ** -0.5 * LOG2E
    scale[Z_CK:Z_CK + 512] = HEAD_W ** -0.5
    wm = jnp.concatenate([w[..., :BF_OFF], w[..., BF_OFF + HEADS:]], axis=-1) * scale
    bm = jnp.concatenate([b[..., :BF_OFF], b[..., BF_OFF + HEADS:]], axis=-1) * scale
    pad = BF_PAD - HEADS
    wf = jnp.pad(w[..., BF_OFF:BF_OFF + HEADS], ((0, 0), (0, 0), (0, pad)))
    bf = jnp.pad(b[..., BF_OFF:BF_OFF + HEADS], ((0, 0), (0, pad)))
    return wm.astype(BF16), bm[:, None, :], wf.astype(BF16), bf[:, None, :]


def kernel(x, c, w_ada, b_ada, w_in, b_in, lam_q1, lam_k1, lam_q2, lam_k2, g_diff, g_ret, w_pa, w_pb, w_pc, w_out, ln_g, ln_b, w_up, w_conv, b_conv, w_down):
    B, S, D = x.shape
    depth = w_ada.shape[0]
    alpha = (2 * depth) ** 0.25
    chunk = 64
    mod = _ada(c, w_ada, b_ada)
    cos_t, sin_t = _rope_tables(S)
    wm, bm, wf, bf = _prep_in_proj(w_in, b_in)
    wpa, wpb, wpc, wout = (w.astype(BF16) for w in (w_pa, w_pb, w_pc, w_out))
    wup, wdown = w_up.astype(BF16), w_down.astype(BF16)
    for l in range(depth):
        mod_l = mod[l].reshape(B, 1, 6 * D)
        lam_init = jnp.full((1, 1), 0.8 - 0.6 * math.exp(-0.3 * l), F32)
        z, lf = _inproj(x, mod_l, wm, bm, wf, bf, l)
        f = _cumsum(lf)
        ya = _diff_attn(z, lam_q1[l][None], lam_k1[l][None], lam_q2[l][None], lam_k2[l][None],
                        lam_init, g_diff[l][None], chunk)
        yb = _fox_attn(z, f.reshape(B * f.shape[1], 1, S))
        yc = _retention(z, cos_t, sin_t, g_ret[l][None])
        x, h2 = _merge(ya, yb, yc, z, x, mod_l, wpa, wpb, wpc, wout,
                       ln_g[l, 0][None], ln_b[l, 0][None], alpha, l)
        x = _ffn(h2, x, mod_l, wup, w_conv[l], b_conv[l][None], wdown,
                 ln_g[l, 1][None], ln_b[l, 1][None], alpha, l)
    return x
```

```python
import functools
import math

import numpy as np
import jax
import jax.numpy as jnp
from jax import lax
from jax.experimental import pallas as pl
from jax.experimental.pallas import tpu as pltpu

F32 = jnp.float32
BF16 = jnp.bfloat16

HEADS = 4
A_DQK = 64
HEAD_W = 128
C_DV = 256
ROPE_BASE = 10000.0
LN_EPS = 1e-5
CONV_W = 3

Z_AQ, Z_AK, Z_AV = 0, 512, 1024
Z_BQ, Z_BK, Z_BV = 1536, 2048, 2560
Z_CQ, Z_CK, Z_CV, Z_CG = 3072, 3584, 4096, 5120
Z_GATES = 6144
Z_WIDTH = 9216
BF_OFF = 3072
BF_PAD = 128

NEG = -1e30
LOG2E = 1.4426950408889634
VT_ROWS = 144

TN_ADA = 1024
TM_IN, TN_IN = 1024, 1024
TQ_A = 512
TK_B = 512
NH_A = 4
NH_B = 2
PROLOGUE_CHUNK = 512
SKIP_MARGIN = 160.0
L_RET = 256
TM_MERGE = 512
TM_FFN = 512
CW_FFN = 256

VMEM_LIMIT = 56 * 1024 * 1024


def _cparams(sem):
    return pltpu.CompilerParams(dimension_semantics=sem, vmem_limit_bytes=VMEM_LIMIT)


def _ada_kernel(ct_ref, w_ref, b_ref, o_ref):
    w = w_ref[0]
    rows = []
    for b in range(ct_ref.shape[0]):
        cb = ct_ref[b]
        ca = cb * jax.nn.sigmoid(cb)
        rows.append(jnp.sum(ca * w, axis=0, keepdims=True))
    o_ref[0] = jnp.concatenate(rows, axis=0) + b_ref[0]


def _ada(c, w_ada, b_ada):
    B, D = c.shape
    L, _, N = w_ada.shape
    tn = min(TN_ADA, N)
    return pl.pallas_call(
        _ada_kernel,
        grid=(L, N // tn),
        in_specs=[
            pl.BlockSpec((B, D, 1), lambda l, j: (0, 0, 0)),
            pl.BlockSpec((1, D, tn), lambda l, j: (l, 0, j)),
            pl.BlockSpec((1, 1, tn), lambda l, j: (l, 0, j)),
        ],
        out_specs=pl.BlockSpec((1, B, tn), lambda l, j: (l, 0, j)),
        out_shape=jax.ShapeDtypeStruct((L, B, N), F32),
        compiler_params=_cparams(("parallel", "parallel")),
        name="ada_mod",
    )(c.reshape(B, D, 1), w_ada, b_ada.reshape(L, 1, N))


def _inproj_kernel(x_ref, sh_ref, sc_ref, wa_ref, ba_ref, wb_ref, bb_ref, wf_ref, bf_ref,
                   z_ref, lf_ref, h_scr, *, na):
    j = pl.program_id(2)

    @pl.when(j == 0)
    def _():
        h = x_ref[0] * (1.0 + sc_ref[0]) + sh_ref[0]
        hb = h.astype(BF16)
        h_scr[...] = hb
        zf = jnp.dot(hb, wf_ref[...], preferred_element_type=F32) + bf_ref[...]
        lt = zf.T[:8]
        lf_ref[0] = jnp.minimum(lt, 0.0) - jnp.log1p(jnp.exp(-jnp.abs(lt)))

    def emit(w_ref, b_ref):
        acc = jnp.dot(h_scr[...], w_ref[...], preferred_element_type=F32)
        z_ref[0] = (acc + b_ref[...]).astype(BF16)

    @pl.when(j < na)
    def _():
        emit(wa_ref, ba_ref)

    @pl.when(j >= na)
    def _():
        emit(wb_ref, bb_ref)


def _inproj(x, mod_l, wa, ba, wb, bb, wf, bf, l):
    B, S, D = x.shape
    N = wa.shape[2] + wb.shape[2]
    tm, tn = min(TM_IN, S), min(TN_IN, wa.shape[2])
    na = wa.shape[2] // tn
    return pl.pallas_call(
        functools.partial(_inproj_kernel, na=na),
        grid=(B, S // tm, N // tn),
        in_specs=[
            pl.BlockSpec((1, tm, D), lambda b, i, j: (b, i, 0)),
            pl.BlockSpec((1, 1, D), lambda b, i, j: (b, 0, 0)),
            pl.BlockSpec((1, 1, D), lambda b, i, j: (b, 0, 1)),
            pl.BlockSpec((None, D, tn), lambda b, i, j: (l, 0, jnp.minimum(j, na - 1))),
            pl.BlockSpec((None, 1, tn), lambda b, i, j: (l, 0, jnp.minimum(j, na - 1))),
            pl.BlockSpec((None, D, tn), lambda b, i, j: (l, 0, jnp.maximum(j - na, 0))),
            pl.BlockSpec((None, 1, tn), lambda b, i, j: (l, 0, jnp.maximum(j - na, 0))),
            pl.BlockSpec((None, D, BF_PAD), lambda b, i, j: (l, 0, 0)),
            pl.BlockSpec((None, 1, BF_PAD), lambda b, i, j: (l, 0, 0)),
        ],
        out_specs=[
            pl.BlockSpec((1, tm, tn), lambda b, i, j: (b, i, j)),
            pl.BlockSpec((1, 8, tm), lambda b, i, j: (b, 0, i)),
        ],
        out_shape=[
            jax.ShapeDtypeStruct((B, S, N), BF16),
            jax.ShapeDtypeStruct((B, 8, S), F32),
        ],
        scratch_shapes=[pltpu.VMEM((tm, D), BF16)],
        compiler_params=_cparams(("parallel", "parallel", "arbitrary")),
        name="in_proj",
    )(x, mod_l, mod_l, wa, ba, wb, bb, wf, bf)


def _cumsum_kernel(lf_ref, f_ref):
    x = lf_ref[0]
    lane = lax.broadcasted_iota(jnp.int32, x.shape, 1)
    k = 1
    while k < x.shape[1]:
        x = x + jnp.where(lane >= k, pltpu.roll(x, k, 1), 0.0)
        k *= 2
    f_ref[0] = x


def _cumsum(lf):
    B, R, S = lf.shape
    return pl.pallas_call(
        _cumsum_kernel,
        grid=(B,),
        in_specs=[pl.BlockSpec((1, R, S), lambda b: (b, 0, 0))],
        out_specs=pl.BlockSpec((1, R, S), lambda b: (b, 0, 0)),
        out_shape=jax.ShapeDtypeStruct((B, R, S), F32),
        compiler_params=_cparams(("parallel",)),
        name="forget_cumsum",
    )(lf)


def _build_vt(v_ref, vt_scr, tk):
    ones_row = (lax.broadcasted_iota(jnp.int32, (VT_ROWS - HEAD_W, tk), 0) == 0).astype(BF16)
    for c in range(v_ref.shape[1] // tk):
        vb = v_ref[0, c * tk:(c + 1) * tk, :].astype(F32)
        vt_scr[c, 0:HEAD_W, :] = vb.T.astype(BF16)
        vt_scr[c, HEAD_W:VT_ROWS, :] = ones_row


def _softmax_step_t(s, vt, m_scr, acc_scr):
    m_prev = m_scr[...]
    m_new = jnp.maximum(m_prev, jnp.max(s, axis=0, keepdims=True))
    p = jnp.exp2(s - m_new).astype(BF16)
    alpha = jnp.exp2(m_prev - m_new)
    acc_scr[...] = alpha * acc_scr[...] + jnp.dot(vt, p, preferred_element_type=F32)
    m_scr[...] = m_new


def _init_streams(streams, first_tile):
    for st in streams:
        st["m"][...] = jnp.full(st["m"].shape, NEG, F32)
        st["acc"][...] = jnp.zeros(st["acc"].shape, F32)
        st["sa"][...] = st["scores"](first_tile)


def _pair_body(streams):
    def pair(t, carry):
        j = 2 * t
        for st in streams:
            st["sb"][...] = st["scores"](j + 1)
            _softmax_step_t(st["sa"][...], st["vt"][j], st["m"], st["acc"])
        for st in streams:
            st["sa"][...] = st["scores"](j + 2)
            _softmax_step_t(st["sb"][...], st["vt"][j + 1], st["m"], st["acc"])
        return carry
    return pair


def _sweep(qi, streams):
    _init_streams(streams, 0)
    lax.fori_loop(0, qi // 2, _pair_body(streams), 0)

    @pl.when(qi % 2 == 0)
    def _():
        for st in streams:
            _softmax_step_t(st["mask"](st["sa"][...]), st["vt"][qi], st["m"], st["acc"])

    @pl.when(qi % 2 == 1)
    def _():
        for st in streams:
            st["sb"][...] = st["scores"](qi)
            _softmax_step_t(st["sa"][...], st["vt"][qi - 1], st["m"], st["acc"])
        for st in streams:
            _softmax_step_t(st["mask"](st["sb"][...]), st["vt"][qi], st["m"], st["acc"])


def _sweep_wide(qi, t0, streams):
    _init_streams(streams, 2 * t0)
    lax.fori_loop(t0, qi, _pair_body(streams), 0)
    for st in streams:
        st["sl"][...] = st["scores_late"](2 * qi + 1)
        _softmax_step_t(st["mask"](st["sa"][...]), st["vt"][2 * qi], st["m"], st["acc"])
    for st in streams:
        nl = st["sl"].shape[1]
        _softmax_step_t(st["mask"](st["sl"][...]), st["vt"][2 * qi + 1],
                        st["m"].at[:, nl:], st["acc"].at[:, nl:])


_NT = (((1,), (1,)), ((), ()))


def _attn_scratch(S, tk, nq):
    return [pltpu.VMEM((S // tk, VT_ROWS, tk), BF16),
            pltpu.VMEM((1, nq), F32),
            pltpu.VMEM((VT_ROWS, nq), F32),
            pltpu.VMEM((tk, nq), F32), pltpu.VMEM((tk, nq), F32)]


def _head_cols(hh):
    return slice(hh * HEAD_W, (hh + 1) * HEAD_W)


def _diff_attn_kernel(q_ref, k_ref, v_ref, lq1_ref, lk1_ref, lq2_ref, lk2_ref, li_ref, g_ref,
                      o_ref, *scr, tq, chunk, nh):
    qi = pl.program_id(2)
    per = len(scr) // nh
    names = ("vt", "m", "acc", "sa", "sb")
    streams = [dict(zip(names, scr[hh * per:(hh + 1) * per])) for hh in range(nh)]

    @pl.when(qi == 0)
    def _():
        for hh, st in enumerate(streams):
            _build_vt(v_ref.at[:, :, _head_cols(hh)], st["vt"], tq)

    def mask_fn(s):
        key = lax.broadcasted_iota(jnp.int32, s.shape, 0)
        col = lax.broadcasted_iota(jnp.int32, s.shape, 1)
        qry = jnp.where(col >= tq, col - tq, col)
        return jnp.where(key // chunk <= qry // chunk, s, NEG)

    for hh, st in enumerate(streams):
        q = q_ref[0, :, _head_cols(hh)]
        lane = lax.broadcasted_iota(jnp.int32, q.shape, 1)
        zero = jnp.zeros_like(q)
        qs = jnp.concatenate([jnp.where(lane < A_DQK, q, zero),
                              jnp.where(lane >= A_DQK, q, zero)], axis=0)

        def scores(j, qs=qs, hh=hh):
            off = pl.multiple_of(j * tq, tq)
            return lax.dot_general(k_ref[0, pl.ds(off, tq), _head_cols(hh)], qs, _NT,
                                   preferred_element_type=F32)

        st["scores"], st["mask"] = scores, mask_fn

    _sweep(qi, streams)

    lam_init = li_ref[...]
    lam = (jnp.exp(jnp.sum(lq1_ref[...] * lk1_ref[...], axis=1, keepdims=True))
           - jnp.exp(jnp.sum(lq2_ref[...] * lk2_ref[...], axis=1, keepdims=True)) + lam_init)
    for hh, st in enumerate(streams):
        acc = st["acc"][...]
        ot_all = acc[0:HEAD_W] / acc[HEAD_W:HEAD_W + 1]
        o = (ot_all[:, :tq] - lam * ot_all[:, tq:]).T
        o = o * lax.rsqrt(jnp.mean(o * o, axis=1, keepdims=True) + LN_EPS)
        o_ref[0, :, _head_cols(hh)] = (o * g_ref[:, _head_cols(hh)] * (1.0 - lam_init)).astype(BF16)


def _resident(shape, index_map):
    return pl.BlockSpec(shape, index_map, pipeline_mode=pl.Buffered(1))


def _diff_attn(z, lq1, lk1, lq2, lk2, lam_init, g_diff, chunk):
    B, S, _ = z.shape
    tq = min(TQ_A, S)
    nh = NH_A
    w = nh * HEAD_W
    qb, kb, vb = Z_AQ // w, Z_AK // w, Z_AV // w
    small = pl.BlockSpec((1, A_DQK), lambda b, h, i: (0, 0))
    return pl.pallas_call(
        functools.partial(_diff_attn_kernel, tq=tq, chunk=chunk, nh=nh),
        grid=(B, HEADS // nh, S // tq),
        in_specs=[
            pl.BlockSpec((1, tq, w), lambda b, h, i: (b, i, qb + h)),
            _resident((1, S, w), lambda b, h, i: (b, 0, kb + h)),
            _resident((1, S, w), lambda b, h, i: (b, 0, vb + h)),
            small, small, small, small,
            pl.BlockSpec((1, 1), lambda b, h, i: (0, 0)),
            pl.BlockSpec((1, w), lambda b, h, i: (0, h)),
        ],
        out_specs=pl.BlockSpec((1, tq, w), lambda b, h, i: (b, i, h)),
        out_shape=jax.ShapeDtypeStruct((B, S, HEADS * HEAD_W), BF16),
        scratch_shapes=_attn_scratch(S, tq, 2 * tq) * nh,
        compiler_params=_cparams(("parallel", "parallel", "arbitrary")),
        name="diff_attn",
    )(z, z, z, lq1, lk1, lq2, lk2, lam_init, g_diff)


def _row_norm_max(x):
    xf = x.astype(F32)
    return jnp.sqrt(jnp.max(jnp.sum(xf * xf, axis=1, keepdims=True)))


def _fox_attn_kernel(q_ref, k_ref, v_ref, f_ref, o_ref, fmin_s, fmax_s, c0_s, kn_s, *scr, tk, nh):
    qi = pl.program_id(2)
    S = k_ref.shape[1]
    tq = 2 * tk
    n_blk = S // tq
    pc = min(PROLOGUE_CHUNK, S)
    per = len(scr) // nh
    names = ("frep", "vt", "m", "acc", "sa", "sb", "sl")
    streams = [dict(zip(names, scr[hh * per:(hh + 1) * per])) for hh in range(nh)]

    @pl.when(qi == 0)
    def _():
        for hh, st in enumerate(streams):
            _build_vt(v_ref.at[:, :, _head_cols(hh)], st["vt"], tk)
            kn = jnp.float32(0.0)
            for c in range(S // pc):
                fb = jnp.broadcast_to(f_ref[hh, :, c * pc:(c + 1) * pc] * LOG2E, (HEAD_W, pc))
                st["frep"][c * pc:(c + 1) * pc, :] = fb.T
                kn = jnp.maximum(kn, _row_norm_max(k_ref[0, c * pc:(c + 1) * pc, _head_cols(hh)]))
            kn_s[hh] = kn
            for t in range(n_blk):
                fb = f_ref[hh, :, t * tq:(t + 1) * tq] * LOG2E
                fmin_s[hh, t] = jnp.min(fb)
                fmax_s[hh, t] = jnp.max(fb)
                c0_s[hh, t] = jnp.max(fb[:, 0:1])

    def mask_fn(s):
        key = lax.broadcasted_iota(jnp.int32, s.shape, 0)
        qry = lax.broadcasted_iota(jnp.int32, s.shape, 1)
        return jnp.where(key <= qry, s, NEG)

    q_off = pl.multiple_of(qi * tq, tq)
    t0 = qi
    for hh, st in enumerate(streams):
        q = q_ref[0, :, _head_cols(hh)]
        c0 = st["frep"][pl.ds(q_off, 8), :][0:1, 0:1]

        qk = _row_norm_max(q) * kn_s[hh] * 1.01 + 1.0
        thr = 2.0 * qk + (fmax_s[hh, qi] - c0_s[hh, qi]) + SKIP_MARGIN
        t0h = jnp.int32(0)
        for t in range(n_blk - 1):
            skip = (t < qi) & (t0h == t) & (fmin_s[hh, t] - c0_s[hh, qi] >= thr)
            t0h = t0h + skip.astype(jnp.int32)
        t0 = jnp.minimum(t0, t0h)

        def biased(j, qq, st=st, hh=hh, c0=c0):
            off = pl.multiple_of(j * tk, tk)
            fsub = st["frep"][pl.ds(off, tk), :] - c0
            s = lax.dot_general(k_ref[0, pl.ds(off, tk), _head_cols(hh)], qq, _NT,
                                preferred_element_type=F32)
            return s - jnp.tile(fsub, (1, qq.shape[0] // HEAD_W))

        st["scores"] = functools.partial(biased, qq=q)
        st["scores_late"] = functools.partial(biased, qq=q[tk:])
        st["mask"] = mask_fn

    _sweep_wide(qi, t0, streams)

    for hh, st in enumerate(streams):
        acc = st["acc"][...]
        o_ref[0, :, _head_cols(hh)] = (acc[0:HEAD_W] / acc[HEAD_W:HEAD_W + 1]).T.astype(BF16)


def _fox_attn(z, f3):
    B, S, _ = z.shape
    tk = min(TK_B, S // 2)
    tq = 2 * tk
    nh = NH_B
    w = nh * HEAD_W
    rows = f3.shape[0] // B
    qb, kb, vb = Z_BQ // w, Z_BK // w, Z_BV // w
    per_head = ([pltpu.VMEM((S, HEAD_W), F32)] + _attn_scratch(S, tk, tq)
                + [pltpu.VMEM((tk, tq // 2), F32)])
    return pl.pallas_call(
        functools.partial(_fox_attn_kernel, tk=tk, nh=nh),
        grid=(B, HEADS // nh, S // tq),
        in_specs=[
            pl.BlockSpec((1, tq, w), lambda b, h, i: (b, i, qb + h)),
            _resident((1, S, w), lambda b, h, i: (b, 0, kb + h)),
            _resident((1, S, w), lambda b, h, i: (b, 0, vb + h)),
            pl.BlockSpec((nh, 1, S), lambda b, h, i: (b * (rows // nh) + h, 0, 0)),
        ],
        out_specs=pl.BlockSpec((1, tq, w), lambda b, h, i: (b, i, h)),
        out_shape=jax.ShapeDtypeStruct((B, S, HEADS * HEAD_W), BF16),
        scratch_shapes=([pltpu.SMEM((nh, S // tq), F32)] * 3 + [pltpu.SMEM((nh,), F32)]
                        + per_head * nh),
        compiler_params=_cparams(("parallel", "parallel", "arbitrary")),
        name="fox_attn",
    )(z, z, z, f3)


def _retention_kernel(q_ref, k_ref, v_ref, cg_ref, cos_ref, sin_ref, dm_ref, g_ref, o_ref, r_scr,
                      *, L):
    @pl.when(pl.program_id(1) == 0)
    def _():
        r_scr[...] = jnp.zeros(r_scr.shape, F32)

    cosf = cos_ref[...]
    sinf = sin_ref[...]
    pos = lax.broadcasted_iota(jnp.int32, (L, 1), 0).astype(F32)

    for h in range(HEADS):
        lg = math.log(1.0 - 2.0 ** (-5.0 - h))
        qh = q_ref[0, :, h * HEAD_W:(h + 1) * HEAD_W].astype(F32)
        kh = k_ref[0, :, h * HEAD_W:(h + 1) * HEAD_W].astype(F32)
        qr = qh * cosf + pltpu.roll(qh, HEAD_W // 2, 1) * sinf
        kr = kh * cosf + pltpu.roll(kh, HEAD_W // 2, 1) * sinf
        v = v_ref[0, :, h * C_DV:(h + 1) * C_DV]
        s = lax.dot_general(qr.astype(BF16), kr.astype(BF16), _NT,
                            preferred_element_type=F32) * dm_ref[h]
        inner = jnp.dot(s.astype(BF16), v, preferred_element_type=F32)
        r = r_scr[h]
        q_dec = qr * jnp.exp(lg * (pos + 1.0))
        cross = jnp.dot(q_dec.astype(BF16), r.astype(BF16), preferred_element_type=F32)
        k_dec = kr * jnp.exp(lg * (L - 1.0 - pos))
        r_scr[h] = math.exp(lg * L) * r + jnp.dot(k_dec.T.astype(BF16), v, preferred_element_type=F32)
        o = inner + cross
        mu = jnp.mean(o, axis=1, keepdims=True)
        var = jnp.mean(jnp.square(o - mu), axis=1, keepdims=True)
        y = (o - mu) * lax.rsqrt(var + LN_EPS) * g_ref[:, h * C_DV:(h + 1) * C_DV]
        cg = cg_ref[0, :, h * C_DV:(h + 1) * C_DV].astype(F32)
        o_ref[0, :, h * C_DV:(h + 1) * C_DV] = (y * (cg * jax.nn.sigmoid(cg))).astype(BF16)


def _retention(z, cos_t, sin_t, g_ret):
    B, S, _ = z.shape
    L = min(L_RET, S)
    qk_w, v_w = HEADS * HEAD_W, HEADS * C_DV
    dist = np.arange(L)[:, None] - np.arange(L)[None, :]
    gam = 1.0 - 2.0 ** (-5.0 - np.arange(HEADS, dtype=np.float64))
    dm = jnp.asarray(np.where(dist >= 0, gam[:, None, None] ** np.maximum(dist, 0), 0.0), F32)
    return pl.pallas_call(
        functools.partial(_retention_kernel, L=L),
        grid=(B, S // L),
        in_specs=[
            pl.BlockSpec((1, L, qk_w), lambda b, n: (b, n, Z_CQ // qk_w)),
            pl.BlockSpec((1, L, qk_w), lambda b, n: (b, n, Z_CK // qk_w)),
            pl.BlockSpec((1, L, v_w), lambda b, n: (b, n, Z_CV // v_w)),
            pl.BlockSpec((1, L, v_w), lambda b, n: (b, n, Z_CG // v_w)),
            pl.BlockSpec((L, HEAD_W), lambda b, n: (n, 0)),
            pl.BlockSpec((L, HEAD_W), lambda b, n: (n, 0)),
            _const_spec(dm.shape),
            pl.BlockSpec((1, v_w), lambda b, n: (0, 0)),
        ],
        out_specs=pl.BlockSpec((1, L, v_w), lambda b, n: (b, n, 0)),
        out_shape=jax.ShapeDtypeStruct((B, S, v_w), BF16),
        scratch_shapes=[pltpu.VMEM((HEADS, HEAD_W, C_DV), F32)],
        compiler_params=_cparams(("parallel", "arbitrary")),
        name="retention",
    )(z, z, z, z, cos_t, sin_t, dm, g_ret)


def _rope_tables(S):
    half = HEAD_W // 2
    inv = 1.0 / (ROPE_BASE ** np.linspace(0.0, 1.0, half))
    ang = np.arange(S, dtype=np.float64)[:, None] * inv[None, :]
    cos, sin = np.cos(ang), np.sin(ang)
    return (jnp.asarray(np.concatenate([cos, cos], axis=1), F32),
            jnp.asarray(np.concatenate([-sin, sin], axis=1), F32))


def _layernorm(r, g, b):
    mu = jnp.mean(r, axis=1, keepdims=True)
    var = jnp.mean(jnp.square(r - mu), axis=1, keepdims=True)
    return (r - mu) * lax.rsqrt(var + LN_EPS) * g + b


def _merge_kernel(ya_ref, yb_ref, yc_ref, ga_ref, gb_ref, gc_ref, x_ref, gt_ref, sh2_ref, sc2_ref,
                  wpa_ref, wpb_ref, wpc_ref, wout_ref, lng_ref, lnb_ref, xo_ref, ho_ref, *, alpha):
    def branch(y_ref, g_ref, w_ref):
        gate = jax.nn.sigmoid(g_ref[0].astype(F32))
        return gate * jnp.dot(y_ref[0], w_ref[...], preferred_element_type=F32)

    m = branch(ya_ref, ga_ref, wpa_ref) + branch(yb_ref, gb_ref, wpb_ref) + branch(yc_ref, gc_ref, wpc_ref)
    y = jnp.dot(m.astype(BF16), wout_ref[...], preferred_element_type=F32)
    xn = _layernorm(alpha * x_ref[0] + gt_ref[0] * y, lng_ref[...], lnb_ref[...])
    xo_ref[0] = xn
    ho_ref[0] = (xn * (1.0 + sc2_ref[0]) + sh2_ref[0]).astype(BF16)


def _const_spec(shape):
    return pl.BlockSpec(shape, lambda b, i: (0,) * len(shape), pipeline_mode=pl.Buffered(1))


def _layer_spec(stacked, l):
    shape = stacked.shape[1:]
    return pl.BlockSpec((None,) + shape, lambda b, i: (l,) + (0,) * len(shape),
                        pipeline_mode=pl.Buffered(1))


def _merge(ya, yb, yc, z, x, mod_l, wpa, wpb, wpc, wout, lng, lnb, alpha, l):
    B, S, D = x.shape
    tm = min(TM_MERGE, S)
    gblk = Z_GATES // D
    tok = lambda w, col: pl.BlockSpec((1, tm, w), lambda b, i: (b, i, col))
    modv = lambda col: pl.BlockSpec((1, 1, D), lambda b, i: (b, 0, col))
    return pl.pallas_call(
        functools.partial(_merge_kernel, alpha=alpha),
        grid=(B, S // tm),
        in_specs=[
            tok(ya.shape[2], 0), tok(yb.shape[2], 0), tok(yc.shape[2], 0),
            tok(D, gblk), tok(D, gblk + 1), tok(D, gblk + 2),
            tok(D, 0),
            modv(2), modv(3), modv(4),
            _layer_spec(wpa, l), _layer_spec(wpb, l), _layer_spec(wpc, l),
            _layer_spec(wout, l), _const_spec(lng.shape), _const_spec(lnb.shape),
        ],
        out_specs=[tok(D, 0), tok(D, 0)],
        out_shape=[jax.ShapeDtypeStruct((B, S, D), F32), jax.ShapeDtypeStruct((B, S, D), BF16)],
        compiler_params=_cparams(("parallel", "parallel")),
        name="merge_out_ln",
    )(ya, yb, yc, z, z, z, x, mod_l, mod_l, mod_l, wpa, wpb, wpc, wout, lng, lnb)


def _ffn_kernel(h_ref, x_ref, gt_ref, wup_ref, wconv_ref, bconv_ref, wdown_ref, lng_ref, lnb_ref,
                xo_ref, a_scr, tail_scr, *, alpha, cw):
    @pl.when(pl.program_id(1) == 0)
    def _():
        tail_scr[...] = jnp.zeros(tail_scr.shape, F32)

    h = h_ref[0]
    tm = h.shape[0]
    dff = a_scr.shape[1]
    row = lax.broadcasted_iota(jnp.int32, (tm, cw), 0)
    for ci in range(dff // cw):
        lo = ci * cw
        u = jnp.dot(h, wup_ref[:, lo:lo + cw], preferred_element_type=F32)
        g = jnp.dot(h, wup_ref[:, dff + lo:dff + lo + cw], preferred_element_type=F32)
        tail = tail_scr[:, lo:lo + cw]
        p1, p2 = tail[7:8], tail[6:7]
        um1 = jnp.where(row == 0, p1, pltpu.roll(u, 1, 0))
        um2 = jnp.where(row == 0, p2, jnp.where(row == 1, p1, pltpu.roll(u, 2, 0)))
        tail_scr[:, lo:lo + cw] = u[tm - 8:]
        conv = bconv_ref[:, lo:lo + cw] + um2 * wconv_ref[0:1, lo:lo + cw]
        conv = conv + um1 * wconv_ref[1:2, lo:lo + cw]
        conv = conv + u * wconv_ref[2:3, lo:lo + cw]
        act = 0.5 * conv * (1.0 + lax.erf(conv * (2.0 ** -0.5)))
        a_scr[:, lo:lo + cw] = (act * g).astype(BF16)
    y = jnp.dot(a_scr[...], wdown_ref[...], preferred_element_type=F32)
    xo_ref[0] = _layernorm(alpha * x_ref[0] + gt_ref[0] * y, lng_ref[...], lnb_ref[...])


def _ffn(h, x, mod_l, wup, wconv, bconv, wdown, lng, lnb, alpha, l):
    B, S, D = x.shape
    dff = wdown.shape[1]
    tm = min(TM_FFN, S)
    tok = pl.BlockSpec((1, tm, D), lambda b, i: (b, i, 0))
    return pl.pallas_call(
        functools.partial(_ffn_kernel, alpha=alpha, cw=CW_FFN),
        grid=(B, S // tm),
        in_specs=[
            tok, tok,
            pl.BlockSpec((1, 1, D), lambda b, i: (b, 0, 5)),
            _layer_spec(wup, l), _const_spec(wconv.shape), _const_spec(bconv.shape),
            _layer_spec(wdown, l), _const_spec(lng.shape), _const_spec(lnb.shape),
        ],
        out_specs=tok,
        out_shape=jax.ShapeDtypeStruct((B, S, D), F32),
        scratch_shapes=[pltpu.VMEM((tm, dff), BF16), pltpu.VMEM((8, dff), F32)],
        compiler_params=_cparams(("parallel", "arbitrary")),
        name="conv_ffn_ln",
    )(h, x, mod_l, wup, wconv, bconv, wdown, lng, lnb)


def _prep_in_proj(w, b):
    scale = np.ones((Z_WIDTH,), np.float32)
    scale[Z_AQ:Z_AQ + 512] = A_DQK ** -0.5 * LOG2E
    scale[Z_BQ:Z_BQ + 512] = HEAD_W ** -0.5 * LOG2E
    scale[Z_CK:Z_CK + 512] = HEAD_W ** -0.5
    sa, sb = scale[:BF_OFF], scale[BF_OFF:]
    wa = (w[..., :BF_OFF] * sa).astype(BF16)
    wb = (w[..., BF_OFF + HEADS:] * sb).astype(BF16)
    ba = (b[..., :BF_OFF] * sa)[:, None, :]
    bb = (b[..., BF_OFF + HEADS:] * sb)[:, None, :]
    pad = BF_PAD - HEADS
    wf = jnp.pad(w[..., BF_OFF:BF_OFF + HEADS], ((0, 0), (0, 0), (0, pad)))
    bf = jnp.pad(b[..., BF_OFF:BF_OFF + HEADS], ((0, 0), (0, pad)))
    return wa, ba, wb, bb, wf.astype(BF16), bf[:, None, :]


def kernel(x, c, w_ada, b_ada, w_in, b_in, lam_q1, lam_k1, lam_q2, lam_k2, g_diff, g_ret, w_pa, w_pb, w_pc, w_out, ln_g, ln_b, w_up, w_conv, b_conv, w_down):
    B, S, D = x.shape
    depth = w_ada.shape[0]
    alpha = (2 * depth) ** 0.25
    chunk = 64
    mod = _ada(c, w_ada, b_ada)
    cos_t, sin_t = _rope_tables(S)
    in_w = _prep_in_proj(w_in, b_in)
    wpa, wpb, wpc, wout = (w.astype(BF16) for w in (w_pa, w_pb, w_pc, w_out))
    wup, wdown = w_up.astype(BF16), w_down.astype(BF16)
    for l in range(depth):
        mod_l = mod[l].reshape(B, 1, 6 * D)
        lam_init = jnp.full((1, 1), 0.8 - 0.6 * math.exp(-0.3 * l), F32)
        z, lf = _inproj(x, mod_l, *in_w, l)
        f = _cumsum(lf)
        ya = _diff_attn(z, lam_q1[l][None], lam_k1[l][None], lam_q2[l][None], lam_k2[l][None],
                        lam_init, g_diff[l][None], chunk)
        yb = _fox_attn(z, f.reshape(B * f.shape[1], 1, S))
        yc = _retention(z, cos_t, sin_t, g_ret[l][None])
        x, h2 = _merge(ya, yb, yc, z, x, mod_l, wpa, wpb, wpc, wout,
                       ln_g[l, 0][None], ln_b[l, 0][None], alpha, l)
        x = _ffn(h2, x, mod_l, wup, w_conv[l], b_conv[l][None], wdown,
                 ln_g[l, 1][None], ln_b[l, 1][None], alpha, l)
    return x
```

```python
import functools
import math

import numpy as np
import jax
import jax.numpy as jnp
from jax import lax
from jax.experimental import pallas as pl
from jax.experimental.pallas import tpu as pltpu

F32 = jnp.float32
BF16 = jnp.bfloat16

HEADS = 4
A_DQK = 64
HEAD_W = 128
C_DV = 256
ROPE_BASE = 10000.0
LN_EPS = 1e-5
CONV_W = 3

Z_AQ, Z_AK, Z_AV = 0, 512, 1024
Z_BQ, Z_BK, Z_BV = 1536, 2048, 2560
Z_CQ, Z_CK, Z_CV, Z_CG = 3072, 3584, 4096, 5120
Z_GATES = 6144
Z_WIDTH = 9216
BF_OFF = 3072
BF_PAD = 128

NEG = -1e30
LOG2E = 1.4426950408889634
VT_ROWS = 144

TN_ADA = 1024
TM_IN, TN_IN = 1024, 1024
TQ_A = 512
TK_B = 512
NH_A = 4
NH_B = 2
PROLOGUE_CHUNK = 512
SKIP_MARGIN = 160.0
L_RET = 256
TM_MERGE = 512
TM_FFN = 512
CW_FFN = 256

VMEM_LIMIT = 56 * 1024 * 1024


def _cparams(sem):
    return pltpu.CompilerParams(dimension_semantics=sem, vmem_limit_bytes=VMEM_LIMIT)


def _ada_kernel(ct_ref, w_ref, b_ref, o_ref):
    w = w_ref[0]
    rows = []
    for b in range(ct_ref.shape[0]):
        cb = ct_ref[b]
        ca = cb * jax.nn.sigmoid(cb)
        rows.append(jnp.sum(ca * w, axis=0, keepdims=True))
    o_ref[0] = jnp.concatenate(rows, axis=0) + b_ref[0]


def _ada(c, w_ada, b_ada):
    B, D = c.shape
    L, _, N = w_ada.shape
    tn = min(TN_ADA, N)
    return pl.pallas_call(
        _ada_kernel,
        grid=(L, N // tn),
        in_specs=[
            pl.BlockSpec((B, D, 1), lambda l, j: (0, 0, 0)),
            pl.BlockSpec((1, D, tn), lambda l, j: (l, 0, j)),
            pl.BlockSpec((1, 1, tn), lambda l, j: (l, 0, j)),
        ],
        out_specs=pl.BlockSpec((1, B, tn), lambda l, j: (l, 0, j)),
        out_shape=jax.ShapeDtypeStruct((L, B, N), F32),
        compiler_params=_cparams(("parallel", "parallel")),
        name="ada_mod",
    )(c.reshape(B, D, 1), w_ada, b_ada.reshape(L, 1, N))


def _inproj_kernel(x_ref, sh_ref, sc_ref, w_ref, b_ref, wf_ref, bf_ref, z_ref, lf_ref, h_scr):
    @pl.when(pl.program_id(2) == 0)
    def _():
        h = x_ref[0] * (1.0 + sc_ref[0]) + sh_ref[0]
        hb = h.astype(BF16)
        h_scr[...] = hb
        zf = jnp.dot(hb, wf_ref[...], preferred_element_type=F32) + bf_ref[...]
        lt = zf.T[:8]
        lf_ref[0] = jnp.minimum(lt, 0.0) - jnp.log1p(jnp.exp(-jnp.abs(lt)))

    acc = jnp.dot(h_scr[...], w_ref[...], preferred_element_type=F32)
    z_ref[0] = (acc + b_ref[...]).astype(BF16)


def _inproj(x, mod_l, w, b, wf, bf, l):
    B, S, D = x.shape
    N = w.shape[2]
    tm, tn = min(TM_IN, S), min(TN_IN, N)
    return pl.pallas_call(
        _inproj_kernel,
        grid=(B, S // tm, N // tn),
        in_specs=[
            pl.BlockSpec((1, tm, D), lambda b, i, j: (b, i, 0)),
            pl.BlockSpec((1, 1, D), lambda b, i, j: (b, 0, 0)),
            pl.BlockSpec((1, 1, D), lambda b, i, j: (b, 0, 1)),
            pl.BlockSpec((None, D, tn), lambda b, i, j: (l, 0, j)),
            pl.BlockSpec((None, 1, tn), lambda b, i, j: (l, 0, j)),
            pl.BlockSpec((None, D, BF_PAD), lambda b, i, j: (l, 0, 0)),
            pl.BlockSpec((None, 1, BF_PAD), lambda b, i, j: (l, 0, 0)),
        ],
        out_specs=[
            pl.BlockSpec((1, tm, tn), lambda b, i, j: (b, i, j)),
            pl.BlockSpec((1, 8, tm), lambda b, i, j: (b, 0, i)),
        ],
        out_shape=[
            jax.ShapeDtypeStruct((B, S, N), BF16),
            jax.ShapeDtypeStruct((B, 8, S), F32),
        ],
        scratch_shapes=[pltpu.VMEM((tm, D), BF16)],
        compiler_params=_cparams(("parallel", "parallel", "arbitrary")),
        name="in_proj",
    )(x, mod_l, mod_l, w, b, wf, bf)


def _cumsum_kernel(lf_ref, f_ref):
    x = lf_ref[0]
    lane = lax.broadcasted_iota(jnp.int32, x.shape, 1)
    k = 1
    while k < x.shape[1]:
        x = x + jnp.where(lane >= k, pltpu.roll(x, k, 1), 0.0)
        k *= 2
    f_ref[0] = x


def _cumsum(lf):
    B, R, S = lf.shape
    return pl.pallas_call(
        _cumsum_kernel,
        grid=(B,),
        in_specs=[pl.BlockSpec((1, R, S), lambda b: (b, 0, 0))],
        out_specs=pl.BlockSpec((1, R, S), lambda b: (b, 0, 0)),
        out_shape=jax.ShapeDtypeStruct((B, R, S), F32),
        compiler_params=_cparams(("parallel",)),
        name="forget_cumsum",
    )(lf)


def _build_vt(v_ref, vt_scr, tk):
    ones_row = (lax.broadcasted_iota(jnp.int32, (VT_ROWS - HEAD_W, tk), 0) == 0).astype(BF16)
    for c in range(v_ref.shape[1] // tk):
        vb = v_ref[0, c * tk:(c + 1) * tk, :].astype(F32)
        vt_scr[c, 0:HEAD_W, :] = vb.T.astype(BF16)
        vt_scr[c, HEAD_W:VT_ROWS, :] = ones_row


def _softmax_step_t(s, vt, m_scr, acc_scr):
    m_prev = m_scr[...]
    m_new = jnp.maximum(m_prev, jnp.max(s, axis=0, keepdims=True))
    p = jnp.exp2(s - m_new).astype(BF16)
    alpha = jnp.exp2(m_prev - m_new)
    acc_scr[...] = alpha * acc_scr[...] + jnp.dot(vt, p, preferred_element_type=F32)
    m_scr[...] = m_new


def _init_streams(streams, first_tile):
    for st in streams:
        st["m"][...] = jnp.full(st["m"].shape, NEG, F32)
        st["acc"][...] = jnp.zeros(st["acc"].shape, F32)
        st["sa"][...] = st["scores"](first_tile)


def _pair_body(streams):
    def pair(t, carry):
        j = 2 * t
        for st in streams:
            st["sb"][...] = st["scores"](j + 1)
            _softmax_step_t(st["sa"][...], st["vt"][j], st["m"], st["acc"])
        for st in streams:
            st["sa"][...] = st["scores"](j + 2)
            _softmax_step_t(st["sb"][...], st["vt"][j + 1], st["m"], st["acc"])
        return carry
    return pair


def _sweep(qi, streams):
    _init_streams(streams, 0)
    lax.fori_loop(0, qi // 2, _pair_body(streams), 0)

    @pl.when(qi % 2 == 0)
    def _():
        for st in streams:
            _softmax_step_t(st["mask"](st["sa"][...]), st["vt"][qi], st["m"], st["acc"])

    @pl.when(qi % 2 == 1)
    def _():
        for st in streams:
            st["sb"][...] = st["scores"](qi)
            _softmax_step_t(st["sa"][...], st["vt"][qi - 1], st["m"], st["acc"])
        for st in streams:
            _softmax_step_t(st["mask"](st["sb"][...]), st["vt"][qi], st["m"], st["acc"])


def _sweep_wide(qi, t0, streams):
    _init_streams(streams, 2 * t0)
    lax.fori_loop(t0, qi, _pair_body(streams), 0)
    for st in streams:
        st["sl"][...] = st["scores_late"](2 * qi + 1)
        _softmax_step_t(st["mask"](st["sa"][...]), st["vt"][2 * qi], st["m"], st["acc"])
    for st in streams:
        nl = st["sl"].shape[1]
        _softmax_step_t(st["mask"](st["sl"][...]), st["vt"][2 * qi + 1],
                        st["m"].at[:, nl:], st["acc"].at[:, nl:])


_NT = (((1,), (1,)), ((), ()))


def _attn_scratch(S, tk, nq):
    return [pltpu.VMEM((S // tk, VT_ROWS, tk), BF16),
            pltpu.VMEM((1, nq), F32),
            pltpu.VMEM((VT_ROWS, nq), F32),
            pltpu.VMEM((tk, nq), F32), pltpu.VMEM((tk, nq), F32)]


def _head_cols(hh):
    return slice(hh * HEAD_W, (hh + 1) * HEAD_W)


def _diff_attn_kernel(q_ref, k_ref, v_ref, lq1_ref, lk1_ref, lq2_ref, lk2_ref, li_ref, g_ref,
                      o_ref, *scr, tq, chunk, nh):
    qi = pl.program_id(2)
    per = len(scr) // nh
    names = ("vt", "m", "acc", "sa", "sb")
    streams = [dict(zip(names, scr[hh * per:(hh + 1) * per])) for hh in range(nh)]

    @pl.when(qi == 0)
    def _():
        for hh, st in enumerate(streams):
            _build_vt(v_ref.at[:, :, _head_cols(hh)], st["vt"], tq)

    def mask_fn(s):
        key = lax.broadcasted_iota(jnp.int32, s.shape, 0)
        col = lax.broadcasted_iota(jnp.int32, s.shape, 1)
        qry = jnp.where(col >= tq, col - tq, col)
        return jnp.where(key // chunk <= qry // chunk, s, NEG)

    for hh, st in enumerate(streams):
        q = q_ref[0, :, _head_cols(hh)]
        lane = lax.broadcasted_iota(jnp.int32, q.shape, 1)
        zero = jnp.zeros_like(q)
        qs = jnp.concatenate([jnp.where(lane < A_DQK, q, zero),
                              jnp.where(lane >= A_DQK, q, zero)], axis=0)

        def scores(j, qs=qs, hh=hh):
            off = pl.multiple_of(j * tq, tq)
            return lax.dot_general(k_ref[0, pl.ds(off, tq), _head_cols(hh)], qs, _NT,
                                   preferred_element_type=F32)

        st["scores"], st["mask"] = scores, mask_fn

    _sweep(qi, streams)

    lam_init = li_ref[...]
    lam = (jnp.exp(jnp.sum(lq1_ref[...] * lk1_ref[...], axis=1, keepdims=True))
           - jnp.exp(jnp.sum(lq2_ref[...] * lk2_ref[...], axis=1, keepdims=True)) + lam_init)
    for hh, st in enumerate(streams):
        acc = st["acc"][...]
        ot_all = acc[0:HEAD_W] / acc[HEAD_W:HEAD_W + 1]
        o = (ot_all[:, :tq] - lam * ot_all[:, tq:]).T
        o = o * lax.rsqrt(jnp.mean(o * o, axis=1, keepdims=True) + LN_EPS)
        o_ref[0, :, _head_cols(hh)] = (o * g_ref[:, _head_cols(hh)] * (1.0 - lam_init)).astype(BF16)


def _resident(shape, index_map):
    return pl.BlockSpec(shape, index_map, pipeline_mode=pl.Buffered(1))


def _diff_attn(z, lq1, lk1, lq2, lk2, lam_init, g_diff, chunk):
    B, S, _ = z.shape
    tq = min(TQ_A, S)
    nh = NH_A
    w = nh * HEAD_W
    qb, kb, vb = Z_AQ // w, Z_AK // w, Z_AV // w
    small = pl.BlockSpec((1, A_DQK), lambda b, h, i: (0, 0))
    return pl.pallas_call(
        functools.partial(_diff_attn_kernel, tq=tq, chunk=chunk, nh=nh),
        grid=(B, HEADS // nh, S // tq),
        in_specs=[
            pl.BlockSpec((1, tq, w), lambda b, h, i: (b, i, qb + h)),
            _resident((1, S, w), lambda b, h, i: (b, 0, kb + h)),
            _resident((1, S, w), lambda b, h, i: (b, 0, vb + h)),
            small, small, small, small,
            pl.BlockSpec((1, 1), lambda b, h, i: (0, 0)),
            pl.BlockSpec((1, w), lambda b, h, i: (0, h)),
        ],
        out_specs=pl.BlockSpec((1, tq, w), lambda b, h, i: (b, i, h)),
        out_shape=jax.ShapeDtypeStruct((B, S, HEADS * HEAD_W), BF16),
        scratch_shapes=_attn_scratch(S, tq, 2 * tq) * nh,
        compiler_params=_cparams(("parallel", "parallel", "arbitrary")),
        name="diff_attn",
    )(z, z, z, lq1, lk1, lq2, lk2, lam_init, g_diff)


def _row_norm_max(x):
    xf = x.astype(F32)
    return jnp.sqrt(jnp.max(jnp.sum(xf * xf, axis=1, keepdims=True)))


def _fox_attn_kernel(q_ref, k_ref, v_ref, f_ref, o_ref, fmin_s, fmax_s, c0_s, kn_s, *scr, tk, nh):
    qi = pl.program_id(2)
    S = k_ref.shape[1]
    tq = 2 * tk
    n_blk = S // tq
    pc = min(PROLOGUE_CHUNK, S)
    per = len(scr) // nh
    names = ("frep", "vt", "m", "acc", "sa", "sb", "sl")
    streams = [dict(zip(names, scr[hh * per:(hh + 1) * per])) for hh in range(nh)]

    @pl.when(qi == 0)
    def _():
        for hh, st in enumerate(streams):
            _build_vt(v_ref.at[:, :, _head_cols(hh)], st["vt"], tk)
            kn = jnp.float32(0.0)
            for c in range(S // pc):
                fb = jnp.broadcast_to(f_ref[hh, :, c * pc:(c + 1) * pc] * LOG2E, (HEAD_W, pc))
                st["frep"][c * pc:(c + 1) * pc, :] = fb.T
                kn = jnp.maximum(kn, _row_norm_max(k_ref[0, c * pc:(c + 1) * pc, _head_cols(hh)]))
            kn_s[hh] = kn
            for t in range(n_blk):
                fb = f_ref[hh, :, t * tq:(t + 1) * tq] * LOG2E
                fmin_s[hh, t] = jnp.min(fb)
                fmax_s[hh, t] = jnp.max(fb)
                c0_s[hh, t] = jnp.max(fb[:, 0:1])

    def mask_fn(s):
        key = lax.broadcasted_iota(jnp.int32, s.shape, 0)
        qry = lax.broadcasted_iota(jnp.int32, s.shape, 1)
        return jnp.where(key <= qry, s, NEG)

    q_off = pl.multiple_of(qi * tq, tq)
    t0 = qi
    for hh, st in enumerate(streams):
        q = q_ref[0, :, _head_cols(hh)]
        c0 = st["frep"][pl.ds(q_off, 8), :][0:1, 0:1]

        qk = _row_norm_max(q) * kn_s[hh] * 1.01 + 1.0
        thr = 2.0 * qk + (fmax_s[hh, qi] - c0_s[hh, qi]) + SKIP_MARGIN
        t0h = jnp.int32(0)
        for t in range(n_blk - 1):
            skip = (t < qi) & (t0h == t) & (fmin_s[hh, t] - c0_s[hh, qi] >= thr)
            t0h = t0h + skip.astype(jnp.int32)
        t0 = jnp.minimum(t0, t0h)

        def biased(j, qq, st=st, hh=hh, c0=c0):
            off = pl.multiple_of(j * tk, tk)
            fsub = st["frep"][pl.ds(off, tk), :] - c0
            s = lax.dot_general(k_ref[0, pl.ds(off, tk), _head_cols(hh)], qq, _NT,
                                preferred_element_type=F32)
            return s - jnp.tile(fsub, (1, qq.shape[0] // HEAD_W))

        st["scores"] = functools.partial(biased, qq=q)
        st["scores_late"] = functools.partial(biased, qq=q[tk:])
        st["mask"] = mask_fn

    _sweep_wide(qi, t0, streams)

    for hh, st in enumerate(streams):
        acc = st["acc"][...]
        o_ref[0, :, _head_cols(hh)] = (acc[0:HEAD_W] / acc[HEAD_W:HEAD_W + 1]).T.astype(BF16)


def _fox_attn(z, f3):
    B, S, _ = z.shape
    tk = min(TK_B, S // 2)
    tq = 2 * tk
    nh = NH_B
    w = nh * HEAD_W
    rows = f3.shape[0] // B
    qb, kb, vb = Z_BQ // w, Z_BK // w, Z_BV // w
    per_head = ([pltpu.VMEM((S, HEAD_W), F32)] + _attn_scratch(S, tk, tq)
                + [pltpu.VMEM((tk, tq // 2), F32)])
    return pl.pallas_call(
        functools.partial(_fox_attn_kernel, tk=tk, nh=nh),
        grid=(B, HEADS // nh, S // tq),
        in_specs=[
            pl.BlockSpec((1, tq, w), lambda b, h, i: (b, i, qb + h)),
            _resident((1, S, w), lambda b, h, i: (b, 0, kb + h)),
            _resident((1, S, w), lambda b, h, i: (b, 0, vb + h)),
            pl.BlockSpec((nh, 1, S), lambda b, h, i: (b * (rows // nh) + h, 0, 0)),
        ],
        out_specs=pl.BlockSpec((1, tq, w), lambda b, h, i: (b, i, h)),
        out_shape=jax.ShapeDtypeStruct((B, S, HEADS * HEAD_W), BF16),
        scratch_shapes=([pltpu.SMEM((nh, S // tq), F32)] * 3 + [pltpu.SMEM((nh,), F32)]
                        + per_head * nh),
        compiler_params=_cparams(("parallel", "parallel", "arbitrary")),
        name="fox_attn",
    )(z, z, z, f3)


def _retention_kernel(q_ref, k_ref, v_ref, cg_ref, cos_ref, sin_ref, dm_ref, g_ref, o_ref, r_scr,
                      *, L):
    @pl.when(pl.program_id(1) == 0)
    def _():
        r_scr[...] = jnp.zeros(r_scr.shape, F32)

    cosf = cos_ref[...]
    sinf = sin_ref[...]
    pos = lax.broadcasted_iota(jnp.int32, (L, 1), 0).astype(F32)

    for h in range(HEADS):
        lg = math.log(1.0 - 2.0 ** (-5.0 - h))
        qh = q_ref[0, :, h * HEAD_W:(h + 1) * HEAD_W].astype(F32)
        kh = k_ref[0, :, h * HEAD_W:(h + 1) * HEAD_W].astype(F32)
        qr = qh * cosf + pltpu.roll(qh, HEAD_W // 2, 1) * sinf
        kr = kh * cosf + pltpu.roll(kh, HEAD_W // 2, 1) * sinf
        v = v_ref[0, :, h * C_DV:(h + 1) * C_DV]
        s = lax.dot_general(qr.astype(BF16), kr.astype(BF16), _NT,
                            preferred_element_type=F32) * dm_ref[h]
        inner = jnp.dot(s.astype(BF16), v, preferred_element_type=F32)
        r = r_scr[h]
        q_dec = qr * jnp.exp(lg * (pos + 1.0))
        cross = jnp.dot(q_dec.astype(BF16), r.astype(BF16), preferred_element_type=F32)
        k_dec = kr * jnp.exp(lg * (L - 1.0 - pos))
        r_scr[h] = math.exp(lg * L) * r + jnp.dot(k_dec.T.astype(BF16), v, preferred_element_type=F32)
        o = inner + cross
        mu = jnp.mean(o, axis=1, keepdims=True)
        var = jnp.mean(jnp.square(o - mu), axis=1, keepdims=True)
        y = (o - mu) * lax.rsqrt(var + LN_EPS) * g_ref[:, h * C_DV:(h + 1) * C_DV]
        cg = cg_ref[0, :, h * C_DV:(h + 1) * C_DV].astype(F32)
        o_ref[0, :, h * C_DV:(h + 1) * C_DV] = (y * (cg * jax.nn.sigmoid(cg))).astype(BF16)


def _retention(z, cos_t, sin_t, g_ret):
    B, S, _ = z.shape
    L = min(L_RET, S)
    qk_w, v_w = HEADS * HEAD_W, HEADS * C_DV
    dist = np.arange(L)[:, None] - np.arange(L)[None, :]
    gam = 1.0 - 2.0 ** (-5.0 - np.arange(HEADS, dtype=np.float64))
    dm = jnp.asarray(np.where(dist >= 0, gam[:, None, None] ** np.maximum(dist, 0), 0.0), F32)
    return pl.pallas_call(
        functools.partial(_retention_kernel, L=L),
        grid=(B, S // L),
        in_specs=[
            pl.BlockSpec((1, L, qk_w), lambda b, n: (b, n, Z_CQ // qk_w)),
            pl.BlockSpec((1, L, qk_w), lambda b, n: (b, n, Z_CK // qk_w)),
            pl.BlockSpec((1, L, v_w), lambda b, n: (b, n, Z_CV // v_w)),
            pl.BlockSpec((1, L, v_w), lambda b, n: (b, n, Z_CG // v_w)),
            pl.BlockSpec((L, HEAD_W), lambda b, n: (n, 0)),
            pl.BlockSpec((L, HEAD_W), lambda b, n: (n, 0)),
            _const_spec(dm.shape),
            pl.BlockSpec((1, v_w), lambda b, n: (0, 0)),
        ],
        out_specs=pl.BlockSpec((1, L, v_w), lambda b, n: (b, n, 0)),
        out_shape=jax.ShapeDtypeStruct((B, S, v_w), BF16),
        scratch_shapes=[pltpu.VMEM((HEADS, HEAD_W, C_DV), F32)],
        compiler_params=_cparams(("parallel", "arbitrary")),
        name="retention",
    )(z, z, z, z, cos_t, sin_t, dm, g_ret)


def _rope_tables(S):
    half = HEAD_W // 2
    inv = 1.0 / (ROPE_BASE ** np.linspace(0.0, 1.0, half))
    ang = np.arange(S, dtype=np.float64)[:, None] * inv[None, :]
    cos, sin = np.cos(ang), np.sin(ang)
    return (jnp.asarray(np.concatenate([cos, cos], axis=1), F32),
            jnp.asarray(np.concatenate([-sin, sin], axis=1), F32))


def _layernorm(r, g, b):
    mu = jnp.mean(r, axis=1, keepdims=True)
    var = jnp.mean(jnp.square(r - mu), axis=1, keepdims=True)
    return (r - mu) * lax.rsqrt(var + LN_EPS) * g + b


def _merge_kernel(ya_ref, yb_ref, yc_ref, ga_ref, gb_ref, gc_ref, x_ref, gt_ref, sh2_ref, sc2_ref,
                  wpa_ref, wpb_ref, wpc_ref, wout_ref, lng_ref, lnb_ref, xo_ref, ho_ref, *, alpha):
    def branch(y_ref, g_ref, w_ref):
        gate = jax.nn.sigmoid(g_ref[0].astype(F32))
        return gate * jnp.dot(y_ref[0], w_ref[...], preferred_element_type=F32)

    m = branch(ya_ref, ga_ref, wpa_ref) + branch(yb_ref, gb_ref, wpb_ref) + branch(yc_ref, gc_ref, wpc_ref)
    y = jnp.dot(m.astype(BF16), wout_ref[...], preferred_element_type=F32)
    xn = _layernorm(alpha * x_ref[0] + gt_ref[0] * y, lng_ref[...], lnb_ref[...])
    xo_ref[0] = xn
    ho_ref[0] = (xn * (1.0 + sc2_ref[0]) + sh2_ref[0]).astype(BF16)


def _const_spec(shape):
    return pl.BlockSpec(shape, lambda b, i: (0,) * len(shape), pipeline_mode=pl.Buffered(1))


def _layer_spec(stacked, l):
    shape = stacked.shape[1:]
    return pl.BlockSpec((None,) + shape, lambda b, i: (l,) + (0,) * len(shape),
                        pipeline_mode=pl.Buffered(1))


def _merge(ya, yb, yc, z, x, mod_l, wpa, wpb, wpc, wout, lng, lnb, alpha, l):
    B, S, D = x.shape
    tm = min(TM_MERGE, S)
    gblk = Z_GATES // D
    tok = lambda w, col: pl.BlockSpec((1, tm, w), lambda b, i: (b, i, col))
    modv = lambda col: pl.BlockSpec((1, 1, D), lambda b, i: (b, 0, col))
    return pl.pallas_call(
        functools.partial(_merge_kernel, alpha=alpha),
        grid=(B, S // tm),
        in_specs=[
            tok(ya.shape[2], 0), tok(yb.shape[2], 0), tok(yc.shape[2], 0),
            tok(D, gblk), tok(D, gblk + 1), tok(D, gblk + 2),
            tok(D, 0),
            modv(2), modv(3), modv(4),
            _layer_spec(wpa, l), _layer_spec(wpb, l), _layer_spec(wpc, l),
            _layer_spec(wout, l), _const_spec(lng.shape), _const_spec(lnb.shape),
        ],
        out_specs=[tok(D, 0), tok(D, 0)],
        out_shape=[jax.ShapeDtypeStruct((B, S, D), F32), jax.ShapeDtypeStruct((B, S, D), BF16)],
        compiler_params=_cparams(("parallel", "parallel")),
        name="merge_out_ln",
    )(ya, yb, yc, z, z, z, x, mod_l, mod_l, mod_l, wpa, wpb, wpc, wout, lng, lnb)


def _ffn_kernel(h_ref, x_ref, gt_ref, wup_ref, wconv_ref, bconv_ref, wdown_ref, lng_ref, lnb_ref,
                xo_ref, a_scr, tail_scr, *, alpha, cw):
    @pl.when(pl.program_id(1) == 0)
    def _():
        tail_scr[...] = jnp.zeros(tail_scr.shape, F32)

    h = h_ref[0]
    tm = h.shape[0]
    dff = a_scr.shape[1]
    row = lax.broadcasted_iota(jnp.int32, (tm, cw), 0)
    for ci in range(dff // cw):
        lo = ci * cw
        u = jnp.dot(h, wup_ref[:, lo:lo + cw], preferred_element_type=F32)
        g = jnp.dot(h, wup_ref[:, dff + lo:dff + lo + cw], preferred_element_type=F32)
        tail = tail_scr[:, lo:lo + cw]
        p1, p2 = tail[7:8], tail[6:7]
        um1 = jnp.where(row == 0, p1, pltpu.roll(u, 1, 0))
        um2 = jnp.where(row == 0, p2, jnp.where(row == 1, p1, pltpu.roll(u, 2, 0)))
        tail_scr[:, lo:lo + cw] = u[tm - 8:]
        conv = bconv_ref[:, lo:lo + cw] + um2 * wconv_ref[0:1, lo:lo + cw]
        conv = conv + um1 * wconv_ref[1:2, lo:lo + cw]
        conv = conv + u * wconv_ref[2:3, lo:lo + cw]
        act = 0.5 * conv * (1.0 + lax.erf(conv * (2.0 ** -0.5)))
        a_scr[:, lo:lo + cw] = (act * g).astype(BF16)
    y = jnp.dot(a_scr[...], wdown_ref[...], preferred_element_type=F32)
    xo_ref[0] = _layernorm(alpha * x_ref[0] + gt_ref[0] * y, lng_ref[...], lnb_ref[...])


def _ffn(h, x, mod_l, wup, wconv, bconv, wdown, lng, lnb, alpha, l):
    B, S, D = x.shape
    dff = wdown.shape[1]
    tm = min(TM_FFN, S)
    tok = pl.BlockSpec((1, tm, D), lambda b, i: (b, i, 0))
    return pl.pallas_call(
        functools.partial(_ffn_kernel, alpha=alpha, cw=CW_FFN),
        grid=(B, S // tm),
        in_specs=[
            tok, tok,
            pl.BlockSpec((1, 1, D), lambda b, i: (b, 0, 5)),
            _layer_spec(wup, l), _const_spec(wconv.shape), _const_spec(bconv.shape),
            _layer_spec(wdown, l), _const_spec(lng.shape), _const_spec(lnb.shape),
        ],
        out_specs=tok,
        out_shape=jax.ShapeDtypeStruct((B, S, D), F32),
        scratch_shapes=[pltpu.VMEM((tm, dff), BF16), pltpu.VMEM((8, dff), F32)],
        compiler_params=_cparams(("parallel", "arbitrary")),
        name="conv_ffn_ln",
    )(h, x, mod_l, wup, wconv, bconv, wdown, lng, lnb)


def _prep_in_proj(w, b):
    scale = np.ones((1, Z_WIDTH), np.float32)
    scale[:, Z_AQ:Z_AQ + 512] = A_DQK ** -0.5 * LOG2E
    scale[:, Z_BQ:Z_BQ + 512] = HEAD_W ** -0.5 * LOG2E
    scale[:, Z_CK:Z_CK + 512] = HEAD_W ** -0.5
    L, D, _ = w.shape
    tn = TN_IN
    wm = pl.pallas_call(
        functools.partial(_win_prep_kernel, first_shifted=BF_OFF // tn, shift=HEADS),
        grid=(L, Z_WIDTH // tn),
        in_specs=[
            pl.BlockSpec((None, D, tn), lambda l, j: (l, 0, j)),
            pl.BlockSpec((None, D, HEAD_W), lambda l, j: (l, 0, (j + 1) * (tn // HEAD_W))),
            pl.BlockSpec((1, tn), lambda l, j: (0, j)),
        ],
        out_specs=pl.BlockSpec((None, D, tn), lambda l, j: (l, 0, j)),
        out_shape=jax.ShapeDtypeStruct((L, D, Z_WIDTH), BF16),
        compiler_params=_cparams(("parallel", "parallel")),
        name="w_in_prep",
    )(w, w, jnp.asarray(scale))
    bm = jnp.concatenate([b[..., :BF_OFF], b[..., BF_OFF + HEADS:]], axis=-1) * scale
    pad = BF_PAD - HEADS
    wf = jnp.pad(w[..., BF_OFF:BF_OFF + HEADS], ((0, 0), (0, 0), (0, pad)))
    bf = jnp.pad(b[..., BF_OFF:BF_OFF + HEADS], ((0, 0), (0, pad)))
    return wm, bm[:, None, :], wf.astype(BF16), bf[:, None, :]


def _win_prep_kernel(w_ref, wnext_ref, s_ref, o_ref, *, first_shifted, shift):
    j = pl.program_id(1)

    @pl.when(j < first_shifted)
    def _():
        o_ref[...] = (w_ref[...] * s_ref[...]).astype(BF16)

    @pl.when(j >= first_shifted)
    def _():
        tn = w_ref.shape[1]
        wide = jnp.concatenate([w_ref[...], wnext_ref[...]], axis=1)
        moved = pltpu.roll(wide, wide.shape[1] - shift, 1)[:, :tn]
        o_ref[...] = (moved * s_ref[...]).astype(BF16)


def kernel(x, c, w_ada, b_ada, w_in, b_in, lam_q1, lam_k1, lam_q2, lam_k2, g_diff, g_ret, w_pa, w_pb, w_pc, w_out, ln_g, ln_b, w_up, w_conv, b_conv, w_down):
    B, S, D = x.shape
    depth = w_ada.shape[0]
    alpha = (2 * depth) ** 0.25
    chunk = 64
    mod = _ada(c, w_ada, b_ada)
    cos_t, sin_t = _rope_tables(S)
    wm, bm, wf, bf = _prep_in_proj(w_in, b_in)
    wpa, wpb, wpc, wout = (w.astype(BF16) for w in (w_pa, w_pb, w_pc, w_out))
    wup, wdown = w_up.astype(BF16), w_down.astype(BF16)
    for l in range(depth):
        mod_l = mod[l].reshape(B, 1, 6 * D)
        lam_init = jnp.full((1, 1), 0.8 - 0.6 * math.exp(-0.3 * l), F32)
        z, lf = _inproj(x, mod_l, wm, bm, wf, bf, l)
        f = _cumsum(lf)
        ya = _diff_attn(z, lam_q1[l][None], lam_k1[l][None], lam_q2[l][None], lam_k2[l][None],
                        lam_init, g_diff[l][None], chunk)
        yb = _fox_attn(z, f.reshape(B * f.shape[1], 1, S))
        yc = _retention(z, cos_t, sin_t, g_ret[l][None])
        x, h2 = _merge(ya, yb, yc, z, x, mod_l, wpa, wpb, wpc, wout,
                       ln_g[l, 0][None], ln_b[l, 0][None], alpha, l)
        x = _ffn(h2, x, mod_l, wup, w_conv[l], b_conv[l][None], wdown,
                 ln_g[l, 1][None], ln_b[l, 1][None], alpha, l)
    return x
```

```python
import functools
import math

import numpy as np
import jax
import jax.numpy as jnp
from jax import lax
from jax.experimental import pallas as pl
from jax.experimental.pallas import tpu as pltpu

F32 = jnp.float32
BF16 = jnp.bfloat16

HEADS = 4
A_DQK = 64
HEAD_W = 128
C_DV = 256
ROPE_BASE = 10000.0
LN_EPS = 1e-5
CONV_W = 3

Z_AQ, Z_AK, Z_AV = 0, 512, 1024
Z_BQ, Z_BK, Z_BV = 1536, 2048, 2560
Z_CQ, Z_CK, Z_CV, Z_CG = 3072, 3584, 4096, 5120
Z_GATES = 6144
Z_WIDTH = 9216
BF_OFF = 3072
BF_PAD = 128

NEG = -1e30
LOG2E = 1.4426950408889634
VT_ROWS = 144

TN_ADA = 1024
TM_IN, TN_IN = 1024, 2304
TN_PREP = 1024
TQ_A = 512
TK_B = 512
NH_A = 4
NH_B = 2
PROLOGUE_CHUNK = 512
SKIP_MARGIN = 160.0
L_RET = 256
TM_MERGE = 512
TM_FFN = 512
CW_FFN = 256

VMEM_LIMIT = 56 * 1024 * 1024


def _cparams(sem):
    return pltpu.CompilerParams(dimension_semantics=sem, vmem_limit_bytes=VMEM_LIMIT)


def _ada_kernel(ct_ref, w_ref, b_ref, o_ref):
    w = w_ref[0]
    rows = []
    for b in range(ct_ref.shape[0]):
        cb = ct_ref[b]
        ca = cb * jax.nn.sigmoid(cb)
        rows.append(jnp.sum(ca * w, axis=0, keepdims=True))
    o_ref[0] = jnp.concatenate(rows, axis=0) + b_ref[0]


def _ada(c, w_ada, b_ada):
    B, D = c.shape
    L, _, N = w_ada.shape
    tn = min(TN_ADA, N)
    return pl.pallas_call(
        _ada_kernel,
        grid=(L, N // tn),
        in_specs=[
            pl.BlockSpec((B, D, 1), lambda l, j: (0, 0, 0)),
            pl.BlockSpec((1, D, tn), lambda l, j: (l, 0, j)),
            pl.BlockSpec((1, 1, tn), lambda l, j: (l, 0, j)),
        ],
        out_specs=pl.BlockSpec((1, B, tn), lambda l, j: (l, 0, j)),
        out_shape=jax.ShapeDtypeStruct((L, B, N), F32),
        compiler_params=_cparams(("parallel", "parallel")),
        name="ada_mod",
    )(c.reshape(B, D, 1), w_ada, b_ada.reshape(L, 1, N))


def _inproj_kernel(x_ref, sh_ref, sc_ref, w_ref, b_ref, wf_ref, bf_ref, z_ref, lf_ref, h_scr):
    @pl.when(pl.program_id(2) == 0)
    def _():
        h = x_ref[0] * (1.0 + sc_ref[0]) + sh_ref[0]
        hb = h.astype(BF16)
        h_scr[...] = hb
        zf = jnp.dot(hb, wf_ref[...], preferred_element_type=F32) + bf_ref[...]
        lt = zf.T[:8]
        lf_ref[0] = jnp.minimum(lt, 0.0) - jnp.log1p(jnp.exp(-jnp.abs(lt)))

    acc = jnp.dot(h_scr[...], w_ref[...], preferred_element_type=F32)
    z_ref[0] = (acc + b_ref[...]).astype(BF16)


def _inproj(x, mod_l, w, b, wf, bf, l):
    B, S, D = x.shape
    N = w.shape[2]
    tm, tn = min(TM_IN, S), min(TN_IN, N)
    return pl.pallas_call(
        _inproj_kernel,
        grid=(B, S // tm, N // tn),
        in_specs=[
            pl.BlockSpec((1, tm, D), lambda b, i, j: (b, i, 0)),
            pl.BlockSpec((1, 1, D), lambda b, i, j: (b, 0, 0)),
            pl.BlockSpec((1, 1, D), lambda b, i, j: (b, 0, 1)),
            pl.BlockSpec((None, D, tn), lambda b, i, j: (l, 0, j)),
            pl.BlockSpec((None, 1, tn), lambda b, i, j: (l, 0, j)),
            pl.BlockSpec((None, D, BF_PAD), lambda b, i, j: (l, 0, 0)),
            pl.BlockSpec((None, 1, BF_PAD), lambda b, i, j: (l, 0, 0)),
        ],
        out_specs=[
            pl.BlockSpec((1, tm, tn), lambda b, i, j: (b, i, j)),
            pl.BlockSpec((1, 8, tm), lambda b, i, j: (b, 0, i)),
        ],
        out_shape=[
            jax.ShapeDtypeStruct((B, S, N), BF16),
            jax.ShapeDtypeStruct((B, 8, S), F32),
        ],
        scratch_shapes=[pltpu.VMEM((tm, D), BF16)],
        compiler_params=_cparams(("parallel", "parallel", "arbitrary")),
        name="in_proj",
    )(x, mod_l, mod_l, w, b, wf, bf)


def _cumsum_kernel(lf_ref, f_ref):
    x = lf_ref[0]
    lane = lax.broadcasted_iota(jnp.int32, x.shape, 1)
    k = 1
    while k < x.shape[1]:
        x = x + jnp.where(lane >= k, pltpu.roll(x, k, 1), 0.0)
        k *= 2
    f_ref[0] = x


def _cumsum(lf):
    B, R, S = lf.shape
    return pl.pallas_call(
        _cumsum_kernel,
        grid=(B,),
        in_specs=[pl.BlockSpec((1, R, S), lambda b: (b, 0, 0))],
        out_specs=pl.BlockSpec((1, R, S), lambda b: (b, 0, 0)),
        out_shape=jax.ShapeDtypeStruct((B, R, S), F32),
        compiler_params=_cparams(("parallel",)),
        name="forget_cumsum",
    )(lf)


def _build_vt(v_ref, vt_scr, tk):
    ones_row = (lax.broadcasted_iota(jnp.int32, (VT_ROWS - HEAD_W, tk), 0) == 0).astype(BF16)
    for c in range(v_ref.shape[1] // tk):
        vb = v_ref[0, c * tk:(c + 1) * tk, :].astype(F32)
        vt_scr[c, 0:HEAD_W, :] = vb.T.astype(BF16)
        vt_scr[c, HEAD_W:VT_ROWS, :] = ones_row


def _softmax_step_t(s, vt, m_scr, acc_scr):
    m_prev = m_scr[...]
    m_new = jnp.maximum(m_prev, jnp.max(s, axis=0, keepdims=True))
    p = jnp.exp2(s - m_new).astype(BF16)
    alpha = jnp.exp2(m_prev - m_new)
    acc_scr[...] = alpha * acc_scr[...] + jnp.dot(vt, p, preferred_element_type=F32)
    m_scr[...] = m_new


def _init_streams(streams, first_tile):
    for st in streams:
        st["m"][...] = jnp.full(st["m"].shape, NEG, F32)
        st["acc"][...] = jnp.zeros(st["acc"].shape, F32)
        st["sa"][...] = st["scores"](first_tile)


def _pair_body(streams):
    def pair(t, carry):
        j = 2 * t
        for st in streams:
            st["sb"][...] = st["scores"](j + 1)
            _softmax_step_t(st["sa"][...], st["vt"][j], st["m"], st["acc"])
        for st in streams:
            st["sa"][...] = st["scores"](j + 2)
            _softmax_step_t(st["sb"][...], st["vt"][j + 1], st["m"], st["acc"])
        return carry
    return pair


def _sweep(qi, streams):
    _init_streams(streams, 0)
    lax.fori_loop(0, qi // 2, _pair_body(streams), 0)

    @pl.when(qi % 2 == 0)
    def _():
        for st in streams:
            _softmax_step_t(st["mask"](st["sa"][...]), st["vt"][qi], st["m"], st["acc"])

    @pl.when(qi % 2 == 1)
    def _():
        for st in streams:
            st["sb"][...] = st["scores"](qi)
            _softmax_step_t(st["sa"][...], st["vt"][qi - 1], st["m"], st["acc"])
        for st in streams:
            _softmax_step_t(st["mask"](st["sb"][...]), st["vt"][qi], st["m"], st["acc"])


def _sweep_wide(qi, t0, streams):
    _init_streams(streams, 2 * t0)
    lax.fori_loop(t0, qi, _pair_body(streams), 0)
    for st in streams:
        st["sl"][...] = st["scores_late"](2 * qi + 1)
        _softmax_step_t(st["mask"](st["sa"][...]), st["vt"][2 * qi], st["m"], st["acc"])
    for st in streams:
        nl = st["sl"].shape[1]
        _softmax_step_t(st["mask"](st["sl"][...]), st["vt"][2 * qi + 1],
                        st["m"].at[:, nl:], st["acc"].at[:, nl:])


_NT = (((1,), (1,)), ((), ()))


def _attn_scratch(S, tk, nq):
    return [pltpu.VMEM((S // tk, VT_ROWS, tk), BF16),
            pltpu.VMEM((1, nq), F32),
            pltpu.VMEM((VT_ROWS, nq), F32),
            pltpu.VMEM((tk, nq), F32), pltpu.VMEM((tk, nq), F32)]


def _head_cols(hh):
    return slice(hh * HEAD_W, (hh + 1) * HEAD_W)


def _diff_attn_kernel(q_ref, k_ref, v_ref, lq1_ref, lk1_ref, lq2_ref, lk2_ref, li_ref, g_ref,
                      o_ref, *scr, tq, chunk, nh):
    qi = pl.program_id(2)
    per = len(scr) // nh
    names = ("vt", "m", "acc", "sa", "sb")
    streams = [dict(zip(names, scr[hh * per:(hh + 1) * per])) for hh in range(nh)]

    @pl.when(qi == 0)
    def _():
        for hh, st in enumerate(streams):
            _build_vt(v_ref.at[:, :, _head_cols(hh)], st["vt"], tq)

    def mask_fn(s):
        key = lax.broadcasted_iota(jnp.int32, s.shape, 0)
        col = lax.broadcasted_iota(jnp.int32, s.shape, 1)
        qry = jnp.where(col >= tq, col - tq, col)
        return jnp.where(key // chunk <= qry // chunk, s, NEG)

    for hh, st in enumerate(streams):
        q = q_ref[0, :, _head_cols(hh)]
        lane = lax.broadcasted_iota(jnp.int32, q.shape, 1)
        zero = jnp.zeros_like(q)
        qs = jnp.concatenate([jnp.where(lane < A_DQK, q, zero),
                              jnp.where(lane >= A_DQK, q, zero)], axis=0)

        def scores(j, qs=qs, hh=hh):
            off = pl.multiple_of(j * tq, tq)
            return lax.dot_general(k_ref[0, pl.ds(off, tq), _head_cols(hh)], qs, _NT,
                                   preferred_element_type=F32)

        st["scores"], st["mask"] = scores, mask_fn

    _sweep(qi, streams)

    lam_init = li_ref[...]
    lam = (jnp.exp(jnp.sum(lq1_ref[...] * lk1_ref[...], axis=1, keepdims=True))
           - jnp.exp(jnp.sum(lq2_ref[...] * lk2_ref[...], axis=1, keepdims=True)) + lam_init)
    for hh, st in enumerate(streams):
        acc = st["acc"][...]
        ot_all = acc[0:HEAD_W] / acc[HEAD_W:HEAD_W + 1]
        o = (ot_all[:, :tq] - lam * ot_all[:, tq:]).T
        o = o * lax.rsqrt(jnp.mean(o * o, axis=1, keepdims=True) + LN_EPS)
        o_ref[0, :, _head_cols(hh)] = (o * g_ref[:, _head_cols(hh)] * (1.0 - lam_init)).astype(BF16)


def _resident(shape, index_map):
    return pl.BlockSpec(shape, index_map, pipeline_mode=pl.Buffered(1))


def _diff_attn(z, lq1, lk1, lq2, lk2, lam_init, g_diff, chunk):
    B, S, _ = z.shape
    tq = min(TQ_A, S)
    nh = NH_A
    w = nh * HEAD_W
    qb, kb, vb = Z_AQ // w, Z_AK // w, Z_AV // w
    small = pl.BlockSpec((1, A_DQK), lambda b, h, i: (0, 0))
    return pl.pallas_call(
        functools.partial(_diff_attn_kernel, tq=tq, chunk=chunk, nh=nh),
        grid=(B, HEADS // nh, S // tq),
        in_specs=[
            pl.BlockSpec((1, tq, w), lambda b, h, i: (b, i, qb + h)),
            _resident((1, S, w), lambda b, h, i: (b, 0, kb + h)),
            _resident((1, S, w), lambda b, h, i: (b, 0, vb + h)),
            small, small, small, small,
            pl.BlockSpec((1, 1), lambda b, h, i: (0, 0)),
            pl.BlockSpec((1, w), lambda b, h, i: (0, h)),
        ],
        out_specs=pl.BlockSpec((1, tq, w), lambda b, h, i: (b, i, h)),
        out_shape=jax.ShapeDtypeStruct((B, S, HEADS * HEAD_W), BF16),
        scratch_shapes=_attn_scratch(S, tq, 2 * tq) * nh,
        compiler_params=_cparams(("parallel", "parallel", "arbitrary")),
        name="diff_attn",
    )(z, z, z, lq1, lk1, lq2, lk2, lam_init, g_diff)


def _row_norm_max(x):
    xf = x.astype(F32)
    return jnp.sqrt(jnp.max(jnp.sum(xf * xf, axis=1, keepdims=True)))


def _fox_attn_kernel(q_ref, k_ref, v_ref, f_ref, o_ref, fmin_s, fmax_s, c0_s, kn_s, *scr, tk, nh):
    qi = pl.program_id(2)
    S = k_ref.shape[1]
    tq = 2 * tk
    n_blk = S // tq
    pc = min(PROLOGUE_CHUNK, S)
    per = len(scr) // nh
    names = ("frep", "vt", "m", "acc", "sa", "sb", "sl")
    streams = [dict(zip(names, scr[hh * per:(hh + 1) * per])) for hh in range(nh)]

    @pl.when(qi == 0)
    def _():
        for hh, st in enumerate(streams):
            _build_vt(v_ref.at[:, :, _head_cols(hh)], st["vt"], tk)
            kn = jnp.float32(0.0)
            for c in range(S // pc):
                fb = jnp.broadcast_to(f_ref[hh, :, c * pc:(c + 1) * pc] * LOG2E, (HEAD_W, pc))
                st["frep"][c * pc:(c + 1) * pc, :] = fb.T
                kn = jnp.maximum(kn, _row_norm_max(k_ref[0, c * pc:(c + 1) * pc, _head_cols(hh)]))
            kn_s[hh] = kn
            for t in range(n_blk):
                fb = f_ref[hh, :, t * tq:(t + 1) * tq] * LOG2E
                fmin_s[hh, t] = jnp.min(fb)
                fmax_s[hh, t] = jnp.max(fb)
                c0_s[hh, t] = jnp.max(fb[:, 0:1])

    def mask_fn(s):
        key = lax.broadcasted_iota(jnp.int32, s.shape, 0)
        qry = lax.broadcasted_iota(jnp.int32, s.shape, 1)
        return jnp.where(key <= qry, s, NEG)

    q_off = pl.multiple_of(qi * tq, tq)
    t0 = qi
    for hh, st in enumerate(streams):
        q = q_ref[0, :, _head_cols(hh)]
        c0 = st["frep"][pl.ds(q_off, 8), :][0:1, 0:1]

        qk = _row_norm_max(q) * kn_s[hh] * 1.01 + 1.0
        thr = 2.0 * qk + (fmax_s[hh, qi] - c0_s[hh, qi]) + SKIP_MARGIN
        t0h = jnp.int32(0)
        for t in range(n_blk - 1):
            skip = (t < qi) & (t0h == t) & (fmin_s[hh, t] - c0_s[hh, qi] >= thr)
            t0h = t0h + skip.astype(jnp.int32)
        t0 = jnp.minimum(t0, t0h)

        def biased(j, qq, st=st, hh=hh, c0=c0):
            off = pl.multiple_of(j * tk, tk)
            fsub = st["frep"][pl.ds(off, tk), :] - c0
            s = lax.dot_general(k_ref[0, pl.ds(off, tk), _head_cols(hh)], qq, _NT,
                                preferred_element_type=F32)
            return s - jnp.tile(fsub, (1, qq.shape[0] // HEAD_W))

        st["scores"] = functools.partial(biased, qq=q)
        st["scores_late"] = functools.partial(biased, qq=q[tk:])
        st["mask"] = mask_fn

    _sweep_wide(qi, t0, streams)

    for hh, st in enumerate(streams):
        acc = st["acc"][...]
        o_ref[0, :, _head_cols(hh)] = (acc[0:HEAD_W] / acc[HEAD_W:HEAD_W + 1]).T.astype(BF16)


def _fox_attn(z, f3):
    B, S, _ = z.shape
    tk = min(TK_B, S // 2)
    tq = 2 * tk
    nh = NH_B
    w = nh * HEAD_W
    rows = f3.shape[0] // B
    qb, kb, vb = Z_BQ // w, Z_BK // w, Z_BV // w
    per_head = ([pltpu.VMEM((S, HEAD_W), F32)] + _attn_scratch(S, tk, tq)
                + [pltpu.VMEM((tk, tq // 2), F32)])
    return pl.pallas_call(
        functools.partial(_fox_attn_kernel, tk=tk, nh=nh),
        grid=(B, HEADS // nh, S // tq),
        in_specs=[
            pl.BlockSpec((1, tq, w), lambda b, h, i: (b, i, qb + h)),
            _resident((1, S, w), lambda b, h, i: (b, 0, kb + h)),
            _resident((1, S, w), lambda b, h, i: (b, 0, vb + h)),
            pl.BlockSpec((nh, 1, S), lambda b, h, i: (b * (rows // nh) + h, 0, 0)),
        ],
        out_specs=pl.BlockSpec((1, tq, w), lambda b, h, i: (b, i, h)),
        out_shape=jax.ShapeDtypeStruct((B, S, HEADS * HEAD_W), BF16),
        scratch_shapes=([pltpu.SMEM((nh, S // tq), F32)] * 3 + [pltpu.SMEM((nh,), F32)]
                        + per_head * nh),
        compiler_params=_cparams(("parallel", "parallel", "arbitrary")),
        name="fox_attn",
    )(z, z, z, f3)


def _retention_kernel(q_ref, k_ref, v_ref, cg_ref, cos_ref, sin_ref, dm_ref, g_ref, o_ref, r_scr,
                      *, L):
    @pl.when(pl.program_id(1) == 0)
    def _():
        r_scr[...] = jnp.zeros(r_scr.shape, F32)

    cosf = cos_ref[...]
    sinf = sin_ref[...]
    pos = lax.broadcasted_iota(jnp.int32, (L, 1), 0).astype(F32)

    for h in range(HEADS):
        lg = math.log(1.0 - 2.0 ** (-5.0 - h))
        qh = q_ref[0, :, h * HEAD_W:(h + 1) * HEAD_W].astype(F32)
        kh = k_ref[0, :, h * HEAD_W:(h + 1) * HEAD_W].astype(F32)
        qr = qh * cosf + pltpu.roll(qh, HEAD_W // 2, 1) * sinf
        kr = kh * cosf + pltpu.roll(kh, HEAD_W // 2, 1) * sinf
        v = v_ref[0, :, h * C_DV:(h + 1) * C_DV]
        s = lax.dot_general(qr.astype(BF16), kr.astype(BF16), _NT,
                            preferred_element_type=F32) * dm_ref[h]
        inner = jnp.dot(s.astype(BF16), v, preferred_element_type=F32)
        r = r_scr[h]
        q_dec = qr * jnp.exp(lg * (pos + 1.0))
        cross = jnp.dot(q_dec.astype(BF16), r.astype(BF16), preferred_element_type=F32)
        k_dec = kr * jnp.exp(lg * (L - 1.0 - pos))
        r_scr[h] = math.exp(lg * L) * r + jnp.dot(k_dec.T.astype(BF16), v, preferred_element_type=F32)
        o = inner + cross
        mu = jnp.mean(o, axis=1, keepdims=True)
        var = jnp.mean(jnp.square(o - mu), axis=1, keepdims=True)
        y = (o - mu) * lax.rsqrt(var + LN_EPS) * g_ref[:, h * C_DV:(h + 1) * C_DV]
        cg = cg_ref[0, :, h * C_DV:(h + 1) * C_DV].astype(F32)
        o_ref[0, :, h * C_DV:(h + 1) * C_DV] = (y * (cg * jax.nn.sigmoid(cg))).astype(BF16)


def _retention(z, cos_t, sin_t, g_ret):
    B, S, _ = z.shape
    L = min(L_RET, S)
    qk_w, v_w = HEADS * HEAD_W, HEADS * C_DV
    dist = np.arange(L)[:, None] - np.arange(L)[None, :]
    gam = 1.0 - 2.0 ** (-5.0 - np.arange(HEADS, dtype=np.float64))
    dm = jnp.asarray(np.where(dist >= 0, gam[:, None, None] ** np.maximum(dist, 0), 0.0), F32)
    return pl.pallas_call(
        functools.partial(_retention_kernel, L=L),
        grid=(B, S // L),
        in_specs=[
            pl.BlockSpec((1, L, qk_w), lambda b, n: (b, n, Z_CQ // qk_w)),
            pl.BlockSpec((1, L, qk_w), lambda b, n: (b, n, Z_CK // qk_w)),
            pl.BlockSpec((1, L, v_w), lambda b, n: (b, n, Z_CV // v_w)),
            pl.BlockSpec((1, L, v_w), lambda b, n: (b, n, Z_CG // v_w)),
            pl.BlockSpec((L, HEAD_W), lambda b, n: (n, 0)),
            pl.BlockSpec((L, HEAD_W), lambda b, n: (n, 0)),
            _const_spec(dm.shape),
            pl.BlockSpec((1, v_w), lambda b, n: (0, 0)),
        ],
        out_specs=pl.BlockSpec((1, L, v_w), lambda b, n: (b, n, 0)),
        out_shape=jax.ShapeDtypeStruct((B, S, v_w), BF16),
        scratch_shapes=[pltpu.VMEM((HEADS, HEAD_W, C_DV), F32)],
        compiler_params=_cparams(("parallel", "arbitrary")),
        name="retention",
    )(z, z, z, z, cos_t, sin_t, dm, g_ret)


def _rope_tables(S):
    half = HEAD_W // 2
    inv = 1.0 / (ROPE_BASE ** np.linspace(0.0, 1.0, half))
    ang = np.arange(S, dtype=np.float64)[:, None] * inv[None, :]
    cos, sin = np.cos(ang), np.sin(ang)
    return (jnp.asarray(np.concatenate([cos, cos], axis=1), F32),
            jnp.asarray(np.concatenate([-sin, sin], axis=1), F32))


def _layernorm(r, g, b):
    mu = jnp.mean(r, axis=1, keepdims=True)
    var = jnp.mean(jnp.square(r - mu), axis=1, keepdims=True)
    return (r - mu) * lax.rsqrt(var + LN_EPS) * g + b


def _merge_kernel(ya_ref, yb_ref, yc_ref, ga_ref, gb_ref, gc_ref, x_ref, gt_ref, sh2_ref, sc2_ref,
                  wpa_ref, wpb_ref, wpc_ref, wout_ref, lng_ref, lnb_ref, xo_ref, ho_ref, *, alpha):
    hs = x_ref.shape[1] // 2
    halves = [slice(0, hs), slice(hs, 2 * hs)]

    def branch(rows, y_ref, g_ref, w_ref):
        gate = jax.nn.sigmoid(g_ref[0, rows].astype(F32))
        return gate * jnp.dot(y_ref[0, rows], w_ref[...], preferred_element_type=F32)

    ms = [branch(r, ya_ref, ga_ref, wpa_ref) + branch(r, yb_ref, gb_ref, wpb_ref)
          + branch(r, yc_ref, gc_ref, wpc_ref) for r in halves]
    ys = [jnp.dot(m.astype(BF16), wout_ref[...], preferred_element_type=F32) for m in ms]
    for r, y in zip(halves, ys):
        xn = _layernorm(alpha * x_ref[0, r] + gt_ref[0] * y, lng_ref[...], lnb_ref[...])
        xo_ref[0, r] = xn
        ho_ref[0, r] = (xn * (1.0 + sc2_ref[0]) + sh2_ref[0]).astype(BF16)


def _const_spec(shape):
    return pl.BlockSpec(shape, lambda b, i: (0,) * len(shape), pipeline_mode=pl.Buffered(1))


def _layer_spec(stacked, l):
    shape = stacked.shape[1:]
    return pl.BlockSpec((None,) + shape, lambda b, i: (l,) + (0,) * len(shape),
                        pipeline_mode=pl.Buffered(1))


def _merge(ya, yb, yc, z, x, mod_l, wpa, wpb, wpc, wout, lng, lnb, alpha, l):
    B, S, D = x.shape
    tm = min(TM_MERGE, S)
    gblk = Z_GATES // D
    tok = lambda w, col: pl.BlockSpec((1, tm, w), lambda b, i: (b, i, col))
    modv = lambda col: pl.BlockSpec((1, 1, D), lambda b, i: (b, 0, col))
    return pl.pallas_call(
        functools.partial(_merge_kernel, alpha=alpha),
        grid=(B, S // tm),
        in_specs=[
            tok(ya.shape[2], 0), tok(yb.shape[2], 0), tok(yc.shape[2], 0),
            tok(D, gblk), tok(D, gblk + 1), tok(D, gblk + 2),
            tok(D, 0),
            modv(2), modv(3), modv(4),
            _layer_spec(wpa, l), _layer_spec(wpb, l), _layer_spec(wpc, l),
            _layer_spec(wout, l), _const_spec(lng.shape), _const_spec(lnb.shape),
        ],
        out_specs=[tok(D, 0), tok(D, 0)],
        out_shape=[jax.ShapeDtypeStruct((B, S, D), F32), jax.ShapeDtypeStruct((B, S, D), BF16)],
        compiler_params=_cparams(("parallel", "parallel")),
        name="merge_out_ln",
    )(ya, yb, yc, z, z, z, x, mod_l, mod_l, mod_l, wpa, wpb, wpc, wout, lng, lnb)


def _ffn_kernel(h_ref, x_ref, gt_ref, wup_ref, wconv_ref, bconv_ref, wdown_ref, lng_ref, lnb_ref,
                xo_ref, a_scr, tail_scr, *, alpha, cw):
    @pl.when(pl.program_id(1) == 0)
    def _():
        tail_scr[...] = jnp.zeros(tail_scr.shape, F32)

    h = h_ref[0]
    tm = h.shape[0]
    dff = a_scr.shape[1]
    row = lax.broadcasted_iota(jnp.int32, (tm, cw), 0)
    for ci in range(dff // cw):
        lo = ci * cw
        u = jnp.dot(h, wup_ref[:, lo:lo + cw], preferred_element_type=F32)
        g = jnp.dot(h, wup_ref[:, dff + lo:dff + lo + cw], preferred_element_type=F32)
        tail = tail_scr[:, lo:lo + cw]
        p1, p2 = tail[7:8], tail[6:7]
        um1 = jnp.where(row == 0, p1, pltpu.roll(u, 1, 0))
        um2 = jnp.where(row == 0, p2, jnp.where(row == 1, p1, pltpu.roll(u, 2, 0)))
        tail_scr[:, lo:lo + cw] = u[tm - 8:]
        conv = bconv_ref[:, lo:lo + cw] + um2 * wconv_ref[0:1, lo:lo + cw]
        conv = conv + um1 * wconv_ref[1:2, lo:lo + cw]
        conv = conv + u * wconv_ref[2:3, lo:lo + cw]
        act = 0.5 * conv * (1.0 + lax.erf(conv * (2.0 ** -0.5)))
        a_scr[:, lo:lo + cw] = (act * g).astype(BF16)
    y = jnp.dot(a_scr[...], wdown_ref[...], preferred_element_type=F32)
    xo_ref[0] = _layernorm(alpha * x_ref[0] + gt_ref[0] * y, lng_ref[...], lnb_ref[...])


def _ffn(h, x, mod_l, wup, wconv, bconv, wdown, lng, lnb, alpha, l):
    B, S, D = x.shape
    dff = wdown.shape[1]
    tm = min(TM_FFN, S)
    tok = pl.BlockSpec((1, tm, D), lambda b, i: (b, i, 0))
    return pl.pallas_call(
        functools.partial(_ffn_kernel, alpha=alpha, cw=CW_FFN),
        grid=(B, S // tm),
        in_specs=[
            tok, tok,
            pl.BlockSpec((1, 1, D), lambda b, i: (b, 0, 5)),
            _layer_spec(wup, l), _const_spec(wconv.shape), _const_spec(bconv.shape),
            _layer_spec(wdown, l), _const_spec(lng.shape), _const_spec(lnb.shape),
        ],
        out_specs=tok,
        out_shape=jax.ShapeDtypeStruct((B, S, D), F32),
        scratch_shapes=[pltpu.VMEM((tm, dff), BF16), pltpu.VMEM((8, dff), F32)],
        compiler_params=_cparams(("parallel", "arbitrary")),
        name="conv_ffn_ln",
    )(h, x, mod_l, wup, wconv, bconv, wdown, lng, lnb)


def _prep_in_proj(w, b):
    scale = np.ones((1, Z_WIDTH), np.float32)
    scale[:, Z_AQ:Z_AQ + 512] = A_DQK ** -0.5 * LOG2E
    scale[:, Z_BQ:Z_BQ + 512] = HEAD_W ** -0.5 * LOG2E
    scale[:, Z_CK:Z_CK + 512] = HEAD_W ** -0.5
    L, D, _ = w.shape
    tn = TN_PREP
    wm = pl.pallas_call(
        functools.partial(_win_prep_kernel, first_shifted=BF_OFF // tn, shift=HEADS),
        grid=(L, Z_WIDTH // tn),
        in_specs=[
            pl.BlockSpec((None, D, tn), lambda l, j: (l, 0, j)),
            pl.BlockSpec((None, D, HEAD_W), lambda l, j: (l, 0, (j + 1) * (tn // HEAD_W))),
            pl.BlockSpec((1, tn), lambda l, j: (0, j)),
        ],
        out_specs=pl.BlockSpec((None, D, tn), lambda l, j: (l, 0, j)),
        out_shape=jax.ShapeDtypeStruct((L, D, Z_WIDTH), BF16),
        compiler_params=_cparams(("parallel", "parallel")),
        name="w_in_prep",
    )(w, w, jnp.asarray(scale))
    bm = jnp.concatenate([b[..., :BF_OFF], b[..., BF_OFF + HEADS:]], axis=-1) * scale
    pad = BF_PAD - HEADS
    wf = jnp.pad(w[..., BF_OFF:BF_OFF + HEADS], ((0, 0), (0, 0), (0, pad)))
    bf = jnp.pad(b[..., BF_OFF:BF_OFF + HEADS], ((0, 0), (0, pad)))
    return wm, bm[:, None, :], wf.astype(BF16), bf[:, None, :]


def _win_prep_kernel(w_ref, wnext_ref, s_ref, o_ref, *, first_shifted, shift):
    j = pl.program_id(1)

    @pl.when(j < first_shifted)
    def _():
        o_ref[...] = (w_ref[...] * s_ref[...]).astype(BF16)

    @pl.when(j >= first_shifted)
    def _():
        tn = w_ref.shape[1]
        wide = jnp.concatenate([w_ref[...], wnext_ref[...]], axis=1)
        moved = pltpu.roll(wide, wide.shape[1] - shift, 1)[:, :tn]
        o_ref[...] = (moved * s_ref[...]).astype(BF16)


def kernel(x, c, w_ada, b_ada, w_in, b_in, lam_q1, lam_k1, lam_q2, lam_k2, g_diff, g_ret, w_pa, w_pb, w_pc, w_out, ln_g, ln_b, w_up, w_conv, b_conv, w_down):
    B, S, D = x.shape
    depth = w_ada.shape[0]
    alpha = (2 * depth) ** 0.25
    chunk = 64
    mod = _ada(c, w_ada, b_ada)
    cos_t, sin_t = _rope_tables(S)
    wm, bm, wf, bf = _prep_in_proj(w_in, b_in)
    wpa, wpb, wpc, wout = (w.astype(BF16) for w in (w_pa, w_pb, w_pc, w_out))
    wup, wdown = w_up.astype(BF16), w_down.astype(BF16)
    for l in range(depth):
        mod_l = mod[l].reshape(B, 1, 6 * D)
        lam_init = jnp.full((1, 1), 0.8 - 0.6 * math.exp(-0.3 * l), F32)
        z, lf = _inproj(x, mod_l, wm, bm, wf, bf, l)
        f = _cumsum(lf)
        ya = _diff_attn(z, lam_q1[l][None], lam_k1[l][None], lam_q2[l][None], lam_k2[l][None],
                        lam_init, g_diff[l][None], chunk)
        yb = _fox_attn(z, f.reshape(B * f.shape[1], 1, S))
        yc = _retention(z, cos_t, sin_t, g_ret[l][None])
        x, h2 = _merge(ya, yb, yc, z, x, mod_l, wpa, wpb, wpc, wout,
                       ln_g[l, 0][None], ln_b[l, 0][None], alpha, l)
        x = _ffn(h2, x, mod_l, wup, w_conv[l], b_conv[l][None], wdown,
                 ln_g[l, 1][None], ln_b[l, 1][None], alpha, l)
    return x
```

```python
import functools
import math

import numpy as np
import jax
import jax.numpy as jnp
from jax import lax
from jax.experimental import pallas as pl
from jax.experimental.pallas import tpu as pltpu

F32 = jnp.float32
BF16 = jnp.bfloat16

HEADS = 4
A_DQK = 64
HEAD_W = 128
C_DV = 256
ROPE_BASE = 10000.0
LN_EPS = 1e-5
CONV_W = 3

Z_AQ, Z_AK, Z_AV = 0, 512, 1024
Z_BQ, Z_BK, Z_BV = 1536, 2048, 2560
Z_CQ, Z_CK, Z_CV, Z_CG = 3072, 3584, 4096, 5120
Z_GATES = 6144
Z_WIDTH = 9216
BF_OFF = 3072
BF_PAD = 128

NEG = -1e30
LOG2E = 1.4426950408889634
VT_ROWS = 144

TN_ADA = 1024
TM_IN, TN_IN = 1024, 2304
TN_PREP = 1024
TQ_A = 512
TK_B = 512
NH_A = 4
NH_B = 2
PROLOGUE_CHUNK = 512
SKIP_MARGIN = 160.0
L_RET = 256
TM_MERGE = 512
TM_FFN = 512
CW_FFN = 256

VMEM_LIMIT = 56 * 1024 * 1024


def _cparams(sem):
    return pltpu.CompilerParams(dimension_semantics=sem, vmem_limit_bytes=VMEM_LIMIT)


def _ada_kernel(ct_ref, w_ref, b_ref, o_ref):
    w = w_ref[0]
    rows = []
    for b in range(ct_ref.shape[0]):
        cb = ct_ref[b]
        ca = cb * jax.nn.sigmoid(cb)
        rows.append(jnp.sum(ca * w, axis=0, keepdims=True))
    o_ref[0] = jnp.concatenate(rows, axis=0) + b_ref[0]


def _ada(c, w_ada, b_ada):
    B, D = c.shape
    L, _, N = w_ada.shape
    tn = min(TN_ADA, N)
    return pl.pallas_call(
        _ada_kernel,
        grid=(L, N // tn),
        in_specs=[
            pl.BlockSpec((B, D, 1), lambda l, j: (0, 0, 0)),
            pl.BlockSpec((1, D, tn), lambda l, j: (l, 0, j)),
            pl.BlockSpec((1, 1, tn), lambda l, j: (l, 0, j)),
        ],
        out_specs=pl.BlockSpec((1, B, tn), lambda l, j: (l, 0, j)),
        out_shape=jax.ShapeDtypeStruct((L, B, N), F32),
        compiler_params=_cparams(("parallel", "parallel")),
        name="ada_mod",
    )(c.reshape(B, D, 1), w_ada, b_ada.reshape(L, 1, N))


def _inproj_kernel(x_ref, sh_ref, sc_ref, w_ref, b_ref, wf_ref, bf_ref, z_ref, lf_ref, h_scr):
    @pl.when(pl.program_id(2) == 0)
    def _():
        h = x_ref[0] * (1.0 + sc_ref[0]) + sh_ref[0]
        hb = h.astype(BF16)
        h_scr[...] = hb
        zf = jnp.dot(hb, wf_ref[...], preferred_element_type=F32) + bf_ref[...]
        lt = zf.T[:8]
        lf_ref[0] = jnp.minimum(lt, 0.0) - jnp.log1p(jnp.exp(-jnp.abs(lt)))

    acc = jnp.dot(h_scr[...], w_ref[...], preferred_element_type=F32)
    z_ref[0] = (acc + b_ref[...]).astype(BF16)


def _inproj(x, mod_l, w, b, wf, bf, l):
    B, S, D = x.shape
    N = w.shape[2]
    tm, tn = min(TM_IN, S), min(TN_IN, N)
    return pl.pallas_call(
        _inproj_kernel,
        grid=(B, S // tm, N // tn),
        in_specs=[
            pl.BlockSpec((1, tm, D), lambda b, i, j: (b, i, 0)),
            pl.BlockSpec((1, 1, D), lambda b, i, j: (b, 0, 0)),
            pl.BlockSpec((1, 1, D), lambda b, i, j: (b, 0, 1)),
            pl.BlockSpec((None, D, tn), lambda b, i, j: (l, 0, j)),
            pl.BlockSpec((None, 1, tn), lambda b, i, j: (l, 0, j)),
            pl.BlockSpec((None, D, BF_PAD), lambda b, i, j: (l, 0, 0)),
            pl.BlockSpec((None, 1, BF_PAD), lambda b, i, j: (l, 0, 0)),
        ],
        out_specs=[
            pl.BlockSpec((1, tm, tn), lambda b, i, j: (b, i, j)),
            pl.BlockSpec((1, 8, tm), lambda b, i, j: (b, 0, i)),
        ],
        out_shape=[
            jax.ShapeDtypeStruct((B, S, N), BF16),
            jax.ShapeDtypeStruct((B, 8, S), F32),
        ],
        scratch_shapes=[pltpu.VMEM((tm, D), BF16)],
        compiler_params=_cparams(("parallel", "parallel", "arbitrary")),
        name="in_proj",
    )(x, mod_l, mod_l, w, b, wf, bf)


def _cumsum_kernel(lf_ref, f_ref):
    x = lf_ref[0]
    lane = lax.broadcasted_iota(jnp.int32, x.shape, 1)
    k = 1
    while k < x.shape[1]:
        x = x + jnp.where(lane >= k, pltpu.roll(x, k, 1), 0.0)
        k *= 2
    f_ref[0] = x


def _cumsum(lf):
    B, R, S = lf.shape
    return pl.pallas_call(
        _cumsum_kernel,
        grid=(B,),
        in_specs=[pl.BlockSpec((1, R, S), lambda b: (b, 0, 0))],
        out_specs=pl.BlockSpec((1, R, S), lambda b: (b, 0, 0)),
        out_shape=jax.ShapeDtypeStruct((B, R, S), F32),
        compiler_params=_cparams(("parallel",)),
        name="forget_cumsum",
    )(lf)


def _build_vt(v_ref, vt_scr, tk):
    ones_row = (lax.broadcasted_iota(jnp.int32, (VT_ROWS - HEAD_W, tk), 0) == 0).astype(BF16)
    for c in range(v_ref.shape[1] // tk):
        vb = v_ref[0, c * tk:(c + 1) * tk, :].astype(F32)
        vt_scr[c, 0:HEAD_W, :] = vb.T.astype(BF16)
        vt_scr[c, HEAD_W:VT_ROWS, :] = ones_row


def _softmax_step_t(s, vt, m_scr, acc_scr):
    m_prev = m_scr[...]
    m_new = jnp.maximum(m_prev, jnp.max(s, axis=0, keepdims=True))
    p = jnp.exp2(s - m_new).astype(BF16)
    alpha = jnp.exp2(m_prev - m_new)
    acc_scr[...] = alpha * acc_scr[...] + jnp.dot(vt, p, preferred_element_type=F32)
    m_scr[...] = m_new


def _init_streams(streams, first_tile):
    for st in streams:
        st["m"][...] = jnp.full(st["m"].shape, NEG, F32)
        st["acc"][...] = jnp.zeros(st["acc"].shape, F32)
        st["sa"][...] = st["scores"](first_tile)


def _pair_body(streams):
    def pair(t, carry):
        j = 2 * t
        for st in streams:
            st["sb"][...] = st["scores"](j + 1)
            _softmax_step_t(st["sa"][...], st["vt"][j], st["m"], st["acc"])
        for st in streams:
            st["sa"][...] = st["scores"](j + 2)
            _softmax_step_t(st["sb"][...], st["vt"][j + 1], st["m"], st["acc"])
        return carry
    return pair


def _sweep(qi, streams):
    _init_streams(streams, 0)
    lax.fori_loop(0, qi // 2, _pair_body(streams), 0)

    @pl.when(qi % 2 == 0)
    def _():
        for st in streams:
            _softmax_step_t(st["mask"](st["sa"][...]), st["vt"][qi], st["m"], st["acc"])

    @pl.when(qi % 2 == 1)
    def _():
        for st in streams:
            st["sb"][...] = st["scores"](qi)
            _softmax_step_t(st["sa"][...], st["vt"][qi - 1], st["m"], st["acc"])
        for st in streams:
            _softmax_step_t(st["mask"](st["sb"][...]), st["vt"][qi], st["m"], st["acc"])


def _sweep_wide(qi, j0, streams):
    for st in streams:
        st["m"][...] = jnp.full(st["m"].shape, NEG, F32)
        st["acc"][...] = jnp.zeros(st["acc"].shape, F32)

    @pl.when(j0 % 2 == 1)
    def _():
        for st in streams:
            st["sb"][...] = st["scores"](j0)
        for st in streams:
            _softmax_step_t(st["sb"][...], st["vt"][j0], st["m"], st["acc"])

    t0 = (j0 + 1) // 2
    for st in streams:
        st["sa"][...] = st["scores"](2 * t0)
    lax.fori_loop(t0, qi, _pair_body(streams), 0)
    for st in streams:
        st["sl"][...] = st["scores_late"](2 * qi + 1)
        _softmax_step_t(st["mask"](st["sa"][...]), st["vt"][2 * qi], st["m"], st["acc"])
    for st in streams:
        nl = st["sl"].shape[1]
        _softmax_step_t(st["mask"](st["sl"][...]), st["vt"][2 * qi + 1],
                        st["m"].at[:, nl:], st["acc"].at[:, nl:])


_NT = (((1,), (1,)), ((), ()))


def _attn_scratch(S, tk, nq):
    return [pltpu.VMEM((S // tk, VT_ROWS, tk), BF16),
            pltpu.VMEM((1, nq), F32),
            pltpu.VMEM((VT_ROWS, nq), F32),
            pltpu.VMEM((tk, nq), F32), pltpu.VMEM((tk, nq), F32)]


def _head_cols(hh):
    return slice(hh * HEAD_W, (hh + 1) * HEAD_W)


def _diff_attn_kernel(q_ref, k_ref, v_ref, lq1_ref, lk1_ref, lq2_ref, lk2_ref, li_ref, g_ref,
                      o_ref, *scr, tq, chunk, nh):
    qi = pl.program_id(2)
    per = len(scr) // nh
    names = ("vt", "m", "acc", "sa", "sb")
    streams = [dict(zip(names, scr[hh * per:(hh + 1) * per])) for hh in range(nh)]

    @pl.when(qi == 0)
    def _():
        for hh, st in enumerate(streams):
            _build_vt(v_ref.at[:, :, _head_cols(hh)], st["vt"], tq)

    def mask_fn(s):
        key = lax.broadcasted_iota(jnp.int32, s.shape, 0)
        col = lax.broadcasted_iota(jnp.int32, s.shape, 1)
        qry = jnp.where(col >= tq, col - tq, col)
        return jnp.where(key // chunk <= qry // chunk, s, NEG)

    for hh, st in enumerate(streams):
        q = q_ref[0, :, _head_cols(hh)]
        lane = lax.broadcasted_iota(jnp.int32, q.shape, 1)
        zero = jnp.zeros_like(q)
        qs = jnp.concatenate([jnp.where(lane < A_DQK, q, zero),
                              jnp.where(lane >= A_DQK, q, zero)], axis=0)

        def scores(j, qs=qs, hh=hh):
            off = pl.multiple_of(j * tq, tq)
            return lax.dot_general(k_ref[0, pl.ds(off, tq), _head_cols(hh)], qs, _NT,
                                   preferred_element_type=F32)

        st["scores"], st["mask"] = scores, mask_fn

    _sweep(qi, streams)

    lam_init = li_ref[...]
    lam = (jnp.exp(jnp.sum(lq1_ref[...] * lk1_ref[...], axis=1, keepdims=True))
           - jnp.exp(jnp.sum(lq2_ref[...] * lk2_ref[...], axis=1, keepdims=True)) + lam_init)
    for hh, st in enumerate(streams):
        acc = st["acc"][...]
        ot_all = acc[0:HEAD_W] / acc[HEAD_W:HEAD_W + 1]
        o = (ot_all[:, :tq] - lam * ot_all[:, tq:]).T
        o = o * lax.rsqrt(jnp.mean(o * o, axis=1, keepdims=True) + LN_EPS)
        o_ref[0, :, _head_cols(hh)] = (o * g_ref[:, _head_cols(hh)] * (1.0 - lam_init)).astype(BF16)


def _resident(shape, index_map):
    return pl.BlockSpec(shape, index_map, pipeline_mode=pl.Buffered(1))


def _diff_attn(z, lq1, lk1, lq2, lk2, lam_init, g_diff, chunk):
    B, S, _ = z.shape
    tq = min(TQ_A, S)
    nh = NH_A
    w = nh * HEAD_W
    qb, kb, vb = Z_AQ // w, Z_AK // w, Z_AV // w
    small = pl.BlockSpec((1, A_DQK), lambda b, h, i: (0, 0))
    return pl.pallas_call(
        functools.partial(_diff_attn_kernel, tq=tq, chunk=chunk, nh=nh),
        grid=(B, HEADS // nh, S // tq),
        in_specs=[
            pl.BlockSpec((1, tq, w), lambda b, h, i: (b, i, qb + h)),
            _resident((1, S, w), lambda b, h, i: (b, 0, kb + h)),
            _resident((1, S, w), lambda b, h, i: (b, 0, vb + h)),
            small, small, small, small,
            pl.BlockSpec((1, 1), lambda b, h, i: (0, 0)),
            pl.BlockSpec((1, w), lambda b, h, i: (0, h)),
        ],
        out_specs=pl.BlockSpec((1, tq, w), lambda b, h, i: (b, i, h)),
        out_shape=jax.ShapeDtypeStruct((B, S, HEADS * HEAD_W), BF16),
        scratch_shapes=_attn_scratch(S, tq, 2 * tq) * nh,
        compiler_params=_cparams(("parallel", "parallel", "arbitrary")),
        name="diff_attn",
    )(z, z, z, lq1, lk1, lq2, lk2, lam_init, g_diff)


def _row_norm_max(x):
    xf = x.astype(F32)
    return jnp.sqrt(jnp.max(jnp.sum(xf * xf, axis=1, keepdims=True)))


def _fox_attn_kernel(q_ref, k_ref, v_ref, f_ref, o_ref, fmin_s, fmax_s, c0_s, kn_s, *scr, tk, nh):
    qi = pl.program_id(2)
    S = k_ref.shape[1]
    tq = 2 * tk
    n_blk = S // tq
    pc = min(PROLOGUE_CHUNK, S)
    per = len(scr) // nh
    names = ("frep", "vt", "m", "acc", "sa", "sb", "sl")
    streams = [dict(zip(names, scr[hh * per:(hh + 1) * per])) for hh in range(nh)]

    @pl.when(qi == 0)
    def _():
        for hh, st in enumerate(streams):
            _build_vt(v_ref.at[:, :, _head_cols(hh)], st["vt"], tk)
            kn = jnp.float32(0.0)
            for c in range(S // pc):
                fb = jnp.broadcast_to(f_ref[hh, :, c * pc:(c + 1) * pc] * LOG2E, (HEAD_W, pc))
                st["frep"][c * pc:(c + 1) * pc, :] = fb.T
                kn = jnp.maximum(kn, _row_norm_max(k_ref[0, c * pc:(c + 1) * pc, _head_cols(hh)]))
            kn_s[hh] = kn
            for t in range(n_blk):
                fb = f_ref[hh, :, t * tq:(t + 1) * tq] * LOG2E
                fmax_s[hh, t] = jnp.max(fb)
                c0_s[hh, t] = jnp.max(fb[:, 0:1])
            for j in range(2 * n_blk):
                fmin_s[hh, j] = jnp.min(f_ref[hh, :, j * tk:(j + 1) * tk] * LOG2E)

    def mask_fn(s):
        key = lax.broadcasted_iota(jnp.int32, s.shape, 0)
        qry = lax.broadcasted_iota(jnp.int32, s.shape, 1)
        return jnp.where(key <= qry, s, NEG)

    q_off = pl.multiple_of(qi * tq, tq)
    j0 = 2 * qi
    for hh, st in enumerate(streams):
        q = q_ref[0, :, _head_cols(hh)]
        c0 = st["frep"][pl.ds(q_off, 8), :][0:1, 0:1]

        qk = _row_norm_max(q) * kn_s[hh] * 1.01 + 1.0
        thr = 2.0 * qk + (fmax_s[hh, qi] - c0_s[hh, qi]) + SKIP_MARGIN
        j0h = jnp.int32(0)
        for j in range(2 * n_blk - 2):
            skip = (j < 2 * qi) & (j0h == j) & (fmin_s[hh, j] - c0_s[hh, qi] >= thr)
            j0h = j0h + skip.astype(jnp.int32)
        j0 = jnp.minimum(j0, j0h)

        def biased(j, qq, st=st, hh=hh, c0=c0):
            off = pl.multiple_of(j * tk, tk)
            fsub = st["frep"][pl.ds(off, tk), :] - c0
            s = lax.dot_general(k_ref[0, pl.ds(off, tk), _head_cols(hh)], qq, _NT,
                                preferred_element_type=F32)
            return s - jnp.tile(fsub, (1, qq.shape[0] // HEAD_W))

        st["scores"] = functools.partial(biased, qq=q)
        st["scores_late"] = functools.partial(biased, qq=q[tk:])
        st["mask"] = mask_fn

    _sweep_wide(qi, j0, streams)

    for hh, st in enumerate(streams):
        acc = st["acc"][...]
        o_ref[0, :, _head_cols(hh)] = (acc[0:HEAD_W] / acc[HEAD_W:HEAD_W + 1]).T.astype(BF16)


def _fox_attn(z, f3):
    B, S, _ = z.shape
    tk = min(TK_B, S // 2)
    tq = 2 * tk
    nh = NH_B
    w = nh * HEAD_W
    rows = f3.shape[0] // B
    qb, kb, vb = Z_BQ // w, Z_BK // w, Z_BV // w
    per_head = ([pltpu.VMEM((S, HEAD_W), F32)] + _attn_scratch(S, tk, tq)
                + [pltpu.VMEM((tk, tq // 2), F32)])
    return pl.pallas_call(
        functools.partial(_fox_attn_kernel, tk=tk, nh=nh),
        grid=(B, HEADS // nh, S // tq),
        in_specs=[
            pl.BlockSpec((1, tq, w), lambda b, h, i: (b, i, qb + h)),
            _resident((1, S, w), lambda b, h, i: (b, 0, kb + h)),
            _resident((1, S, w), lambda b, h, i: (b, 0, vb + h)),
            pl.BlockSpec((nh, 1, S), lambda b, h, i: (b * (rows // nh) + h, 0, 0)),
        ],
        out_specs=pl.BlockSpec((1, tq, w), lambda b, h, i: (b, i, h)),
        out_shape=jax.ShapeDtypeStruct((B, S, HEADS * HEAD_W), BF16),
        scratch_shapes=([pltpu.SMEM((nh, S // tk), F32)] + [pltpu.SMEM((nh, S // tq), F32)] * 2
                        + [pltpu.SMEM((nh,), F32)]
                        + per_head * nh),
        compiler_params=_cparams(("parallel", "parallel", "arbitrary")),
        name="fox_attn",
    )(z, z, z, f3)


def _retention_kernel(q_ref, k_ref, v_ref, cg_ref, cos_ref, sin_ref, dm_ref, g_ref, o_ref, r_scr,
                      *, L):
    @pl.when(pl.program_id(1) == 0)
    def _():
        r_scr[...] = jnp.zeros(r_scr.shape, F32)

    cosf = cos_ref[...]
    sinf = sin_ref[...]
    pos = lax.broadcasted_iota(jnp.int32, (L, 1), 0).astype(F32)

    for h in range(HEADS):
        lg = math.log(1.0 - 2.0 ** (-5.0 - h))
        qh = q_ref[0, :, h * HEAD_W:(h + 1) * HEAD_W].astype(F32)
        kh = k_ref[0, :, h * HEAD_W:(h + 1) * HEAD_W].astype(F32)
        qr = qh * cosf + pltpu.roll(qh, HEAD_W // 2, 1) * sinf
        kr = kh * cosf + pltpu.roll(kh, HEAD_W // 2, 1) * sinf
        v = v_ref[0, :, h * C_DV:(h + 1) * C_DV]
        s = lax.dot_general(qr.astype(BF16), kr.astype(BF16), _NT,
                            preferred_element_type=F32) * dm_ref[h]
        inner = jnp.dot(s.astype(BF16), v, preferred_element_type=F32)
        r = r_scr[h]
        q_dec = qr * jnp.exp(lg * (pos + 1.0))
        cross = jnp.dot(q_dec.astype(BF16), r.astype(BF16), preferred_element_type=F32)
        k_dec = kr * jnp.exp(lg * (L - 1.0 - pos))
        r_scr[h] = math.exp(lg * L) * r + jnp.dot(k_dec.T.astype(BF16), v, preferred_element_type=F32)
        o = inner + cross
        mu = jnp.mean(o, axis=1, keepdims=True)
        var = jnp.mean(jnp.square(o - mu), axis=1, keepdims=True)
        y = (o - mu) * lax.rsqrt(var + LN_EPS) * g_ref[:, h * C_DV:(h + 1) * C_DV]
        cg = cg_ref[0, :, h * C_DV:(h + 1) * C_DV].astype(F32)
        o_ref[0, :, h * C_DV:(h + 1) * C_DV] = (y * (cg * jax.nn.sigmoid(cg))).astype(BF16)


def _retention(z, cos_t, sin_t, g_ret):
    B, S, _ = z.shape
    L = min(L_RET, S)
    qk_w, v_w = HEADS * HEAD_W, HEADS * C_DV
    dist = np.arange(L)[:, None] - np.arange(L)[None, :]
    gam = 1.0 - 2.0 ** (-5.0 - np.arange(HEADS, dtype=np.float64))
    dm = jnp.asarray(np.where(dist >= 0, gam[:, None, None] ** np.maximum(dist, 0), 0.0), F32)
    return pl.pallas_call(
        functools.partial(_retention_kernel, L=L),
        grid=(B, S // L),
        in_specs=[
            pl.BlockSpec((1, L, qk_w), lambda b, n: (b, n, Z_CQ // qk_w)),
            pl.BlockSpec((1, L, qk_w), lambda b, n: (b, n, Z_CK // qk_w)),
            pl.BlockSpec((1, L, v_w), lambda b, n: (b, n, Z_CV // v_w)),
            pl.BlockSpec((1, L, v_w), lambda b, n: (b, n, Z_CG // v_w)),
            pl.BlockSpec((L, HEAD_W), lambda b, n: (n, 0)),
            pl.BlockSpec((L, HEAD_W), lambda b, n: (n, 0)),
            _const_spec(dm.shape),
            pl.BlockSpec((1, v_w), lambda b, n: (0, 0)),
        ],
        out_specs=pl.BlockSpec((1, L, v_w), lambda b, n: (b, n, 0)),
        out_shape=jax.ShapeDtypeStruct((B, S, v_w), BF16),
        scratch_shapes=[pltpu.VMEM((HEADS, HEAD_W, C_DV), F32)],
        compiler_params=_cparams(("parallel", "arbitrary")),
        name="retention",
    )(z, z, z, z, cos_t, sin_t, dm, g_ret)


def _rope_tables(S):
    half = HEAD_W // 2
    inv = 1.0 / (ROPE_BASE ** np.linspace(0.0, 1.0, half))
    ang = np.arange(S, dtype=np.float64)[:, None] * inv[None, :]
    cos, sin = np.cos(ang), np.sin(ang)
    return (jnp.asarray(np.concatenate([cos, cos], axis=1), F32),
            jnp.asarray(np.concatenate([-sin, sin], axis=1), F32))


def _layernorm(r, g, b):
    mu = jnp.mean(r, axis=1, keepdims=True)
    var = jnp.mean(jnp.square(r - mu), axis=1, keepdims=True)
    return (r - mu) * lax.rsqrt(var + LN_EPS) * g + b


def _merge_kernel(ya_ref, yb_ref, yc_ref, ga_ref, gb_ref, gc_ref, x_ref, gt_ref, sh2_ref, sc2_ref,
                  wpa_ref, wpb_ref, wpc_ref, wout_ref, lng_ref, lnb_ref, xo_ref, ho_ref, *, alpha):
    hs = x_ref.shape[1] // 2
    halves = [slice(0, hs), slice(hs, 2 * hs)]

    def branch(rows, y_ref, g_ref, w_ref):
        gate = jax.nn.sigmoid(g_ref[0, rows].astype(F32))
        return gate * jnp.dot(y_ref[0, rows], w_ref[...], preferred_element_type=F32)

    ms = [branch(r, ya_ref, ga_ref, wpa_ref) + branch(r, yb_ref, gb_ref, wpb_ref)
          + branch(r, yc_ref, gc_ref, wpc_ref) for r in halves]
    ys = [jnp.dot(m.astype(BF16), wout_ref[...], preferred_element_type=F32) for m in ms]
    for r, y in zip(halves, ys):
        xn = _layernorm(alpha * x_ref[0, r] + gt_ref[0] * y, lng_ref[...], lnb_ref[...])
        xo_ref[0, r] = xn
        ho_ref[0, r] = (xn * (1.0 + sc2_ref[0]) + sh2_ref[0]).astype(BF16)


def _const_spec(shape):
    return pl.BlockSpec(shape, lambda b, i: (0,) * len(shape), pipeline_mode=pl.Buffered(1))


def _layer_spec(stacked, l):
    shape = stacked.shape[1:]
    return pl.BlockSpec((None,) + shape, lambda b, i: (l,) + (0,) * len(shape),
                        pipeline_mode=pl.Buffered(1))


def _merge(ya, yb, yc, z, x, mod_l, wpa, wpb, wpc, wout, lng, lnb, alpha, l):
    B, S, D = x.shape
    tm = min(TM_MERGE, S)
    gblk = Z_GATES // D
    tok = lambda w, col: pl.BlockSpec((1, tm, w), lambda b, i: (b, i, col))
    modv = lambda col: pl.BlockSpec((1, 1, D), lambda b, i: (b, 0, col))
    return pl.pallas_call(
        functools.partial(_merge_kernel, alpha=alpha),
        grid=(B, S // tm),
        in_specs=[
            tok(ya.shape[2], 0), tok(yb.shape[2], 0), tok(yc.shape[2], 0),
            tok(D, gblk), tok(D, gblk + 1), tok(D, gblk + 2),
            tok(D, 0),
            modv(2), modv(3), modv(4),
            _layer_spec(wpa, l), _layer_spec(wpb, l), _layer_spec(wpc, l),
            _layer_spec(wout, l), _const_spec(lng.shape), _const_spec(lnb.shape),
        ],
        out_specs=[tok(D, 0), tok(D, 0)],
        out_shape=[jax.ShapeDtypeStruct((B, S, D), F32), jax.ShapeDtypeStruct((B, S, D), BF16)],
        compiler_params=_cparams(("parallel", "parallel")),
        name="merge_out_ln",
    )(ya, yb, yc, z, z, z, x, mod_l, mod_l, mod_l, wpa, wpb, wpc, wout, lng, lnb)


def _ffn_kernel(h_ref, x_ref, gt_ref, wup_ref, wconv_ref, bconv_ref, wdown_ref, lng_ref, lnb_ref,
                xo_ref, a_scr, tail_scr, *, alpha, cw):
    @pl.when(pl.program_id(1) == 0)
    def _():
        tail_scr[...] = jnp.zeros(tail_scr.shape, F32)

    h = h_ref[0]
    tm = h.shape[0]
    dff = a_scr.shape[1]
    row = lax.broadcasted_iota(jnp.int32, (tm, cw), 0)
    for ci in range(dff // cw):
        lo = ci * cw
        u = jnp.dot(h, wup_ref[:, lo:lo + cw], preferred_element_type=F32)
        g = jnp.dot(h, wup_ref[:, dff + lo:dff + lo + cw], preferred_element_type=F32)
        tail = tail_scr[:, lo:lo + cw]
        p1, p2 = tail[7:8], tail[6:7]
        um1 = jnp.where(row == 0, p1, pltpu.roll(u, 1, 0))
        um2 = jnp.where(row == 0, p2, jnp.where(row == 1, p1, pltpu.roll(u, 2, 0)))
        tail_scr[:, lo:lo + cw] = u[tm - 8:]
        conv = bconv_ref[:, lo:lo + cw] + um2 * wconv_ref[0:1, lo:lo + cw]
        conv = conv + um1 * wconv_ref[1:2, lo:lo + cw]
        conv = conv + u * wconv_ref[2:3, lo:lo + cw]
        act = 0.5 * conv * (1.0 + lax.erf(conv * (2.0 ** -0.5)))
        a_scr[:, lo:lo + cw] = (act * g).astype(BF16)
    y = jnp.dot(a_scr[...], wdown_ref[...], preferred_element_type=F32)
    xo_ref[0] = _layernorm(alpha * x_ref[0] + gt_ref[0] * y, lng_ref[...], lnb_ref[...])


def _ffn(h, x, mod_l, wup, wconv, bconv, wdown, lng, lnb, alpha, l):
    B, S, D = x.shape
    dff = wdown.shape[1]
    tm = min(TM_FFN, S)
    tok = pl.BlockSpec((1, tm, D), lambda b, i: (b, i, 0))
    return pl.pallas_call(
        functools.partial(_ffn_kernel, alpha=alpha, cw=CW_FFN),
        grid=(B, S // tm),
        in_specs=[
            tok, tok,
            pl.BlockSpec((1, 1, D), lambda b, i: (b, 0, 5)),
            _layer_spec(wup, l), _const_spec(wconv.shape), _const_spec(bconv.shape),
            _layer_spec(wdown, l), _const_spec(lng.shape), _const_spec(lnb.shape),
        ],
        out_specs=tok,
        out_shape=jax.ShapeDtypeStruct((B, S, D), F32),
        scratch_shapes=[pltpu.VMEM((tm, dff), BF16), pltpu.VMEM((8, dff), F32)],
        compiler_params=_cparams(("parallel", "arbitrary")),
        name="conv_ffn_ln",
    )(h, x, mod_l, wup, wconv, bconv, wdown, lng, lnb)


def _prep_in_proj(w, b):
    scale = np.ones((1, Z_WIDTH), np.float32)
    scale[:, Z_AQ:Z_AQ + 512] = A_DQK ** -0.5 * LOG2E
    scale[:, Z_BQ:Z_BQ + 512] = HEAD_W ** -0.5 * LOG2E
    scale[:, Z_CK:Z_CK + 512] = HEAD_W ** -0.5
    L, D, _ = w.shape
    tn = TN_PREP
    wm = pl.pallas_call(
        functools.partial(_win_prep_kernel, first_shifted=BF_OFF // tn, shift=HEADS),
        grid=(L, Z_WIDTH // tn),
        in_specs=[
            pl.BlockSpec((None, D, tn), lambda l, j: (l, 0, j)),
            pl.BlockSpec((None, D, HEAD_W), lambda l, j: (l, 0, (j + 1) * (tn // HEAD_W))),
            pl.BlockSpec((1, tn), lambda l, j: (0, j)),
        ],
        out_specs=pl.BlockSpec((None, D, tn), lambda l, j: (l, 0, j)),
        out_shape=jax.ShapeDtypeStruct((L, D, Z_WIDTH), BF16),
        compiler_params=_cparams(("parallel", "parallel")),
        name="w_in_prep",
    )(w, w, jnp.asarray(scale))
    bm = jnp.concatenate([b[..., :BF_OFF], b[..., BF_OFF + HEADS:]], axis=-1) * scale
    pad = BF_PAD - HEADS
    wf = jnp.pad(w[..., BF_OFF:BF_OFF + HEADS], ((0, 0), (0, 0), (0, pad)))
    bf = jnp.pad(b[..., BF_OFF:BF_OFF + HEADS], ((0, 0), (0, pad)))
    return wm, bm[:, None, :], wf.astype(BF16), bf[:, None, :]


def _win_prep_kernel(w_ref, wnext_ref, s_ref, o_ref, *, first_shifted, shift):
    j = pl.program_id(1)

    @pl.when(j < first_shifted)
    def _():
        o_ref[...] = (w_ref[...] * s_ref[...]).astype(BF16)

    @pl.when(j >= first_shifted)
    def _():
        tn = w_ref.shape[1]
        wide = jnp.concatenate([w_ref[...], wnext_ref[...]], axis=1)
        moved = pltpu.roll(wide, wide.shape[1] - shift, 1)[:, :tn]
        o_ref[...] = (moved * s_ref[...]).astype(BF16)


def kernel(x, c, w_ada, b_ada, w_in, b_in, lam_q1, lam_k1, lam_q2, lam_k2, g_diff, g_ret, w_pa, w_pb, w_pc, w_out, ln_g, ln_b, w_up, w_conv, b_conv, w_down):
    B, S, D = x.shape
    depth = w_ada.shape[0]
    alpha = (2 * depth) ** 0.25
    chunk = 64
    mod = _ada(c, w_ada, b_ada)
    cos_t, sin_t = _rope_tables(S)
    wm, bm, wf, bf = _prep_in_proj(w_in, b_in)
    wpa, wpb, wpc, wout = (w.astype(BF16) for w in (w_pa, w_pb, w_pc, w_out))
    wup, wdown = w_up.astype(BF16), w_down.astype(BF16)
    for l in range(depth):
        mod_l = mod[l].reshape(B, 1, 6 * D)
        lam_init = jnp.full((1, 1), 0.8 - 0.6 * math.exp(-0.3 * l), F32)
        z, lf = _inproj(x, mod_l, wm, bm, wf, bf, l)
        f = _cumsum(lf)
        ya = _diff_attn(z, lam_q1[l][None], lam_k1[l][None], lam_q2[l][None], lam_k2[l][None],
                        lam_init, g_diff[l][None], chunk)
        yb = _fox_attn(z, f.reshape(B * f.shape[1], 1, S))
        yc = _retention(z, cos_t, sin_t, g_ret[l][None])
        x, h2 = _merge(ya, yb, yc, z, x, mod_l, wpa, wpb, wpc, wout,
                       ln_g[l, 0][None], ln_b[l, 0][None], alpha, l)
        x = _ffn(h2, x, mod_l, wup, w_conv[l], b_conv[l][None], wdown,
                 ln_g[l, 1][None], ln_b[l, 1][None], alpha, l)
    return x
```

```python
import functools
import math

import numpy as np
import jax
import jax.numpy as jnp
from jax import lax
from jax.experimental import pallas as pl
from jax.experimental.pallas import tpu as pltpu

F32 = jnp.float32
BF16 = jnp.bfloat16

HEADS = 4
A_DQK = 64
HEAD_W = 128
C_DV = 256
ROPE_BASE = 10000.0
LN_EPS = 1e-5
CONV_W = 3

Z_AQ, Z_AK, Z_AV = 0, 512, 1024
Z_BQ, Z_BK, Z_BV = 1536, 2048, 2560
Z_CQ, Z_CK, Z_CV, Z_CG = 3072, 3584, 4096, 5120
Z_GATES = 6144
Z_WIDTH = 9216
BF_OFF = 3072
BF_PAD = 128

NEG = -1e30
LOG2E = 1.4426950408889634
VT_ROWS = 144

TN_ADA = 1024
TM_IN, TN_IN = 1024, 2304
TN_PREP = 1024
TQ_A = 512
TK_B = 512
NH_A = 4
NH_B = 2
PROLOGUE_CHUNK = 512
SKIP_MARGIN = 160.0
L_RET = 256
TM_MERGE = 512
TM_FFN = 512
CW_FFN = 256

VMEM_LIMIT = 56 * 1024 * 1024
VMEM_LIMIT_ATTN_A = 60 * 1024 * 1024


def _cparams(sem, vmem_limit=VMEM_LIMIT):
    return pltpu.CompilerParams(dimension_semantics=sem, vmem_limit_bytes=vmem_limit)


def _ada_kernel(ct_ref, w_ref, b_ref, o_ref):
    w = w_ref[0]
    rows = []
    for b in range(ct_ref.shape[0]):
        cb = ct_ref[b]
        ca = cb * jax.nn.sigmoid(cb)
        rows.append(jnp.sum(ca * w, axis=0, keepdims=True))
    o_ref[0] = jnp.concatenate(rows, axis=0) + b_ref[0]


def _ada(c, w_ada, b_ada):
    B, D = c.shape
    L, _, N = w_ada.shape
    tn = min(TN_ADA, N)
    return pl.pallas_call(
        _ada_kernel,
        grid=(L, N // tn),
        in_specs=[
            pl.BlockSpec((B, D, 1), lambda l, j: (0, 0, 0)),
            pl.BlockSpec((1, D, tn), lambda l, j: (l, 0, j)),
            pl.BlockSpec((1, 1, tn), lambda l, j: (l, 0, j)),
        ],
        out_specs=pl.BlockSpec((1, B, tn), lambda l, j: (l, 0, j)),
        out_shape=jax.ShapeDtypeStruct((L, B, N), F32),
        compiler_params=_cparams(("parallel", "parallel")),
        name="ada_mod",
    )(c.reshape(B, D, 1), w_ada, b_ada.reshape(L, 1, N))


def _inproj_kernel(x_ref, sh_ref, sc_ref, w_ref, b_ref, wf_ref, bf_ref, z_ref, lf_ref, h_scr):
    @pl.when(pl.program_id(2) == 0)
    def _():
        h = x_ref[0] * (1.0 + sc_ref[0]) + sh_ref[0]
        hb = h.astype(BF16)
        h_scr[...] = hb
        zf = jnp.dot(hb, wf_ref[...], preferred_element_type=F32) + bf_ref[...]
        lt = zf.T[:8]
        lf_ref[0] = jnp.minimum(lt, 0.0) - jnp.log1p(jnp.exp(-jnp.abs(lt)))

    acc = jnp.dot(h_scr[...], w_ref[...], preferred_element_type=F32)
    z_ref[0] = (acc + b_ref[...]).astype(BF16)


def _inproj(x, mod_l, w, b, wf, bf, l):
    B, S, D = x.shape
    N = w.shape[2]
    tm, tn = min(TM_IN, S), min(TN_IN, N)
    return pl.pallas_call(
        _inproj_kernel,
        grid=(B, S // tm, N // tn),
        in_specs=[
            pl.BlockSpec((1, tm, D), lambda b, i, j: (b, i, 0)),
            pl.BlockSpec((1, 1, D), lambda b, i, j: (b, 0, 0)),
            pl.BlockSpec((1, 1, D), lambda b, i, j: (b, 0, 1)),
            pl.BlockSpec((None, D, tn), lambda b, i, j: (l, 0, j)),
            pl.BlockSpec((None, 1, tn), lambda b, i, j: (l, 0, j)),
            pl.BlockSpec((None, D, BF_PAD), lambda b, i, j: (l, 0, 0)),
            pl.BlockSpec((None, 1, BF_PAD), lambda b, i, j: (l, 0, 0)),
        ],
        out_specs=[
            pl.BlockSpec((1, tm, tn), lambda b, i, j: (b, i, j)),
            pl.BlockSpec((1, 8, tm), lambda b, i, j: (b, 0, i)),
        ],
        out_shape=[
            jax.ShapeDtypeStruct((B, S, N), BF16),
            jax.ShapeDtypeStruct((B, 8, S), F32),
        ],
        scratch_shapes=[pltpu.VMEM((tm, D), BF16)],
        compiler_params=_cparams(("parallel", "parallel", "arbitrary")),
        name="in_proj",
    )(x, mod_l, mod_l, w, b, wf, bf)


def _cumsum_kernel(lf_ref, f_ref):
    x = lf_ref[0]
    lane = lax.broadcasted_iota(jnp.int32, x.shape, 1)
    k = 1
    while k < x.shape[1]:
        x = x + jnp.where(lane >= k, pltpu.roll(x, k, 1), 0.0)
        k *= 2
    f_ref[0] = x


def _cumsum(lf):
    B, R, S = lf.shape
    return pl.pallas_call(
        _cumsum_kernel,
        grid=(B,),
        in_specs=[pl.BlockSpec((1, R, S), lambda b: (b, 0, 0))],
        out_specs=pl.BlockSpec((1, R, S), lambda b: (b, 0, 0)),
        out_shape=jax.ShapeDtypeStruct((B, R, S), F32),
        compiler_params=_cparams(("parallel",)),
        name="forget_cumsum",
    )(lf)


def _build_vt(v_ref, vt_scr, tk):
    ones_row = (lax.broadcasted_iota(jnp.int32, (VT_ROWS - HEAD_W, tk), 0) == 0).astype(BF16)
    for c in range(v_ref.shape[1] // tk):
        vb = v_ref[0, c * tk:(c + 1) * tk, :].astype(F32)
        vt_scr[c, 0:HEAD_W, :] = vb.T.astype(BF16)
        vt_scr[c, HEAD_W:VT_ROWS, :] = ones_row


def _softmax_step_t(s, vt, m_scr, acc_scr):
    m_prev = m_scr[...]
    m_new = jnp.maximum(m_prev, jnp.max(s, axis=0, keepdims=True))
    p = jnp.exp2(s - m_new).astype(BF16)
    alpha = jnp.exp2(m_prev - m_new)
    acc_scr[...] = alpha * acc_scr[...] + jnp.dot(vt, p, preferred_element_type=F32)
    m_scr[...] = m_new


def _init_streams(streams, first_tile):
    for st in streams:
        st["m"][...] = jnp.full(st["m"].shape, NEG, F32)
        st["acc"][...] = jnp.zeros(st["acc"].shape, F32)
        st["sa"][...] = st["scores"](first_tile)


def _pair_body(streams):
    def pair(t, carry):
        j = 2 * t
        for st in streams:
            st["sb"][...] = st["scores"](j + 1)
            _softmax_step_t(st["sa"][...], st["vt"][j], st["m"], st["acc"])
        for st in streams:
            st["sa"][...] = st["scores"](j + 2)
            _softmax_step_t(st["sb"][...], st["vt"][j + 1], st["m"], st["acc"])
        return carry
    return pair


def _sweep(qi, streams):
    _init_streams(streams, 0)
    lax.fori_loop(0, qi // 2, _pair_body(streams), 0)

    @pl.when(qi % 2 == 0)
    def _():
        for st in streams:
            _softmax_step_t(st["mask"](st["sa"][...]), st["vt"][qi], st["m"], st["acc"])

    @pl.when(qi % 2 == 1)
    def _():
        for st in streams:
            st["sb"][...] = st["scores"](qi)
            _softmax_step_t(st["sa"][...], st["vt"][qi - 1], st["m"], st["acc"])
        for st in streams:
            _softmax_step_t(st["mask"](st["sb"][...]), st["vt"][qi], st["m"], st["acc"])


def _sweep_wide(qi, j0, streams):
    for st in streams:
        st["m"][...] = jnp.full(st["m"].shape, NEG, F32)
        st["acc"][...] = jnp.zeros(st["acc"].shape, F32)

    @pl.when(j0 % 2 == 1)
    def _():
        for st in streams:
            st["sb"][...] = st["scores"](j0)
        for st in streams:
            st["sa"][...] = st["scores"](j0 + 1)
            _softmax_step_t(st["sb"][...], st["vt"][j0], st["m"], st["acc"])

    @pl.when(j0 % 2 == 0)
    def _():
        for st in streams:
            st["sa"][...] = st["scores"](j0)

    lax.fori_loop((j0 + 1) // 2, qi, _pair_body(streams), 0)
    for st in streams:
        st["sl"][...] = st["scores_late"](2 * qi + 1)
        _softmax_step_t(st["mask"](st["sa"][...]), st["vt"][2 * qi], st["m"], st["acc"])
    for st in streams:
        nl = st["sl"].shape[1]
        _softmax_step_t(st["mask"](st["sl"][...]), st["vt"][2 * qi + 1],
                        st["m"].at[:, nl:], st["acc"].at[:, nl:])


_NT = (((1,), (1,)), ((), ()))


def _attn_scratch(S, tk, nq):
    return [pltpu.VMEM((S // tk, VT_ROWS, tk), BF16),
            pltpu.VMEM((1, nq), F32),
            pltpu.VMEM((VT_ROWS, nq), F32),
            pltpu.VMEM((tk, nq), F32), pltpu.VMEM((tk, nq), F32)]


def _head_cols(hh):
    return slice(hh * HEAD_W, (hh + 1) * HEAD_W)


def _diff_attn_kernel(q_ref, k_ref, v_ref, lq1_ref, lk1_ref, lq2_ref, lk2_ref, li_ref, g_ref,
                      o_ref, *scr, tq, chunk, nh):
    qi = pl.program_id(2)
    per = len(scr) // nh
    names = ("vt", "m", "acc", "sa", "sb")
    streams = [dict(zip(names, scr[hh * per:(hh + 1) * per])) for hh in range(nh)]

    @pl.when(qi == 0)
    def _():
        for hh, st in enumerate(streams):
            _build_vt(v_ref.at[:, :, _head_cols(hh)], st["vt"], tq)

    def mask_fn(s):
        key = lax.broadcasted_iota(jnp.int32, s.shape, 0)
        col = lax.broadcasted_iota(jnp.int32, s.shape, 1)
        qry = jnp.where(col >= tq, col - tq, col)
        return jnp.where(key // chunk <= qry // chunk, s, NEG)

    for hh, st in enumerate(streams):
        q = q_ref[0, :, _head_cols(hh)]
        lane = lax.broadcasted_iota(jnp.int32, q.shape, 1)
        zero = jnp.zeros_like(q)
        qs = jnp.concatenate([jnp.where(lane < A_DQK, q, zero),
                              jnp.where(lane >= A_DQK, q, zero)], axis=0)

        def scores(j, qs=qs, hh=hh):
            off = pl.multiple_of(j * tq, tq)
            return lax.dot_general(k_ref[0, pl.ds(off, tq), _head_cols(hh)], qs, _NT,
                                   preferred_element_type=F32)

        st["scores"], st["mask"] = scores, mask_fn

    _sweep(qi, streams)

    lam_init = li_ref[...]
    lam = (jnp.exp(jnp.sum(lq1_ref[...] * lk1_ref[...], axis=1, keepdims=True))
           - jnp.exp(jnp.sum(lq2_ref[...] * lk2_ref[...], axis=1, keepdims=True)) + lam_init)
    for hh, st in enumerate(streams):
        acc = st["acc"][...]
        ot_all = acc[0:HEAD_W] / acc[HEAD_W:HEAD_W + 1]
        o = (ot_all[:, :tq] - lam * ot_all[:, tq:]).T
        o = o * lax.rsqrt(jnp.mean(o * o, axis=1, keepdims=True) + LN_EPS)
        o_ref[0, :, _head_cols(hh)] = (o * g_ref[:, _head_cols(hh)] * (1.0 - lam_init)).astype(BF16)


def _resident(shape, index_map):
    return pl.BlockSpec(shape, index_map, pipeline_mode=pl.Buffered(1))


def _diff_attn(z, lq1, lk1, lq2, lk2, lam_init, g_diff, chunk):
    B, S, _ = z.shape
    tq = min(TQ_A, S)
    nh = NH_A
    w = nh * HEAD_W
    qb, kb, vb = Z_AQ // w, Z_AK // w, Z_AV // w
    small = pl.BlockSpec((1, A_DQK), lambda b, h, i: (0, 0))
    return pl.pallas_call(
        functools.partial(_diff_attn_kernel, tq=tq, chunk=chunk, nh=nh),
        grid=(B, HEADS // nh, S // tq),
        in_specs=[
            pl.BlockSpec((1, tq, w), lambda b, h, i: (b, i, qb + h)),
            pl.BlockSpec((1, S, w), lambda b, h, i: (b, 0, kb + h)),
            _resident((1, S, w), lambda b, h, i: (b, 0, vb + h)),
            small, small, small, small,
            pl.BlockSpec((1, 1), lambda b, h, i: (0, 0)),
            pl.BlockSpec((1, w), lambda b, h, i: (0, h)),
        ],
        out_specs=pl.BlockSpec((1, tq, w), lambda b, h, i: (b, i, h)),
        out_shape=jax.ShapeDtypeStruct((B, S, HEADS * HEAD_W), BF16),
        scratch_shapes=_attn_scratch(S, tq, 2 * tq) * nh,
        compiler_params=_cparams(("parallel", "parallel", "arbitrary"), VMEM_LIMIT_ATTN_A),
        name="diff_attn",
    )(z, z, z, lq1, lk1, lq2, lk2, lam_init, g_diff)


def _row_norm_max(x):
    xf = x.astype(F32)
    return jnp.sqrt(jnp.max(jnp.sum(xf * xf, axis=1, keepdims=True)))


def _fox_attn_kernel(q_ref, k_ref, v_ref, f_ref, o_ref, fmin_s, fmax_s, c0_s, kn_s, *scr, tk, nh):
    qi = pl.program_id(2)
    S = k_ref.shape[1]
    tq = 2 * tk
    n_blk = S // tq
    pc = min(PROLOGUE_CHUNK, S)
    per = len(scr) // nh
    names = ("frep", "vt", "m", "acc", "sa", "sb", "sl")
    streams = [dict(zip(names, scr[hh * per:(hh + 1) * per])) for hh in range(nh)]

    @pl.when(qi == 0)
    def _():
        for hh, st in enumerate(streams):
            _build_vt(v_ref.at[:, :, _head_cols(hh)], st["vt"], tk)
            kn = jnp.float32(0.0)
            for c in range(S // pc):
                fb = jnp.broadcast_to(f_ref[hh, :, c * pc:(c + 1) * pc] * LOG2E, (HEAD_W, pc))
                st["frep"][c * pc:(c + 1) * pc, :] = fb.T
                kn = jnp.maximum(kn, _row_norm_max(k_ref[0, c * pc:(c + 1) * pc, _head_cols(hh)]))
            kn_s[hh] = kn
            for t in range(n_blk):
                fb = f_ref[hh, :, t * tq:(t + 1) * tq] * LOG2E
                fmax_s[hh, t] = jnp.max(fb)
                c0_s[hh, t] = jnp.max(fb[:, 0:1])
            for j in range(2 * n_blk):
                fmin_s[hh, j] = jnp.min(f_ref[hh, :, j * tk:(j + 1) * tk] * LOG2E)

    def mask_fn(s):
        key = lax.broadcasted_iota(jnp.int32, s.shape, 0)
        qry = lax.broadcasted_iota(jnp.int32, s.shape, 1)
        return jnp.where(key <= qry, s, NEG)

    q_off = pl.multiple_of(qi * tq, tq)
    j0 = 2 * qi
    for hh, st in enumerate(streams):
        q = q_ref[0, :, _head_cols(hh)]
        c0 = st["frep"][pl.ds(q_off, 8), :][0:1, 0:1]

        qk = _row_norm_max(q) * kn_s[hh] * 1.01 + 1.0
        thr = 2.0 * qk + (fmax_s[hh, qi] - c0_s[hh, qi]) + SKIP_MARGIN
        j0h = jnp.int32(0)
        for j in range(2 * n_blk - 2):
            skip = (j < 2 * qi) & (j0h == j) & (fmin_s[hh, j] - c0_s[hh, qi] >= thr)
            j0h = j0h + skip.astype(jnp.int32)
        j0 = jnp.minimum(j0, j0h)

        def biased(j, qq, st=st, hh=hh, c0=c0):
            off = pl.multiple_of(j * tk, tk)
            fsub = st["frep"][pl.ds(off, tk), :] - c0
            s = lax.dot_general(k_ref[0, pl.ds(off, tk), _head_cols(hh)], qq, _NT,
                                preferred_element_type=F32)
            return s - jnp.tile(fsub, (1, qq.shape[0] // HEAD_W))

        st["scores"] = functools.partial(biased, qq=q)
        st["scores_late"] = functools.partial(biased, qq=q[tk:])
        st["mask"] = mask_fn

    _sweep_wide(qi, j0, streams)

    for hh, st in enumerate(streams):
        acc = st["acc"][...]
        o_ref[0, :, _head_cols(hh)] = (acc[0:HEAD_W] / acc[HEAD_W:HEAD_W + 1]).T.astype(BF16)


def _fox_attn(z, f3):
    B, S, _ = z.shape
    tk = min(TK_B, S // 2)
    tq = 2 * tk
    nh = NH_B
    w = nh * HEAD_W
    rows = f3.shape[0] // B
    qb, kb, vb = Z_BQ // w, Z_BK // w, Z_BV // w
    per_head = ([pltpu.VMEM((S, HEAD_W), F32)] + _attn_scratch(S, tk, tq)
                + [pltpu.VMEM((tk, tq // 2), F32)])
    return pl.pallas_call(
        functools.partial(_fox_attn_kernel, tk=tk, nh=nh),
        grid=(B, HEADS // nh, S // tq),
        in_specs=[
            pl.BlockSpec((1, tq, w), lambda b, h, i: (b, i, qb + h)),
            pl.BlockSpec((1, S, w), lambda b, h, i: (b, 0, kb + h)),
            pl.BlockSpec((1, S, w), lambda b, h, i: (b, 0, vb + h)),
            pl.BlockSpec((nh, 1, S), lambda b, h, i: (b * (rows // nh) + h, 0, 0)),
        ],
        out_specs=pl.BlockSpec((1, tq, w), lambda b, h, i: (b, i, h)),
        out_shape=jax.ShapeDtypeStruct((B, S, HEADS * HEAD_W), BF16),
        scratch_shapes=([pltpu.SMEM((nh, S // tk), F32)] + [pltpu.SMEM((nh, S // tq), F32)] * 2
                        + [pltpu.SMEM((nh,), F32)]
                        + per_head * nh),
        compiler_params=_cparams(("parallel", "parallel", "arbitrary")),
        name="fox_attn",
    )(z, z, z, f3)


def _retention_kernel(q_ref, k_ref, v_ref, cg_ref, cos_ref, sin_ref, dm_ref, g_ref, o_ref, r_scr,
                      *, L):
    @pl.when(pl.program_id(1) == 0)
    def _():
        r_scr[...] = jnp.zeros(r_scr.shape, F32)

    cosf = cos_ref[...]
    sinf = sin_ref[...]
    pos = lax.broadcasted_iota(jnp.int32, (L, 1), 0).astype(F32)

    for h in range(HEADS):
        lg = math.log(1.0 - 2.0 ** (-5.0 - h))
        qh = q_ref[0, :, h * HEAD_W:(h + 1) * HEAD_W].astype(F32)
        kh = k_ref[0, :, h * HEAD_W:(h + 1) * HEAD_W].astype(F32)
        qr = qh * cosf + pltpu.roll(qh, HEAD_W // 2, 1) * sinf
        kr = kh * cosf + pltpu.roll(kh, HEAD_W // 2, 1) * sinf
        v = v_ref[0, :, h * C_DV:(h + 1) * C_DV]
        s = lax.dot_general(qr.astype(BF16), kr.astype(BF16), _NT,
                            preferred_element_type=F32) * dm_ref[h]
        inner = jnp.dot(s.astype(BF16), v, preferred_element_type=F32)
        r = r_scr[h]
        q_dec = qr * jnp.exp(lg * (pos + 1.0))
        cross = jnp.dot(q_dec.astype(BF16), r.astype(BF16), preferred_element_type=F32)
        k_dec = kr * jnp.exp(lg * (L - 1.0 - pos))
        r_scr[h] = math.exp(lg * L) * r + jnp.dot(k_dec.T.astype(BF16), v, preferred_element_type=F32)
        o = inner + cross
        mu = jnp.mean(o, axis=1, keepdims=True)
        var = jnp.mean(jnp.square(o - mu), axis=1, keepdims=True)
        y = (o - mu) * lax.rsqrt(var + LN_EPS) * g_ref[:, h * C_DV:(h + 1) * C_DV]
        cg = cg_ref[0, :, h * C_DV:(h + 1) * C_DV].astype(F32)
        o_ref[0, :, h * C_DV:(h + 1) * C_DV] = (y * (cg * jax.nn.sigmoid(cg))).astype(BF16)


def _retention(z, cos_t, sin_t, g_ret):
    B, S, _ = z.shape
    L = min(L_RET, S)
    qk_w, v_w = HEADS * HEAD_W, HEADS * C_DV
    dist = np.arange(L)[:, None] - np.arange(L)[None, :]
    gam = 1.0 - 2.0 ** (-5.0 - np.arange(HEADS, dtype=np.float64))
    dm = jnp.asarray(np.where(dist >= 0, gam[:, None, None] ** np.maximum(dist, 0), 0.0), F32)
    return pl.pallas_call(
        functools.partial(_retention_kernel, L=L),
        grid=(B, S // L),
        in_specs=[
            pl.BlockSpec((1, L, qk_w), lambda b, n: (b, n, Z_CQ // qk_w)),
            pl.BlockSpec((1, L, qk_w), lambda b, n: (b, n, Z_CK // qk_w)),
            pl.BlockSpec((1, L, v_w), lambda b, n: (b, n, Z_CV // v_w)),
            pl.BlockSpec((1, L, v_w), lambda b, n: (b, n, Z_CG // v_w)),
            pl.BlockSpec((L, HEAD_W), lambda b, n: (n, 0)),
            pl.BlockSpec((L, HEAD_W), lambda b, n: (n, 0)),
            _const_spec(dm.shape),
            pl.BlockSpec((1, v_w), lambda b, n: (0, 0)),
        ],
        out_specs=pl.BlockSpec((1, L, v_w), lambda b, n: (b, n, 0)),
        out_shape=jax.ShapeDtypeStruct((B, S, v_w), BF16),
        scratch_shapes=[pltpu.VMEM((HEADS, HEAD_W, C_DV), F32)],
        compiler_params=_cparams(("parallel", "arbitrary")),
        name="retention",
    )(z, z, z, z, cos_t, sin_t, dm, g_ret)


def _rope_tables(S):
    half = HEAD_W // 2
    inv = 1.0 / (ROPE_BASE ** np.linspace(0.0, 1.0, half))
    ang = np.arange(S, dtype=np.float64)[:, None] * inv[None, :]
    cos, sin = np.cos(ang), np.sin(ang)
    return (jnp.asarray(np.concatenate([cos, cos], axis=1), F32),
            jnp.asarray(np.concatenate([-sin, sin], axis=1), F32))


def _layernorm(r, g, b):
    mu = jnp.mean(r, axis=1, keepdims=True)
    var = jnp.mean(jnp.square(r - mu), axis=1, keepdims=True)
    return (r - mu) * lax.rsqrt(var + LN_EPS) * g + b


def _merge_kernel(ya_ref, yb_ref, yc_ref, ga_ref, gb_ref, gc_ref, x_ref, gt_ref, sh2_ref, sc2_ref,
                  wpa_ref, wpb_ref, wpc_ref, wout_ref, lng_ref, lnb_ref, xo_ref, ho_ref, *, alpha):
    hs = x_ref.shape[1] // 2
    halves = [slice(0, hs), slice(hs, 2 * hs)]

    def branch(rows, y_ref, g_ref, w_ref):
        gate = jax.nn.sigmoid(g_ref[0, rows].astype(F32))
        return gate * jnp.dot(y_ref[0, rows], w_ref[...], preferred_element_type=F32)

    ms = [branch(r, ya_ref, ga_ref, wpa_ref) + branch(r, yb_ref, gb_ref, wpb_ref)
          + branch(r, yc_ref, gc_ref, wpc_ref) for r in halves]
    ys = [jnp.dot(m.astype(BF16), wout_ref[...], preferred_element_type=F32) for m in ms]
    for r, y in zip(halves, ys):
        xn = _layernorm(alpha * x_ref[0, r] + gt_ref[0] * y, lng_ref[...], lnb_ref[...])
        xo_ref[0, r] = xn
        ho_ref[0, r] = (xn * (1.0 + sc2_ref[0]) + sh2_ref[0]).astype(BF16)


def _const_spec(shape):
    return pl.BlockSpec(shape, lambda b, i: (0,) * len(shape), pipeline_mode=pl.Buffered(1))


def _layer_spec(stacked, l):
    shape = stacked.shape[1:]
    return pl.BlockSpec((None,) + shape, lambda b, i: (l,) + (0,) * len(shape),
                        pipeline_mode=pl.Buffered(1))


def _merge(ya, yb, yc, z, x, mod_l, wpa, wpb, wpc, wout, lng, lnb, alpha, l):
    B, S, D = x.shape
    tm = min(TM_MERGE, S)
    gblk = Z_GATES // D
    tok = lambda w, col: pl.BlockSpec((1, tm, w), lambda b, i: (b, i, col))
    modv = lambda col: pl.BlockSpec((1, 1, D), lambda b, i: (b, 0, col))
    return pl.pallas_call(
        functools.partial(_merge_kernel, alpha=alpha),
        grid=(B, S // tm),
        in_specs=[
            tok(ya.shape[2], 0), tok(yb.shape[2], 0), tok(yc.shape[2], 0),
            tok(D, gblk), tok(D, gblk + 1), tok(D, gblk + 2),
            tok(D, 0),
            modv(2), modv(3), modv(4),
            _layer_spec(wpa, l), _layer_spec(wpb, l), _layer_spec(wpc, l),
            _layer_spec(wout, l), _const_spec(lng.shape), _const_spec(lnb.shape),
        ],
        out_specs=[tok(D, 0), tok(D, 0)],
        out_shape=[jax.ShapeDtypeStruct((B, S, D), F32), jax.ShapeDtypeStruct((B, S, D), BF16)],
        compiler_params=_cparams(("parallel", "parallel")),
        name="merge_out_ln",
    )(ya, yb, yc, z, z, z, x, mod_l, mod_l, mod_l, wpa, wpb, wpc, wout, lng, lnb)


def _ffn_kernel(h_ref, x_ref, gt_ref, wup_ref, wconv_ref, bconv_ref, wdown_ref, lng_ref, lnb_ref,
                xo_ref, a_scr, tail_scr, *, alpha, cw):
    @pl.when(pl.program_id(1) == 0)
    def _():
        tail_scr[...] = jnp.zeros(tail_scr.shape, F32)

    h = h_ref[0]
    tm = h.shape[0]
    dff = a_scr.shape[1]
    row = lax.broadcasted_iota(jnp.int32, (tm, cw), 0)
    for ci in range(dff // cw):
        lo = ci * cw
        u = jnp.dot(h, wup_ref[:, lo:lo + cw], preferred_element_type=F32)
        g = jnp.dot(h, wup_ref[:, dff + lo:dff + lo + cw], preferred_element_type=F32)
        tail = tail_scr[:, lo:lo + cw]
        p1, p2 = tail[7:8], tail[6:7]
        um1 = jnp.where(row == 0, p1, pltpu.roll(u, 1, 0))
        um2 = jnp.where(row == 0, p2, jnp.where(row == 1, p1, pltpu.roll(u, 2, 0)))
        tail_scr[:, lo:lo + cw] = u[tm - 8:]
        conv = bconv_ref[:, lo:lo + cw] + um2 * wconv_ref[0:1, lo:lo + cw]
        conv = conv + um1 * wconv_ref[1:2, lo:lo + cw]
        conv = conv + u * wconv_ref[2:3, lo:lo + cw]
        act = 0.5 * conv * (1.0 + lax.erf(conv * (2.0 ** -0.5)))
        a_scr[:, lo:lo + cw] = (act * g).astype(BF16)
    halves = [slice(0, tm // 2), slice(tm // 2, tm)]
    ys = [jnp.dot(a_scr[r, :], wdown_ref[...], preferred_element_type=F32) for r in halves]
    for r, y in zip(halves, ys):
        xo_ref[0, r] = _layernorm(alpha * x_ref[0, r] + gt_ref[0] * y, lng_ref[...], lnb_ref[...])


def _ffn(h, x, mod_l, wup, wconv, bconv, wdown, lng, lnb, alpha, l):
    B, S, D = x.shape
    dff = wdown.shape[1]
    tm = min(TM_FFN, S)
    tok = pl.BlockSpec((1, tm, D), lambda b, i: (b, i, 0))
    return pl.pallas_call(
        functools.partial(_ffn_kernel, alpha=alpha, cw=CW_FFN),
        grid=(B, S // tm),
        in_specs=[
            tok, tok,
            pl.BlockSpec((1, 1, D), lambda b, i: (b, 0, 5)),
            _layer_spec(wup, l), _const_spec(wconv.shape), _const_spec(bconv.shape),
            _layer_spec(wdown, l), _const_spec(lng.shape), _const_spec(lnb.shape),
        ],
        out_specs=tok,
        out_shape=jax.ShapeDtypeStruct((B, S, D), F32),
        scratch_shapes=[pltpu.VMEM((tm, dff), BF16), pltpu.VMEM((8, dff), F32)],
        compiler_params=_cparams(("parallel", "arbitrary")),
        name="conv_ffn_ln",
    )(h, x, mod_l, wup, wconv, bconv, wdown, lng, lnb)


def _prep_in_proj(w, b):
    scale = np.ones((1, Z_WIDTH), np.float32)
    scale[:, Z_AQ:Z_AQ + 512] = A_DQK ** -0.5 * LOG2E
    scale[:, Z_BQ:Z_BQ + 512] = HEAD_W ** -0.5 * LOG2E
    scale[:, Z_CK:Z_CK + 512] = HEAD_W ** -0.5
    L, D, _ = w.shape
    tn = TN_PREP
    wm = pl.pallas_call(
        functools.partial(_win_prep_kernel, first_shifted=BF_OFF // tn, shift=HEADS),
        grid=(L, Z_WIDTH // tn),
        in_specs=[
            pl.BlockSpec((None, D, tn), lambda l, j: (l, 0, j)),
            pl.BlockSpec((None, D, HEAD_W), lambda l, j: (l, 0, (j + 1) * (tn // HEAD_W))),
            pl.BlockSpec((1, tn), lambda l, j: (0, j)),
        ],
        out_specs=pl.BlockSpec((None, D, tn), lambda l, j: (l, 0, j)),
        out_shape=jax.ShapeDtypeStruct((L, D, Z_WIDTH), BF16),
        compiler_params=_cparams(("parallel", "parallel")),
        name="w_in_prep",
    )(w, w, jnp.asarray(scale))
    bm = jnp.concatenate([b[..., :BF_OFF], b[..., BF_OFF + HEADS:]], axis=-1) * scale
    pad = BF_PAD - HEADS
    wf = jnp.pad(w[..., BF_OFF:BF_OFF + HEADS], ((0, 0), (0, 0), (0, pad)))
    bf = jnp.pad(b[..., BF_OFF:BF_OFF + HEADS], ((0, 0), (0, pad)))
    return wm, bm[:, None, :], wf.astype(BF16), bf[:, None, :]


def _win_prep_kernel(w_ref, wnext_ref, s_ref, o_ref, *, first_shifted, shift):
    j = pl.program_id(1)

    @pl.when(j < first_shifted)
    def _():
        o_ref[...] = (w_ref[...] * s_ref[...]).astype(BF16)

    @pl.when(j >= first_shifted)
    def _():
        tn = w_ref.shape[1]
        wide = jnp.concatenate([w_ref[...], wnext_ref[...]], axis=1)
        moved = pltpu.roll(wide, wide.shape[1] - shift, 1)[:, :tn]
        o_ref[...] = (moved * s_ref[...]).astype(BF16)


def kernel(x, c, w_ada, b_ada, w_in, b_in, lam_q1, lam_k1, lam_q2, lam_k2, g_diff, g_ret, w_pa, w_pb, w_pc, w_out, ln_g, ln_b, w_up, w_conv, b_conv, w_down):
    B, S, D = x.shape
    depth = w_ada.shape[0]
    alpha = (2 * depth) ** 0.25
    chunk = 64
    mod = _ada(c, w_ada, b_ada)
    cos_t, sin_t = _rope_tables(S)
    wm, bm, wf, bf = _prep_in_proj(w_in, b_in)
    wpa, wpb, wpc, wout = (w.astype(BF16) for w in (w_pa, w_pb, w_pc, w_out))
    wup, wdown = w_up.astype(BF16), w_down.astype(BF16)
    for l in range(depth):
        mod_l = mod[l].reshape(B, 1, 6 * D)
        lam_init = jnp.full((1, 1), 0.8 - 0.6 * math.exp(-0.3 * l), F32)
        z, lf = _inproj(x, mod_l, wm, bm, wf, bf, l)
        f = _cumsum(lf)
        ya = _diff_attn(z, lam_q1[l][None], lam_k1[l][None], lam_q2[l][None], lam_k2[l][None],
                        lam_init, g_diff[l][None], chunk)
        yb = _fox_attn(z, f.reshape(B * f.shape[1], 1, S))
        yc = _retention(z, cos_t, sin_t, g_ret[l][None])
        x, h2 = _merge(ya, yb, yc, z, x, mod_l, wpa, wpb, wpc, wout,
                       ln_g[l, 0][None], ln_b[l, 0][None], alpha, l)
        x = _ffn(h2, x, mod_l, wup, w_conv[l], b_conv[l][None], wdown,
                 ln_g[l, 1][None], ln_b[l, 1][None], alpha, l)
    return x
```

```python
import functools
import math

import numpy as np
import jax
import jax.numpy as jnp
from jax import lax
from jax.experimental import pallas as pl
from jax.experimental.pallas import tpu as pltpu

F32 = jnp.float32
BF16 = jnp.bfloat16

HEADS = 4
A_DQK = 64
HEAD_W = 128
C_DV = 256
ROPE_BASE = 10000.0
LN_EPS = 1e-5
CONV_W = 3

Z_AQ, Z_AK, Z_AV = 0, 512, 1024
Z_BQ, Z_BK, Z_BV = 1536, 2048, 2560
Z_CQ, Z_CK, Z_CV, Z_CG = 3072, 3584, 4096, 5120
Z_GATES = 6144
Z_WIDTH = 9216
BF_OFF = 3072
BF_PAD = 128

NEG = -1e30
LOG2E = 1.4426950408889634
VT_ROWS = 144

TN_ADA = 1024
TM_IN, TN_IN = 1024, 2304
TN_PREP = 1024
TQ_A = 512
TK_B = 512
NH_A = 4
NH_B = 2
PROLOGUE_CHUNK = 512
SKIP_MARGIN = 160.0
L_RET = 256
TM_MERGE = 512
TM_FFN = 512
CW_FFN = 256

VMEM_LIMIT = 56 * 1024 * 1024
VMEM_LIMIT_ATTN_A = 60 * 1024 * 1024


def _cparams(sem, vmem_limit=VMEM_LIMIT):
    return pltpu.CompilerParams(dimension_semantics=sem, vmem_limit_bytes=vmem_limit)


def _ada_kernel(ct_ref, w_ref, b_ref, o_ref):
    w = w_ref[0]
    rows = []
    for b in range(ct_ref.shape[0]):
        cb = ct_ref[b]
        ca = cb * jax.nn.sigmoid(cb)
        rows.append(jnp.sum(ca * w, axis=0, keepdims=True))
    o_ref[0] = jnp.concatenate(rows, axis=0) + b_ref[0]


def _ada(c, w_ada, b_ada):
    B, D = c.shape
    L, _, N = w_ada.shape
    tn = min(TN_ADA, N)
    return pl.pallas_call(
        _ada_kernel,
        grid=(L, N // tn),
        in_specs=[
            pl.BlockSpec((B, D, 1), lambda l, j: (0, 0, 0)),
            pl.BlockSpec((1, D, tn), lambda l, j: (l, 0, j)),
            pl.BlockSpec((1, 1, tn), lambda l, j: (l, 0, j)),
        ],
        out_specs=pl.BlockSpec((1, B, tn), lambda l, j: (l, 0, j)),
        out_shape=jax.ShapeDtypeStruct((L, B, N), F32),
        compiler_params=_cparams(("parallel", "parallel")),
        name="ada_mod",
    )(c.reshape(B, D, 1), w_ada, b_ada.reshape(L, 1, N))


def _inproj_kernel(x_ref, sh_ref, sc_ref, w_ref, b_ref, wf_ref, bf_ref, z_ref, lf_ref, h_scr):
    @pl.when(pl.program_id(2) == 0)
    def _():
        h = x_ref[0] * (1.0 + sc_ref[0]) + sh_ref[0]
        hb = h.astype(BF16)
        h_scr[...] = hb
        zf = jnp.dot(hb, wf_ref[...], preferred_element_type=F32) + bf_ref[...]
        lt = zf.T[:8]
        lf_ref[0] = jnp.minimum(lt, 0.0) - jnp.log1p(jnp.exp(-jnp.abs(lt)))

    acc = lax.dot_general(h_scr[...], w_ref[...], _NT, preferred_element_type=F32)
    z_ref[0] = (acc + b_ref[...]).astype(BF16)


def _inproj(x, mod_l, w, b, wf, bf, l):
    B, S, D = x.shape
    N = w.shape[1]
    tm, tn = min(TM_IN, S), min(TN_IN, N)
    return pl.pallas_call(
        _inproj_kernel,
        grid=(B, S // tm, N // tn),
        in_specs=[
            pl.BlockSpec((1, tm, D), lambda b, i, j: (b, i, 0)),
            pl.BlockSpec((1, 1, D), lambda b, i, j: (b, 0, 0)),
            pl.BlockSpec((1, 1, D), lambda b, i, j: (b, 0, 1)),
            pl.BlockSpec((None, tn, D), lambda b, i, j: (l, j, 0)),
            pl.BlockSpec((None, 1, tn), lambda b, i, j: (l, 0, j)),
            pl.BlockSpec((None, D, BF_PAD), lambda b, i, j: (l, 0, 0)),
            pl.BlockSpec((None, 1, BF_PAD), lambda b, i, j: (l, 0, 0)),
        ],
        out_specs=[
            pl.BlockSpec((1, tm, tn), lambda b, i, j: (b, i, j)),
            pl.BlockSpec((1, 8, tm), lambda b, i, j: (b, 0, i)),
        ],
        out_shape=[
            jax.ShapeDtypeStruct((B, S, N), BF16),
            jax.ShapeDtypeStruct((B, 8, S), F32),
        ],
        scratch_shapes=[pltpu.VMEM((tm, D), BF16)],
        compiler_params=_cparams(("parallel", "parallel", "arbitrary")),
        name="in_proj",
    )(x, mod_l, mod_l, w, b, wf, bf)


def _cumsum_kernel(lf_ref, f_ref):
    x = lf_ref[0]
    lane = lax.broadcasted_iota(jnp.int32, x.shape, 1)
    k = 1
    while k < x.shape[1]:
        x = x + jnp.where(lane >= k, pltpu.roll(x, k, 1), 0.0)
        k *= 2
    f_ref[0] = x


def _cumsum(lf):
    B, R, S = lf.shape
    return pl.pallas_call(
        _cumsum_kernel,
        grid=(B,),
        in_specs=[pl.BlockSpec((1, R, S), lambda b: (b, 0, 0))],
        out_specs=pl.BlockSpec((1, R, S), lambda b: (b, 0, 0)),
        out_shape=jax.ShapeDtypeStruct((B, R, S), F32),
        compiler_params=_cparams(("parallel",)),
        name="forget_cumsum",
    )(lf)


def _build_vt(v_ref, vt_scr, tk):
    ones_row = (lax.broadcasted_iota(jnp.int32, (VT_ROWS - HEAD_W, tk), 0) == 0).astype(BF16)
    for c in range(v_ref.shape[1] // tk):
        vb = v_ref[0, c * tk:(c + 1) * tk, :].astype(F32)
        vt_scr[c, 0:HEAD_W, :] = vb.T.astype(BF16)
        vt_scr[c, HEAD_W:VT_ROWS, :] = ones_row


def _softmax_step_t(s, vt, m_scr, acc_scr):
    m_prev = m_scr[...]
    m_new = jnp.maximum(m_prev, jnp.max(s, axis=0, keepdims=True))
    p = jnp.exp2(s - m_new).astype(BF16)
    alpha = jnp.exp2(m_prev - m_new)
    acc_scr[...] = alpha * acc_scr[...] + jnp.dot(vt, p, preferred_element_type=F32)
    m_scr[...] = m_new


def _init_streams(streams, first_tile):
    for st in streams:
        st["m"][...] = jnp.full(st["m"].shape, NEG, F32)
        st["acc"][...] = jnp.zeros(st["acc"].shape, F32)
        st["sa"][...] = st["scores"](first_tile)


def _pair_body(streams):
    def pair(t, carry):
        j = 2 * t
        for st in streams:
            st["sb"][...] = st["scores"](j + 1)
            _softmax_step_t(st["sa"][...], st["vt"][j], st["m"], st["acc"])
        for st in streams:
            st["sa"][...] = st["scores"](j + 2)
            _softmax_step_t(st["sb"][...], st["vt"][j + 1], st["m"], st["acc"])
        return carry
    return pair


def _sweep(qi, streams):
    _init_streams(streams, 0)
    lax.fori_loop(0, qi // 2, _pair_body(streams), 0)

    @pl.when(qi % 2 == 0)
    def _():
        for st in streams:
            _softmax_step_t(st["mask"](st["sa"][...]), st["vt"][qi], st["m"], st["acc"])

    @pl.when(qi % 2 == 1)
    def _():
        for st in streams:
            st["sb"][...] = st["scores"](qi)
            _softmax_step_t(st["sa"][...], st["vt"][qi - 1], st["m"], st["acc"])
        for st in streams:
            _softmax_step_t(st["mask"](st["sb"][...]), st["vt"][qi], st["m"], st["acc"])


def _sweep_wide(qi, j0, streams):
    for st in streams:
        st["m"][...] = jnp.full(st["m"].shape, NEG, F32)
        st["acc"][...] = jnp.zeros(st["acc"].shape, F32)

    @pl.when(j0 % 2 == 1)
    def _():
        for st in streams:
            st["sb"][...] = st["scores"](j0)
        for st in streams:
            st["sa"][...] = st["scores"](j0 + 1)
            _softmax_step_t(st["sb"][...], st["vt"][j0], st["m"], st["acc"])

    @pl.when(j0 % 2 == 0)
    def _():
        for st in streams:
            st["sa"][...] = st["scores"](j0)

    lax.fori_loop((j0 + 1) // 2, qi, _pair_body(streams), 0)
    for st in streams:
        st["sl"][...] = st["scores_late"](2 * qi + 1)
        _softmax_step_t(st["mask"](st["sa"][...]), st["vt"][2 * qi], st["m"], st["acc"])
    for st in streams:
        nl = st["sl"].shape[1]
        _softmax_step_t(st["mask"](st["sl"][...]), st["vt"][2 * qi + 1],
                        st["m"].at[:, nl:], st["acc"].at[:, nl:])


_NT = (((1,), (1,)), ((), ()))


def _attn_scratch(S, tk, nq):
    return [pltpu.VMEM((S // tk, VT_ROWS, tk), BF16),
            pltpu.VMEM((1, nq), F32),
            pltpu.VMEM((VT_ROWS, nq), F32),
            pltpu.VMEM((tk, nq), F32), pltpu.VMEM((tk, nq), F32)]


def _head_cols(hh):
    return slice(hh * HEAD_W, (hh + 1) * HEAD_W)


def _diff_attn_kernel(q_ref, k_ref, v_ref, lq1_ref, lk1_ref, lq2_ref, lk2_ref, li_ref, g_ref,
                      o_ref, *scr, tq, chunk, nh):
    qi = pl.program_id(2)
    per = len(scr) // nh
    names = ("vt", "m", "acc", "sa", "sb")
    streams = [dict(zip(names, scr[hh * per:(hh + 1) * per])) for hh in range(nh)]

    @pl.when(qi == 0)
    def _():
        for hh, st in enumerate(streams):
            _build_vt(v_ref.at[:, :, _head_cols(hh)], st["vt"], tq)

    def mask_fn(s):
        key = lax.broadcasted_iota(jnp.int32, s.shape, 0)
        col = lax.broadcasted_iota(jnp.int32, s.shape, 1)
        qry = jnp.where(col >= tq, col - tq, col)
        return jnp.where(key // chunk <= qry // chunk, s, NEG)

    for hh, st in enumerate(streams):
        q = q_ref[0, :, _head_cols(hh)]
        lane = lax.broadcasted_iota(jnp.int32, q.shape, 1)
        zero = jnp.zeros_like(q)
        qs = jnp.concatenate([jnp.where(lane < A_DQK, q, zero),
                              jnp.where(lane >= A_DQK, q, zero)], axis=0)

        def scores(j, qs=qs, hh=hh):
            off = pl.multiple_of(j * tq, tq)
            return lax.dot_general(k_ref[0, pl.ds(off, tq), _head_cols(hh)], qs, _NT,
                                   preferred_element_type=F32)

        st["scores"], st["mask"] = scores, mask_fn

    _sweep(qi, streams)

    lam_init = li_ref[...]
    lam = (jnp.exp(jnp.sum(lq1_ref[...] * lk1_ref[...], axis=1, keepdims=True))
           - jnp.exp(jnp.sum(lq2_ref[...] * lk2_ref[...], axis=1, keepdims=True)) + lam_init)
    for hh, st in enumerate(streams):
        acc = st["acc"][...]
        ot_all = acc[0:HEAD_W] / acc[HEAD_W:HEAD_W + 1]
        o = (ot_all[:, :tq] - lam * ot_all[:, tq:]).T
        o = o * lax.rsqrt(jnp.mean(o * o, axis=1, keepdims=True) + LN_EPS)
        o_ref[0, :, _head_cols(hh)] = (o * g_ref[:, _head_cols(hh)] * (1.0 - lam_init)).astype(BF16)


def _resident(shape, index_map):
    return pl.BlockSpec(shape, index_map, pipeline_mode=pl.Buffered(1))


def _diff_attn(z, lq1, lk1, lq2, lk2, lam_init, g_diff, chunk):
    B, S, _ = z.shape
    tq = min(TQ_A, S)
    nh = NH_A
    w = nh * HEAD_W
    qb, kb, vb = Z_AQ // w, Z_AK // w, Z_AV // w
    small = pl.BlockSpec((1, A_DQK), lambda b, h, i: (0, 0))
    return pl.pallas_call(
        functools.partial(_diff_attn_kernel, tq=tq, chunk=chunk, nh=nh),
        grid=(B, HEADS // nh, S // tq),
        in_specs=[
            pl.BlockSpec((1, tq, w), lambda b, h, i: (b, i, qb + h)),
            pl.BlockSpec((1, S, w), lambda b, h, i: (b, 0, kb + h)),
            _resident((1, S, w), lambda b, h, i: (b, 0, vb + h)),
            small, small, small, small,
            pl.BlockSpec((1, 1), lambda b, h, i: (0, 0)),
            pl.BlockSpec((1, w), lambda b, h, i: (0, h)),
        ],
        out_specs=pl.BlockSpec((1, tq, w), lambda b, h, i: (b, i, h)),
        out_shape=jax.ShapeDtypeStruct((B, S, HEADS * HEAD_W), BF16),
        scratch_shapes=_attn_scratch(S, tq, 2 * tq) * nh,
        compiler_params=_cparams(("parallel", "parallel", "arbitrary"), VMEM_LIMIT_ATTN_A),
        name="diff_attn",
    )(z, z, z, lq1, lk1, lq2, lk2, lam_init, g_diff)


def _row_norm_max(x):
    xf = x.astype(F32)
    return jnp.sqrt(jnp.max(jnp.sum(xf * xf, axis=1, keepdims=True)))


def _fox_attn_kernel(q_ref, k_ref, v_ref, f_ref, o_ref, fmin_s, fmax_s, c0_s, kn_s, *scr, tk, nh):
    qi = pl.program_id(2)
    S = k_ref.shape[1]
    tq = 2 * tk
    n_blk = S // tq
    pc = min(PROLOGUE_CHUNK, S)
    per = len(scr) // nh
    names = ("frep", "vt", "m", "acc", "sa", "sb", "sl")
    streams = [dict(zip(names, scr[hh * per:(hh + 1) * per])) for hh in range(nh)]

    @pl.when(qi == 0)
    def _():
        for hh, st in enumerate(streams):
            _build_vt(v_ref.at[:, :, _head_cols(hh)], st["vt"], tk)
            kn = jnp.float32(0.0)
            for c in range(S // pc):
                fb = jnp.broadcast_to(f_ref[hh, :, c * pc:(c + 1) * pc] * LOG2E, (HEAD_W, pc))
                st["frep"][c * pc:(c + 1) * pc, :] = fb.T
                kn = jnp.maximum(kn, _row_norm_max(k_ref[0, c * pc:(c + 1) * pc, _head_cols(hh)]))
            kn_s[hh] = kn
            for t in range(n_blk):
                fb = f_ref[hh, :, t * tq:(t + 1) * tq] * LOG2E
                fmax_s[hh, t] = jnp.max(fb)
                c0_s[hh, t] = jnp.max(fb[:, 0:1])
            for j in range(2 * n_blk):
                fmin_s[hh, j] = jnp.min(f_ref[hh, :, j * tk:(j + 1) * tk] * LOG2E)

    def mask_fn(s):
        key = lax.broadcasted_iota(jnp.int32, s.shape, 0)
        qry = lax.broadcasted_iota(jnp.int32, s.shape, 1)
        return jnp.where(key <= qry, s, NEG)

    q_off = pl.multiple_of(qi * tq, tq)
    j0 = 2 * qi
    for hh, st in enumerate(streams):
        q = q_ref[0, :, _head_cols(hh)]
        c0 = st["frep"][pl.ds(q_off, 8), :][0:1, 0:1]

        qk = _row_norm_max(q) * kn_s[hh] * 1.01 + 1.0
        thr = 2.0 * qk + (fmax_s[hh, qi] - c0_s[hh, qi]) + SKIP_MARGIN
        j0h = jnp.int32(0)
        for j in range(2 * n_blk - 2):
            skip = (j < 2 * qi) & (j0h == j) & (fmin_s[hh, j] - c0_s[hh, qi] >= thr)
            j0h = j0h + skip.astype(jnp.int32)
        j0 = jnp.minimum(j0, j0h)

        def biased(j, qq, st=st, hh=hh, c0=c0):
            off = pl.multiple_of(j * tk, tk)
            fsub = st["frep"][pl.ds(off, tk), :] - c0
            s = lax.dot_general(k_ref[0, pl.ds(off, tk), _head_cols(hh)], qq, _NT,
                                preferred_element_type=F32)
            return s - jnp.tile(fsub, (1, qq.shape[0] // HEAD_W))

        st["scores"] = functools.partial(biased, qq=q)
        st["scores_late"] = functools.partial(biased, qq=q[tk:])
        st["mask"] = mask_fn

    _sweep_wide(qi, j0, streams)

    for hh, st in enumerate(streams):
        acc = st["acc"][...]
        o_ref[0, :, _head_cols(hh)] = (acc[0:HEAD_W] / acc[HEAD_W:HEAD_W + 1]).T.astype(BF16)


def _fox_attn(z, f3):
    B, S, _ = z.shape
    tk = min(TK_B, S // 2)
    tq = 2 * tk
    nh = NH_B
    w = nh * HEAD_W
    rows = f3.shape[0] // B
    qb, kb, vb = Z_BQ // w, Z_BK // w, Z_BV // w
    per_head = ([pltpu.VMEM((S, HEAD_W), F32)] + _attn_scratch(S, tk, tq)
                + [pltpu.VMEM((tk, tq // 2), F32)])
    return pl.pallas_call(
        functools.partial(_fox_attn_kernel, tk=tk, nh=nh),
        grid=(B, HEADS // nh, S // tq),
        in_specs=[
            pl.BlockSpec((1, tq, w), lambda b, h, i: (b, i, qb + h)),
            pl.BlockSpec((1, S, w), lambda b, h, i: (b, 0, kb + h)),
            pl.BlockSpec((1, S, w), lambda b, h, i: (b, 0, vb + h)),
            pl.BlockSpec((nh, 1, S), lambda b, h, i: (b * (rows // nh) + h, 0, 0)),
        ],
        out_specs=pl.BlockSpec((1, tq, w), lambda b, h, i: (b, i, h)),
        out_shape=jax.ShapeDtypeStruct((B, S, HEADS * HEAD_W), BF16),
        scratch_shapes=([pltpu.SMEM((nh, S // tk), F32)] + [pltpu.SMEM((nh, S // tq), F32)] * 2
                        + [pltpu.SMEM((nh,), F32)]
                        + per_head * nh),
        compiler_params=_cparams(("parallel", "parallel", "arbitrary")),
        name="fox_attn",
    )(z, z, z, f3)


def _retention_kernel(q_ref, k_ref, v_ref, cg_ref, cos_ref, sin_ref, dm_ref, g_ref, o_ref, r_scr,
                      *, L):
    @pl.when(pl.program_id(1) == 0)
    def _():
        r_scr[...] = jnp.zeros(r_scr.shape, F32)

    cosf = cos_ref[...]
    sinf = sin_ref[...]
    pos = lax.broadcasted_iota(jnp.int32, (L, 1), 0).astype(F32)

    for h in range(HEADS):
        lg = math.log(1.0 - 2.0 ** (-5.0 - h))
        qh = q_ref[0, :, h * HEAD_W:(h + 1) * HEAD_W].astype(F32)
        kh = k_ref[0, :, h * HEAD_W:(h + 1) * HEAD_W].astype(F32)
        qr = qh * cosf + pltpu.roll(qh, HEAD_W // 2, 1) * sinf
        kr = kh * cosf + pltpu.roll(kh, HEAD_W // 2, 1) * sinf
        v = v_ref[0, :, h * C_DV:(h + 1) * C_DV]
        s = lax.dot_general(qr.astype(BF16), kr.astype(BF16), _NT,
                            preferred_element_type=F32) * dm_ref[h]
        inner = jnp.dot(s.astype(BF16), v, preferred_element_type=F32)
        r = r_scr[h]
        q_dec = qr * jnp.exp(lg * (pos + 1.0))
        cross = jnp.dot(q_dec.astype(BF16), r.astype(BF16), preferred_element_type=F32)
        k_dec = kr * jnp.exp(lg * (L - 1.0 - pos))
        r_scr[h] = math.exp(lg * L) * r + jnp.dot(k_dec.T.astype(BF16), v, preferred_element_type=F32)
        o = inner + cross
        mu = jnp.mean(o, axis=1, keepdims=True)
        var = jnp.mean(jnp.square(o - mu), axis=1, keepdims=True)
        y = (o - mu) * lax.rsqrt(var + LN_EPS) * g_ref[:, h * C_DV:(h + 1) * C_DV]
        cg = cg_ref[0, :, h * C_DV:(h + 1) * C_DV].astype(F32)
        o_ref[0, :, h * C_DV:(h + 1) * C_DV] = (y * (cg * jax.nn.sigmoid(cg))).astype(BF16)


def _retention(z, cos_t, sin_t, g_ret):
    B, S, _ = z.shape
    L = min(L_RET, S)
    qk_w, v_w = HEADS * HEAD_W, HEADS * C_DV
    dist = np.arange(L)[:, None] - np.arange(L)[None, :]
    gam = 1.0 - 2.0 ** (-5.0 - np.arange(HEADS, dtype=np.float64))
    dm = jnp.asarray(np.where(dist >= 0, gam[:, None, None] ** np.maximum(dist, 0), 0.0), F32)
    return pl.pallas_call(
        functools.partial(_retention_kernel, L=L),
        grid=(B, S // L),
        in_specs=[
            pl.BlockSpec((1, L, qk_w), lambda b, n: (b, n, Z_CQ // qk_w)),
            pl.BlockSpec((1, L, qk_w), lambda b, n: (b, n, Z_CK // qk_w)),
            pl.BlockSpec((1, L, v_w), lambda b, n: (b, n, Z_CV // v_w)),
            pl.BlockSpec((1, L, v_w), lambda b, n: (b, n, Z_CG // v_w)),
            pl.BlockSpec((L, HEAD_W), lambda b, n: (n, 0)),
            pl.BlockSpec((L, HEAD_W), lambda b, n: (n, 0)),
            _const_spec(dm.shape),
            pl.BlockSpec((1, v_w), lambda b, n: (0, 0)),
        ],
        out_specs=pl.BlockSpec((1, L, v_w), lambda b, n: (b, n, 0)),
        out_shape=jax.ShapeDtypeStruct((B, S, v_w), BF16),
        scratch_shapes=[pltpu.VMEM((HEADS, HEAD_W, C_DV), F32)],
        compiler_params=_cparams(("parallel", "arbitrary")),
        name="retention",
    )(z, z, z, z, cos_t, sin_t, dm, g_ret)


def _rope_tables(S):
    half = HEAD_W // 2
    inv = 1.0 / (ROPE_BASE ** np.linspace(0.0, 1.0, half))
    ang = np.arange(S, dtype=np.float64)[:, None] * inv[None, :]
    cos, sin = np.cos(ang), np.sin(ang)
    return (jnp.asarray(np.concatenate([cos, cos], axis=1), F32),
            jnp.asarray(np.concatenate([-sin, sin], axis=1), F32))


def _layernorm(r, g, b):
    mu = jnp.mean(r, axis=1, keepdims=True)
    var = jnp.mean(jnp.square(r - mu), axis=1, keepdims=True)
    return (r - mu) * lax.rsqrt(var + LN_EPS) * g + b


def _merge_kernel(ya_ref, yb_ref, yc_ref, ga_ref, gb_ref, gc_ref, x_ref, gt_ref, sh2_ref, sc2_ref,
                  wpa_ref, wpb_ref, wpc_ref, wout_ref, lng_ref, lnb_ref, xo_ref, ho_ref, *, alpha):
    hs = x_ref.shape[1] // 2
    halves = [slice(0, hs), slice(hs, 2 * hs)]

    def branch(rows, y_ref, g_ref, w_ref):
        gate = jax.nn.sigmoid(g_ref[0, rows].astype(F32))
        return gate * jnp.dot(y_ref[0, rows], w_ref[...], preferred_element_type=F32)

    ms = [branch(r, ya_ref, ga_ref, wpa_ref) + branch(r, yb_ref, gb_ref, wpb_ref)
          + branch(r, yc_ref, gc_ref, wpc_ref) for r in halves]
    ys = [jnp.dot(m.astype(BF16), wout_ref[...], preferred_element_type=F32) for m in ms]
    for r, y in zip(halves, ys):
        xn = _layernorm(alpha * x_ref[0, r] + gt_ref[0] * y, lng_ref[...], lnb_ref[...])
        xo_ref[0, r] = xn
        ho_ref[0, r] = (xn * (1.0 + sc2_ref[0]) + sh2_ref[0]).astype(BF16)


def _const_spec(shape):
    return pl.BlockSpec(shape, lambda b, i: (0,) * len(shape), pipeline_mode=pl.Buffered(1))


def _layer_spec(stacked, l):
    shape = stacked.shape[1:]
    return pl.BlockSpec((None,) + shape, lambda b, i: (l,) + (0,) * len(shape),
                        pipeline_mode=pl.Buffered(1))


def _merge(ya, yb, yc, z, x, mod_l, wpa, wpb, wpc, wout, lng, lnb, alpha, l):
    B, S, D = x.shape
    tm = min(TM_MERGE, S)
    gblk = Z_GATES // D
    tok = lambda w, col: pl.BlockSpec((1, tm, w), lambda b, i: (b, i, col))
    modv = lambda col: pl.BlockSpec((1, 1, D), lambda b, i: (b, 0, col))
    return pl.pallas_call(
        functools.partial(_merge_kernel, alpha=alpha),
        grid=(B, S // tm),
        in_specs=[
            tok(ya.shape[2], 0), tok(yb.shape[2], 0), tok(yc.shape[2], 0),
            tok(D, gblk), tok(D, gblk + 1), tok(D, gblk + 2),
            tok(D, 0),
            modv(2), modv(3), modv(4),
            _layer_spec(wpa, l), _layer_spec(wpb, l), _layer_spec(wpc, l),
            _layer_spec(wout, l), _const_spec(lng.shape), _const_spec(lnb.shape),
        ],
        out_specs=[tok(D, 0), tok(D, 0)],
        out_shape=[jax.ShapeDtypeStruct((B, S, D), F32), jax.ShapeDtypeStruct((B, S, D), BF16)],
        compiler_params=_cparams(("parallel", "parallel")),
        name="merge_out_ln",
    )(ya, yb, yc, z, z, z, x, mod_l, mod_l, mod_l, wpa, wpb, wpc, wout, lng, lnb)


def _ffn_kernel(h_ref, x_ref, gt_ref, wup_ref, wconv_ref, bconv_ref, wdown_ref, lng_ref, lnb_ref,
                xo_ref, a_scr, tail_scr, *, alpha, cw):
    @pl.when(pl.program_id(1) == 0)
    def _():
        tail_scr[...] = jnp.zeros(tail_scr.shape, F32)

    h = h_ref[0]
    tm = h.shape[0]
    dff = a_scr.shape[1]
    row = lax.broadcasted_iota(jnp.int32, (tm, cw), 0)
    for ci in range(dff // cw):
        lo = ci * cw
        u = jnp.dot(h, wup_ref[:, lo:lo + cw], preferred_element_type=F32)
        g = jnp.dot(h, wup_ref[:, dff + lo:dff + lo + cw], preferred_element_type=F32)
        tail = tail_scr[:, lo:lo + cw]
        p1, p2 = tail[7:8], tail[6:7]
        um1 = jnp.where(row == 0, p1, pltpu.roll(u, 1, 0))
        um2 = jnp.where(row == 0, p2, jnp.where(row == 1, p1, pltpu.roll(u, 2, 0)))
        tail_scr[:, lo:lo + cw] = u[tm - 8:]
        conv = bconv_ref[:, lo:lo + cw] + um2 * wconv_ref[0:1, lo:lo + cw]
        conv = conv + um1 * wconv_ref[1:2, lo:lo + cw]
        conv = conv + u * wconv_ref[2:3, lo:lo + cw]
        act = 0.5 * conv * (1.0 + lax.erf(conv * (2.0 ** -0.5)))
        a_scr[:, lo:lo + cw] = (act * g).astype(BF16)
    halves = [slice(0, tm // 2), slice(tm // 2, tm)]
    ys = [jnp.dot(a_scr[r, :], wdown_ref[...], preferred_element_type=F32) for r in halves]
    for r, y in zip(halves, ys):
        xo_ref[0, r] = _layernorm(alpha * x_ref[0, r] + gt_ref[0] * y, lng_ref[...], lnb_ref[...])


def _ffn(h, x, mod_l, wup, wconv, bconv, wdown, lng, lnb, alpha, l):
    B, S, D = x.shape
    dff = wdown.shape[1]
    tm = min(TM_FFN, S)
    tok = pl.BlockSpec((1, tm, D), lambda b, i: (b, i, 0))
    return pl.pallas_call(
        functools.partial(_ffn_kernel, alpha=alpha, cw=CW_FFN),
        grid=(B, S // tm),
        in_specs=[
            tok, tok,
            pl.BlockSpec((1, 1, D), lambda b, i: (b, 0, 5)),
            _layer_spec(wup, l), _const_spec(wconv.shape), _const_spec(bconv.shape),
            _layer_spec(wdown, l), _const_spec(lng.shape), _const_spec(lnb.shape),
        ],
        out_specs=tok,
        out_shape=jax.ShapeDtypeStruct((B, S, D), F32),
        scratch_shapes=[pltpu.VMEM((tm, dff), BF16), pltpu.VMEM((8, dff), F32)],
        compiler_params=_cparams(("parallel", "arbitrary")),
        name="conv_ffn_ln",
    )(h, x, mod_l, wup, wconv, bconv, wdown, lng, lnb)


def _prep_in_proj(w, b):
    scale = np.ones((1, Z_WIDTH), np.float32)
    scale[:, Z_AQ:Z_AQ + 512] = A_DQK ** -0.5 * LOG2E
    scale[:, Z_BQ:Z_BQ + 512] = HEAD_W ** -0.5 * LOG2E
    scale[:, Z_CK:Z_CK + 512] = HEAD_W ** -0.5
    wt = jnp.swapaxes(w, 1, 2)
    wm = jnp.concatenate([wt[:, :BF_OFF], wt[:, BF_OFF + HEADS:]], axis=1)
    wm = (wm * scale.reshape(-1, 1)).astype(BF16)
    bm = jnp.concatenate([b[..., :BF_OFF], b[..., BF_OFF + HEADS:]], axis=-1) * scale
    pad = BF_PAD - HEADS
    wf = jnp.pad(w[..., BF_OFF:BF_OFF + HEADS], ((0, 0), (0, 0), (0, pad)))
    bf = jnp.pad(b[..., BF_OFF:BF_OFF + HEADS], ((0, 0), (0, pad)))
    return wm, bm[:, None, :], wf.astype(BF16), bf[:, None, :]


def kernel(x, c, w_ada, b_ada, w_in, b_in, lam_q1, lam_k1, lam_q2, lam_k2, g_diff, g_ret, w_pa, w_pb, w_pc, w_out, ln_g, ln_b, w_up, w_conv, b_conv, w_down):
    B, S, D = x.shape
    depth = w_ada.shape[0]
    alpha = (2 * depth) ** 0.25
    chunk = 64
    mod = _ada(c, w_ada, b_ada)
    cos_t, sin_t = _rope_tables(S)
    wm, bm, wf, bf = _prep_in_proj(w_in, b_in)
    wpa, wpb, wpc, wout = (w.astype(BF16) for w in (w_pa, w_pb, w_pc, w_out))
    wup, wdown = w_up.astype(BF16), w_down.astype(BF16)
    for l in range(depth):
        mod_l = mod[l].reshape(B, 1, 6 * D)
        lam_init = jnp.full((1, 1), 0.8 - 0.6 * math.exp(-0.3 * l), F32)
        z, lf = _inproj(x, mod_l, wm, bm, wf, bf, l)
        f = _cumsum(lf)
        ya = _diff_attn(z, lam_q1[l][None], lam_k1[l][None], lam_q2[l][None], lam_k2[l][None],
                        lam_init, g_diff[l][None], chunk)
        yb = _fox_attn(z, f.reshape(B * f.shape[1], 1, S))
        yc = _retention(z, cos_t, sin_t, g_ret[l][None])
        x, h2 = _merge(ya, yb, yc, z, x, mod_l, wpa, wpb, wpc, wout,
                       ln_g[l, 0][None], ln_b[l, 0][None], alpha, l)
        x = _ffn(h2, x, mod_l, wup, w_conv[l], b_conv[l][None], wdown,
                 ln_g[l, 1][None], ln_b[l, 1][None], alpha, l)
    return x
```

```python
import functools
import math

import numpy as np
import jax
import jax.numpy as jnp
from jax import lax
from jax.experimental import pallas as pl
from jax.experimental.pallas import tpu as pltpu

F32 = jnp.float32
BF16 = jnp.bfloat16

HEADS = 4
A_DQK = 64
HEAD_W = 128
C_DV = 256
ROPE_BASE = 10000.0
LN_EPS = 1e-5
CONV_W = 3

Z_AQ, Z_AK, Z_AV = 0, 512, 1024
Z_BQ, Z_BK, Z_BV = 1536, 2048, 2560
Z_CQ, Z_CK, Z_CV, Z_CG = 3072, 3584, 4096, 5120
Z_GATES = 6144
Z_WIDTH = 9216
BF_OFF = 3072
BF_PAD = 128

NEG = -1e30
LOG2E = 1.4426950408889634
VT_ROWS = 144

TN_ADA = 1024
TM_IN, TN_IN = 1024, 2304
TR_PREP = 1024
TQ_A = 512
TK_B = 512
NH_A = 4
NH_B = 2
PROLOGUE_CHUNK = 512
SKIP_MARGIN = 160.0
L_RET = 256
TM_MERGE = 512
TM_FFN = 512
CW_FFN = 256

VMEM_LIMIT = 56 * 1024 * 1024
VMEM_LIMIT_ATTN_A = 60 * 1024 * 1024


def _cparams(sem, vmem_limit=VMEM_LIMIT):
    return pltpu.CompilerParams(dimension_semantics=sem, vmem_limit_bytes=vmem_limit)


def _ada_kernel(ct_ref, w_ref, b_ref, o_ref):
    w = w_ref[0]
    rows = []
    for b in range(ct_ref.shape[0]):
        cb = ct_ref[b]
        ca = cb * jax.nn.sigmoid(cb)
        rows.append(jnp.sum(ca * w, axis=0, keepdims=True))
    o_ref[0] = jnp.concatenate(rows, axis=0) + b_ref[0]


def _ada(c, w_ada, b_ada):
    B, D = c.shape
    L, _, N = w_ada.shape
    tn = min(TN_ADA, N)
    return pl.pallas_call(
        _ada_kernel,
        grid=(L, N // tn),
        in_specs=[
            pl.BlockSpec((B, D, 1), lambda l, j: (0, 0, 0)),
            pl.BlockSpec((1, D, tn), lambda l, j: (l, 0, j)),
            pl.BlockSpec((1, 1, tn), lambda l, j: (l, 0, j)),
        ],
        out_specs=pl.BlockSpec((1, B, tn), lambda l, j: (l, 0, j)),
        out_shape=jax.ShapeDtypeStruct((L, B, N), F32),
        compiler_params=_cparams(("parallel", "parallel")),
        name="ada_mod",
    )(c.reshape(B, D, 1), w_ada, b_ada.reshape(L, 1, N))


def _inproj_kernel(x_ref, sh_ref, sc_ref, w_ref, s_ref, b_ref, wf_ref, bf_ref, z_ref, lf_ref,
                   h_scr):
    @pl.when(pl.program_id(2) == 0)
    def _():
        h = x_ref[0] * (1.0 + sc_ref[0]) + sh_ref[0]
        hb = h.astype(BF16)
        h_scr[...] = hb
        zf = jnp.dot(hb, wf_ref[...], preferred_element_type=F32) + bf_ref[...]
        lt = zf.T[:8]
        lf_ref[0] = jnp.minimum(lt, 0.0) - jnp.log1p(jnp.exp(-jnp.abs(lt)))

    acc = lax.dot_general(h_scr[...], w_ref[...], _NT, preferred_element_type=F32)
    z_ref[0] = (acc * s_ref[...] + b_ref[...]).astype(BF16)


def _inproj(x, mod_l, w, scale, b, wf, bf, l):
    B, S, D = x.shape
    N = w.shape[1]
    tm, tn = min(TM_IN, S), min(TN_IN, N)
    return pl.pallas_call(
        _inproj_kernel,
        grid=(B, S // tm, N // tn),
        in_specs=[
            pl.BlockSpec((1, tm, D), lambda b, i, j: (b, i, 0)),
            pl.BlockSpec((1, 1, D), lambda b, i, j: (b, 0, 0)),
            pl.BlockSpec((1, 1, D), lambda b, i, j: (b, 0, 1)),
            pl.BlockSpec((None, tn, D), lambda b, i, j: (l, j, 0)),
            pl.BlockSpec((1, tn), lambda b, i, j: (0, j)),
            pl.BlockSpec((None, 1, tn), lambda b, i, j: (l, 0, j)),
            pl.BlockSpec((None, D, BF_PAD), lambda b, i, j: (l, 0, 0)),
            pl.BlockSpec((None, 1, BF_PAD), lambda b, i, j: (l, 0, 0)),
        ],
        out_specs=[
            pl.BlockSpec((1, tm, tn), lambda b, i, j: (b, i, j)),
            pl.BlockSpec((1, 8, tm), lambda b, i, j: (b, 0, i)),
        ],
        out_shape=[
            jax.ShapeDtypeStruct((B, S, N), BF16),
            jax.ShapeDtypeStruct((B, 8, S), F32),
        ],
        scratch_shapes=[pltpu.VMEM((tm, D), BF16)],
        compiler_params=_cparams(("parallel", "parallel", "arbitrary")),
        name="in_proj",
    )(x, mod_l, mod_l, w, scale, b, wf, bf)


def _cumsum_kernel(lf_ref, f_ref):
    x = lf_ref[0]
    lane = lax.broadcasted_iota(jnp.int32, x.shape, 1)
    k = 1
    while k < x.shape[1]:
        x = x + jnp.where(lane >= k, pltpu.roll(x, k, 1), 0.0)
        k *= 2
    f_ref[0] = x


def _cumsum(lf):
    B, R, S = lf.shape
    return pl.pallas_call(
        _cumsum_kernel,
        grid=(B,),
        in_specs=[pl.BlockSpec((1, R, S), lambda b: (b, 0, 0))],
        out_specs=pl.BlockSpec((1, R, S), lambda b: (b, 0, 0)),
        out_shape=jax.ShapeDtypeStruct((B, R, S), F32),
        compiler_params=_cparams(("parallel",)),
        name="forget_cumsum",
    )(lf)


def _build_vt(v_ref, vt_scr, tk):
    ones_row = (lax.broadcasted_iota(jnp.int32, (VT_ROWS - HEAD_W, tk), 0) == 0).astype(BF16)
    for c in range(v_ref.shape[1] // tk):
        vb = v_ref[0, c * tk:(c + 1) * tk, :].astype(F32)
        vt_scr[c, 0:HEAD_W, :] = vb.T.astype(BF16)
        vt_scr[c, HEAD_W:VT_ROWS, :] = ones_row


def _softmax_step_t(s, vt, m_scr, acc_scr):
    m_prev = m_scr[...]
    m_new = jnp.maximum(m_prev, jnp.max(s, axis=0, keepdims=True))
    p = jnp.exp2(s - m_new).astype(BF16)
    alpha = jnp.exp2(m_prev - m_new)
    acc_scr[...] = alpha * acc_scr[...] + jnp.dot(vt, p, preferred_element_type=F32)
    m_scr[...] = m_new


def _init_streams(streams, first_tile):
    for st in streams:
        st["m"][...] = jnp.full(st["m"].shape, NEG, F32)
        st["acc"][...] = jnp.zeros(st["acc"].shape, F32)
        st["sa"][...] = st["scores"](first_tile)


def _pair_body(streams):
    def pair(t, carry):
        j = 2 * t
        for st in streams:
            st["sb"][...] = st["scores"](j + 1)
            _softmax_step_t(st["sa"][...], st["vt"][j], st["m"], st["acc"])
        for st in streams:
            st["sa"][...] = st["scores"](j + 2)
            _softmax_step_t(st["sb"][...], st["vt"][j + 1], st["m"], st["acc"])
        return carry
    return pair


def _sweep(qi, streams):
    _init_streams(streams, 0)
    lax.fori_loop(0, qi // 2, _pair_body(streams), 0)

    @pl.when(qi % 2 == 0)
    def _():
        for st in streams:
            _softmax_step_t(st["mask"](st["sa"][...]), st["vt"][qi], st["m"], st["acc"])

    @pl.when(qi % 2 == 1)
    def _():
        for st in streams:
            st["sb"][...] = st["scores"](qi)
            _softmax_step_t(st["sa"][...], st["vt"][qi - 1], st["m"], st["acc"])
        for st in streams:
            _softmax_step_t(st["mask"](st["sb"][...]), st["vt"][qi], st["m"], st["acc"])


def _sweep_wide(qi, j0, streams):
    for st in streams:
        st["m"][...] = jnp.full(st["m"].shape, NEG, F32)
        st["acc"][...] = jnp.zeros(st["acc"].shape, F32)

    @pl.when(j0 % 2 == 1)
    def _():
        for st in streams:
            st["sb"][...] = st["scores"](j0)
        for st in streams:
            st["sa"][...] = st["scores"](j0 + 1)
            _softmax_step_t(st["sb"][...], st["vt"][j0], st["m"], st["acc"])

    @pl.when(j0 % 2 == 0)
    def _():
        for st in streams:
            st["sa"][...] = st["scores"](j0)

    lax.fori_loop((j0 + 1) // 2, qi, _pair_body(streams), 0)
    for st in streams:
        st["sl"][...] = st["scores_late"](2 * qi + 1)
        _softmax_step_t(st["mask"](st["sa"][...]), st["vt"][2 * qi], st["m"], st["acc"])
    for st in streams:
        nl = st["sl"].shape[1]
        _softmax_step_t(st["mask"](st["sl"][...]), st["vt"][2 * qi + 1],
                        st["m"].at[:, nl:], st["acc"].at[:, nl:])


_NT = (((1,), (1,)), ((), ()))


def _attn_scratch(S, tk, nq):
    return [pltpu.VMEM((S // tk, VT_ROWS, tk), BF16),
            pltpu.VMEM((1, nq), F32),
            pltpu.VMEM((VT_ROWS, nq), F32),
            pltpu.VMEM((tk, nq), F32), pltpu.VMEM((tk, nq), F32)]


def _head_cols(hh):
    return slice(hh * HEAD_W, (hh + 1) * HEAD_W)


def _diff_attn_kernel(q_ref, k_ref, v_ref, lq1_ref, lk1_ref, lq2_ref, lk2_ref, li_ref, g_ref,
                      o_ref, *scr, tq, chunk, nh):
    qi = pl.program_id(2)
    per = len(scr) // nh
    names = ("vt", "m", "acc", "sa", "sb")
    streams = [dict(zip(names, scr[hh * per:(hh + 1) * per])) for hh in range(nh)]

    @pl.when(qi == 0)
    def _():
        for hh, st in enumerate(streams):
            _build_vt(v_ref.at[:, :, _head_cols(hh)], st["vt"], tq)

    def mask_fn(s):
        key = lax.broadcasted_iota(jnp.int32, s.shape, 0)
        col = lax.broadcasted_iota(jnp.int32, s.shape, 1)
        qry = jnp.where(col >= tq, col - tq, col)
        return jnp.where(key // chunk <= qry // chunk, s, NEG)

    for hh, st in enumerate(streams):
        q = q_ref[0, :, _head_cols(hh)]
        lane = lax.broadcasted_iota(jnp.int32, q.shape, 1)
        zero = jnp.zeros_like(q)
        qs = jnp.concatenate([jnp.where(lane < A_DQK, q, zero),
                              jnp.where(lane >= A_DQK, q, zero)], axis=0)

        def scores(j, qs=qs, hh=hh):
            off = pl.multiple_of(j * tq, tq)
            return lax.dot_general(k_ref[0, pl.ds(off, tq), _head_cols(hh)], qs, _NT,
                                   preferred_element_type=F32)

        st["scores"], st["mask"] = scores, mask_fn

    _sweep(qi, streams)

    lam_init = li_ref[...]
    lam = (jnp.exp(jnp.sum(lq1_ref[...] * lk1_ref[...], axis=1, keepdims=True))
           - jnp.exp(jnp.sum(lq2_ref[...] * lk2_ref[...], axis=1, keepdims=True)) + lam_init)
    for hh, st in enumerate(streams):
        acc = st["acc"][...]
        ot_all = acc[0:HEAD_W] / acc[HEAD_W:HEAD_W + 1]
        o = (ot_all[:, :tq] - lam * ot_all[:, tq:]).T
        o = o * lax.rsqrt(jnp.mean(o * o, axis=1, keepdims=True) + LN_EPS)
        o_ref[0, :, _head_cols(hh)] = (o * g_ref[:, _head_cols(hh)] * (1.0 - lam_init)).astype(BF16)


def _resident(shape, index_map):
    return pl.BlockSpec(shape, index_map, pipeline_mode=pl.Buffered(1))


def _diff_attn(z, lq1, lk1, lq2, lk2, lam_init, g_diff, chunk):
    B, S, _ = z.shape
    tq = min(TQ_A, S)
    nh = NH_A
    w = nh * HEAD_W
    qb, kb, vb = Z_AQ // w, Z_AK // w, Z_AV // w
    small = pl.BlockSpec((1, A_DQK), lambda b, h, i: (0, 0))
    return pl.pallas_call(
        functools.partial(_diff_attn_kernel, tq=tq, chunk=chunk, nh=nh),
        grid=(B, HEADS // nh, S // tq),
        in_specs=[
            pl.BlockSpec((1, tq, w), lambda b, h, i: (b, i, qb + h)),
            pl.BlockSpec((1, S, w), lambda b, h, i: (b, 0, kb + h)),
            _resident((1, S, w), lambda b, h, i: (b, 0, vb + h)),
            small, small, small, small,
            pl.BlockSpec((1, 1), lambda b, h, i: (0, 0)),
            pl.BlockSpec((1, w), lambda b, h, i: (0, h)),
        ],
        out_specs=pl.BlockSpec((1, tq, w), lambda b, h, i: (b, i, h)),
        out_shape=jax.ShapeDtypeStruct((B, S, HEADS * HEAD_W), BF16),
        scratch_shapes=_attn_scratch(S, tq, 2 * tq) * nh,
        compiler_params=_cparams(("parallel", "parallel", "arbitrary"), VMEM_LIMIT_ATTN_A),
        name="diff_attn",
    )(z, z, z, lq1, lk1, lq2, lk2, lam_init, g_diff)


def _row_norm_max(x):
    xf = x.astype(F32)
    return jnp.sqrt(jnp.max(jnp.sum(xf * xf, axis=1, keepdims=True)))


def _fox_attn_kernel(q_ref, k_ref, v_ref, f_ref, o_ref, fmin_s, fmax_s, c0_s, kn_s, *scr, tk, nh):
    qi = pl.program_id(2)
    S = k_ref.shape[1]
    tq = 2 * tk
    n_blk = S // tq
    pc = min(PROLOGUE_CHUNK, S)
    per = len(scr) // nh
    names = ("frep", "vt", "m", "acc", "sa", "sb", "sl")
    streams = [dict(zip(names, scr[hh * per:(hh + 1) * per])) for hh in range(nh)]

    @pl.when(qi == 0)
    def _():
        for hh, st in enumerate(streams):
            _build_vt(v_ref.at[:, :, _head_cols(hh)], st["vt"], tk)
            kn = jnp.float32(0.0)
            for c in range(S // pc):
                fb = jnp.broadcast_to(f_ref[hh, :, c * pc:(c + 1) * pc] * LOG2E, (HEAD_W, pc))
                st["frep"][c * pc:(c + 1) * pc, :] = fb.T
                kn = jnp.maximum(kn, _row_norm_max(k_ref[0, c * pc:(c + 1) * pc, _head_cols(hh)]))
            kn_s[hh] = kn
            for t in range(n_blk):
                fb = f_ref[hh, :, t * tq:(t + 1) * tq] * LOG2E
                fmax_s[hh, t] = jnp.max(fb)
                c0_s[hh, t] = jnp.max(fb[:, 0:1])
            for j in range(2 * n_blk):
                fmin_s[hh, j] = jnp.min(f_ref[hh, :, j * tk:(j + 1) * tk] * LOG2E)

    def mask_fn(s):
        key = lax.broadcasted_iota(jnp.int32, s.shape, 0)
        qry = lax.broadcasted_iota(jnp.int32, s.shape, 1)
        return jnp.where(key <= qry, s, NEG)

    q_off = pl.multiple_of(qi * tq, tq)
    j0 = 2 * qi
    for hh, st in enumerate(streams):
        q = q_ref[0, :, _head_cols(hh)]
        c0 = st["frep"][pl.ds(q_off, 8), :][0:1, 0:1]

        qk = _row_norm_max(q) * kn_s[hh] * 1.01 + 1.0
        thr = 2.0 * qk + (fmax_s[hh, qi] - c0_s[hh, qi]) + SKIP_MARGIN
        j0h = jnp.int32(0)
        for j in range(2 * n_blk - 2):
            skip = (j < 2 * qi) & (j0h == j) & (fmin_s[hh, j] - c0_s[hh, qi] >= thr)
            j0h = j0h + skip.astype(jnp.int32)
        j0 = jnp.minimum(j0, j0h)

        def biased(j, qq, st=st, hh=hh, c0=c0):
            off = pl.multiple_of(j * tk, tk)
            fsub = st["frep"][pl.ds(off, tk), :] - c0
            s = lax.dot_general(k_ref[0, pl.ds(off, tk), _head_cols(hh)], qq, _NT,
                                preferred_element_type=F32)
            return s - jnp.tile(fsub, (1, qq.shape[0] // HEAD_W))

        st["scores"] = functools.partial(biased, qq=q)
        st["scores_late"] = functools.partial(biased, qq=q[tk:])
        st["mask"] = mask_fn

    _sweep_wide(qi, j0, streams)

    for hh, st in enumerate(streams):
        acc = st["acc"][...]
        o_ref[0, :, _head_cols(hh)] = (acc[0:HEAD_W] / acc[HEAD_W:HEAD_W + 1]).T.astype(BF16)


def _fox_attn(z, f3):
    B, S, _ = z.shape
    tk = min(TK_B, S // 2)
    tq = 2 * tk
    nh = NH_B
    w = nh * HEAD_W
    rows = f3.shape[0] // B
    qb, kb, vb = Z_BQ // w, Z_BK // w, Z_BV // w
    per_head = ([pltpu.VMEM((S, HEAD_W), F32)] + _attn_scratch(S, tk, tq)
                + [pltpu.VMEM((tk, tq // 2), F32)])
    return pl.pallas_call(
        functools.partial(_fox_attn_kernel, tk=tk, nh=nh),
        grid=(B, HEADS // nh, S // tq),
        in_specs=[
            pl.BlockSpec((1, tq, w), lambda b, h, i: (b, i, qb + h)),
            pl.BlockSpec((1, S, w), lambda b, h, i: (b, 0, kb + h)),
            pl.BlockSpec((1, S, w), lambda b, h, i: (b, 0, vb + h)),
            pl.BlockSpec((nh, 1, S), lambda b, h, i: (b * (rows // nh) + h, 0, 0)),
        ],
        out_specs=pl.BlockSpec((1, tq, w), lambda b, h, i: (b, i, h)),
        out_shape=jax.ShapeDtypeStruct((B, S, HEADS * HEAD_W), BF16),
        scratch_shapes=([pltpu.SMEM((nh, S // tk), F32)] + [pltpu.SMEM((nh, S // tq), F32)] * 2
                        + [pltpu.SMEM((nh,), F32)]
                        + per_head * nh),
        compiler_params=_cparams(("parallel", "parallel", "arbitrary")),
        name="fox_attn",
    )(z, z, z, f3)


def _retention_kernel(q_ref, k_ref, v_ref, cg_ref, cos_ref, sin_ref, dm_ref, g_ref, o_ref, r_scr,
                      *, L):
    @pl.when(pl.program_id(1) == 0)
    def _():
        r_scr[...] = jnp.zeros(r_scr.shape, F32)

    cosf = cos_ref[...]
    sinf = sin_ref[...]
    pos = lax.broadcasted_iota(jnp.int32, (L, 1), 0).astype(F32)

    for h in range(HEADS):
        lg = math.log(1.0 - 2.0 ** (-5.0 - h))
        qh = q_ref[0, :, h * HEAD_W:(h + 1) * HEAD_W].astype(F32)
        kh = k_ref[0, :, h * HEAD_W:(h + 1) * HEAD_W].astype(F32)
        qr = qh * cosf + pltpu.roll(qh, HEAD_W // 2, 1) * sinf
        kr = kh * cosf + pltpu.roll(kh, HEAD_W // 2, 1) * sinf
        v = v_ref[0, :, h * C_DV:(h + 1) * C_DV]
        s = lax.dot_general(qr.astype(BF16), kr.astype(BF16), _NT,
                            preferred_element_type=F32) * dm_ref[h]
        inner = jnp.dot(s.astype(BF16), v, preferred_element_type=F32)
        r = r_scr[h]
        q_dec = qr * jnp.exp(lg * (pos + 1.0))
        cross = jnp.dot(q_dec.astype(BF16), r.astype(BF16), preferred_element_type=F32)
        k_dec = kr * jnp.exp(lg * (L - 1.0 - pos))
        r_scr[h] = math.exp(lg * L) * r + jnp.dot(k_dec.T.astype(BF16), v, preferred_element_type=F32)
        o = inner + cross
        mu = jnp.mean(o, axis=1, keepdims=True)
        var = jnp.mean(jnp.square(o - mu), axis=1, keepdims=True)
        y = (o - mu) * lax.rsqrt(var + LN_EPS) * g_ref[:, h * C_DV:(h + 1) * C_DV]
        cg = cg_ref[0, :, h * C_DV:(h + 1) * C_DV].astype(F32)
        o_ref[0, :, h * C_DV:(h + 1) * C_DV] = (y * (cg * jax.nn.sigmoid(cg))).astype(BF16)


def _retention(z, cos_t, sin_t, g_ret):
    B, S, _ = z.shape
    L = min(L_RET, S)
    qk_w, v_w = HEADS * HEAD_W, HEADS * C_DV
    dist = np.arange(L)[:, None] - np.arange(L)[None, :]
    gam = 1.0 - 2.0 ** (-5.0 - np.arange(HEADS, dtype=np.float64))
    dm = jnp.asarray(np.where(dist >= 0, gam[:, None, None] ** np.maximum(dist, 0), 0.0), F32)
    return pl.pallas_call(
        functools.partial(_retention_kernel, L=L),
        grid=(B, S // L),
        in_specs=[
            pl.BlockSpec((1, L, qk_w), lambda b, n: (b, n, Z_CQ // qk_w)),
            pl.BlockSpec((1, L, qk_w), lambda b, n: (b, n, Z_CK // qk_w)),
            pl.BlockSpec((1, L, v_w), lambda b, n: (b, n, Z_CV // v_w)),
            pl.BlockSpec((1, L, v_w), lambda b, n: (b, n, Z_CG // v_w)),
            pl.BlockSpec((L, HEAD_W), lambda b, n: (n, 0)),
            pl.BlockSpec((L, HEAD_W), lambda b, n: (n, 0)),
            _const_spec(dm.shape),
            pl.BlockSpec((1, v_w), lambda b, n: (0, 0)),
        ],
        out_specs=pl.BlockSpec((1, L, v_w), lambda b, n: (b, n, 0)),
        out_shape=jax.ShapeDtypeStruct((B, S, v_w), BF16),
        scratch_shapes=[pltpu.VMEM((HEADS, HEAD_W, C_DV), F32)],
        compiler_params=_cparams(("parallel", "arbitrary")),
        name="retention",
    )(z, z, z, z, cos_t, sin_t, dm, g_ret)


def _rope_tables(S):
    half = HEAD_W // 2
    inv = 1.0 / (ROPE_BASE ** np.linspace(0.0, 1.0, half))
    ang = np.arange(S, dtype=np.float64)[:, None] * inv[None, :]
    cos, sin = np.cos(ang), np.sin(ang)
    return (jnp.asarray(np.concatenate([cos, cos], axis=1), F32),
            jnp.asarray(np.concatenate([-sin, sin], axis=1), F32))


def _layernorm(r, g, b):
    mu = jnp.mean(r, axis=1, keepdims=True)
    var = jnp.mean(jnp.square(r - mu), axis=1, keepdims=True)
    return (r - mu) * lax.rsqrt(var + LN_EPS) * g + b


def _merge_kernel(ya_ref, yb_ref, yc_ref, ga_ref, gb_ref, gc_ref, x_ref, gt_ref, sh2_ref, sc2_ref,
                  wpa_ref, wpb_ref, wpc_ref, wout_ref, lng_ref, lnb_ref, xo_ref, ho_ref, *, alpha):
    hs = x_ref.shape[1] // 2
    halves = [slice(0, hs), slice(hs, 2 * hs)]

    def branch(rows, y_ref, g_ref, w_ref):
        gate = jax.nn.sigmoid(g_ref[0, rows].astype(F32))
        return gate * jnp.dot(y_ref[0, rows], w_ref[...], preferred_element_type=F32)

    ms = [branch(r, ya_ref, ga_ref, wpa_ref) + branch(r, yb_ref, gb_ref, wpb_ref)
          + branch(r, yc_ref, gc_ref, wpc_ref) for r in halves]
    ys = [jnp.dot(m.astype(BF16), wout_ref[...], preferred_element_type=F32) for m in ms]
    for r, y in zip(halves, ys):
        xn = _layernorm(alpha * x_ref[0, r] + gt_ref[0] * y, lng_ref[...], lnb_ref[...])
        xo_ref[0, r] = xn
        ho_ref[0, r] = (xn * (1.0 + sc2_ref[0]) + sh2_ref[0]).astype(BF16)


def _const_spec(shape):
    return pl.BlockSpec(shape, lambda b, i: (0,) * len(shape), pipeline_mode=pl.Buffered(1))


def _layer_spec(stacked, l):
    shape = stacked.shape[1:]
    return pl.BlockSpec((None,) + shape, lambda b, i: (l,) + (0,) * len(shape),
                        pipeline_mode=pl.Buffered(1))


def _merge(ya, yb, yc, z, x, mod_l, wpa, wpb, wpc, wout, lng, lnb, alpha, l):
    B, S, D = x.shape
    tm = min(TM_MERGE, S)
    gblk = Z_GATES // D
    tok = lambda w, col: pl.BlockSpec((1, tm, w), lambda b, i: (b, i, col))
    modv = lambda col: pl.BlockSpec((1, 1, D), lambda b, i: (b, 0, col))
    return pl.pallas_call(
        functools.partial(_merge_kernel, alpha=alpha),
        grid=(B, S // tm),
        in_specs=[
            tok(ya.shape[2], 0), tok(yb.shape[2], 0), tok(yc.shape[2], 0),
            tok(D, gblk), tok(D, gblk + 1), tok(D, gblk + 2),
            tok(D, 0),
            modv(2), modv(3), modv(4),
            _layer_spec(wpa, l), _layer_spec(wpb, l), _layer_spec(wpc, l),
            _layer_spec(wout, l), _const_spec(lng.shape), _const_spec(lnb.shape),
        ],
        out_specs=[tok(D, 0), tok(D, 0)],
        out_shape=[jax.ShapeDtypeStruct((B, S, D), F32), jax.ShapeDtypeStruct((B, S, D), BF16)],
        compiler_params=_cparams(("parallel", "parallel")),
        name="merge_out_ln",
    )(ya, yb, yc, z, z, z, x, mod_l, mod_l, mod_l, wpa, wpb, wpc, wout, lng, lnb)


def _ffn_kernel(h_ref, x_ref, gt_ref, wup_ref, wconv_ref, bconv_ref, wdown_ref, lng_ref, lnb_ref,
                xo_ref, a_scr, tail_scr, *, alpha, cw):
    @pl.when(pl.program_id(1) == 0)
    def _():
        tail_scr[...] = jnp.zeros(tail_scr.shape, F32)

    h = h_ref[0]
    tm = h.shape[0]
    dff = a_scr.shape[1]
    row = lax.broadcasted_iota(jnp.int32, (tm, cw), 0)
    for ci in range(dff // cw):
        lo = ci * cw
        u = jnp.dot(h, wup_ref[:, lo:lo + cw], preferred_element_type=F32)
        g = jnp.dot(h, wup_ref[:, dff + lo:dff + lo + cw], preferred_element_type=F32)
        tail = tail_scr[:, lo:lo + cw]
        p1, p2 = tail[7:8], tail[6:7]
        um1 = jnp.where(row == 0, p1, pltpu.roll(u, 1, 0))
        um2 = jnp.where(row == 0, p2, jnp.where(row == 1, p1, pltpu.roll(u, 2, 0)))
        tail_scr[:, lo:lo + cw] = u[tm - 8:]
        conv = bconv_ref[:, lo:lo + cw] + um2 * wconv_ref[0:1, lo:lo + cw]
        conv = conv + um1 * wconv_ref[1:2, lo:lo + cw]
        conv = conv + u * wconv_ref[2:3, lo:lo + cw]
        act = 0.5 * conv * (1.0 + lax.erf(conv * (2.0 ** -0.5)))
        a_scr[:, lo:lo + cw] = (act * g).astype(BF16)
    halves = [slice(0, tm // 2), slice(tm // 2, tm)]
    ys = [jnp.dot(a_scr[r, :], wdown_ref[...], preferred_element_type=F32) for r in halves]
    for r, y in zip(halves, ys):
        xo_ref[0, r] = _layernorm(alpha * x_ref[0, r] + gt_ref[0] * y, lng_ref[...], lnb_ref[...])


def _ffn(h, x, mod_l, wup, wconv, bconv, wdown, lng, lnb, alpha, l):
    B, S, D = x.shape
    dff = wdown.shape[1]
    tm = min(TM_FFN, S)
    tok = pl.BlockSpec((1, tm, D), lambda b, i: (b, i, 0))
    return pl.pallas_call(
        functools.partial(_ffn_kernel, alpha=alpha, cw=CW_FFN),
        grid=(B, S // tm),
        in_specs=[
            tok, tok,
            pl.BlockSpec((1, 1, D), lambda b, i: (b, 0, 5)),
            _layer_spec(wup, l), _const_spec(wconv.shape), _const_spec(bconv.shape),
            _layer_spec(wdown, l), _const_spec(lng.shape), _const_spec(lnb.shape),
        ],
        out_specs=tok,
        out_shape=jax.ShapeDtypeStruct((B, S, D), F32),
        scratch_shapes=[pltpu.VMEM((tm, dff), BF16), pltpu.VMEM((8, dff), F32)],
        compiler_params=_cparams(("parallel", "arbitrary")),
        name="conv_ffn_ln",
    )(h, x, mod_l, wup, wconv, bconv, wdown, lng, lnb)


def _prep_in_proj(w, b):
    scale = np.ones((1, Z_WIDTH), np.float32)
    scale[:, Z_AQ:Z_AQ + 512] = A_DQK ** -0.5 * LOG2E
    scale[:, Z_BQ:Z_BQ + 512] = HEAD_W ** -0.5 * LOG2E
    scale[:, Z_CK:Z_CK + 512] = HEAD_W ** -0.5
    L, D, _ = w.shape
    wt = jnp.swapaxes(w, 1, 2).astype(BF16)
    tr = TR_PREP
    wm = pl.pallas_call(
        functools.partial(_row_removal_kernel, first_shifted=BF_OFF // tr, shift=HEADS),
        grid=(L, Z_WIDTH // tr),
        in_specs=[
            pl.BlockSpec((None, tr, D), lambda l, j: (l, j, 0)),
            pl.BlockSpec((None, 16, D), lambda l, j: (l, (j + 1) * (tr // 16), 0)),
        ],
        out_specs=pl.BlockSpec((None, tr, D), lambda l, j: (l, j, 0)),
        out_shape=jax.ShapeDtypeStruct((L, Z_WIDTH, D), BF16),
        compiler_params=_cparams(("parallel", "parallel")),
        name="w_in_rows",
    )(wt, wt)
    bm = jnp.concatenate([b[..., :BF_OFF], b[..., BF_OFF + HEADS:]], axis=-1) * scale
    pad = BF_PAD - HEADS
    wf = jnp.pad(w[..., BF_OFF:BF_OFF + HEADS], ((0, 0), (0, 0), (0, pad)))
    bf = jnp.pad(b[..., BF_OFF:BF_OFF + HEADS], ((0, 0), (0, pad)))
    return wm, jnp.asarray(scale), bm[:, None, :], wf.astype(BF16), bf[:, None, :]


def _row_removal_kernel(w_ref, wnext_ref, o_ref, *, first_shifted, shift):
    j = pl.program_id(1)

    @pl.when(j < first_shifted)
    def _():
        o_ref[...] = w_ref[...]

    @pl.when(j >= first_shifted)
    def _():
        tr = w_ref.shape[0]
        tall = jnp.concatenate([w_ref[...].astype(F32), wnext_ref[...].astype(F32)], axis=0)
        o_ref[...] = tall[shift:shift + tr].astype(BF16)


def kernel(x, c, w_ada, b_ada, w_in, b_in, lam_q1, lam_k1, lam_q2, lam_k2, g_diff, g_ret, w_pa, w_pb, w_pc, w_out, ln_g, ln_b, w_up, w_conv, b_conv, w_down):
    B, S, D = x.shape
    depth = w_ada.shape[0]
    alpha = (2 * depth) ** 0.25
    chunk = 64
    mod = _ada(c, w_ada, b_ada)
    cos_t, sin_t = _rope_tables(S)
    in_w = _prep_in_proj(w_in, b_in)
    wpa, wpb, wpc, wout = (w.astype(BF16) for w in (w_pa, w_pb, w_pc, w_out))
    wup, wdown = w_up.astype(BF16), w_down.astype(BF16)
    for l in range(depth):
        mod_l = mod[l].reshape(B, 1, 6 * D)
        lam_init = jnp.full((1, 1), 0.8 - 0.6 * math.exp(-0.3 * l), F32)
        z, lf = _inproj(x, mod_l, *in_w, l)
        f = _cumsum(lf)
        ya = _diff_attn(z, lam_q1[l][None], lam_k1[l][None], lam_q2[l][None], lam_k2[l][None],
                        lam_init, g_diff[l][None], chunk)
        yb = _fox_attn(z, f.reshape(B * f.shape[1], 1, S))
        yc = _retention(z, cos_t, sin_t, g_ret[l][None])
        x, h2 = _merge(ya, yb, yc, z, x, mod_l, wpa, wpb, wpc, wout,
                       ln_g[l, 0][None], ln_b[l, 0][None], alpha, l)
        x = _ffn(h2, x, mod_l, wup, w_conv[l], b_conv[l][None], wdown,
                 ln_g[l, 1][None], ln_b[l, 1][None], alpha, l)
    return x
```

```python
import functools
import math

import numpy as np
import jax
import jax.numpy as jnp
from jax import lax
from jax.experimental import pallas as pl
from jax.experimental.pallas import tpu as pltpu

F32 = jnp.float32
BF16 = jnp.bfloat16

HEADS = 4
A_DQK = 64
HEAD_W = 128
C_DV = 256
ROPE_BASE = 10000.0
LN_EPS = 1e-5
CONV_W = 3

Z_AQ, Z_AK, Z_AV = 0, 512, 1024
Z_BQ, Z_BK, Z_BV = 1536, 2048, 2560
Z_CQ, Z_CK, Z_CV, Z_CG = 3072, 3584, 4096, 5120
Z_GATES = 6144
Z_WIDTH = 9216
BF_OFF = 3072
BF_PAD = 128

NEG = -1e30
LOG2E = 1.4426950408889634
VT_ROWS = 144

TN_ADA = 1024
TM_IN, TN_IN = 1024, 2304
TN_PREP = 1024
TQ_A = 512
TK_B = 512
NH_A = 4
NH_B = 2
PROLOGUE_CHUNK = 512
SKIP_MARGIN = 160.0
L_RET = 256
TM_MERGE = 1024
TM_FFN = 1024
CW_FFN = 256

VMEM_LIMIT = 56 * 1024 * 1024
VMEM_LIMIT_ATTN_A = 60 * 1024 * 1024


def _cparams(sem, vmem_limit=VMEM_LIMIT):
    return pltpu.CompilerParams(dimension_semantics=sem, vmem_limit_bytes=vmem_limit)


def _ada_kernel(ct_ref, w_ref, b_ref, o_ref):
    w = w_ref[0]
    rows = []
    for b in range(ct_ref.shape[0]):
        cb = ct_ref[b]
        ca = cb * jax.nn.sigmoid(cb)
        rows.append(jnp.sum(ca * w, axis=0, keepdims=True))
    o_ref[0] = jnp.concatenate(rows, axis=0) + b_ref[0]


def _ada(c, w_ada, b_ada):
    B, D = c.shape
    L, _, N = w_ada.shape
    tn = min(TN_ADA, N)
    return pl.pallas_call(
        _ada_kernel,
        grid=(L, N // tn),
        in_specs=[
            pl.BlockSpec((B, D, 1), lambda l, j: (0, 0, 0)),
            pl.BlockSpec((1, D, tn), lambda l, j: (l, 0, j)),
            pl.BlockSpec((1, 1, tn), lambda l, j: (l, 0, j)),
        ],
        out_specs=pl.BlockSpec((1, B, tn), lambda l, j: (l, 0, j)),
        out_shape=jax.ShapeDtypeStruct((L, B, N), F32),
        compiler_params=_cparams(("parallel", "parallel")),
        name="ada_mod",
    )(c.reshape(B, D, 1), w_ada, b_ada.reshape(L, 1, N))


def _inproj_kernel(x_ref, sh_ref, sc_ref, w_ref, b_ref, wf_ref, bf_ref, z_ref, lf_ref, h_scr):
    @pl.when(pl.program_id(2) == 0)
    def _():
        h = x_ref[0] * (1.0 + sc_ref[0]) + sh_ref[0]
        hb = h.astype(BF16)
        h_scr[...] = hb
        zf = jnp.dot(hb, wf_ref[...], preferred_element_type=F32) + bf_ref[...]
        lt = zf.T[:8]
        lf_ref[0] = jnp.minimum(lt, 0.0) - jnp.log1p(jnp.exp(-jnp.abs(lt)))

    acc = jnp.dot(h_scr[...], w_ref[...], preferred_element_type=F32)
    z_ref[0] = (acc + b_ref[...]).astype(BF16)


def _inproj(x, mod_l, w, b, wf, bf, l):
    B, S, D = x.shape
    N = w.shape[2]
    tm, tn = min(TM_IN, S), min(TN_IN, N)
    return pl.pallas_call(
        _inproj_kernel,
        grid=(B, S // tm, N // tn),
        in_specs=[
            pl.BlockSpec((1, tm, D), lambda b, i, j: (b, i, 0)),
            pl.BlockSpec((1, 1, D), lambda b, i, j: (b, 0, 0)),
            pl.BlockSpec((1, 1, D), lambda b, i, j: (b, 0, 1)),
            pl.BlockSpec((None, D, tn), lambda b, i, j: (l, 0, j)),
            pl.BlockSpec((None, 1, tn), lambda b, i, j: (l, 0, j)),
            pl.BlockSpec((None, D, BF_PAD), lambda b, i, j: (l, 0, 0)),
            pl.BlockSpec((None, 1, BF_PAD), lambda b, i, j: (l, 0, 0)),
        ],
        out_specs=[
            pl.BlockSpec((1, tm, tn), lambda b, i, j: (b, i, j)),
            pl.BlockSpec((1, 8, tm), lambda b, i, j: (b, 0, i)),
        ],
        out_shape=[
            jax.ShapeDtypeStruct((B, S, N), BF16),
            jax.ShapeDtypeStruct((B, 8, S), F32),
        ],
        scratch_shapes=[pltpu.VMEM((tm, D), BF16)],
        compiler_params=_cparams(("parallel", "parallel", "arbitrary")),
        name="in_proj",
    )(x, mod_l, mod_l, w, b, wf, bf)


def _cumsum_kernel(lf_ref, f_ref):
    x = lf_ref[0]
    lane = lax.broadcasted_iota(jnp.int32, x.shape, 1)
    k = 1
    while k < x.shape[1]:
        x = x + jnp.where(lane >= k, pltpu.roll(x, k, 1), 0.0)
        k *= 2
    f_ref[0] = x


def _cumsum(lf):
    B, R, S = lf.shape
    return pl.pallas_call(
        _cumsum_kernel,
        grid=(B,),
        in_specs=[pl.BlockSpec((1, R, S), lambda b: (b, 0, 0))],
        out_specs=pl.BlockSpec((1, R, S), lambda b: (b, 0, 0)),
        out_shape=jax.ShapeDtypeStruct((B, R, S), F32),
        compiler_params=_cparams(("parallel",)),
        name="forget_cumsum",
    )(lf)


def _build_vt(v_ref, vt_scr, tk):
    ones_row = (lax.broadcasted_iota(jnp.int32, (VT_ROWS - HEAD_W, tk), 0) == 0).astype(BF16)
    for c in range(v_ref.shape[1] // tk):
        vb = v_ref[0, c * tk:(c + 1) * tk, :].astype(F32)
        vt_scr[c, 0:HEAD_W, :] = vb.T.astype(BF16)
        vt_scr[c, HEAD_W:VT_ROWS, :] = ones_row


def _softmax_step_t(s, vt, m_scr, acc_scr):
    m_prev = m_scr[...]
    m_new = jnp.maximum(m_prev, jnp.max(s, axis=0, keepdims=True))
    p = jnp.exp2(s - m_new).astype(BF16)
    alpha = jnp.exp2(m_prev - m_new)
    acc_scr[...] = alpha * acc_scr[...] + jnp.dot(vt, p, preferred_element_type=F32)
    m_scr[...] = m_new


def _init_streams(streams, first_tile):
    for st in streams:
        st["m"][...] = jnp.full(st["m"].shape, NEG, F32)
        st["acc"][...] = jnp.zeros(st["acc"].shape, F32)
        st["sa"][...] = st["scores"](first_tile)


def _pair_body(streams):
    def pair(t, carry):
        j = 2 * t
        for st in streams:
            st["sb"][...] = st["scores"](j + 1)
            _softmax_step_t(st["sa"][...], st["vt"][j], st["m"], st["acc"])
        for st in streams:
            st["sa"][...] = st["scores"](j + 2)
            _softmax_step_t(st["sb"][...], st["vt"][j + 1], st["m"], st["acc"])
        return carry
    return pair


def _sweep(qi, streams):
    _init_streams(streams, 0)
    lax.fori_loop(0, qi // 2, _pair_body(streams), 0)

    @pl.when(qi % 2 == 0)
    def _():
        for st in streams:
            _softmax_step_t(st["mask"](st["sa"][...]), st["vt"][qi], st["m"], st["acc"])

    @pl.when(qi % 2 == 1)
    def _():
        for st in streams:
            st["sb"][...] = st["scores"](qi)
            _softmax_step_t(st["sa"][...], st["vt"][qi - 1], st["m"], st["acc"])
        for st in streams:
            _softmax_step_t(st["mask"](st["sb"][...]), st["vt"][qi], st["m"], st["acc"])


def _sweep_wide(qi, j0, streams):
    for st in streams:
        st["m"][...] = jnp.full(st["m"].shape, NEG, F32)
        st["acc"][...] = jnp.zeros(st["acc"].shape, F32)

    @pl.when(j0 % 2 == 1)
    def _():
        for st in streams:
            st["sb"][...] = st["scores"](j0)
        for st in streams:
            st["sa"][...] = st["scores"](j0 + 1)
            _softmax_step_t(st["sb"][...], st["vt"][j0], st["m"], st["acc"])

    @pl.when(j0 % 2 == 0)
    def _():
        for st in streams:
            st["sa"][...] = st["scores"](j0)

    lax.fori_loop((j0 + 1) // 2, qi, _pair_body(streams), 0)
    for st in streams:
        st["sl"][...] = st["scores_late"](2 * qi + 1)
        _softmax_step_t(st["mask"](st["sa"][...]), st["vt"][2 * qi], st["m"], st["acc"])
    for st in streams:
        nl = st["sl"].shape[1]
        _softmax_step_t(st["mask"](st["sl"][...]), st["vt"][2 * qi + 1],
                        st["m"].at[:, nl:], st["acc"].at[:, nl:])


_NT = (((1,), (1,)), ((), ()))


def _attn_scratch(S, tk, nq):
    return [pltpu.VMEM((S // tk, VT_ROWS, tk), BF16),
            pltpu.VMEM((1, nq), F32),
            pltpu.VMEM((VT_ROWS, nq), F32),
            pltpu.VMEM((tk, nq), F32), pltpu.VMEM((tk, nq), F32)]


def _head_cols(hh):
    return slice(hh * HEAD_W, (hh + 1) * HEAD_W)


def _diff_attn_kernel(q_ref, k_ref, v_ref, lq1_ref, lk1_ref, lq2_ref, lk2_ref, li_ref, g_ref,
                      o_ref, *scr, tq, chunk, nh):
    qi = pl.program_id(2)
    per = len(scr) // nh
    names = ("vt", "m", "acc", "sa", "sb")
    streams = [dict(zip(names, scr[hh * per:(hh + 1) * per])) for hh in range(nh)]

    @pl.when(qi == 0)
    def _():
        for hh, st in enumerate(streams):
            _build_vt(v_ref.at[:, :, _head_cols(hh)], st["vt"], tq)

    def mask_fn(s):
        key = lax.broadcasted_iota(jnp.int32, s.shape, 0)
        col = lax.broadcasted_iota(jnp.int32, s.shape, 1)
        qry = jnp.where(col >= tq, col - tq, col)
        return jnp.where(key // chunk <= qry // chunk, s, NEG)

    for hh, st in enumerate(streams):
        q = q_ref[0, :, _head_cols(hh)]
        lane = lax.broadcasted_iota(jnp.int32, q.shape, 1)
        zero = jnp.zeros_like(q)
        qs = jnp.concatenate([jnp.where(lane < A_DQK, q, zero),
                              jnp.where(lane >= A_DQK, q, zero)], axis=0)

        def scores(j, qs=qs, hh=hh):
            off = pl.multiple_of(j * tq, tq)
            return lax.dot_general(k_ref[0, pl.ds(off, tq), _head_cols(hh)], qs, _NT,
                                   preferred_element_type=F32)

        st["scores"], st["mask"] = scores, mask_fn

    _sweep(qi, streams)

    lam_init = li_ref[...]
    lam = (jnp.exp(jnp.sum(lq1_ref[...] * lk1_ref[...], axis=1, keepdims=True))
           - jnp.exp(jnp.sum(lq2_ref[...] * lk2_ref[...], axis=1, keepdims=True)) + lam_init)
    for hh, st in enumerate(streams):
        acc = st["acc"][...]
        ot_all = acc[0:HEAD_W] / acc[HEAD_W:HEAD_W + 1]
        o = (ot_all[:, :tq] - lam * ot_all[:, tq:]).T
        o = o * lax.rsqrt(jnp.mean(o * o, axis=1, keepdims=True) + LN_EPS)
        o_ref[0, :, _head_cols(hh)] = (o * g_ref[:, _head_cols(hh)] * (1.0 - lam_init)).astype(BF16)


def _resident(shape, index_map):
    return pl.BlockSpec(shape, index_map, pipeline_mode=pl.Buffered(1))


def _diff_attn(z, lq1, lk1, lq2, lk2, lam_init, g_diff, chunk):
    B, S, _ = z.shape
    tq = min(TQ_A, S)
    nh = NH_A
    w = nh * HEAD_W
    qb, kb, vb = Z_AQ // w, Z_AK // w, Z_AV // w
    small = pl.BlockSpec((1, A_DQK), lambda b, h, i: (0, 0))
    return pl.pallas_call(
        functools.partial(_diff_attn_kernel, tq=tq, chunk=chunk, nh=nh),
        grid=(B, HEADS // nh, S // tq),
        in_specs=[
            pl.BlockSpec((1, tq, w), lambda b, h, i: (b, i, qb + h)),
            pl.BlockSpec((1, S, w), lambda b, h, i: (b, 0, kb + h)),
            _resident((1, S, w), lambda b, h, i: (b, 0, vb + h)),
            small, small, small, small,
            pl.BlockSpec((1, 1), lambda b, h, i: (0, 0)),
            pl.BlockSpec((1, w), lambda b, h, i: (0, h)),
        ],
        out_specs=pl.BlockSpec((1, tq, w), lambda b, h, i: (b, i, h)),
        out_shape=jax.ShapeDtypeStruct((B, S, HEADS * HEAD_W), BF16),
        scratch_shapes=_attn_scratch(S, tq, 2 * tq) * nh,
        compiler_params=_cparams(("parallel", "parallel", "arbitrary"), VMEM_LIMIT_ATTN_A),
        name="diff_attn",
    )(z, z, z, lq1, lk1, lq2, lk2, lam_init, g_diff)


def _row_norm_max(x):
    xf = x.astype(F32)
    return jnp.sqrt(jnp.max(jnp.sum(xf * xf, axis=1, keepdims=True)))


def _fox_attn_kernel(q_ref, k_ref, v_ref, f_ref, o_ref, fmin_s, fmax_s, c0_s, kn_s, *scr, tk, nh):
    qi = pl.program_id(2)
    S = k_ref.shape[1]
    tq = 2 * tk
    n_blk = S // tq
    pc = min(PROLOGUE_CHUNK, S)
    per = len(scr) // nh
    names = ("frep", "vt", "m", "acc", "sa", "sb", "sl")
    streams = [dict(zip(names, scr[hh * per:(hh + 1) * per])) for hh in range(nh)]

    @pl.when(qi == 0)
    def _():
        for hh, st in enumerate(streams):
            _build_vt(v_ref.at[:, :, _head_cols(hh)], st["vt"], tk)
            kn = jnp.float32(0.0)
            for c in range(S // pc):
                fb = jnp.broadcast_to(f_ref[hh, :, c * pc:(c + 1) * pc] * LOG2E, (HEAD_W, pc))
                st["frep"][c * pc:(c + 1) * pc, :] = fb.T
                kn = jnp.maximum(kn, _row_norm_max(k_ref[0, c * pc:(c + 1) * pc, _head_cols(hh)]))
            kn_s[hh] = kn
            for t in range(n_blk):
                fb = f_ref[hh, :, t * tq:(t + 1) * tq] * LOG2E
                fmax_s[hh, t] = jnp.max(fb)
                c0_s[hh, t] = jnp.max(fb[:, 0:1])
            for j in range(2 * n_blk):
                fmin_s[hh, j] = jnp.min(f_ref[hh, :, j * tk:(j + 1) * tk] * LOG2E)

    def mask_fn(s):
        key = lax.broadcasted_iota(jnp.int32, s.shape, 0)
        qry = lax.broadcasted_iota(jnp.int32, s.shape, 1)
        return jnp.where(key <= qry, s, NEG)

    q_off = pl.multiple_of(qi * tq, tq)
    j0 = 2 * qi
    for hh, st in enumerate(streams):
        q = q_ref[0, :, _head_cols(hh)]
        c0 = st["frep"][pl.ds(q_off, 8), :][0:1, 0:1]

        qk = _row_norm_max(q) * kn_s[hh] * 1.01 + 1.0
        thr = 2.0 * qk + (fmax_s[hh, qi] - c0_s[hh, qi]) + SKIP_MARGIN
        j0h = jnp.int32(0)
        for j in range(2 * n_blk - 2):
            skip = (j < 2 * qi) & (j0h == j) & (fmin_s[hh, j] - c0_s[hh, qi] >= thr)
            j0h = j0h + skip.astype(jnp.int32)
        j0 = jnp.minimum(j0, j0h)

        def biased(j, qq, st=st, hh=hh, c0=c0):
            off = pl.multiple_of(j * tk, tk)
            fsub = st["frep"][pl.ds(off, tk), :] - c0
            s = lax.dot_general(k_ref[0, pl.ds(off, tk), _head_cols(hh)], qq, _NT,
                                preferred_element_type=F32)
            return s - jnp.tile(fsub, (1, qq.shape[0] // HEAD_W))

        st["scores"] = functools.partial(biased, qq=q)
        st["scores_late"] = functools.partial(biased, qq=q[tk:])
        st["mask"] = mask_fn

    _sweep_wide(qi, j0, streams)

    for hh, st in enumerate(streams):
        acc = st["acc"][...]
        o_ref[0, :, _head_cols(hh)] = (acc[0:HEAD_W] / acc[HEAD_W:HEAD_W + 1]).T.astype(BF16)


def _fox_attn(z, f3):
    B, S, _ = z.shape
    tk = min(TK_B, S // 2)
    tq = 2 * tk
    nh = NH_B
    w = nh * HEAD_W
    rows = f3.shape[0] // B
    qb, kb, vb = Z_BQ // w, Z_BK // w, Z_BV // w
    per_head = ([pltpu.VMEM((S, HEAD_W), F32)] + _attn_scratch(S, tk, tq)
                + [pltpu.VMEM((tk, tq // 2), F32)])
    return pl.pallas_call(
        functools.partial(_fox_attn_kernel, tk=tk, nh=nh),
        grid=(B, HEADS // nh, S // tq),
        in_specs=[
            pl.BlockSpec((1, tq, w), lambda b, h, i: (b, i, qb + h)),
            pl.BlockSpec((1, S, w), lambda b, h, i: (b, 0, kb + h)),
            pl.BlockSpec((1, S, w), lambda b, h, i: (b, 0, vb + h)),
            pl.BlockSpec((nh, 1, S), lambda b, h, i: (b * (rows // nh) + h, 0, 0)),
        ],
        out_specs=pl.BlockSpec((1, tq, w), lambda b, h, i: (b, i, h)),
        out_shape=jax.ShapeDtypeStruct((B, S, HEADS * HEAD_W), BF16),
        scratch_shapes=([pltpu.SMEM((nh, S // tk), F32)] + [pltpu.SMEM((nh, S // tq), F32)] * 2
                        + [pltpu.SMEM((nh,), F32)]
                        + per_head * nh),
        compiler_params=_cparams(("parallel", "parallel", "arbitrary")),
        name="fox_attn",
    )(z, z, z, f3)


def _retention_kernel(q_ref, k_ref, v_ref, cg_ref, cos_ref, sin_ref, dm_ref, g_ref, o_ref, r_scr,
                      *, L):
    @pl.when(pl.program_id(1) == 0)
    def _():
        r_scr[...] = jnp.zeros(r_scr.shape, F32)

    cosf = cos_ref[...]
    sinf = sin_ref[...]
    pos = lax.broadcasted_iota(jnp.int32, (L, 1), 0).astype(F32)

    for h in range(HEADS):
        lg = math.log(1.0 - 2.0 ** (-5.0 - h))
        qh = q_ref[0, :, h * HEAD_W:(h + 1) * HEAD_W].astype(F32)
        kh = k_ref[0, :, h * HEAD_W:(h + 1) * HEAD_W].astype(F32)
        qr = qh * cosf + pltpu.roll(qh, HEAD_W // 2, 1) * sinf
        kr = kh * cosf + pltpu.roll(kh, HEAD_W // 2, 1) * sinf
        v = v_ref[0, :, h * C_DV:(h + 1) * C_DV]
        s = lax.dot_general(qr.astype(BF16), kr.astype(BF16), _NT,
                            preferred_element_type=F32) * dm_ref[h]
        inner = jnp.dot(s.astype(BF16), v, preferred_element_type=F32)
        r = r_scr[h]
        q_dec = qr * jnp.exp(lg * (pos + 1.0))
        cross = jnp.dot(q_dec.astype(BF16), r.astype(BF16), preferred_element_type=F32)
        k_dec = kr * jnp.exp(lg * (L - 1.0 - pos))
        r_scr[h] = math.exp(lg * L) * r + jnp.dot(k_dec.T.astype(BF16), v, preferred_element_type=F32)
        o = inner + cross
        mu = jnp.mean(o, axis=1, keepdims=True)
        var = jnp.mean(jnp.square(o - mu), axis=1, keepdims=True)
        y = (o - mu) * lax.rsqrt(var + LN_EPS) * g_ref[:, h * C_DV:(h + 1) * C_DV]
        cg = cg_ref[0, :, h * C_DV:(h + 1) * C_DV].astype(F32)
        o_ref[0, :, h * C_DV:(h + 1) * C_DV] = (y * (cg * jax.nn.sigmoid(cg))).astype(BF16)


def _retention(z, cos_t, sin_t, g_ret):
    B, S, _ = z.shape
    L = min(L_RET, S)
    qk_w, v_w = HEADS * HEAD_W, HEADS * C_DV
    dist = np.arange(L)[:, None] - np.arange(L)[None, :]
    gam = 1.0 - 2.0 ** (-5.0 - np.arange(HEADS, dtype=np.float64))
    dm = jnp.asarray(np.where(dist >= 0, gam[:, None, None] ** np.maximum(dist, 0), 0.0), F32)
    return pl.pallas_call(
        functools.partial(_retention_kernel, L=L),
        grid=(B, S // L),
        in_specs=[
            pl.BlockSpec((1, L, qk_w), lambda b, n: (b, n, Z_CQ // qk_w)),
            pl.BlockSpec((1, L, qk_w), lambda b, n: (b, n, Z_CK // qk_w)),
            pl.BlockSpec((1, L, v_w), lambda b, n: (b, n, Z_CV // v_w)),
            pl.BlockSpec((1, L, v_w), lambda b, n: (b, n, Z_CG // v_w)),
            pl.BlockSpec((L, HEAD_W), lambda b, n: (n, 0)),
            pl.BlockSpec((L, HEAD_W), lambda b, n: (n, 0)),
            _const_spec(dm.shape),
            pl.BlockSpec((1, v_w), lambda b, n: (0, 0)),
        ],
        out_specs=pl.BlockSpec((1, L, v_w), lambda b, n: (b, n, 0)),
        out_shape=jax.ShapeDtypeStruct((B, S, v_w), BF16),
        scratch_shapes=[pltpu.VMEM((HEADS, HEAD_W, C_DV), F32)],
        compiler_params=_cparams(("parallel", "arbitrary")),
        name="retention",
    )(z, z, z, z, cos_t, sin_t, dm, g_ret)


def _rope_tables(S):
    half = HEAD_W // 2
    inv = 1.0 / (ROPE_BASE ** np.linspace(0.0, 1.0, half))
    ang = np.arange(S, dtype=np.float64)[:, None] * inv[None, :]
    cos, sin = np.cos(ang), np.sin(ang)
    return (jnp.asarray(np.concatenate([cos, cos], axis=1), F32),
            jnp.asarray(np.concatenate([-sin, sin], axis=1), F32))


def _layernorm(r, g, b):
    mu = jnp.mean(r, axis=1, keepdims=True)
    var = jnp.mean(jnp.square(r - mu), axis=1, keepdims=True)
    return (r - mu) * lax.rsqrt(var + LN_EPS) * g + b


def _merge_kernel(ya_ref, yb_ref, yc_ref, ga_ref, gb_ref, gc_ref, x_ref, gt_ref, sh2_ref, sc2_ref,
                  wpa_ref, wpb_ref, wpc_ref, wout_ref, lng_ref, lnb_ref, xo_ref, ho_ref, *, alpha):
    hs = x_ref.shape[1] // 2
    halves = [slice(0, hs), slice(hs, 2 * hs)]

    def branch(rows, y_ref, g_ref, w_ref):
        gate = jax.nn.sigmoid(g_ref[0, rows].astype(F32))
        return gate * jnp.dot(y_ref[0, rows], w_ref[...], preferred_element_type=F32)

    ms = [branch(r, ya_ref, ga_ref, wpa_ref) + branch(r, yb_ref, gb_ref, wpb_ref)
          + branch(r, yc_ref, gc_ref, wpc_ref) for r in halves]
    ys = [jnp.dot(m.astype(BF16), wout_ref[...], preferred_element_type=F32) for m in ms]
    for r, y in zip(halves, ys):
        xn = _layernorm(alpha * x_ref[0, r] + gt_ref[0] * y, lng_ref[...], lnb_ref[...])
        xo_ref[0, r] = xn
        ho_ref[0, r] = (xn * (1.0 + sc2_ref[0]) + sh2_ref[0]).astype(BF16)


def _const_spec(shape):
    return pl.BlockSpec(shape, lambda b, i: (0,) * len(shape), pipeline_mode=pl.Buffered(1))


def _layer_spec(stacked, l):
    shape = stacked.shape[1:]
    return pl.BlockSpec((None,) + shape, lambda b, i: (l,) + (0,) * len(shape),
                        pipeline_mode=pl.Buffered(1))


def _merge(ya, yb, yc, z, x, mod_l, wpa, wpb, wpc, wout, lng, lnb, alpha, l):
    B, S, D = x.shape
    tm = min(TM_MERGE, S)
    gblk = Z_GATES // D
    tok = lambda w, col: pl.BlockSpec((1, tm, w), lambda b, i: (b, i, col))
    modv = lambda col: pl.BlockSpec((1, 1, D), lambda b, i: (b, 0, col))
    return pl.pallas_call(
        functools.partial(_merge_kernel, alpha=alpha),
        grid=(B, S // tm),
        in_specs=[
            tok(ya.shape[2], 0), tok(yb.shape[2], 0), tok(yc.shape[2], 0),
            tok(D, gblk), tok(D, gblk + 1), tok(D, gblk + 2),
            tok(D, 0),
            modv(2), modv(3), modv(4),
            _layer_spec(wpa, l), _layer_spec(wpb, l), _layer_spec(wpc, l),
            _layer_spec(wout, l), _const_spec(lng.shape), _const_spec(lnb.shape),
        ],
        out_specs=[tok(D, 0), tok(D, 0)],
        out_shape=[jax.ShapeDtypeStruct((B, S, D), F32), jax.ShapeDtypeStruct((B, S, D), BF16)],
        compiler_params=_cparams(("parallel", "parallel")),
        name="merge_out_ln",
    )(ya, yb, yc, z, z, z, x, mod_l, mod_l, mod_l, wpa, wpb, wpc, wout, lng, lnb)


def _ffn_kernel(h_ref, x_ref, gt_ref, wup_ref, wconv_ref, bconv_ref, wdown_ref, lng_ref, lnb_ref,
                xo_ref, a_scr, tail_scr, *, alpha, cw):
    @pl.when(pl.program_id(1) == 0)
    def _():
        tail_scr[...] = jnp.zeros(tail_scr.shape, F32)

    h = h_ref[0]
    tm = h.shape[0]
    dff = a_scr.shape[1]
    row = lax.broadcasted_iota(jnp.int32, (tm, cw), 0)
    for ci in range(dff // cw):
        lo = ci * cw
        u = jnp.dot(h, wup_ref[:, lo:lo + cw], preferred_element_type=F32)
        g = jnp.dot(h, wup_ref[:, dff + lo:dff + lo + cw], preferred_element_type=F32)
        tail = tail_scr[:, lo:lo + cw]
        p1, p2 = tail[7:8], tail[6:7]
        um1 = jnp.where(row == 0, p1, pltpu.roll(u, 1, 0))
        um2 = jnp.where(row == 0, p2, jnp.where(row == 1, p1, pltpu.roll(u, 2, 0)))
        tail_scr[:, lo:lo + cw] = u[tm - 8:]
        conv = bconv_ref[:, lo:lo + cw] + um2 * wconv_ref[0:1, lo:lo + cw]
        conv = conv + um1 * wconv_ref[1:2, lo:lo + cw]
        conv = conv + u * wconv_ref[2:3, lo:lo + cw]
        act = 0.5 * conv * (1.0 + lax.erf(conv * (2.0 ** -0.5)))
        a_scr[:, lo:lo + cw] = (act * g).astype(BF16)
    halves = [slice(0, tm // 2), slice(tm // 2, tm)]
    ys = [jnp.dot(a_scr[r, :], wdown_ref[...], preferred_element_type=F32) for r in halves]
    for r, y in zip(halves, ys):
        xo_ref[0, r] = _layernorm(alpha * x_ref[0, r] + gt_ref[0] * y, lng_ref[...], lnb_ref[...])


def _ffn(h, x, mod_l, wup, wconv, bconv, wdown, lng, lnb, alpha, l):
    B, S, D = x.shape
    dff = wdown.shape[1]
    tm = min(TM_FFN, S)
    tok = pl.BlockSpec((1, tm, D), lambda b, i: (b, i, 0))
    return pl.pallas_call(
        functools.partial(_ffn_kernel, alpha=alpha, cw=CW_FFN),
        grid=(B, S // tm),
        in_specs=[
            tok, tok,
            pl.BlockSpec((1, 1, D), lambda b, i: (b, 0, 5)),
            _layer_spec(wup, l), _const_spec(wconv.shape), _const_spec(bconv.shape),
            _layer_spec(wdown, l), _const_spec(lng.shape), _const_spec(lnb.shape),
        ],
        out_specs=tok,
        out_shape=jax.ShapeDtypeStruct((B, S, D), F32),
        scratch_shapes=[pltpu.VMEM((tm, dff), BF16), pltpu.VMEM((8, dff), F32)],
        compiler_params=_cparams(("parallel", "arbitrary")),
        name="conv_ffn_ln",
    )(h, x, mod_l, wup, wconv, bconv, wdown, lng, lnb)


def _prep_in_proj(w, b):
    scale = np.ones((1, Z_WIDTH), np.float32)
    scale[:, Z_AQ:Z_AQ + 512] = A_DQK ** -0.5 * LOG2E
    scale[:, Z_BQ:Z_BQ + 512] = HEAD_W ** -0.5 * LOG2E
    scale[:, Z_CK:Z_CK + 512] = HEAD_W ** -0.5
    L, D, _ = w.shape
    tn = TN_PREP
    wm = pl.pallas_call(
        functools.partial(_win_prep_kernel, first_shifted=BF_OFF // tn, shift=HEADS),
        grid=(L, Z_WIDTH // tn),
        in_specs=[
            pl.BlockSpec((None, D, tn), lambda l, j: (l, 0, j)),
            pl.BlockSpec((None, D, HEAD_W), lambda l, j: (l, 0, (j + 1) * (tn // HEAD_W))),
            pl.BlockSpec((1, tn), lambda l, j: (0, j)),
        ],
        out_specs=pl.BlockSpec((None, D, tn), lambda l, j: (l, 0, j)),
        out_shape=jax.ShapeDtypeStruct((L, D, Z_WIDTH), BF16),
        compiler_params=_cparams(("parallel", "parallel")),
        name="w_in_prep",
    )(w, w, jnp.asarray(scale))
    bm = jnp.concatenate([b[..., :BF_OFF], b[..., BF_OFF + HEADS:]], axis=-1) * scale
    pad = BF_PAD - HEADS
    wf = jnp.pad(w[..., BF_OFF:BF_OFF + HEADS], ((0, 0), (0, 0), (0, pad)))
    bf = jnp.pad(b[..., BF_OFF:BF_OFF + HEADS], ((0, 0), (0, pad)))
    return wm, bm[:, None, :], wf.astype(BF16), bf[:, None, :]


def _win_prep_kernel(w_ref, wnext_ref, s_ref, o_ref, *, first_shifted, shift):
    j = pl.program_id(1)

    @pl.when(j < first_shifted)
    def _():
        o_ref[...] = (w_ref[...] * s_ref[...]).astype(BF16)

    @pl.when(j >= first_shifted)
    def _():
        tn = w_ref.shape[1]
        wide = jnp.concatenate([w_ref[...], wnext_ref[...]], axis=1)
        moved = pltpu.roll(wide, wide.shape[1] - shift, 1)[:, :tn]
        o_ref[...] = (moved * s_ref[...]).astype(BF16)


def kernel(x, c, w_ada, b_ada, w_in, b_in, lam_q1, lam_k1, lam_q2, lam_k2, g_diff, g_ret, w_pa, w_pb, w_pc, w_out, ln_g, ln_b, w_up, w_conv, b_conv, w_down):
    B, S, D = x.shape
    depth = w_ada.shape[0]
    alpha = (2 * depth) ** 0.25
    chunk = 64
    mod = _ada(c, w_ada, b_ada)
    cos_t, sin_t = _rope_tables(S)
    wm, bm, wf, bf = _prep_in_proj(w_in, b_in)
    wpa, wpb, wpc, wout = (w.astype(BF16) for w in (w_pa, w_pb, w_pc, w_out))
    wup, wdown = w_up.astype(BF16), w_down.astype(BF16)
    for l in range(depth):
        mod_l = mod[l].reshape(B, 1, 6 * D)
        lam_init = jnp.full((1, 1), 0.8 - 0.6 * math.exp(-0.3 * l), F32)
        z, lf = _inproj(x, mod_l, wm, bm, wf, bf, l)
        f = _cumsum(lf)
        ya = _diff_attn(z, lam_q1[l][None], lam_k1[l][None], lam_q2[l][None], lam_k2[l][None],
                        lam_init, g_diff[l][None], chunk)
        yb = _fox_attn(z, f.reshape(B * f.shape[1], 1, S))
        yc = _retention(z, cos_t, sin_t, g_ret[l][None])
        x, h2 = _merge(ya, yb, yc, z, x, mod_l, wpa, wpb, wpc, wout,
                       ln_g[l, 0][None], ln_b[l, 0][None], alpha, l)
        x = _ffn(h2, x, mod_l, wup, w_conv[l], b_conv[l][None], wdown,
                 ln_g[l, 1][None], ln_b[l, 1][None], alpha, l)
    return x
```

```python
import functools
import math

import numpy as np
import jax
import jax.numpy as jnp
from jax import lax
from jax.experimental import pallas as pl
from jax.experimental.pallas import tpu as pltpu

F32 = jnp.float32
BF16 = jnp.bfloat16

HEADS = 4
A_DQK = 64
HEAD_W = 128
C_DV = 256
ROPE_BASE = 10000.0
LN_EPS = 1e-5
CONV_W = 3

Z_AQ, Z_AK, Z_AV = 0, 512, 1024
Z_BQ, Z_BK, Z_BV = 1536, 2048, 2560
Z_CQ, Z_CK, Z_CV, Z_CG = 3072, 3584, 4096, 5120
Z_GATES = 6144
Z_WIDTH = 9216
BF_OFF = 3072
BF_PAD = 128

NEG = -1e30
LOG2E = 1.4426950408889634
VT_ROWS = 144

TN_ADA = 1024
TM_IN, TN_IN = 1024, 2304
TN_PREP = 1024
TQ_A = 512
TK_B = 512
NH_A = 4
NH_B = 2
PROLOGUE_CHUNK = 512
SKIP_MARGIN = 160.0
L_RET = 256
TM_MERGE = 1024
TM_FFN = 1024
CW_FFN = 256

VMEM_LIMIT = 56 * 1024 * 1024
VMEM_LIMIT_ATTN_A = 60 * 1024 * 1024


def _cparams(sem, vmem_limit=VMEM_LIMIT):
    return pltpu.CompilerParams(dimension_semantics=sem, vmem_limit_bytes=vmem_limit)


def _ada_kernel(ct_ref, w_ref, b_ref, o_ref):
    w = w_ref[0]
    rows = []
    for b in range(ct_ref.shape[0]):
        cb = ct_ref[b]
        ca = cb * jax.nn.sigmoid(cb)
        rows.append(jnp.sum(ca * w, axis=0, keepdims=True))
    o_ref[0] = jnp.concatenate(rows, axis=0) + b_ref[0]


def _ada(c, w_ada, b_ada):
    B, D = c.shape
    L, _, N = w_ada.shape
    tn = min(TN_ADA, N)
    return pl.pallas_call(
        _ada_kernel,
        grid=(L, N // tn),
        in_specs=[
            pl.BlockSpec((B, D, 1), lambda l, j: (0, 0, 0)),
            pl.BlockSpec((1, D, tn), lambda l, j: (l, 0, j)),
            pl.BlockSpec((1, 1, tn), lambda l, j: (l, 0, j)),
        ],
        out_specs=pl.BlockSpec((1, B, tn), lambda l, j: (l, 0, j)),
        out_shape=jax.ShapeDtypeStruct((L, B, N), F32),
        compiler_params=_cparams(("parallel", "parallel")),
        name="ada_mod",
    )(c.reshape(B, D, 1), w_ada, b_ada.reshape(L, 1, N))


def _inproj_kernel(x_ref, sh_ref, sc_ref, w_ref, b_ref, wf_ref, bf_ref, z_ref, lf_ref, h_scr):
    @pl.when(pl.program_id(2) == 0)
    def _():
        h = x_ref[0] * (1.0 + sc_ref[0]) + sh_ref[0]
        hb = h.astype(BF16)
        h_scr[...] = hb
        zf = jnp.dot(hb, wf_ref[...], preferred_element_type=F32) + bf_ref[...]
        lt = zf.T[:8]
        lf_ref[0] = jnp.minimum(lt, 0.0) - jnp.log1p(jnp.exp(-jnp.abs(lt)))

    acc = jnp.dot(h_scr[...], w_ref[...], preferred_element_type=F32)
    z_ref[0] = (acc + b_ref[...]).astype(BF16)


def _inproj(x, mod_l, w, b, wf, bf, l):
    B, S, D = x.shape
    N = w.shape[2]
    tm, tn = min(TM_IN, S), min(TN_IN, N)
    return pl.pallas_call(
        _inproj_kernel,
        grid=(B, S // tm, N // tn),
        in_specs=[
            pl.BlockSpec((1, tm, D), lambda b, i, j: (b, i, 0)),
            pl.BlockSpec((1, 1, D), lambda b, i, j: (b, 0, 0)),
            pl.BlockSpec((1, 1, D), lambda b, i, j: (b, 0, 1)),
            pl.BlockSpec((None, D, tn), lambda b, i, j: (l, 0, j)),
            pl.BlockSpec((None, 1, tn), lambda b, i, j: (l, 0, j)),
            pl.BlockSpec((None, D, BF_PAD), lambda b, i, j: (l, 0, 0)),
            pl.BlockSpec((None, 1, BF_PAD), lambda b, i, j: (l, 0, 0)),
        ],
        out_specs=[
            pl.BlockSpec((1, tm, tn), lambda b, i, j: (b, i, j)),
            pl.BlockSpec((1, 8, tm), lambda b, i, j: (b, 0, i)),
        ],
        out_shape=[
            jax.ShapeDtypeStruct((B, S, N), BF16),
            jax.ShapeDtypeStruct((B, 8, S), F32),
        ],
        scratch_shapes=[pltpu.VMEM((tm, D), BF16)],
        compiler_params=_cparams(("parallel", "parallel", "arbitrary")),
        name="in_proj",
    )(x, mod_l, mod_l, w, b, wf, bf)


def _cumsum_kernel(lf_ref, f_ref):
    x = lf_ref[0]
    lane = lax.broadcasted_iota(jnp.int32, x.shape, 1)
    k = 1
    while k < x.shape[1]:
        x = x + jnp.where(lane >= k, pltpu.roll(x, k, 1), 0.0)
        k *= 2
    f_ref[0] = x


def _cumsum(lf):
    B, R, S = lf.shape
    return pl.pallas_call(
        _cumsum_kernel,
        grid=(B,),
        in_specs=[pl.BlockSpec((1, R, S), lambda b: (b, 0, 0))],
        out_specs=pl.BlockSpec((1, R, S), lambda b: (b, 0, 0)),
        out_shape=jax.ShapeDtypeStruct((B, R, S), F32),
        compiler_params=_cparams(("parallel",)),
        name="forget_cumsum",
    )(lf)


def _build_vt(v_ref, vt_scr, tk):
    ones_row = (lax.broadcasted_iota(jnp.int32, (VT_ROWS - HEAD_W, tk), 0) == 0).astype(BF16)
    for c in range(v_ref.shape[1] // tk):
        vb = v_ref[0, c * tk:(c + 1) * tk, :].astype(F32)
        vt_scr[c, 0:HEAD_W, :] = vb.T.astype(BF16)
        vt_scr[c, HEAD_W:VT_ROWS, :] = ones_row


def _softmax_step_t(s, vt, m_scr, acc_scr, colmax=None):
    m_prev = m_scr[...]
    if colmax is None:
        colmax = jnp.max(s, axis=0, keepdims=True)
    m_new = jnp.maximum(m_prev, colmax)
    p = jnp.exp2(s - m_new).astype(BF16)
    alpha = jnp.exp2(m_prev - m_new)
    acc_scr[...] = alpha * acc_scr[...] + jnp.dot(vt, p, preferred_element_type=F32)
    m_scr[...] = m_new


def _put_scores(st, buf, mx, j):
    s = st["scores"](j)
    st[buf][...] = s
    st[mx][...] = jnp.max(s, axis=0, keepdims=True)


def _init_streams(streams, first_tile):
    for st in streams:
        st["m"][...] = jnp.full(st["m"].shape, NEG, F32)
        st["acc"][...] = jnp.zeros(st["acc"].shape, F32)
        _put_scores(st, "sa", "ma", first_tile)


def _pair_body(streams):
    def pair(t, carry):
        j = 2 * t
        for st in streams:
            _put_scores(st, "sb", "mb", j + 1)
            _softmax_step_t(st["sa"][...], st["vt"][j], st["m"], st["acc"], st["ma"][...])
        for st in streams:
            _put_scores(st, "sa", "ma", j + 2)
            _softmax_step_t(st["sb"][...], st["vt"][j + 1], st["m"], st["acc"], st["mb"][...])
        return carry
    return pair


def _sweep(qi, streams):
    _init_streams(streams, 0)
    lax.fori_loop(0, qi // 2, _pair_body(streams), 0)

    @pl.when(qi % 2 == 0)
    def _():
        for st in streams:
            _softmax_step_t(st["mask"](st["sa"][...]), st["vt"][qi], st["m"], st["acc"])

    @pl.when(qi % 2 == 1)
    def _():
        for st in streams:
            st["sb"][...] = st["scores"](qi)
            _softmax_step_t(st["sa"][...], st["vt"][qi - 1], st["m"], st["acc"], st["ma"][...])
        for st in streams:
            _softmax_step_t(st["mask"](st["sb"][...]), st["vt"][qi], st["m"], st["acc"])


def _sweep_wide(qi, j0, streams):
    for st in streams:
        st["m"][...] = jnp.full(st["m"].shape, NEG, F32)
        st["acc"][...] = jnp.zeros(st["acc"].shape, F32)

    @pl.when(j0 % 2 == 1)
    def _():
        for st in streams:
            _put_scores(st, "sb", "mb", j0)
        for st in streams:
            _put_scores(st, "sa", "ma", j0 + 1)
            _softmax_step_t(st["sb"][...], st["vt"][j0], st["m"], st["acc"], st["mb"][...])

    @pl.when(j0 % 2 == 0)
    def _():
        for st in streams:
            _put_scores(st, "sa", "ma", j0)

    lax.fori_loop((j0 + 1) // 2, qi, _pair_body(streams), 0)
    for st in streams:
        st["sl"][...] = st["scores_late"](2 * qi + 1)
        _softmax_step_t(st["mask"](st["sa"][...]), st["vt"][2 * qi], st["m"], st["acc"])
    for st in streams:
        nl = st["sl"].shape[1]
        _softmax_step_t(st["mask"](st["sl"][...]), st["vt"][2 * qi + 1],
                        st["m"].at[:, nl:], st["acc"].at[:, nl:])


_NT = (((1,), (1,)), ((), ()))


def _attn_scratch(S, tk, nq):
    return [pltpu.VMEM((S // tk, VT_ROWS, tk), BF16),
            pltpu.VMEM((1, nq), F32),
            pltpu.VMEM((VT_ROWS, nq), F32),
            pltpu.VMEM((tk, nq), F32), pltpu.VMEM((tk, nq), F32),
            pltpu.VMEM((1, nq), F32), pltpu.VMEM((1, nq), F32)]


def _head_cols(hh):
    return slice(hh * HEAD_W, (hh + 1) * HEAD_W)


def _diff_attn_kernel(q_ref, k_ref, v_ref, lq1_ref, lk1_ref, lq2_ref, lk2_ref, li_ref, g_ref,
                      o_ref, *scr, tq, chunk, nh):
    qi = pl.program_id(2)
    per = len(scr) // nh
    names = ("vt", "m", "acc", "sa", "sb", "ma", "mb")
    streams = [dict(zip(names, scr[hh * per:(hh + 1) * per])) for hh in range(nh)]

    @pl.when(qi == 0)
    def _():
        for hh, st in enumerate(streams):
            _build_vt(v_ref.at[:, :, _head_cols(hh)], st["vt"], tq)

    def mask_fn(s):
        key = lax.broadcasted_iota(jnp.int32, s.shape, 0)
        col = lax.broadcasted_iota(jnp.int32, s.shape, 1)
        qry = jnp.where(col >= tq, col - tq, col)
        return jnp.where(key // chunk <= qry // chunk, s, NEG)

    for hh, st in enumerate(streams):
        q = q_ref[0, :, _head_cols(hh)]
        lane = lax.broadcasted_iota(jnp.int32, q.shape, 1)
        zero = jnp.zeros_like(q)
        qs = jnp.concatenate([jnp.where(lane < A_DQK, q, zero),
                              jnp.where(lane >= A_DQK, q, zero)], axis=0)

        def scores(j, qs=qs, hh=hh):
            off = pl.multiple_of(j * tq, tq)
            return lax.dot_general(k_ref[0, pl.ds(off, tq), _head_cols(hh)], qs, _NT,
                                   preferred_element_type=F32)

        st["scores"], st["mask"] = scores, mask_fn

    _sweep(qi, streams)

    lam_init = li_ref[...]
    lam = (jnp.exp(jnp.sum(lq1_ref[...] * lk1_ref[...], axis=1, keepdims=True))
           - jnp.exp(jnp.sum(lq2_ref[...] * lk2_ref[...], axis=1, keepdims=True)) + lam_init)
    for hh, st in enumerate(streams):
        acc = st["acc"][...]
        ot_all = acc[0:HEAD_W] / acc[HEAD_W:HEAD_W + 1]
        o = (ot_all[:, :tq] - lam * ot_all[:, tq:]).T
        o = o * lax.rsqrt(jnp.mean(o * o, axis=1, keepdims=True) + LN_EPS)
        o_ref[0, :, _head_cols(hh)] = (o * g_ref[:, _head_cols(hh)] * (1.0 - lam_init)).astype(BF16)


def _resident(shape, index_map):
    return pl.BlockSpec(shape, index_map, pipeline_mode=pl.Buffered(1))


def _diff_attn(z, lq1, lk1, lq2, lk2, lam_init, g_diff, chunk):
    B, S, _ = z.shape
    tq = min(TQ_A, S)
    nh = NH_A
    w = nh * HEAD_W
    qb, kb, vb = Z_AQ // w, Z_AK // w, Z_AV // w
    small = pl.BlockSpec((1, A_DQK), lambda b, h, i: (0, 0))
    return pl.pallas_call(
        functools.partial(_diff_attn_kernel, tq=tq, chunk=chunk, nh=nh),
        grid=(B, HEADS // nh, S // tq),
        in_specs=[
            pl.BlockSpec((1, tq, w), lambda b, h, i: (b, i, qb + h)),
            pl.BlockSpec((1, S, w), lambda b, h, i: (b, 0, kb + h)),
            _resident((1, S, w), lambda b, h, i: (b, 0, vb + h)),
            small, small, small, small,
            pl.BlockSpec((1, 1), lambda b, h, i: (0, 0)),
            pl.BlockSpec((1, w), lambda b, h, i: (0, h)),
        ],
        out_specs=pl.BlockSpec((1, tq, w), lambda b, h, i: (b, i, h)),
        out_shape=jax.ShapeDtypeStruct((B, S, HEADS * HEAD_W), BF16),
        scratch_shapes=_attn_scratch(S, tq, 2 * tq) * nh,
        compiler_params=_cparams(("parallel", "parallel", "arbitrary"), VMEM_LIMIT_ATTN_A),
        name="diff_attn",
    )(z, z, z, lq1, lk1, lq2, lk2, lam_init, g_diff)


def _row_norm_max(x):
    xf = x.astype(F32)
    return jnp.sqrt(jnp.max(jnp.sum(xf * xf, axis=1, keepdims=True)))


def _fox_attn_kernel(q_ref, k_ref, v_ref, f_ref, o_ref, fmin_s, fmax_s, c0_s, kn_s, *scr, tk, nh):
    qi = pl.program_id(2)
    S = k_ref.shape[1]
    tq = 2 * tk
    n_blk = S // tq
    pc = min(PROLOGUE_CHUNK, S)
    per = len(scr) // nh
    names = ("frep", "vt", "m", "acc", "sa", "sb", "ma", "mb", "sl")
    streams = [dict(zip(names, scr[hh * per:(hh + 1) * per])) for hh in range(nh)]

    @pl.when(qi == 0)
    def _():
        for hh, st in enumerate(streams):
            _build_vt(v_ref.at[:, :, _head_cols(hh)], st["vt"], tk)
            kn = jnp.float32(0.0)
            for c in range(S // pc):
                fb = jnp.broadcast_to(f_ref[hh, :, c * pc:(c + 1) * pc] * LOG2E, (HEAD_W, pc))
                st["frep"][c * pc:(c + 1) * pc, :] = fb.T
                kn = jnp.maximum(kn, _row_norm_max(k_ref[0, c * pc:(c + 1) * pc, _head_cols(hh)]))
            kn_s[hh] = kn
            for t in range(n_blk):
                fb = f_ref[hh, :, t * tq:(t + 1) * tq] * LOG2E
                fmax_s[hh, t] = jnp.max(fb)
                c0_s[hh, t] = jnp.max(fb[:, 0:1])
            for j in range(2 * n_blk):
                fmin_s[hh, j] = jnp.min(f_ref[hh, :, j * tk:(j + 1) * tk] * LOG2E)

    def mask_fn(s):
        key = lax.broadcasted_iota(jnp.int32, s.shape, 0)
        qry = lax.broadcasted_iota(jnp.int32, s.shape, 1)
        return jnp.where(key <= qry, s, NEG)

    q_off = pl.multiple_of(qi * tq, tq)
    j0 = 2 * qi
    for hh, st in enumerate(streams):
        q = q_ref[0, :, _head_cols(hh)]
        c0 = st["frep"][pl.ds(q_off, 8), :][0:1, 0:1]

        qk = _row_norm_max(q) * kn_s[hh] * 1.01 + 1.0
        thr = 2.0 * qk + (fmax_s[hh, qi] - c0_s[hh, qi]) + SKIP_MARGIN
        j0h = jnp.int32(0)
        for j in range(2 * n_blk - 2):
            skip = (j < 2 * qi) & (j0h == j) & (fmin_s[hh, j] - c0_s[hh, qi] >= thr)
            j0h = j0h + skip.astype(jnp.int32)
        j0 = jnp.minimum(j0, j0h)

        def biased(j, qq, st=st, hh=hh, c0=c0):
            off = pl.multiple_of(j * tk, tk)
            fsub = st["frep"][pl.ds(off, tk), :] - c0
            s = lax.dot_general(k_ref[0, pl.ds(off, tk), _head_cols(hh)], qq, _NT,
                                preferred_element_type=F32)
            return s - jnp.tile(fsub, (1, qq.shape[0] // HEAD_W))

        st["scores"] = functools.partial(biased, qq=q)
        st["scores_late"] = functools.partial(biased, qq=q[tk:])
        st["mask"] = mask_fn

    _sweep_wide(qi, j0, streams)

    for hh, st in enumerate(streams):
        acc = st["acc"][...]
        o_ref[0, :, _head_cols(hh)] = (acc[0:HEAD_W] / acc[HEAD_W:HEAD_W + 1]).T.astype(BF16)


def _fox_attn(z, f3):
    B, S, _ = z.shape
    tk = min(TK_B, S // 2)
    tq = 2 * tk
    nh = NH_B
    w = nh * HEAD_W
    rows = f3.shape[0] // B
    qb, kb, vb = Z_BQ // w, Z_BK // w, Z_BV // w
    per_head = ([pltpu.VMEM((S, HEAD_W), F32)] + _attn_scratch(S, tk, tq)
                + [pltpu.VMEM((tk, tq // 2), F32)])
    return pl.pallas_call(
        functools.partial(_fox_attn_kernel, tk=tk, nh=nh),
        grid=(B, HEADS // nh, S // tq),
        in_specs=[
            pl.BlockSpec((1, tq, w), lambda b, h, i: (b, i, qb + h)),
            pl.BlockSpec((1, S, w), lambda b, h, i: (b, 0, kb + h)),
            pl.BlockSpec((1, S, w), lambda b, h, i: (b, 0, vb + h)),
            pl.BlockSpec((nh, 1, S), lambda b, h, i: (b * (rows // nh) + h, 0, 0)),
        ],
        out_specs=pl.BlockSpec((1, tq, w), lambda b, h, i: (b, i, h)),
        out_shape=jax.ShapeDtypeStruct((B, S, HEADS * HEAD_W), BF16),
        scratch_shapes=([pltpu.SMEM((nh, S // tk), F32)] + [pltpu.SMEM((nh, S // tq), F32)] * 2
                        + [pltpu.SMEM((nh,), F32)]
                        + per_head * nh),
        compiler_params=_cparams(("parallel", "parallel", "arbitrary")),
        name="fox_attn",
    )(z, z, z, f3)


def _retention_kernel(q_ref, k_ref, v_ref, cg_ref, cos_ref, sin_ref, dm_ref, g_ref, o_ref, r_scr,
                      *, L):
    @pl.when(pl.program_id(1) == 0)
    def _():
        r_scr[...] = jnp.zeros(r_scr.shape, F32)

    cosf = cos_ref[...]
    sinf = sin_ref[...]
    pos = lax.broadcasted_iota(jnp.int32, (L, 1), 0).astype(F32)

    for h in range(HEADS):
        lg = math.log(1.0 - 2.0 ** (-5.0 - h))
        qh = q_ref[0, :, h * HEAD_W:(h + 1) * HEAD_W].astype(F32)
        kh = k_ref[0, :, h * HEAD_W:(h + 1) * HEAD_W].astype(F32)
        qr = qh * cosf + pltpu.roll(qh, HEAD_W // 2, 1) * sinf
        kr = kh * cosf + pltpu.roll(kh, HEAD_W // 2, 1) * sinf
        v = v_ref[0, :, h * C_DV:(h + 1) * C_DV]
        s = lax.dot_general(qr.astype(BF16), kr.astype(BF16), _NT,
                            preferred_element_type=F32) * dm_ref[h]
        inner = jnp.dot(s.astype(BF16), v, preferred_element_type=F32)
        r = r_scr[h]
        q_dec = qr * jnp.exp(lg * (pos + 1.0))
        cross = jnp.dot(q_dec.astype(BF16), r.astype(BF16), preferred_element_type=F32)
        k_dec = kr * jnp.exp(lg * (L - 1.0 - pos))
        r_scr[h] = math.exp(lg * L) * r + jnp.dot(k_dec.T.astype(BF16), v, preferred_element_type=F32)
        o = inner + cross
        mu = jnp.mean(o, axis=1, keepdims=True)
        var = jnp.mean(jnp.square(o - mu), axis=1, keepdims=True)
        y = (o - mu) * lax.rsqrt(var + LN_EPS) * g_ref[:, h * C_DV:(h + 1) * C_DV]
        cg = cg_ref[0, :, h * C_DV:(h + 1) * C_DV].astype(F32)
        o_ref[0, :, h * C_DV:(h + 1) * C_DV] = (y * (cg * jax.nn.sigmoid(cg))).astype(BF16)


def _retention(z, cos_t, sin_t, g_ret):
    B, S, _ = z.shape
    L = min(L_RET, S)
    qk_w, v_w = HEADS * HEAD_W, HEADS * C_DV
    dist = np.arange(L)[:, None] - np.arange(L)[None, :]
    gam = 1.0 - 2.0 ** (-5.0 - np.arange(HEADS, dtype=np.float64))
    dm = jnp.asarray(np.where(dist >= 0, gam[:, None, None] ** np.maximum(dist, 0), 0.0), F32)
    return pl.pallas_call(
        functools.partial(_retention_kernel, L=L),
        grid=(B, S // L),
        in_specs=[
            pl.BlockSpec((1, L, qk_w), lambda b, n: (b, n, Z_CQ // qk_w)),
            pl.BlockSpec((1, L, qk_w), lambda b, n: (b, n, Z_CK // qk_w)),
            pl.BlockSpec((1, L, v_w), lambda b, n: (b, n, Z_CV // v_w)),
            pl.BlockSpec((1, L, v_w), lambda b, n: (b, n, Z_CG // v_w)),
            pl.BlockSpec((L, HEAD_W), lambda b, n: (n, 0)),
            pl.BlockSpec((L, HEAD_W), lambda b, n: (n, 0)),
            _const_spec(dm.shape),
            pl.BlockSpec((1, v_w), lambda b, n: (0, 0)),
        ],
        out_specs=pl.BlockSpec((1, L, v_w), lambda b, n: (b, n, 0)),
        out_shape=jax.ShapeDtypeStruct((B, S, v_w), BF16),
        scratch_shapes=[pltpu.VMEM((HEADS, HEAD_W, C_DV), F32)],
        compiler_params=_cparams(("parallel", "arbitrary")),
        name="retention",
    )(z, z, z, z, cos_t, sin_t, dm, g_ret)


def _rope_tables(S):
    half = HEAD_W // 2
    inv = 1.0 / (ROPE_BASE ** np.linspace(0.0, 1.0, half))
    ang = np.arange(S, dtype=np.float64)[:, None] * inv[None, :]
    cos, sin = np.cos(ang), np.sin(ang)
    return (jnp.asarray(np.concatenate([cos, cos], axis=1), F32),
            jnp.asarray(np.concatenate([-sin, sin], axis=1), F32))


def _layernorm(r, g, b):
    mu = jnp.mean(r, axis=1, keepdims=True)
    var = jnp.mean(jnp.square(r - mu), axis=1, keepdims=True)
    return (r - mu) * lax.rsqrt(var + LN_EPS) * g + b


def _merge_kernel(ya_ref, yb_ref, yc_ref, ga_ref, gb_ref, gc_ref, x_ref, gt_ref, sh2_ref, sc2_ref,
                  wpa_ref, wpb_ref, wpc_ref, wout_ref, lng_ref, lnb_ref, xo_ref, ho_ref, *, alpha):
    hs = x_ref.shape[1] // 2
    halves = [slice(0, hs), slice(hs, 2 * hs)]

    def branch(rows, y_ref, g_ref, w_ref):
        gate = jax.nn.sigmoid(g_ref[0, rows].astype(F32))
        return gate * jnp.dot(y_ref[0, rows], w_ref[...], preferred_element_type=F32)

    ms = [branch(r, ya_ref, ga_ref, wpa_ref) + branch(r, yb_ref, gb_ref, wpb_ref)
          + branch(r, yc_ref, gc_ref, wpc_ref) for r in halves]
    ys = [jnp.dot(m.astype(BF16), wout_ref[...], preferred_element_type=F32) for m in ms]
    for r, y in zip(halves, ys):
        xn = _layernorm(alpha * x_ref[0, r] + gt_ref[0] * y, lng_ref[...], lnb_ref[...])
        xo_ref[0, r] = xn
        ho_ref[0, r] = (xn * (1.0 + sc2_ref[0]) + sh2_ref[0]).astype(BF16)


def _const_spec(shape):
    return pl.BlockSpec(shape, lambda b, i: (0,) * len(shape), pipeline_mode=pl.Buffered(1))


def _layer_spec(stacked, l):
    shape = stacked.shape[1:]
    return pl.BlockSpec((None,) + shape, lambda b, i: (l,) + (0,) * len(shape),
                        pipeline_mode=pl.Buffered(1))


def _merge(ya, yb, yc, z, x, mod_l, wpa, wpb, wpc, wout, lng, lnb, alpha, l):
    B, S, D = x.shape
    tm = min(TM_MERGE, S)
    gblk = Z_GATES // D
    tok = lambda w, col: pl.BlockSpec((1, tm, w), lambda b, i: (b, i, col))
    modv = lambda col: pl.BlockSpec((1, 1, D), lambda b, i: (b, 0, col))
    return pl.pallas_call(
        functools.partial(_merge_kernel, alpha=alpha),
        grid=(B, S // tm),
        in_specs=[
            tok(ya.shape[2], 0), tok(yb.shape[2], 0), tok(yc.shape[2], 0),
            tok(D, gblk), tok(D, gblk + 1), tok(D, gblk + 2),
            tok(D, 0),
            modv(2), modv(3), modv(4),
            _layer_spec(wpa, l), _layer_spec(wpb, l), _layer_spec(wpc, l),
            _layer_spec(wout, l), _const_spec(lng.shape), _const_spec(lnb.shape),
        ],
        out_specs=[tok(D, 0), tok(D, 0)],
        out_shape=[jax.ShapeDtypeStruct((B, S, D), F32), jax.ShapeDtypeStruct((B, S, D), BF16)],
        compiler_params=_cparams(("parallel", "parallel")),
        name="merge_out_ln",
    )(ya, yb, yc, z, z, z, x, mod_l, mod_l, mod_l, wpa, wpb, wpc, wout, lng, lnb)


def _ffn_kernel(h_ref, x_ref, gt_ref, wup_ref, wconv_ref, bconv_ref, wdown_ref, lng_ref, lnb_ref,
                xo_ref, a_scr, tail_scr, *, alpha, cw):
    @pl.when(pl.program_id(1) == 0)
    def _():
        tail_scr[...] = jnp.zeros(tail_scr.shape, F32)

    h = h_ref[0]
    tm = h.shape[0]
    dff = a_scr.shape[1]
    row = lax.broadcasted_iota(jnp.int32, (tm, cw), 0)
    for ci in range(dff // cw):
        lo = ci * cw
        u = jnp.dot(h, wup_ref[:, lo:lo + cw], preferred_element_type=F32)
        g = jnp.dot(h, wup_ref[:, dff + lo:dff + lo + cw], preferred_element_type=F32)
        tail = tail_scr[:, lo:lo + cw]
        p1, p2 = tail[7:8], tail[6:7]
        um1 = jnp.where(row == 0, p1, pltpu.roll(u, 1, 0))
        um2 = jnp.where(row == 0, p2, jnp.where(row == 1, p1, pltpu.roll(u, 2, 0)))
        tail_scr[:, lo:lo + cw] = u[tm - 8:]
        conv = bconv_ref[:, lo:lo + cw] + um2 * wconv_ref[0:1, lo:lo + cw]
        conv = conv + um1 * wconv_ref[1:2, lo:lo + cw]
        conv = conv + u * wconv_ref[2:3, lo:lo + cw]
        act = 0.5 * conv * (1.0 + lax.erf(conv * (2.0 ** -0.5)))
        a_scr[:, lo:lo + cw] = (act * g).astype(BF16)
    halves = [slice(0, tm // 2), slice(tm // 2, tm)]
    ys = [jnp.dot(a_scr[r, :], wdown_ref[...], preferred_element_type=F32) for r in halves]
    for r, y in zip(halves, ys):
        xo_ref[0, r] = _layernorm(alpha * x_ref[0, r] + gt_ref[0] * y, lng_ref[...], lnb_ref[...])


def _ffn(h, x, mod_l, wup, wconv, bconv, wdown, lng, lnb, alpha, l):
    B, S, D = x.shape
    dff = wdown.shape[1]
    tm = min(TM_FFN, S)
    tok = pl.BlockSpec((1, tm, D), lambda b, i: (b, i, 0))
    return pl.pallas_call(
        functools.partial(_ffn_kernel, alpha=alpha, cw=CW_FFN),
        grid=(B, S // tm),
        in_specs=[
            tok, tok,
            pl.BlockSpec((1, 1, D), lambda b, i: (b, 0, 5)),
            _layer_spec(wup, l), _const_spec(wconv.shape), _const_spec(bconv.shape),
            _layer_spec(wdown, l), _const_spec(lng.shape), _const_spec(lnb.shape),
        ],
        out_specs=tok,
        out_shape=jax.ShapeDtypeStruct((B, S, D), F32),
        scratch_shapes=[pltpu.VMEM((tm, dff), BF16), pltpu.VMEM((8, dff), F32)],
        compiler_params=_cparams(("parallel", "arbitrary")),
        name="conv_ffn_ln",
    )(h, x, mod_l, wup, wconv, bconv, wdown, lng, lnb)


def _prep_in_proj(w, b):
    scale = np.ones((1, Z_WIDTH), np.float32)
    scale[:, Z_AQ:Z_AQ + 512] = A_DQK ** -0.5 * LOG2E
    scale[:, Z_BQ:Z_BQ + 512] = HEAD_W ** -0.5 * LOG2E
    scale[:, Z_CK:Z_CK + 512] = HEAD_W ** -0.5
    L, D, _ = w.shape
    tn = TN_PREP
    wm = pl.pallas_call(
        functools.partial(_win_prep_kernel, first_shifted=BF_OFF // tn, shift=HEADS),
        grid=(L, Z_WIDTH // tn),
        in_specs=[
            pl.BlockSpec((None, D, tn), lambda l, j: (l, 0, j)),
            pl.BlockSpec((None, D, HEAD_W), lambda l, j: (l, 0, (j + 1) * (tn // HEAD_W))),
            pl.BlockSpec((1, tn), lambda l, j: (0, j)),
        ],
        out_specs=pl.BlockSpec((None, D, tn), lambda l, j: (l, 0, j)),
        out_shape=jax.ShapeDtypeStruct((L, D, Z_WIDTH), BF16),
        compiler_params=_cparams(("parallel", "parallel")),
        name="w_in_prep",
    )(w, w, jnp.asarray(scale))
    bm = jnp.concatenate([b[..., :BF_OFF], b[..., BF_OFF + HEADS:]], axis=-1) * scale
    pad = BF_PAD - HEADS
    wf = jnp.pad(w[..., BF_OFF:BF_OFF + HEADS], ((0, 0), (0, 0), (0, pad)))
    bf = jnp.pad(b[..., BF_OFF:BF_OFF + HEADS], ((0, 0), (0, pad)))
    return wm, bm[:, None, :], wf.astype(BF16), bf[:, None, :]


def _win_prep_kernel(w_ref, wnext_ref, s_ref, o_ref, *, first_shifted, shift):
    j = pl.program_id(1)

    @pl.when(j < first_shifted)
    def _():
        o_ref[...] = (w_ref[...] * s_ref[...]).astype(BF16)

    @pl.when(j >= first_shifted)
    def _():
        tn = w_ref.shape[1]
        wide = jnp.concatenate([w_ref[...], wnext_ref[...]], axis=1)
        moved = pltpu.roll(wide, wide.shape[1] - shift, 1)[:, :tn]
        o_ref[...] = (moved * s_ref[...]).astype(BF16)


def kernel(x, c, w_ada, b_ada, w_in, b_in, lam_q1, lam_k1, lam_q2, lam_k2, g_diff, g_ret, w_pa, w_pb, w_pc, w_out, ln_g, ln_b, w_up, w_conv, b_conv, w_down):
    B, S, D = x.shape
    depth = w_ada.shape[0]
    alpha = (2 * depth) ** 0.25
    chunk = 64
    mod = _ada(c, w_ada, b_ada)
    cos_t, sin_t = _rope_tables(S)
    wm, bm, wf, bf = _prep_in_proj(w_in, b_in)
    wpa, wpb, wpc, wout = (w.astype(BF16) for w in (w_pa, w_pb, w_pc, w_out))
    wup, wdown = w_up.astype(BF16), w_down.astype(BF16)
    for l in range(depth):
        mod_l = mod[l].reshape(B, 1, 6 * D)
        lam_init = jnp.full((1, 1), 0.8 - 0.6 * math.exp(-0.3 * l), F32)
        z, lf = _inproj(x, mod_l, wm, bm, wf, bf, l)
        f = _cumsum(lf)
        ya = _diff_attn(z, lam_q1[l][None], lam_k1[l][None], lam_q2[l][None], lam_k2[l][None],
                        lam_init, g_diff[l][None], chunk)
        yb = _fox_attn(z, f.reshape(B * f.shape[1], 1, S))
        yc = _retention(z, cos_t, sin_t, g_ret[l][None])
        x, h2 = _merge(ya, yb, yc, z, x, mod_l, wpa, wpb, wpc, wout,
                       ln_g[l, 0][None], ln_b[l, 0][None], alpha, l)
        x = _ffn(h2, x, mod_l, wup, w_conv[l], b_conv[l][None], wdown,
                 ln_g[l, 1][None], ln_b[l, 1][None], alpha, l)
    return x
```

```python
import functools
import math

import numpy as np
import jax
import jax.numpy as jnp
from jax import lax
from jax.experimental import pallas as pl
from jax.experimental.pallas import tpu as pltpu

F32 = jnp.float32
BF16 = jnp.bfloat16

HEADS = 4
A_DQK = 64
HEAD_W = 128
C_DV = 256
ROPE_BASE = 10000.0
LN_EPS = 1e-5

Z_AQ, Z_AK, Z_AV = 0, 512, 1024
Z_BQ, Z_BK, Z_BV = 1536, 2048, 2560
Z_CQ, Z_CK, Z_CV, Z_CG = 3072, 3584, 4096, 5120
Z_GATES = 6144
Z_WIDTH = 9216
BF_OFF = 3072
BF_PAD = 128

NEG = -1e30
LOG2E = 1.4426950408889634
VT_ROWS = 144

TN_ADA = 2048
TM_IN, TN_IN = 1024, 3072
TN_PREP = 1024
TQ_A = 512
TK_B = 512
NH_A = 4
NH_B = 2
PROLOGUE_CHUNK = 512
SKIP_MARGIN = 160.0
L_RET = 256
TM_MERGE = 1024
TM_FFN = 1024
CW_FFN = 256

VMEM_LIMIT = 56 * 1024 * 1024
VMEM_LIMIT_ATTN_A = 60 * 1024 * 1024


def _cparams(sem, vmem_limit=VMEM_LIMIT):
    return pltpu.CompilerParams(dimension_semantics=sem, vmem_limit_bytes=vmem_limit)


def _ada_kernel(ct_ref, w_ref, b_ref, o_ref):
    w = w_ref[0]
    rows = []
    for b in range(ct_ref.shape[0]):
        cb = ct_ref[b]
        ca = cb * jax.nn.sigmoid(cb)
        rows.append(jnp.sum(ca * w, axis=0, keepdims=True))
    o_ref[0] = jnp.concatenate(rows, axis=0) + b_ref[0]


def _ada(c, w_ada, b_ada):
    B, D = c.shape
    L, _, N = w_ada.shape
    tn = min(TN_ADA, N)
    return pl.pallas_call(
        _ada_kernel,
        grid=(L, N // tn),
        in_specs=[
            pl.BlockSpec((B, D, 1), lambda l, j: (0, 0, 0)),
            pl.BlockSpec((1, D, tn), lambda l, j: (l, 0, j)),
            pl.BlockSpec((1, 1, tn), lambda l, j: (l, 0, j)),
        ],
        out_specs=pl.BlockSpec((1, B, tn), lambda l, j: (l, 0, j)),
        out_shape=jax.ShapeDtypeStruct((L, B, N), F32),
        compiler_params=_cparams(("parallel", "parallel")),
        name="ada_mod",
    )(c.reshape(B, D, 1), w_ada, b_ada.reshape(L, 1, N))


def _inproj_kernel(x_ref, sh_ref, sc_ref, w_ref, b_ref, wf_ref, bf_ref, z_ref, lf_ref, h_scr):
    @pl.when(pl.program_id(2) == 0)
    def _():
        h = x_ref[0] * (1.0 + sc_ref[0]) + sh_ref[0]
        hb = h.astype(BF16)
        h_scr[...] = hb
        zf = jnp.dot(hb, wf_ref[...], preferred_element_type=F32) + bf_ref[...]
        lt = zf.T[:8]
        lf_ref[0] = jnp.minimum(lt, 0.0) - jnp.log1p(jnp.exp(-jnp.abs(lt)))

    acc = jnp.dot(h_scr[...], w_ref[...], preferred_element_type=F32)
    z_ref[0] = (acc + b_ref[...]).astype(BF16)


def _inproj(x, mod_l, w, b, wf, bf, l):
    B, S, D = x.shape
    N = w.shape[2]
    tm, tn = min(TM_IN, S), min(TN_IN, N)
    return pl.pallas_call(
        _inproj_kernel,
        grid=(B, S // tm, N // tn),
        in_specs=[
            pl.BlockSpec((1, tm, D), lambda b, i, j: (b, i, 0)),
            pl.BlockSpec((1, 1, D), lambda b, i, j: (b, 0, 0)),
            pl.BlockSpec((1, 1, D), lambda b, i, j: (b, 0, 1)),
            pl.BlockSpec((None, D, tn), lambda b, i, j: (l, 0, j)),
            pl.BlockSpec((None, 1, tn), lambda b, i, j: (l, 0, j)),
            pl.BlockSpec((None, D, BF_PAD), lambda b, i, j: (l, 0, 0)),
            pl.BlockSpec((None, 1, BF_PAD), lambda b, i, j: (l, 0, 0)),
        ],
        out_specs=[
            pl.BlockSpec((1, tm, tn), lambda b, i, j: (b, i, j)),
            pl.BlockSpec((1, 8, tm), lambda b, i, j: (b, 0, i)),
        ],
        out_shape=[
            jax.ShapeDtypeStruct((B, S, N), BF16),
            jax.ShapeDtypeStruct((B, 8, S), F32),
        ],
        scratch_shapes=[pltpu.VMEM((tm, D), BF16)],
        compiler_params=_cparams(("parallel", "parallel", "arbitrary")),
        name="in_proj",
    )(x, mod_l, mod_l, w, b, wf, bf)


def _cumsum_kernel(lf_ref, f_ref):
    x = lf_ref[0]
    lane = lax.broadcasted_iota(jnp.int32, x.shape, 1)
    k = 1
    while k < x.shape[1]:
        x = x + jnp.where(lane >= k, pltpu.roll(x, k, 1), 0.0)
        k *= 2
    f_ref[0] = x


def _cumsum(lf):
    B, R, S = lf.shape
    return pl.pallas_call(
        _cumsum_kernel,
        grid=(B,),
        in_specs=[pl.BlockSpec((1, R, S), lambda b: (b, 0, 0))],
        out_specs=pl.BlockSpec((1, R, S), lambda b: (b, 0, 0)),
        out_shape=jax.ShapeDtypeStruct((B, R, S), F32),
        compiler_params=_cparams(("parallel",)),
        name="forget_cumsum",
    )(lf)


def _build_vt(v_ref, vt_scr, tk):
    ones_row = (lax.broadcasted_iota(jnp.int32, (VT_ROWS - HEAD_W, tk), 0) == 0).astype(BF16)
    for c in range(v_ref.shape[1] // tk):
        vb = v_ref[0, c * tk:(c + 1) * tk, :].astype(F32)
        vt_scr[c, 0:HEAD_W, :] = vb.T.astype(BF16)
        vt_scr[c, HEAD_W:VT_ROWS, :] = ones_row


def _softmax_step_t(s, vt, m_scr, acc_scr, colmax=None):
    m_prev = m_scr[...]
    if colmax is None:
        colmax = jnp.max(s, axis=0, keepdims=True)
    m_new = jnp.maximum(m_prev, colmax)
    p = jnp.exp2(s - m_new).astype(BF16)
    alpha = jnp.exp2(m_prev - m_new)
    acc_scr[...] = alpha * acc_scr[...] + jnp.dot(vt, p, preferred_element_type=F32)
    m_scr[...] = m_new


def _put_scores(st, buf, mx, j):
    s = st["scores"](j)
    st[buf][...] = s
    st[mx][...] = jnp.max(s, axis=0, keepdims=True)


def _init_streams(streams, first_tile):
    for st in streams:
        st["m"][...] = jnp.full(st["m"].shape, NEG, F32)
        st["acc"][...] = jnp.zeros(st["acc"].shape, F32)
        _put_scores(st, "sa", "ma", first_tile)


def _pair_body(streams):
    def pair(t, carry):
        j = 2 * t
        for st in streams:
            _put_scores(st, "sb", "mb", j + 1)
            _softmax_step_t(st["sa"][...], st["vt"][j], st["m"], st["acc"], st["ma"][...])
        for st in streams:
            _put_scores(st, "sa", "ma", j + 2)
            _softmax_step_t(st["sb"][...], st["vt"][j + 1], st["m"], st["acc"], st["mb"][...])
        return carry
    return pair


def _sweep(qi, streams):
    _init_streams(streams, 0)
    lax.fori_loop(0, qi // 2, _pair_body(streams), 0)

    @pl.when(qi % 2 == 0)
    def _():
        for st in streams:
            _softmax_step_t(st["mask"](st["sa"][...]), st["vt"][qi], st["m"], st["acc"])

    @pl.when(qi % 2 == 1)
    def _():
        for st in streams:
            st["sb"][...] = st["scores"](qi)
            _softmax_step_t(st["sa"][...], st["vt"][qi - 1], st["m"], st["acc"], st["ma"][...])
        for st in streams:
            _softmax_step_t(st["mask"](st["sb"][...]), st["vt"][qi], st["m"], st["acc"])


def _sweep_wide(qi, j0, streams):
    for st in streams:
        st["m"][...] = jnp.full(st["m"].shape, NEG, F32)
        st["acc"][...] = jnp.zeros(st["acc"].shape, F32)

    @pl.when(j0 % 2 == 1)
    def _():
        for st in streams:
            _put_scores(st, "sb", "mb", j0)
        for st in streams:
            _put_scores(st, "sa", "ma", j0 + 1)
            _softmax_step_t(st["sb"][...], st["vt"][j0], st["m"], st["acc"], st["mb"][...])

    @pl.when(j0 % 2 == 0)
    def _():
        for st in streams:
            _put_scores(st, "sa", "ma", j0)

    lax.fori_loop((j0 + 1) // 2, qi, _pair_body(streams), 0)
    for st in streams:
        st["sl"][...] = st["scores_late"](2 * qi + 1)
        _softmax_step_t(st["mask"](st["sa"][...]), st["vt"][2 * qi], st["m"], st["acc"])
    for st in streams:
        nl = st["sl"].shape[1]
        _softmax_step_t(st["mask"](st["sl"][...]), st["vt"][2 * qi + 1],
                        st["m"].at[:, nl:], st["acc"].at[:, nl:])


_NT = (((1,), (1,)), ((), ()))


def _attn_scratch(S, tk, nq):
    return [pltpu.VMEM((S // tk, VT_ROWS, tk), BF16),
            pltpu.VMEM((1, nq), F32),
            pltpu.VMEM((VT_ROWS, nq), F32),
            pltpu.VMEM((tk, nq), F32), pltpu.VMEM((tk, nq), F32),
            pltpu.VMEM((1, nq), F32), pltpu.VMEM((1, nq), F32)]


def _head_cols(hh):
    return slice(hh * HEAD_W, (hh + 1) * HEAD_W)


def _diff_attn_kernel(q_ref, k_ref, v_ref, lq1_ref, lk1_ref, lq2_ref, lk2_ref, li_ref, g_ref,
                      o_ref, *scr, tq, chunk, nh):
    qi = pl.program_id(2)
    per = len(scr) // nh
    names = ("vt", "m", "acc", "sa", "sb", "ma", "mb")
    streams = [dict(zip(names, scr[hh * per:(hh + 1) * per])) for hh in range(nh)]

    @pl.when(qi == 0)
    def _():
        for hh, st in enumerate(streams):
            _build_vt(v_ref.at[:, :, _head_cols(hh)], st["vt"], tq)

    def mask_fn(s):
        key = lax.broadcasted_iota(jnp.int32, s.shape, 0)
        col = lax.broadcasted_iota(jnp.int32, s.shape, 1)
        qry = jnp.where(col >= tq, col - tq, col)
        return jnp.where(key // chunk <= qry // chunk, s, NEG)

    for hh, st in enumerate(streams):
        q = q_ref[0, :, _head_cols(hh)]
        lane = lax.broadcasted_iota(jnp.int32, q.shape, 1)
        zero = jnp.zeros_like(q)
        qs = jnp.concatenate([jnp.where(lane < A_DQK, q, zero),
                              jnp.where(lane >= A_DQK, q, zero)], axis=0)

        def scores(j, qs=qs, hh=hh):
            off = pl.multiple_of(j * tq, tq)
            return lax.dot_general(k_ref[0, pl.ds(off, tq), _head_cols(hh)], qs, _NT,
                                   preferred_element_type=F32)

        st["scores"], st["mask"] = scores, mask_fn

    _sweep(qi, streams)

    lam_init = li_ref[...]
    lam = (jnp.exp(jnp.sum(lq1_ref[...] * lk1_ref[...], axis=1, keepdims=True))
           - jnp.exp(jnp.sum(lq2_ref[...] * lk2_ref[...], axis=1, keepdims=True)) + lam_init)
    for hh, st in enumerate(streams):
        acc = st["acc"][...]
        ot_all = acc[0:HEAD_W] / acc[HEAD_W:HEAD_W + 1]
        o = (ot_all[:, :tq] - lam * ot_all[:, tq:]).T
        o = o * lax.rsqrt(jnp.mean(o * o, axis=1, keepdims=True) + LN_EPS)
        o_ref[0, :, _head_cols(hh)] = (o * g_ref[:, _head_cols(hh)] * (1.0 - lam_init)).astype(BF16)


def _resident(shape, index_map):
    return pl.BlockSpec(shape, index_map, pipeline_mode=pl.Buffered(1))


def _diff_attn(z, lq1, lk1, lq2, lk2, lam_init, g_diff, chunk):
    B, S, _ = z.shape
    tq = min(TQ_A, S)
    nh = NH_A
    w = nh * HEAD_W
    qb, kb, vb = Z_AQ // w, Z_AK // w, Z_AV // w
    small = pl.BlockSpec((1, A_DQK), lambda b, h, i: (0, 0))
    return pl.pallas_call(
        functools.partial(_diff_attn_kernel, tq=tq, chunk=chunk, nh=nh),
        grid=(B, HEADS // nh, S // tq),
        in_specs=[
            pl.BlockSpec((1, tq, w), lambda b, h, i: (b, i, qb + h)),
            pl.BlockSpec((1, S, w), lambda b, h, i: (b, 0, kb + h)),
            _resident((1, S, w), lambda b, h, i: (b, 0, vb + h)),
            small, small, small, small,
            pl.BlockSpec((1, 1), lambda b, h, i: (0, 0)),
            pl.BlockSpec((1, w), lambda b, h, i: (0, h)),
        ],
        out_specs=pl.BlockSpec((1, tq, w), lambda b, h, i: (b, i, h)),
        out_shape=jax.ShapeDtypeStruct((B, S, HEADS * HEAD_W), BF16),
        scratch_shapes=_attn_scratch(S, tq, 2 * tq) * nh,
        compiler_params=_cparams(("parallel", "parallel", "arbitrary"), VMEM_LIMIT_ATTN_A),
        name="diff_attn",
    )(z, z, z, lq1, lk1, lq2, lk2, lam_init, g_diff)


def _row_norm_max(x):
    xf = x.astype(F32)
    return jnp.sqrt(jnp.max(jnp.sum(xf * xf, axis=1, keepdims=True)))


def _fox_attn_kernel(q_ref, k_ref, v_ref, f_ref, o_ref, fmin_s, fmax_s, c0_s, kn_s, *scr, tk, nh):
    qi = pl.program_id(2)
    S = k_ref.shape[1]
    tq = 2 * tk
    n_blk = S // tq
    pc = min(PROLOGUE_CHUNK, S)
    per = len(scr) // nh
    names = ("frep", "vt", "m", "acc", "sa", "sb", "ma", "mb", "sl")
    streams = [dict(zip(names, scr[hh * per:(hh + 1) * per])) for hh in range(nh)]

    @pl.when(qi == 0)
    def _():
        for hh, st in enumerate(streams):
            _build_vt(v_ref.at[:, :, _head_cols(hh)], st["vt"], tk)
            kn = jnp.float32(0.0)
            for c in range(S // pc):
                fb = jnp.broadcast_to(f_ref[hh, :, c * pc:(c + 1) * pc] * LOG2E, (HEAD_W, pc))
                st["frep"][c * pc:(c + 1) * pc, :] = fb.T
                kn = jnp.maximum(kn, _row_norm_max(k_ref[0, c * pc:(c + 1) * pc, _head_cols(hh)]))
            kn_s[hh] = kn
            for t in range(n_blk):
                fb = f_ref[hh, :, t * tq:(t + 1) * tq] * LOG2E
                fmax_s[hh, t] = jnp.max(fb)
                c0_s[hh, t] = jnp.max(fb[:, 0:1])
            for j in range(2 * n_blk):
                fmin_s[hh, j] = jnp.min(f_ref[hh, :, j * tk:(j + 1) * tk] * LOG2E)

    def mask_fn(s):
        key = lax.broadcasted_iota(jnp.int32, s.shape, 0)
        qry = lax.broadcasted_iota(jnp.int32, s.shape, 1)
        return jnp.where(key <= qry, s, NEG)

    q_off = pl.multiple_of(qi * tq, tq)
    j0 = 2 * qi
    for hh, st in enumerate(streams):
        q = q_ref[0, :, _head_cols(hh)]
        c0 = st["frep"][pl.ds(q_off, 8), :][0:1, 0:1]

        qk = _row_norm_max(q) * kn_s[hh] * 1.01 + 1.0
        thr = 2.0 * qk + (fmax_s[hh, qi] - c0_s[hh, qi]) + SKIP_MARGIN
        j0h = jnp.int32(0)
        for j in range(2 * n_blk - 2):
            skip = (j < 2 * qi) & (j0h == j) & (fmin_s[hh, j] - c0_s[hh, qi] >= thr)
            j0h = j0h + skip.astype(jnp.int32)
        j0 = jnp.minimum(j0, j0h)

        def biased(j, qq, st=st, hh=hh, c0=c0):
            off = pl.multiple_of(j * tk, tk)
            fsub = st["frep"][pl.ds(off, tk), :] - c0
            s = lax.dot_general(k_ref[0, pl.ds(off, tk), _head_cols(hh)], qq, _NT,
                                preferred_element_type=F32)
            return s - jnp.tile(fsub, (1, qq.shape[0] // HEAD_W))

        st["scores"] = functools.partial(biased, qq=q)
        st["scores_late"] = functools.partial(biased, qq=q[tk:])
        st["mask"] = mask_fn

    _sweep_wide(qi, j0, streams)

    for hh, st in enumerate(streams):
        acc = st["acc"][...]
        o_ref[0, :, _head_cols(hh)] = (acc[0:HEAD_W] / acc[HEAD_W:HEAD_W + 1]).T.astype(BF16)


def _fox_attn(z, f3):
    B, S, _ = z.shape
    tk = min(TK_B, S // 2)
    tq = 2 * tk
    nh = NH_B
    w = nh * HEAD_W
    rows = f3.shape[0] // B
    qb, kb, vb = Z_BQ // w, Z_BK // w, Z_BV // w
    per_head = ([pltpu.VMEM((S, HEAD_W), F32)] + _attn_scratch(S, tk, tq)
                + [pltpu.VMEM((tk, tq // 2), F32)])
    return pl.pallas_call(
        functools.partial(_fox_attn_kernel, tk=tk, nh=nh),
        grid=(B, HEADS // nh, S // tq),
        in_specs=[
            pl.BlockSpec((1, tq, w), lambda b, h, i: (b, i, qb + h)),
            pl.BlockSpec((1, S, w), lambda b, h, i: (b, 0, kb + h)),
            pl.BlockSpec((1, S, w), lambda b, h, i: (b, 0, vb + h)),
            pl.BlockSpec((nh, 1, S), lambda b, h, i: (b * (rows // nh) + h, 0, 0)),
        ],
        out_specs=pl.BlockSpec((1, tq, w), lambda b, h, i: (b, i, h)),
        out_shape=jax.ShapeDtypeStruct((B, S, HEADS * HEAD_W), BF16),
        scratch_shapes=([pltpu.SMEM((nh, S // tk), F32)] + [pltpu.SMEM((nh, S // tq), F32)] * 2
                        + [pltpu.SMEM((nh,), F32)]
                        + per_head * nh),
        compiler_params=_cparams(("parallel", "parallel", "arbitrary")),
        name="fox_attn",
    )(z, z, z, f3)


def _retention_kernel(q_ref, k_ref, v_ref, cg_ref, cos_ref, sin_ref, dm_ref, g_ref, o_ref, r_scr,
                      *, L):
    @pl.when(pl.program_id(1) == 0)
    def _():
        r_scr[...] = jnp.zeros(r_scr.shape, F32)

    cosf = cos_ref[...]
    sinf = sin_ref[...]
    pos = lax.broadcasted_iota(jnp.int32, (L, 1), 0).astype(F32)

    for h in range(HEADS):
        lg = math.log(1.0 - 2.0 ** (-5.0 - h))
        qh = q_ref[0, :, h * HEAD_W:(h + 1) * HEAD_W].astype(F32)
        kh = k_ref[0, :, h * HEAD_W:(h + 1) * HEAD_W].astype(F32)
        qr = qh * cosf + pltpu.roll(qh, HEAD_W // 2, 1) * sinf
        kr = kh * cosf + pltpu.roll(kh, HEAD_W // 2, 1) * sinf
        v = v_ref[0, :, h * C_DV:(h + 1) * C_DV]
        s = lax.dot_general(qr.astype(BF16), kr.astype(BF16), _NT,
                            preferred_element_type=F32) * dm_ref[h]
        inner = jnp.dot(s.astype(BF16), v, preferred_element_type=F32)
        r = r_scr[h]
        q_dec = qr * jnp.exp(lg * (pos + 1.0))
        cross = jnp.dot(q_dec.astype(BF16), r.astype(BF16), preferred_element_type=F32)
        k_dec = kr * jnp.exp(lg * (L - 1.0 - pos))
        r_scr[h] = math.exp(lg * L) * r + jnp.dot(k_dec.T.astype(BF16), v, preferred_element_type=F32)
        o = inner + cross
        mu = jnp.mean(o, axis=1, keepdims=True)
        var = jnp.mean(jnp.square(o - mu), axis=1, keepdims=True)
        y = (o - mu) * lax.rsqrt(var + LN_EPS) * g_ref[:, h * C_DV:(h + 1) * C_DV]
        cg = cg_ref[0, :, h * C_DV:(h + 1) * C_DV].astype(F32)
        o_ref[0, :, h * C_DV:(h + 1) * C_DV] = (y * (cg * jax.nn.sigmoid(cg))).astype(BF16)


def _retention(z, cos_t, sin_t, g_ret):
    B, S, _ = z.shape
    L = min(L_RET, S)
    qk_w, v_w = HEADS * HEAD_W, HEADS * C_DV
    dist = np.arange(L)[:, None] - np.arange(L)[None, :]
    gam = 1.0 - 2.0 ** (-5.0 - np.arange(HEADS, dtype=np.float64))
    dm = jnp.asarray(np.where(dist >= 0, gam[:, None, None] ** np.maximum(dist, 0), 0.0), F32)
    return pl.pallas_call(
        functools.partial(_retention_kernel, L=L),
        grid=(B, S // L),
        in_specs=[
            pl.BlockSpec((1, L, qk_w), lambda b, n: (b, n, Z_CQ // qk_w)),
            pl.BlockSpec((1, L, qk_w), lambda b, n: (b, n, Z_CK // qk_w)),
            pl.BlockSpec((1, L, v_w), lambda b, n: (b, n, Z_CV // v_w)),
            pl.BlockSpec((1, L, v_w), lambda b, n: (b, n, Z_CG // v_w)),
            pl.BlockSpec((L, HEAD_W), lambda b, n: (n, 0)),
            pl.BlockSpec((L, HEAD_W), lambda b, n: (n, 0)),
            _const_spec(dm.shape),
            pl.BlockSpec((1, v_w), lambda b, n: (0, 0)),
        ],
        out_specs=pl.BlockSpec((1, L, v_w), lambda b, n: (b, n, 0)),
        out_shape=jax.ShapeDtypeStruct((B, S, v_w), BF16),
        scratch_shapes=[pltpu.VMEM((HEADS, HEAD_W, C_DV), F32)],
        compiler_params=_cparams(("parallel", "arbitrary")),
        name="retention",
    )(z, z, z, z, cos_t, sin_t, dm, g_ret)


def _rope_tables(S):
    half = HEAD_W // 2
    inv = 1.0 / (ROPE_BASE ** np.linspace(0.0, 1.0, half))
    ang = np.arange(S, dtype=np.float64)[:, None] * inv[None, :]
    cos, sin = np.cos(ang), np.sin(ang)
    return (jnp.asarray(np.concatenate([cos, cos], axis=1), F32),
            jnp.asarray(np.concatenate([-sin, sin], axis=1), F32))


def _layernorm(r, g, b):
    mu = jnp.mean(r, axis=1, keepdims=True)
    var = jnp.mean(jnp.square(r - mu), axis=1, keepdims=True)
    return (r - mu) * lax.rsqrt(var + LN_EPS) * g + b


def _merge_kernel(ya_ref, yb_ref, yc_ref, ga_ref, gb_ref, gc_ref, x_ref, gt_ref, sh2_ref, sc2_ref,
                  wpa_ref, wpb_ref, wpc_ref, wout_ref, lng_ref, lnb_ref, xo_ref, ho_ref, *, alpha):
    hs = x_ref.shape[1] // 2
    halves = [slice(0, hs), slice(hs, 2 * hs)]

    def branch(rows, y_ref, g_ref, w_ref):
        gate = jax.nn.sigmoid(g_ref[0, rows].astype(F32))
        return gate * jnp.dot(y_ref[0, rows], w_ref[...], preferred_element_type=F32)

    ms = [branch(r, ya_ref, ga_ref, wpa_ref) + branch(r, yb_ref, gb_ref, wpb_ref)
          + branch(r, yc_ref, gc_ref, wpc_ref) for r in halves]
    ys = [jnp.dot(m.astype(BF16), wout_ref[...], preferred_element_type=F32) for m in ms]
    for r, y in zip(halves, ys):
        xn = _layernorm(alpha * x_ref[0, r] + gt_ref[0] * y, lng_ref[...], lnb_ref[...])
        xo_ref[0, r] = xn
        ho_ref[0, r] = (xn * (1.0 + sc2_ref[0]) + sh2_ref[0]).astype(BF16)


def _const_spec(shape):
    return pl.BlockSpec(shape, lambda b, i: (0,) * len(shape), pipeline_mode=pl.Buffered(1))


def _layer_spec(stacked, l):
    shape = stacked.shape[1:]
    return pl.BlockSpec((None,) + shape, lambda b, i: (l,) + (0,) * len(shape),
                        pipeline_mode=pl.Buffered(1))


def _merge(ya, yb, yc, z, x, mod_l, wpa, wpb, wpc, wout, lng, lnb, alpha, l):
    B, S, D = x.shape
    tm = min(TM_MERGE, S)
    gblk = Z_GATES // D
    tok = lambda w, col: pl.BlockSpec((1, tm, w), lambda b, i: (b, i, col))
    modv = lambda col: pl.BlockSpec((1, 1, D), lambda b, i: (b, 0, col))
    return pl.pallas_call(
        functools.partial(_merge_kernel, alpha=alpha),
        grid=(B, S // tm),
        in_specs=[
            tok(ya.shape[2], 0), tok(yb.shape[2], 0), tok(yc.shape[2], 0),
            tok(D, gblk), tok(D, gblk + 1), tok(D, gblk + 2),
            tok(D, 0),
            modv(2), modv(3), modv(4),
            _layer_spec(wpa, l), _layer_spec(wpb, l), _layer_spec(wpc, l),
            _layer_spec(wout, l), _const_spec(lng.shape), _const_spec(lnb.shape),
        ],
        out_specs=[tok(D, 0), tok(D, 0)],
        out_shape=[jax.ShapeDtypeStruct((B, S, D), F32), jax.ShapeDtypeStruct((B, S, D), BF16)],
        compiler_params=_cparams(("parallel", "parallel")),
        name="merge_out_ln",
    )(ya, yb, yc, z, z, z, x, mod_l, mod_l, mod_l, wpa, wpb, wpc, wout, lng, lnb)


def _ffn_kernel(h_ref, x_ref, gt_ref, wup_ref, wconv_ref, bconv_ref, wdown_ref, lng_ref, lnb_ref,
                xo_ref, a_scr, tail_scr, *, alpha, cw):
    @pl.when(pl.program_id(1) == 0)
    def _():
        tail_scr[...] = jnp.zeros(tail_scr.shape, F32)

    h = h_ref[0]
    tm = h.shape[0]
    dff = a_scr.shape[1]
    row = lax.broadcasted_iota(jnp.int32, (tm, cw), 0)
    for ci in range(dff // cw):
        lo = ci * cw
        u = jnp.dot(h, wup_ref[:, lo:lo + cw], preferred_element_type=F32)
        g = jnp.dot(h, wup_ref[:, dff + lo:dff + lo + cw], preferred_element_type=F32)
        tail = tail_scr[:, lo:lo + cw]
        p1, p2 = tail[7:8], tail[6:7]
        um1 = jnp.where(row == 0, p1, pltpu.roll(u, 1, 0))
        um2 = jnp.where(row == 0, p2, jnp.where(row == 1, p1, pltpu.roll(u, 2, 0)))
        tail_scr[:, lo:lo + cw] = u[tm - 8:]
        conv = bconv_ref[:, lo:lo + cw] + um2 * wconv_ref[0:1, lo:lo + cw]
        conv = conv + um1 * wconv_ref[1:2, lo:lo + cw]
        conv = conv + u * wconv_ref[2:3, lo:lo + cw]
        act = 0.5 * conv * (1.0 + lax.erf(conv * (2.0 ** -0.5)))
        a_scr[:, lo:lo + cw] = (act * g).astype(BF16)
    halves = [slice(0, tm // 2), slice(tm // 2, tm)]
    ys = [jnp.dot(a_scr[r, :], wdown_ref[...], preferred_element_type=F32) for r in halves]
    for r, y in zip(halves, ys):
        xo_ref[0, r] = _layernorm(alpha * x_ref[0, r] + gt_ref[0] * y, lng_ref[...], lnb_ref[...])


def _ffn(h, x, mod_l, wup, wconv, bconv, wdown, lng, lnb, alpha, l):
    B, S, D = x.shape
    dff = wdown.shape[1]
    tm = min(TM_FFN, S)
    tok = pl.BlockSpec((1, tm, D), lambda b, i: (b, i, 0))
    return pl.pallas_call(
        functools.partial(_ffn_kernel, alpha=alpha, cw=CW_FFN),
        grid=(B, S // tm),
        in_specs=[
            tok, tok,
            pl.BlockSpec((1, 1, D), lambda b, i: (b, 0, 5)),
            _layer_spec(wup, l), _const_spec(wconv.shape), _const_spec(bconv.shape),
            _layer_spec(wdown, l), _const_spec(lng.shape), _const_spec(lnb.shape),
        ],
        out_specs=tok,
        out_shape=jax.ShapeDtypeStruct((B, S, D), F32),
        scratch_shapes=[pltpu.VMEM((tm, dff), BF16), pltpu.VMEM((8, dff), F32)],
        compiler_params=_cparams(("parallel", "arbitrary")),
        name="conv_ffn_ln",
    )(h, x, mod_l, wup, wconv, bconv, wdown, lng, lnb)


def _prep_in_proj(w, b):
    scale = np.ones((1, Z_WIDTH), np.float32)
    scale[:, Z_AQ:Z_AQ + 512] = A_DQK ** -0.5 * LOG2E
    scale[:, Z_BQ:Z_BQ + 512] = HEAD_W ** -0.5 * LOG2E
    scale[:, Z_CK:Z_CK + 512] = HEAD_W ** -0.5
    L, D, _ = w.shape
    tn = TN_PREP
    wm = pl.pallas_call(
        functools.partial(_win_prep_kernel, first_shifted=BF_OFF // tn, shift=HEADS),
        grid=(L, Z_WIDTH // tn),
        in_specs=[
            pl.BlockSpec((None, D, tn), lambda l, j: (l, 0, j)),
            pl.BlockSpec((None, D, HEAD_W), lambda l, j: (l, 0, (j + 1) * (tn // HEAD_W))),
            pl.BlockSpec((1, tn), lambda l, j: (0, j)),
        ],
        out_specs=pl.BlockSpec((None, D, tn), lambda l, j: (l, 0, j)),
        out_shape=jax.ShapeDtypeStruct((L, D, Z_WIDTH), BF16),
        compiler_params=_cparams(("parallel", "parallel")),
        name="w_in_prep",
    )(w, w, jnp.asarray(scale))
    bm = jnp.concatenate([b[..., :BF_OFF], b[..., BF_OFF + HEADS:]], axis=-1) * scale
    pad = BF_PAD - HEADS
    wf = jnp.pad(w[..., BF_OFF:BF_OFF + HEADS], ((0, 0), (0, 0), (0, pad)))
    bf = jnp.pad(b[..., BF_OFF:BF_OFF + HEADS], ((0, 0), (0, pad)))
    return wm, bm[:, None, :], wf.astype(BF16), bf[:, None, :]


def _win_prep_kernel(w_ref, wnext_ref, s_ref, o_ref, *, first_shifted, shift):
    j = pl.program_id(1)

    @pl.when(j < first_shifted)
    def _():
        o_ref[...] = (w_ref[...] * s_ref[...]).astype(BF16)

    @pl.when(j >= first_shifted)
    def _():
        tn = w_ref.shape[1]
        wide = jnp.concatenate([w_ref[...], wnext_ref[...]], axis=1)
        moved = pltpu.roll(wide, wide.shape[1] - shift, 1)[:, :tn]
        o_ref[...] = (moved * s_ref[...]).astype(BF16)


def kernel(x, c, w_ada, b_ada, w_in, b_in, lam_q1, lam_k1, lam_q2, lam_k2, g_diff, g_ret, w_pa, w_pb, w_pc, w_out, ln_g, ln_b, w_up, w_conv, b_conv, w_down):
    B, S, D = x.shape
    depth = w_ada.shape[0]
    alpha = (2 * depth) ** 0.25
    chunk = 64
    mod = _ada(c, w_ada, b_ada)
    cos_t, sin_t = _rope_tables(S)
    wm, bm, wf, bf = _prep_in_proj(w_in, b_in)
    wpa, wpb, wpc, wout = (w.astype(BF16) for w in (w_pa, w_pb, w_pc, w_out))
    wup, wdown = w_up.astype(BF16), w_down.astype(BF16)
    for l in range(depth):
        mod_l = mod[l].reshape(B, 1, 6 * D)
        lam_init = jnp.full((1, 1), 0.8 - 0.6 * math.exp(-0.3 * l), F32)
        z, lf = _inproj(x, mod_l, wm, bm, wf, bf, l)
        f = _cumsum(lf)
        ya = _diff_attn(z, lam_q1[l][None], lam_k1[l][None], lam_q2[l][None], lam_k2[l][None],
                        lam_init, g_diff[l][None], chunk)
        yb = _fox_attn(z, f.reshape(B * f.shape[1], 1, S))
        yc = _retention(z, cos_t, sin_t, g_ret[l][None])
        x, h2 = _merge(ya, yb, yc, z, x, mod_l, wpa, wpb, wpc, wout,
                       ln_g[l, 0][None], ln_b[l, 0][None], alpha, l)
        x = _ffn(h2, x, mod_l, wup, w_conv[l], b_conv[l][None], wdown,
                 ln_g[l, 1][None], ln_b[l, 1][None], alpha, l)
    return x
```

```python
import functools
import math

import numpy as np
import jax
import jax.numpy as jnp
from jax import lax
from jax.experimental import pallas as pl
from jax.experimental.pallas import tpu as pltpu

F32 = jnp.float32
BF16 = jnp.bfloat16

HEADS = 4
A_DQK = 64
HEAD_W = 128
C_DV = 256
ROPE_BASE = 10000.0
LN_EPS = 1e-5

Z_AQ, Z_AK, Z_AV = 0, 512, 1024
Z_BQ, Z_BK, Z_BV = 1536, 2048, 2560
Z_CQ, Z_CK, Z_CV, Z_CG = 3072, 3584, 4096, 5120
Z_GATES = 6144
Z_WIDTH = 9216
BF_OFF = 3072
BF_PAD = 128

NEG = -1e30
LOG2E = 1.4426950408889634
VT_ROWS = 144

TN_ADA = 2048
TM_IN, TN_IN = 1024, 3072
TN_PREP = 1024
TQ_A = 512
TK_B = 512
NH_A = 4
NH_B = 4
PROLOGUE_CHUNK = 512
SKIP_MARGIN = 160.0
L_RET = 256
TM_MERGE = 1024
TM_FFN = 1024
CW_FFN = 256

VMEM_LIMIT = 56 * 1024 * 1024
VMEM_LIMIT_ATTN = 60 * 1024 * 1024


def _cparams(sem, vmem_limit=VMEM_LIMIT):
    return pltpu.CompilerParams(dimension_semantics=sem, vmem_limit_bytes=vmem_limit)


def _ada_kernel(ct_ref, w_ref, b_ref, o_ref):
    w = w_ref[0]
    rows = []
    for b in range(ct_ref.shape[0]):
        cb = ct_ref[b]
        ca = cb * jax.nn.sigmoid(cb)
        rows.append(jnp.sum(ca * w, axis=0, keepdims=True))
    o_ref[0] = jnp.concatenate(rows, axis=0) + b_ref[0]


def _ada(c, w_ada, b_ada):
    B, D = c.shape
    L, _, N = w_ada.shape
    tn = min(TN_ADA, N)
    return pl.pallas_call(
        _ada_kernel,
        grid=(L, N // tn),
        in_specs=[
            pl.BlockSpec((B, D, 1), lambda l, j: (0, 0, 0)),
            pl.BlockSpec((1, D, tn), lambda l, j: (l, 0, j)),
            pl.BlockSpec((1, 1, tn), lambda l, j: (l, 0, j)),
        ],
        out_specs=pl.BlockSpec((1, B, tn), lambda l, j: (l, 0, j)),
        out_shape=jax.ShapeDtypeStruct((L, B, N), F32),
        compiler_params=_cparams(("parallel", "parallel")),
        name="ada_mod",
    )(c.reshape(B, D, 1), w_ada, b_ada.reshape(L, 1, N))


def _inproj_kernel(x_ref, sh_ref, sc_ref, w_ref, b_ref, wf_ref, bf_ref, z_ref, lf_ref, h_scr):
    @pl.when(pl.program_id(2) == 0)
    def _():
        h = x_ref[0] * (1.0 + sc_ref[0]) + sh_ref[0]
        hb = h.astype(BF16)
        h_scr[...] = hb
        zf = jnp.dot(hb, wf_ref[...], preferred_element_type=F32) + bf_ref[...]
        lt = zf.T[:8]
        lf_ref[0] = jnp.minimum(lt, 0.0) - jnp.log1p(jnp.exp(-jnp.abs(lt)))

    acc = jnp.dot(h_scr[...], w_ref[...], preferred_element_type=F32)
    z_ref[0] = (acc + b_ref[...]).astype(BF16)


def _inproj(x, mod_l, w, b, wf, bf, l):
    B, S, D = x.shape
    N = w.shape[2]
    tm, tn = min(TM_IN, S), min(TN_IN, N)
    return pl.pallas_call(
        _inproj_kernel,
        grid=(B, S // tm, N // tn),
        in_specs=[
            pl.BlockSpec((1, tm, D), lambda b, i, j: (b, i, 0)),
            pl.BlockSpec((1, 1, D), lambda b, i, j: (b, 0, 0)),
            pl.BlockSpec((1, 1, D), lambda b, i, j: (b, 0, 1)),
            pl.BlockSpec((None, D, tn), lambda b, i, j: (l, 0, j)),
            pl.BlockSpec((None, 1, tn), lambda b, i, j: (l, 0, j)),
            pl.BlockSpec((None, D, BF_PAD), lambda b, i, j: (l, 0, 0)),
            pl.BlockSpec((None, 1, BF_PAD), lambda b, i, j: (l, 0, 0)),
        ],
        out_specs=[
            pl.BlockSpec((1, tm, tn), lambda b, i, j: (b, i, j)),
            pl.BlockSpec((1, 8, tm), lambda b, i, j: (b, 0, i)),
        ],
        out_shape=[
            jax.ShapeDtypeStruct((B, S, N), BF16),
            jax.ShapeDtypeStruct((B, 8, S), F32),
        ],
        scratch_shapes=[pltpu.VMEM((tm, D), BF16)],
        compiler_params=_cparams(("parallel", "parallel", "arbitrary")),
        name="in_proj",
    )(x, mod_l, mod_l, w, b, wf, bf)


def _cumsum_kernel(lf_ref, f_ref):
    x = lf_ref[0]
    lane = lax.broadcasted_iota(jnp.int32, x.shape, 1)
    k = 1
    while k < x.shape[1]:
        x = x + jnp.where(lane >= k, pltpu.roll(x, k, 1), 0.0)
        k *= 2
    f_ref[0] = x


def _cumsum(lf):
    B, R, S = lf.shape
    return pl.pallas_call(
        _cumsum_kernel,
        grid=(B,),
        in_specs=[pl.BlockSpec((1, R, S), lambda b: (b, 0, 0))],
        out_specs=pl.BlockSpec((1, R, S), lambda b: (b, 0, 0)),
        out_shape=jax.ShapeDtypeStruct((B, R, S), F32),
        compiler_params=_cparams(("parallel",)),
        name="forget_cumsum",
    )(lf)


def _build_vt(v_ref, vt_scr, tk):
    ones_row = (lax.broadcasted_iota(jnp.int32, (VT_ROWS - HEAD_W, tk), 0) == 0).astype(BF16)
    for c in range(v_ref.shape[1] // tk):
        vb = v_ref[0, c * tk:(c + 1) * tk, :].astype(F32)
        vt_scr[c, 0:HEAD_W, :] = vb.T.astype(BF16)
        vt_scr[c, HEAD_W:VT_ROWS, :] = ones_row


def _softmax_step_t(s, vt, m_scr, acc_scr, colmax=None):
    m_prev = m_scr[...]
    if colmax is None:
        colmax = jnp.max(s, axis=0, keepdims=True)
    m_new = jnp.maximum(m_prev, colmax)
    p = jnp.exp2(s - m_new).astype(BF16)
    alpha = jnp.exp2(m_prev - m_new)
    acc_scr[...] = alpha * acc_scr[...] + jnp.dot(vt, p, preferred_element_type=F32)
    m_scr[...] = m_new


def _put_scores(st, buf, mx, j):
    s = st["scores"](j)
    st[buf][...] = s
    st[mx][...] = jnp.max(s, axis=0, keepdims=True)


def _init_streams(streams, first_tile):
    for st in streams:
        st["m"][...] = jnp.full(st["m"].shape, NEG, F32)
        st["acc"][...] = jnp.zeros(st["acc"].shape, F32)
        _put_scores(st, "sa", "ma", first_tile)


def _pair_body(streams):
    def pair(t, carry):
        j = 2 * t
        for st in streams:
            _put_scores(st, "sb", "mb", j + 1)
            _softmax_step_t(st["sa"][...], st["vt"][j], st["m"], st["acc"], st["ma"][...])
        for st in streams:
            _put_scores(st, "sa", "ma", j + 2)
            _softmax_step_t(st["sb"][...], st["vt"][j + 1], st["m"], st["acc"], st["mb"][...])
        return carry
    return pair


def _sweep(qi, streams):
    _init_streams(streams, 0)
    lax.fori_loop(0, qi // 2, _pair_body(streams), 0)

    @pl.when(qi % 2 == 0)
    def _():
        for st in streams:
            _softmax_step_t(st["mask"](st["sa"][...]), st["vt"][qi], st["m"], st["acc"])

    @pl.when(qi % 2 == 1)
    def _():
        for st in streams:
            st["sb"][...] = st["scores"](qi)
            _softmax_step_t(st["sa"][...], st["vt"][qi - 1], st["m"], st["acc"], st["ma"][...])
        for st in streams:
            _softmax_step_t(st["mask"](st["sb"][...]), st["vt"][qi], st["m"], st["acc"])


def _sweep_wide(qi, j0, streams):
    for st in streams:
        st["m"][...] = jnp.full(st["m"].shape, NEG, F32)
        st["acc"][...] = jnp.zeros(st["acc"].shape, F32)

    @pl.when(j0 % 2 == 1)
    def _():
        for st in streams:
            _put_scores(st, "sb", "mb", j0)
        for st in streams:
            _put_scores(st, "sa", "ma", j0 + 1)
            _softmax_step_t(st["sb"][...], st["vt"][j0], st["m"], st["acc"], st["mb"][...])

    @pl.when(j0 % 2 == 0)
    def _():
        for st in streams:
            _put_scores(st, "sa", "ma", j0)

    lax.fori_loop((j0 + 1) // 2, qi, _pair_body(streams), 0)
    for st in streams:
        st["sl"][...] = st["scores_late"](2 * qi + 1)
        _softmax_step_t(st["mask"](st["sa"][...]), st["vt"][2 * qi], st["m"], st["acc"])
    for st in streams:
        nl = st["sl"].shape[1]
        _softmax_step_t(st["mask"](st["sl"][...]), st["vt"][2 * qi + 1],
                        st["m"].at[:, nl:], st["acc"].at[:, nl:])


_NT = (((1,), (1,)), ((), ()))


def _attn_scratch(S, tk, nq):
    return [pltpu.VMEM((S // tk, VT_ROWS, tk), BF16),
            pltpu.VMEM((1, nq), F32),
            pltpu.VMEM((VT_ROWS, nq), F32),
            pltpu.VMEM((tk, nq), F32), pltpu.VMEM((tk, nq), F32),
            pltpu.VMEM((1, nq), F32), pltpu.VMEM((1, nq), F32)]


def _head_cols(hh):
    return slice(hh * HEAD_W, (hh + 1) * HEAD_W)


def _diff_attn_kernel(q_ref, k_ref, v_ref, lq1_ref, lk1_ref, lq2_ref, lk2_ref, li_ref, g_ref,
                      o_ref, *scr, tq, chunk, nh):
    qi = pl.program_id(2)
    per = len(scr) // nh
    names = ("vt", "m", "acc", "sa", "sb", "ma", "mb")
    streams = [dict(zip(names, scr[hh * per:(hh + 1) * per])) for hh in range(nh)]

    @pl.when(qi == 0)
    def _():
        for hh, st in enumerate(streams):
            _build_vt(v_ref.at[:, :, _head_cols(hh)], st["vt"], tq)

    def mask_fn(s):
        key = lax.broadcasted_iota(jnp.int32, s.shape, 0)
        col = lax.broadcasted_iota(jnp.int32, s.shape, 1)
        qry = jnp.where(col >= tq, col - tq, col)
        return jnp.where(key // chunk <= qry // chunk, s, NEG)

    for hh, st in enumerate(streams):
        q = q_ref[0, :, _head_cols(hh)]
        lane = lax.broadcasted_iota(jnp.int32, q.shape, 1)
        zero = jnp.zeros_like(q)
        qs = jnp.concatenate([jnp.where(lane < A_DQK, q, zero),
                              jnp.where(lane >= A_DQK, q, zero)], axis=0)

        def scores(j, qs=qs, hh=hh):
            off = pl.multiple_of(j * tq, tq)
            return lax.dot_general(k_ref[0, pl.ds(off, tq), _head_cols(hh)], qs, _NT,
                                   preferred_element_type=F32)

        st["scores"], st["mask"] = scores, mask_fn

    _sweep(qi, streams)

    lam_init = li_ref[...]
    lam = (jnp.exp(jnp.sum(lq1_ref[...] * lk1_ref[...], axis=1, keepdims=True))
           - jnp.exp(jnp.sum(lq2_ref[...] * lk2_ref[...], axis=1, keepdims=True)) + lam_init)
    for hh, st in enumerate(streams):
        acc = st["acc"][...]
        ot_all = acc[0:HEAD_W] / acc[HEAD_W:HEAD_W + 1]
        o = (ot_all[:, :tq] - lam * ot_all[:, tq:]).T
        o = o * lax.rsqrt(jnp.mean(o * o, axis=1, keepdims=True) + LN_EPS)
        o_ref[0, :, _head_cols(hh)] = (o * g_ref[:, _head_cols(hh)] * (1.0 - lam_init)).astype(BF16)


def _resident(shape, index_map):
    return pl.BlockSpec(shape, index_map, pipeline_mode=pl.Buffered(1))


def _diff_attn(z, lq1, lk1, lq2, lk2, lam_init, g_diff, chunk):
    B, S, _ = z.shape
    tq = min(TQ_A, S)
    nh = NH_A
    w = nh * HEAD_W
    qb, kb, vb = Z_AQ // w, Z_AK // w, Z_AV // w
    small = pl.BlockSpec((1, A_DQK), lambda b, h, i: (0, 0))
    return pl.pallas_call(
        functools.partial(_diff_attn_kernel, tq=tq, chunk=chunk, nh=nh),
        grid=(B, HEADS // nh, S // tq),
        in_specs=[
            pl.BlockSpec((1, tq, w), lambda b, h, i: (b, i, qb + h)),
            pl.BlockSpec((1, S, w), lambda b, h, i: (b, 0, kb + h)),
            _resident((1, S, w), lambda b, h, i: (b, 0, vb + h)),
            small, small, small, small,
            pl.BlockSpec((1, 1), lambda b, h, i: (0, 0)),
            pl.BlockSpec((1, w), lambda b, h, i: (0, h)),
        ],
        out_specs=pl.BlockSpec((1, tq, w), lambda b, h, i: (b, i, h)),
        out_shape=jax.ShapeDtypeStruct((B, S, HEADS * HEAD_W), BF16),
        scratch_shapes=_attn_scratch(S, tq, 2 * tq) * nh,
        compiler_params=_cparams(("parallel", "parallel", "arbitrary"), VMEM_LIMIT_ATTN),
        name="diff_attn",
    )(z, z, z, lq1, lk1, lq2, lk2, lam_init, g_diff)


def _row_norm_max(x):
    xf = x.astype(F32)
    return jnp.sqrt(jnp.max(jnp.sum(xf * xf, axis=1, keepdims=True)))


def _fox_attn_kernel(q_ref, k_ref, v_ref, f_ref, o_ref, fmin_s, fmax_s, c0_s, kn_s, *scr, tk, nh):
    qi = pl.program_id(2)
    S = k_ref.shape[1]
    tq = 2 * tk
    n_blk = S // tq
    pc = min(PROLOGUE_CHUNK, S)
    per = len(scr) // nh
    names = ("vt", "m", "acc", "sa", "sb", "ma", "mb", "sl")
    streams = [dict(zip(names, scr[hh * per:(hh + 1) * per])) for hh in range(nh)]

    @pl.when(qi == 0)
    def _():
        for hh, st in enumerate(streams):
            _build_vt(v_ref.at[:, :, _head_cols(hh)], st["vt"], tk)
            kn = jnp.float32(0.0)
            for c in range(S // pc):
                kn = jnp.maximum(kn, _row_norm_max(k_ref[0, c * pc:(c + 1) * pc, _head_cols(hh)]))
            kn_s[hh] = kn
            for t in range(n_blk):
                fb = f_ref[hh, 2 * t:2 * t + 2, :] * LOG2E
                fmax_s[hh, t] = jnp.max(fb)
                c0_s[hh, t] = jnp.max(fb[0:1, 0:1])
            for j in range(2 * n_blk):
                fmin_s[hh, j] = jnp.min(f_ref[hh, j:j + 1, :] * LOG2E)

    def mask_fn(s):
        key = lax.broadcasted_iota(jnp.int32, s.shape, 0)
        qry = lax.broadcasted_iota(jnp.int32, s.shape, 1)
        return jnp.where(key <= qry, s, NEG)

    j0 = 2 * qi
    for hh, st in enumerate(streams):
        q = q_ref[0, :, _head_cols(hh)]
        c0 = c0_s[hh, qi]

        qk = _row_norm_max(q) * kn_s[hh] * 1.01 + 1.0
        thr = 2.0 * qk + (fmax_s[hh, qi] - c0_s[hh, qi]) + SKIP_MARGIN
        j0h = jnp.int32(0)
        for j in range(2 * n_blk - 2):
            skip = (j < 2 * qi) & (j0h == j) & (fmin_s[hh, j] - c0_s[hh, qi] >= thr)
            j0h = j0h + skip.astype(jnp.int32)
        j0 = jnp.minimum(j0, j0h)

        def biased(j, qq, st=st, hh=hh, c0=c0):
            off = pl.multiple_of(j * tk, tk)
            frow = f_ref[hh, pl.ds(j, 1), :] * LOG2E - c0
            fsub = jnp.broadcast_to(frow, (HEAD_W, tk)).T
            s = lax.dot_general(k_ref[0, pl.ds(off, tk), _head_cols(hh)], qq, _NT,
                                preferred_element_type=F32)
            return s - jnp.tile(fsub, (1, qq.shape[0] // HEAD_W))

        st["scores"] = functools.partial(biased, qq=q)
        st["scores_late"] = functools.partial(biased, qq=q[tk:])
        st["mask"] = mask_fn

    _sweep_wide(qi, j0, streams)

    for hh, st in enumerate(streams):
        acc = st["acc"][...]
        o_ref[0, :, _head_cols(hh)] = (acc[0:HEAD_W] / acc[HEAD_W:HEAD_W + 1]).T.astype(BF16)


def _fox_attn(z, f):
    B, S, _ = z.shape
    tk = min(TK_B, S // 2)
    f3 = f.reshape(B * f.shape[1], S // tk, tk)
    tq = 2 * tk
    nh = NH_B
    w = nh * HEAD_W
    rows = f3.shape[0] // B
    qb, kb, vb = Z_BQ // w, Z_BK // w, Z_BV // w
    per_head = _attn_scratch(S, tk, tq) + [pltpu.VMEM((tk, tq // 2), F32)]
    return pl.pallas_call(
        functools.partial(_fox_attn_kernel, tk=tk, nh=nh),
        grid=(B, HEADS // nh, S // tq),
        in_specs=[
            pl.BlockSpec((1, tq, w), lambda b, h, i: (b, i, qb + h)),
            _resident((1, S, w), lambda b, h, i: (b, 0, kb + h)),
            _resident((1, S, w), lambda b, h, i: (b, 0, vb + h)),
            pl.BlockSpec((nh, S // tk, tk), lambda b, h, i: (b * (rows // nh) + h, 0, 0)),
        ],
        out_specs=pl.BlockSpec((1, tq, w), lambda b, h, i: (b, i, h)),
        out_shape=jax.ShapeDtypeStruct((B, S, HEADS * HEAD_W), BF16),
        scratch_shapes=([pltpu.SMEM((nh, S // tk), F32)] + [pltpu.SMEM((nh, S // tq), F32)] * 2
                        + [pltpu.SMEM((nh,), F32)]
                        + per_head * nh),
        compiler_params=_cparams(("parallel", "parallel", "arbitrary"), VMEM_LIMIT_ATTN),
        name="fox_attn",
    )(z, z, z, f3)


def _retention_kernel(q_ref, k_ref, v_ref, cg_ref, cos_ref, sin_ref, dm_ref, g_ref, o_ref, r_scr,
                      *, L):
    @pl.when(pl.program_id(1) == 0)
    def _():
        r_scr[...] = jnp.zeros(r_scr.shape, F32)

    cosf = cos_ref[...]
    sinf = sin_ref[...]
    pos = lax.broadcasted_iota(jnp.int32, (L, 1), 0).astype(F32)

    for h in range(HEADS):
        lg = math.log(1.0 - 2.0 ** (-5.0 - h))
        qh = q_ref[0, :, h * HEAD_W:(h + 1) * HEAD_W].astype(F32)
        kh = k_ref[0, :, h * HEAD_W:(h + 1) * HEAD_W].astype(F32)
        qr = qh * cosf + pltpu.roll(qh, HEAD_W // 2, 1) * sinf
        kr = kh * cosf + pltpu.roll(kh, HEAD_W // 2, 1) * sinf
        v = v_ref[0, :, h * C_DV:(h + 1) * C_DV]
        s = lax.dot_general(qr.astype(BF16), kr.astype(BF16), _NT,
                            preferred_element_type=F32) * dm_ref[h]
        inner = jnp.dot(s.astype(BF16), v, preferred_element_type=F32)
        r = r_scr[h]
        q_dec = qr * jnp.exp(lg * (pos + 1.0))
        cross = jnp.dot(q_dec.astype(BF16), r.astype(BF16), preferred_element_type=F32)
        k_dec = kr * jnp.exp(lg * (L - 1.0 - pos))
        r_scr[h] = math.exp(lg * L) * r + jnp.dot(k_dec.T.astype(BF16), v, preferred_element_type=F32)
        o = inner + cross
        mu = jnp.mean(o, axis=1, keepdims=True)
        var = jnp.mean(jnp.square(o - mu), axis=1, keepdims=True)
        y = (o - mu) * lax.rsqrt(var + LN_EPS) * g_ref[:, h * C_DV:(h + 1) * C_DV]
        cg = cg_ref[0, :, h * C_DV:(h + 1) * C_DV].astype(F32)
        o_ref[0, :, h * C_DV:(h + 1) * C_DV] = (y * (cg * jax.nn.sigmoid(cg))).astype(BF16)


def _retention(z, cos_t, sin_t, g_ret):
    B, S, _ = z.shape
    L = min(L_RET, S)
    qk_w, v_w = HEADS * HEAD_W, HEADS * C_DV
    dist = np.arange(L)[:, None] - np.arange(L)[None, :]
    gam = 1.0 - 2.0 ** (-5.0 - np.arange(HEADS, dtype=np.float64))
    dm = jnp.asarray(np.where(dist >= 0, gam[:, None, None] ** np.maximum(dist, 0), 0.0), F32)
    return pl.pallas_call(
        functools.partial(_retention_kernel, L=L),
        grid=(B, S // L),
        in_specs=[
            pl.BlockSpec((1, L, qk_w), lambda b, n: (b, n, Z_CQ // qk_w)),
            pl.BlockSpec((1, L, qk_w), lambda b, n: (b, n, Z_CK // qk_w)),
            pl.BlockSpec((1, L, v_w), lambda b, n: (b, n, Z_CV // v_w)),
            pl.BlockSpec((1, L, v_w), lambda b, n: (b, n, Z_CG // v_w)),
            pl.BlockSpec((L, HEAD_W), lambda b, n: (n, 0)),
            pl.BlockSpec((L, HEAD_W), lambda b, n: (n, 0)),
            _const_spec(dm.shape),
            pl.BlockSpec((1, v_w), lambda b, n: (0, 0)),
        ],
        out_specs=pl.BlockSpec((1, L, v_w), lambda b, n: (b, n, 0)),
        out_shape=jax.ShapeDtypeStruct((B, S, v_w), BF16),
        scratch_shapes=[pltpu.VMEM((HEADS, HEAD_W, C_DV), F32)],
        compiler_params=_cparams(("parallel", "arbitrary")),
        name="retention",
    )(z, z, z, z, cos_t, sin_t, dm, g_ret)


def _rope_tables(S):
    half = HEAD_W // 2
    inv = 1.0 / (ROPE_BASE ** np.linspace(0.0, 1.0, half))
    ang = np.arange(S, dtype=np.float64)[:, None] * inv[None, :]
    cos, sin = np.cos(ang), np.sin(ang)
    return (jnp.asarray(np.concatenate([cos, cos], axis=1), F32),
            jnp.asarray(np.concatenate([-sin, sin], axis=1), F32))


def _layernorm(r, g, b):
    mu = jnp.mean(r, axis=1, keepdims=True)
    var = jnp.mean(jnp.square(r - mu), axis=1, keepdims=True)
    return (r - mu) * lax.rsqrt(var + LN_EPS) * g + b


def _merge_kernel(ya_ref, yb_ref, yc_ref, ga_ref, gb_ref, gc_ref, x_ref, gt_ref, sh2_ref, sc2_ref,
                  wpa_ref, wpb_ref, wpc_ref, wout_ref, lng_ref, lnb_ref, xo_ref, ho_ref, *, alpha):
    hs = x_ref.shape[1] // 2
    halves = [slice(0, hs), slice(hs, 2 * hs)]

    def branch(rows, y_ref, g_ref, w_ref):
        gate = jax.nn.sigmoid(g_ref[0, rows].astype(F32))
        return gate * jnp.dot(y_ref[0, rows], w_ref[...], preferred_element_type=F32)

    ms = [branch(r, ya_ref, ga_ref, wpa_ref) + branch(r, yb_ref, gb_ref, wpb_ref)
          + branch(r, yc_ref, gc_ref, wpc_ref) for r in halves]
    ys = [jnp.dot(m.astype(BF16), wout_ref[...], preferred_element_type=F32) for m in ms]
    for r, y in zip(halves, ys):
        xn = _layernorm(alpha * x_ref[0, r] + gt_ref[0] * y, lng_ref[...], lnb_ref[...])
        xo_ref[0, r] = xn
        ho_ref[0, r] = (xn * (1.0 + sc2_ref[0]) + sh2_ref[0]).astype(BF16)


def _const_spec(shape):
    return pl.BlockSpec(shape, lambda b, i: (0,) * len(shape), pipeline_mode=pl.Buffered(1))


def _layer_spec(stacked, l):
    shape = stacked.shape[1:]
    return pl.BlockSpec((None,) + shape, lambda b, i: (l,) + (0,) * len(shape),
                        pipeline_mode=pl.Buffered(1))


def _merge(ya, yb, yc, z, x, mod_l, wpa, wpb, wpc, wout, lng, lnb, alpha, l):
    B, S, D = x.shape
    tm = min(TM_MERGE, S)
    gblk = Z_GATES // D
    tok = lambda w, col: pl.BlockSpec((1, tm, w), lambda b, i: (b, i, col))
    modv = lambda col: pl.BlockSpec((1, 1, D), lambda b, i: (b, 0, col))
    return pl.pallas_call(
        functools.partial(_merge_kernel, alpha=alpha),
        grid=(B, S // tm),
        in_specs=[
            tok(ya.shape[2], 0), tok(yb.shape[2], 0), tok(yc.shape[2], 0),
            tok(D, gblk), tok(D, gblk + 1), tok(D, gblk + 2),
            tok(D, 0),
            modv(2), modv(3), modv(4),
            _layer_spec(wpa, l), _layer_spec(wpb, l), _layer_spec(wpc, l),
            _layer_spec(wout, l), _const_spec(lng.shape), _const_spec(lnb.shape),
        ],
        out_specs=[tok(D, 0), tok(D, 0)],
        out_shape=[jax.ShapeDtypeStruct((B, S, D), F32), jax.ShapeDtypeStruct((B, S, D), BF16)],
        compiler_params=_cparams(("parallel", "parallel")),
        name="merge_out_ln",
    )(ya, yb, yc, z, z, z, x, mod_l, mod_l, mod_l, wpa, wpb, wpc, wout, lng, lnb)


def _ffn_kernel(h_ref, x_ref, gt_ref, wup_ref, wconv_ref, bconv_ref, wdown_ref, lng_ref, lnb_ref,
                xo_ref, a_scr, tail_scr, *, alpha, cw):
    @pl.when(pl.program_id(1) == 0)
    def _():
        tail_scr[...] = jnp.zeros(tail_scr.shape, F32)

    h = h_ref[0]
    tm = h.shape[0]
    dff = a_scr.shape[1]
    row = lax.broadcasted_iota(jnp.int32, (tm, cw), 0)
    for ci in range(dff // cw):
        lo = ci * cw
        u = jnp.dot(h, wup_ref[:, lo:lo + cw], preferred_element_type=F32)
        g = jnp.dot(h, wup_ref[:, dff + lo:dff + lo + cw], preferred_element_type=F32)
        tail = tail_scr[:, lo:lo + cw]
        p1, p2 = tail[7:8], tail[6:7]
        um1 = jnp.where(row == 0, p1, pltpu.roll(u, 1, 0))
        um2 = jnp.where(row == 0, p2, jnp.where(row == 1, p1, pltpu.roll(u, 2, 0)))
        tail_scr[:, lo:lo + cw] = u[tm - 8:]
        conv = bconv_ref[:, lo:lo + cw] + um2 * wconv_ref[0:1, lo:lo + cw]
        conv = conv + um1 * wconv_ref[1:2, lo:lo + cw]
        conv = conv + u * wconv_ref[2:3, lo:lo + cw]
        act = 0.5 * conv * (1.0 + lax.erf(conv * (2.0 ** -0.5)))
        a_scr[:, lo:lo + cw] = (act * g).astype(BF16)
    halves = [slice(0, tm // 2), slice(tm // 2, tm)]
    ys = [jnp.dot(a_scr[r, :], wdown_ref[...], preferred_element_type=F32) for r in halves]
    for r, y in zip(halves, ys):
        xo_ref[0, r] = _layernorm(alpha * x_ref[0, r] + gt_ref[0] * y, lng_ref[...], lnb_ref[...])


def _ffn(h, x, mod_l, wup, wconv, bconv, wdown, lng, lnb, alpha, l):
    B, S, D = x.shape
    dff = wdown.shape[1]
    tm = min(TM_FFN, S)
    tok = pl.BlockSpec((1, tm, D), lambda b, i: (b, i, 0))
    return pl.pallas_call(
        functools.partial(_ffn_kernel, alpha=alpha, cw=CW_FFN),
        grid=(B, S // tm),
        in_specs=[
            tok, tok,
            pl.BlockSpec((1, 1, D), lambda b, i: (b, 0, 5)),
            _layer_spec(wup, l), _const_spec(wconv.shape), _const_spec(bconv.shape),
            _layer_spec(wdown, l), _const_spec(lng.shape), _const_spec(lnb.shape),
        ],
        out_specs=tok,
        out_shape=jax.ShapeDtypeStruct((B, S, D), F32),
        scratch_shapes=[pltpu.VMEM((tm, dff), BF16), pltpu.VMEM((8, dff), F32)],
        compiler_params=_cparams(("parallel", "arbitrary")),
        name="conv_ffn_ln",
    )(h, x, mod_l, wup, wconv, bconv, wdown, lng, lnb)


def _prep_in_proj(w, b):
    scale = np.ones((1, Z_WIDTH), np.float32)
    scale[:, Z_AQ:Z_AQ + 512] = A_DQK ** -0.5 * LOG2E
    scale[:, Z_BQ:Z_BQ + 512] = HEAD_W ** -0.5 * LOG2E
    scale[:, Z_CK:Z_CK + 512] = HEAD_W ** -0.5
    L, D, _ = w.shape
    tn = TN_PREP
    wm = pl.pallas_call(
        functools.partial(_win_prep_kernel, first_shifted=BF_OFF // tn, shift=HEADS),
        grid=(L, Z_WIDTH // tn),
        in_specs=[
            pl.BlockSpec((None, D, tn), lambda l, j: (l, 0, j)),
            pl.BlockSpec((None, D, HEAD_W), lambda l, j: (l, 0, (j + 1) * (tn // HEAD_W))),
            pl.BlockSpec((1, tn), lambda l, j: (0, j)),
        ],
        out_specs=pl.BlockSpec((None, D, tn), lambda l, j: (l, 0, j)),
        out_shape=jax.ShapeDtypeStruct((L, D, Z_WIDTH), BF16),
        compiler_params=_cparams(("parallel", "parallel")),
        name="w_in_prep",
    )(w, w, jnp.asarray(scale))
    bm = jnp.concatenate([b[..., :BF_OFF], b[..., BF_OFF + HEADS:]], axis=-1) * scale
    pad = BF_PAD - HEADS
    wf = jnp.pad(w[..., BF_OFF:BF_OFF + HEADS], ((0, 0), (0, 0), (0, pad)))
    bf = jnp.pad(b[..., BF_OFF:BF_OFF + HEADS], ((0, 0), (0, pad)))
    return wm, bm[:, None, :], wf.astype(BF16), bf[:, None, :]


def _win_prep_kernel(w_ref, wnext_ref, s_ref, o_ref, *, first_shifted, shift):
    j = pl.program_id(1)

    @pl.when(j < first_shifted)
    def _():
        o_ref[...] = (w_ref[...] * s_ref[...]).astype(BF16)

    @pl.when(j >= first_shifted)
    def _():
        tn = w_ref.shape[1]
        wide = jnp.concatenate([w_ref[...], wnext_ref[...]], axis=1)
        moved = pltpu.roll(wide, wide.shape[1] - shift, 1)[:, :tn]
        o_ref[...] = (moved * s_ref[...]).astype(BF16)


def kernel(x, c, w_ada, b_ada, w_in, b_in, lam_q1, lam_k1, lam_q2, lam_k2, g_diff, g_ret, w_pa, w_pb, w_pc, w_out, ln_g, ln_b, w_up, w_conv, b_conv, w_down):
    B, S, D = x.shape
    depth = w_ada.shape[0]
    alpha = (2 * depth) ** 0.25
    chunk = 64
    mod = _ada(c, w_ada, b_ada)
    cos_t, sin_t = _rope_tables(S)
    wm, bm, wf, bf = _prep_in_proj(w_in, b_in)
    wpa, wpb, wpc, wout = (w.astype(BF16) for w in (w_pa, w_pb, w_pc, w_out))
    wup, wdown = w_up.astype(BF16), w_down.astype(BF16)
    for l in range(depth):
        mod_l = mod[l].reshape(B, 1, 6 * D)
        lam_init = jnp.full((1, 1), 0.8 - 0.6 * math.exp(-0.3 * l), F32)
        z, lf = _inproj(x, mod_l, wm, bm, wf, bf, l)
        f = _cumsum(lf)
        ya = _diff_attn(z, lam_q1[l][None], lam_k1[l][None], lam_q2[l][None], lam_k2[l][None],
                        lam_init, g_diff[l][None], chunk)
        yb = _fox_attn(z, f)
        yc = _retention(z, cos_t, sin_t, g_ret[l][None])
        x, h2 = _merge(ya, yb, yc, z, x, mod_l, wpa, wpb, wpc, wout,
                       ln_g[l, 0][None], ln_b[l, 0][None], alpha, l)
        x = _ffn(h2, x, mod_l, wup, w_conv[l], b_conv[l][None], wdown,
                 ln_g[l, 1][None], ln_b[l, 1][None], alpha, l)
    return x
```

```python
import functools
import math

import numpy as np
import jax
import jax.numpy as jnp
from jax import lax
from jax.experimental import pallas as pl
from jax.experimental.pallas import tpu as pltpu

F32 = jnp.float32
BF16 = jnp.bfloat16

HEADS = 4
A_DQK = 64
HEAD_W = 128
C_DV = 256
ROPE_BASE = 10000.0
LN_EPS = 1e-5

Z_AQ, Z_AK, Z_AV = 0, 512, 1024
Z_BQ, Z_BK, Z_BV = 1536, 2048, 2560
Z_CQ, Z_CK, Z_CV, Z_CG = 3072, 3584, 4096, 5120
Z_GATES = 6144
Z_WIDTH = 9216
BF_OFF = 3072
BF_PAD = 128

NEG = -1e30
LOG2E = 1.4426950408889634
VT_ROWS = 144

TN_ADA = 2048
TM_IN, TN_IN = 1024, 3072
TR_PREP = 512
TQ_A = 512
TK_B = 512
NH_A = 4
NH_B = 4
PROLOGUE_CHUNK = 512
SKIP_MARGIN = 160.0
L_RET = 256
TM_MERGE = 1024
TM_FFN = 1024
CW_FFN = 256

VMEM_LIMIT = 56 * 1024 * 1024
VMEM_LIMIT_ATTN = 60 * 1024 * 1024


def _cparams(sem, vmem_limit=VMEM_LIMIT):
    return pltpu.CompilerParams(dimension_semantics=sem, vmem_limit_bytes=vmem_limit)


def _ada_kernel(ct_ref, w_ref, b_ref, o_ref):
    w = w_ref[0]
    rows = []
    for b in range(ct_ref.shape[0]):
        cb = ct_ref[b]
        ca = cb * jax.nn.sigmoid(cb)
        rows.append(jnp.sum(ca * w, axis=0, keepdims=True))
    o_ref[0] = jnp.concatenate(rows, axis=0) + b_ref[0]


def _ada(c, w_ada, b_ada):
    B, D = c.shape
    L, _, N = w_ada.shape
    tn = min(TN_ADA, N)
    return pl.pallas_call(
        _ada_kernel,
        grid=(L, N // tn),
        in_specs=[
            pl.BlockSpec((B, D, 1), lambda l, j: (0, 0, 0)),
            pl.BlockSpec((1, D, tn), lambda l, j: (l, 0, j)),
            pl.BlockSpec((1, 1, tn), lambda l, j: (l, 0, j)),
        ],
        out_specs=pl.BlockSpec((1, B, tn), lambda l, j: (l, 0, j)),
        out_shape=jax.ShapeDtypeStruct((L, B, N), F32),
        compiler_params=_cparams(("parallel", "parallel")),
        name="ada_mod",
    )(c.reshape(B, D, 1), w_ada, b_ada.reshape(L, 1, N))


def _inproj_kernel(x_ref, sh_ref, sc_ref, w_ref, b_ref, wf_ref, bf_ref, z_ref, lf_ref, h_scr):
    @pl.when(pl.program_id(2) == 0)
    def _():
        h = x_ref[0] * (1.0 + sc_ref[0]) + sh_ref[0]
        hb = h.astype(BF16)
        h_scr[...] = hb
        zf = jnp.dot(hb, wf_ref[...], preferred_element_type=F32) + bf_ref[...]
        lt = zf.T[:8]
        lf_ref[0] = jnp.minimum(lt, 0.0) - jnp.log1p(jnp.exp(-jnp.abs(lt)))

    acc = lax.dot_general(h_scr[...], w_ref[...], _NT, preferred_element_type=F32)
    z_ref[0] = (acc + b_ref[...]).astype(BF16)


def _inproj(x, mod_l, w, b, wf, bf, l):
    B, S, D = x.shape
    N = w.shape[1]
    tm, tn = min(TM_IN, S), min(TN_IN, N)
    return pl.pallas_call(
        _inproj_kernel,
        grid=(B, S // tm, N // tn),
        in_specs=[
            pl.BlockSpec((1, tm, D), lambda b, i, j: (b, i, 0)),
            pl.BlockSpec((1, 1, D), lambda b, i, j: (b, 0, 0)),
            pl.BlockSpec((1, 1, D), lambda b, i, j: (b, 0, 1)),
            pl.BlockSpec((None, tn, D), lambda b, i, j: (l, j, 0)),
            pl.BlockSpec((None, 1, tn), lambda b, i, j: (l, 0, j)),
            pl.BlockSpec((None, D, BF_PAD), lambda b, i, j: (l, 0, 0)),
            pl.BlockSpec((None, 1, BF_PAD), lambda b, i, j: (l, 0, 0)),
        ],
        out_specs=[
            pl.BlockSpec((1, tm, tn), lambda b, i, j: (b, i, j)),
            pl.BlockSpec((1, 8, tm), lambda b, i, j: (b, 0, i)),
        ],
        out_shape=[
            jax.ShapeDtypeStruct((B, S, N), BF16),
            jax.ShapeDtypeStruct((B, 8, S), F32),
        ],
        scratch_shapes=[pltpu.VMEM((tm, D), BF16)],
        compiler_params=_cparams(("parallel", "parallel", "arbitrary")),
        name="in_proj",
    )(x, mod_l, mod_l, w, b, wf, bf)


def _cumsum_kernel(lf_ref, f_ref):
    x = lf_ref[0]
    lane = lax.broadcasted_iota(jnp.int32, x.shape, 1)
    k = 1
    while k < x.shape[1]:
        x = x + jnp.where(lane >= k, pltpu.roll(x, k, 1), 0.0)
        k *= 2
    f_ref[0] = x


def _cumsum(lf):
    B, R, S = lf.shape
    return pl.pallas_call(
        _cumsum_kernel,
        grid=(B,),
        in_specs=[pl.BlockSpec((1, R, S), lambda b: (b, 0, 0))],
        out_specs=pl.BlockSpec((1, R, S), lambda b: (b, 0, 0)),
        out_shape=jax.ShapeDtypeStruct((B, R, S), F32),
        compiler_params=_cparams(("parallel",)),
        name="forget_cumsum",
    )(lf)


def _build_vt(v_ref, vt_scr, tk):
    ones_row = (lax.broadcasted_iota(jnp.int32, (VT_ROWS - HEAD_W, tk), 0) == 0).astype(BF16)
    for c in range(v_ref.shape[1] // tk):
        vb = v_ref[0, c * tk:(c + 1) * tk, :].astype(F32)
        vt_scr[c, 0:HEAD_W, :] = vb.T.astype(BF16)
        vt_scr[c, HEAD_W:VT_ROWS, :] = ones_row


def _softmax_step_t(s, vt, m_scr, acc_scr, colmax=None):
    m_prev = m_scr[...]
    if colmax is None:
        colmax = jnp.max(s, axis=0, keepdims=True)
    m_new = jnp.maximum(m_prev, colmax)
    p = jnp.exp2(s - m_new).astype(BF16)
    alpha = jnp.exp2(m_prev - m_new)
    acc_scr[...] = alpha * acc_scr[...] + jnp.dot(vt, p, preferred_element_type=F32)
    m_scr[...] = m_new


def _put_scores(st, buf, mx, j):
    s = st["scores"](j)
    st[buf][...] = s
    st[mx][...] = jnp.max(s, axis=0, keepdims=True)


def _init_streams(streams, first_tile):
    for st in streams:
        st["m"][...] = jnp.full(st["m"].shape, NEG, F32)
        st["acc"][...] = jnp.zeros(st["acc"].shape, F32)
        _put_scores(st, "sa", "ma", first_tile)


def _pair_body(streams):
    def pair(t, carry):
        j = 2 * t
        for st in streams:
            _put_scores(st, "sb", "mb", j + 1)
            _softmax_step_t(st["sa"][...], st["vt"][j], st["m"], st["acc"], st["ma"][...])
        for st in streams:
            _put_scores(st, "sa", "ma", j + 2)
            _softmax_step_t(st["sb"][...], st["vt"][j + 1], st["m"], st["acc"], st["mb"][...])
        return carry
    return pair


def _sweep(qi, streams):
    _init_streams(streams, 0)
    lax.fori_loop(0, qi // 2, _pair_body(streams), 0)

    @pl.when(qi % 2 == 0)
    def _():
        for st in streams:
            _softmax_step_t(st["mask"](st["sa"][...]), st["vt"][qi], st["m"], st["acc"])

    @pl.when(qi % 2 == 1)
    def _():
        for st in streams:
            st["sb"][...] = st["scores"](qi)
            _softmax_step_t(st["sa"][...], st["vt"][qi - 1], st["m"], st["acc"], st["ma"][...])
        for st in streams:
            _softmax_step_t(st["mask"](st["sb"][...]), st["vt"][qi], st["m"], st["acc"])


def _sweep_wide(qi, j0, streams):
    for st in streams:
        st["m"][...] = jnp.full(st["m"].shape, NEG, F32)
        st["acc"][...] = jnp.zeros(st["acc"].shape, F32)

    @pl.when(j0 % 2 == 1)
    def _():
        for st in streams:
            _put_scores(st, "sb", "mb", j0)
        for st in streams:
            _put_scores(st, "sa", "ma", j0 + 1)
            _softmax_step_t(st["sb"][...], st["vt"][j0], st["m"], st["acc"], st["mb"][...])

    @pl.when(j0 % 2 == 0)
    def _():
        for st in streams:
            _put_scores(st, "sa", "ma", j0)

    lax.fori_loop((j0 + 1) // 2, qi, _pair_body(streams), 0)
    for st in streams:
        st["sl"][...] = st["scores_late"](2 * qi + 1)
        _softmax_step_t(st["mask"](st["sa"][...]), st["vt"][2 * qi], st["m"], st["acc"])
    for st in streams:
        nl = st["sl"].shape[1]
        _softmax_step_t(st["mask"](st["sl"][...]), st["vt"][2 * qi + 1],
                        st["m"].at[:, nl:], st["acc"].at[:, nl:])


_NT = (((1,), (1,)), ((), ()))


def _attn_scratch(S, tk, nq):
    return [pltpu.VMEM((S // tk, VT_ROWS, tk), BF16),
            pltpu.VMEM((1, nq), F32),
            pltpu.VMEM((VT_ROWS, nq), F32),
            pltpu.VMEM((tk, nq), F32), pltpu.VMEM((tk, nq), F32),
            pltpu.VMEM((1, nq), F32), pltpu.VMEM((1, nq), F32)]


def _head_cols(hh):
    return slice(hh * HEAD_W, (hh + 1) * HEAD_W)


def _diff_attn_kernel(q_ref, k_ref, v_ref, lq1_ref, lk1_ref, lq2_ref, lk2_ref, li_ref, g_ref,
                      o_ref, *scr, tq, chunk, nh):
    qi = pl.program_id(2)
    per = len(scr) // nh
    names = ("vt", "m", "acc", "sa", "sb", "ma", "mb")
    streams = [dict(zip(names, scr[hh * per:(hh + 1) * per])) for hh in range(nh)]

    @pl.when(qi == 0)
    def _():
        for hh, st in enumerate(streams):
            _build_vt(v_ref.at[:, :, _head_cols(hh)], st["vt"], tq)

    def mask_fn(s):
        key = lax.broadcasted_iota(jnp.int32, s.shape, 0)
        col = lax.broadcasted_iota(jnp.int32, s.shape, 1)
        qry = jnp.where(col >= tq, col - tq, col)
        return jnp.where(key // chunk <= qry // chunk, s, NEG)

    for hh, st in enumerate(streams):
        q = q_ref[0, :, _head_cols(hh)]
        lane = lax.broadcasted_iota(jnp.int32, q.shape, 1)
        zero = jnp.zeros_like(q)
        qs = jnp.concatenate([jnp.where(lane < A_DQK, q, zero),
                              jnp.where(lane >= A_DQK, q, zero)], axis=0)

        def scores(j, qs=qs, hh=hh):
            off = pl.multiple_of(j * tq, tq)
            return lax.dot_general(k_ref[0, pl.ds(off, tq), _head_cols(hh)], qs, _NT,
                                   preferred_element_type=F32)

        st["scores"], st["mask"] = scores, mask_fn

    _sweep(qi, streams)

    lam_init = li_ref[...]
    lam = (jnp.exp(jnp.sum(lq1_ref[...] * lk1_ref[...], axis=1, keepdims=True))
           - jnp.exp(jnp.sum(lq2_ref[...] * lk2_ref[...], axis=1, keepdims=True)) + lam_init)
    for hh, st in enumerate(streams):
        acc = st["acc"][...]
        ot_all = acc[0:HEAD_W] / acc[HEAD_W:HEAD_W + 1]
        o = (ot_all[:, :tq] - lam * ot_all[:, tq:]).T
        o = o * lax.rsqrt(jnp.mean(o * o, axis=1, keepdims=True) + LN_EPS)
        o_ref[0, :, _head_cols(hh)] = (o * g_ref[:, _head_cols(hh)] * (1.0 - lam_init)).astype(BF16)


def _resident(shape, index_map):
    return pl.BlockSpec(shape, index_map, pipeline_mode=pl.Buffered(1))


def _diff_attn(z, lq1, lk1, lq2, lk2, lam_init, g_diff, chunk):
    B, S, _ = z.shape
    tq = min(TQ_A, S)
    nh = NH_A
    w = nh * HEAD_W
    qb, kb, vb = Z_AQ // w, Z_AK // w, Z_AV // w
    small = pl.BlockSpec((1, A_DQK), lambda b, h, i: (0, 0))
    return pl.pallas_call(
        functools.partial(_diff_attn_kernel, tq=tq, chunk=chunk, nh=nh),
        grid=(B, HEADS // nh, S // tq),
        in_specs=[
            pl.BlockSpec((1, tq, w), lambda b, h, i: (b, i, qb + h)),
            pl.BlockSpec((1, S, w), lambda b, h, i: (b, 0, kb + h)),
            _resident((1, S, w), lambda b, h, i: (b, 0, vb + h)),
            small, small, small, small,
            pl.BlockSpec((1, 1), lambda b, h, i: (0, 0)),
            pl.BlockSpec((1, w), lambda b, h, i: (0, h)),
        ],
        out_specs=pl.BlockSpec((1, tq, w), lambda b, h, i: (b, i, h)),
        out_shape=jax.ShapeDtypeStruct((B, S, HEADS * HEAD_W), BF16),
        scratch_shapes=_attn_scratch(S, tq, 2 * tq) * nh,
        compiler_params=_cparams(("parallel", "parallel", "arbitrary"), VMEM_LIMIT_ATTN),
        name="diff_attn",
    )(z, z, z, lq1, lk1, lq2, lk2, lam_init, g_diff)


def _row_norm_max(x):
    xf = x.astype(F32)
    return jnp.sqrt(jnp.max(jnp.sum(xf * xf, axis=1, keepdims=True)))


def _fox_attn_kernel(q_ref, k_ref, v_ref, f_ref, o_ref, fmin_s, fmax_s, c0_s, kn_s, *scr, tk, nh):
    qi = pl.program_id(2)
    S = k_ref.shape[1]
    tq = 2 * tk
    n_blk = S // tq
    pc = min(PROLOGUE_CHUNK, S)
    per = len(scr) // nh
    names = ("vt", "m", "acc", "sa", "sb", "ma", "mb", "sl")
    streams = [dict(zip(names, scr[hh * per:(hh + 1) * per])) for hh in range(nh)]

    @pl.when(qi == 0)
    def _():
        for hh, st in enumerate(streams):
            _build_vt(v_ref.at[:, :, _head_cols(hh)], st["vt"], tk)
            kn = jnp.float32(0.0)
            for c in range(S // pc):
                kn = jnp.maximum(kn, _row_norm_max(k_ref[0, c * pc:(c + 1) * pc, _head_cols(hh)]))
            kn_s[hh] = kn
            for t in range(n_blk):
                fb = f_ref[hh, 2 * t:2 * t + 2, :] * LOG2E
                fmax_s[hh, t] = jnp.max(fb)
                c0_s[hh, t] = jnp.max(fb[0:1, 0:1])
            for j in range(2 * n_blk):
                fmin_s[hh, j] = jnp.min(f_ref[hh, j:j + 1, :] * LOG2E)

    def mask_fn(s):
        key = lax.broadcasted_iota(jnp.int32, s.shape, 0)
        qry = lax.broadcasted_iota(jnp.int32, s.shape, 1)
        return jnp.where(key <= qry, s, NEG)

    j0 = 2 * qi
    for hh, st in enumerate(streams):
        q = q_ref[0, :, _head_cols(hh)]
        c0 = c0_s[hh, qi]

        qk = _row_norm_max(q) * kn_s[hh] * 1.01 + 1.0
        thr = 2.0 * qk + (fmax_s[hh, qi] - c0_s[hh, qi]) + SKIP_MARGIN
        j0h = jnp.int32(0)
        for j in range(2 * n_blk - 2):
            skip = (j < 2 * qi) & (j0h == j) & (fmin_s[hh, j] - c0_s[hh, qi] >= thr)
            j0h = j0h + skip.astype(jnp.int32)
        j0 = jnp.minimum(j0, j0h)

        def biased(j, qq, st=st, hh=hh, c0=c0):
            off = pl.multiple_of(j * tk, tk)
            frow = f_ref[hh, pl.ds(j, 1), :] * LOG2E - c0
            fsub = jnp.broadcast_to(frow, (HEAD_W, tk)).T
            s = lax.dot_general(k_ref[0, pl.ds(off, tk), _head_cols(hh)], qq, _NT,
                                preferred_element_type=F32)
            return s - jnp.tile(fsub, (1, qq.shape[0] // HEAD_W))

        st["scores"] = functools.partial(biased, qq=q)
        st["scores_late"] = functools.partial(biased, qq=q[tk:])
        st["mask"] = mask_fn

    _sweep_wide(qi, j0, streams)

    for hh, st in enumerate(streams):
        acc = st["acc"][...]
        o_ref[0, :, _head_cols(hh)] = (acc[0:HEAD_W] / acc[HEAD_W:HEAD_W + 1]).T.astype(BF16)


def _fox_attn(z, f):
    B, S, _ = z.shape
    tk = min(TK_B, S // 2)
    f3 = f.reshape(B * f.shape[1], S // tk, tk)
    tq = 2 * tk
    nh = NH_B
    w = nh * HEAD_W
    rows = f3.shape[0] // B
    qb, kb, vb = Z_BQ // w, Z_BK // w, Z_BV // w
    per_head = _attn_scratch(S, tk, tq) + [pltpu.VMEM((tk, tq // 2), F32)]
    return pl.pallas_call(
        functools.partial(_fox_attn_kernel, tk=tk, nh=nh),
        grid=(B, HEADS // nh, S // tq),
        in_specs=[
            pl.BlockSpec((1, tq, w), lambda b, h, i: (b, i, qb + h)),
            _resident((1, S, w), lambda b, h, i: (b, 0, kb + h)),
            _resident((1, S, w), lambda b, h, i: (b, 0, vb + h)),
            pl.BlockSpec((nh, S // tk, tk), lambda b, h, i: (b * (rows // nh) + h, 0, 0)),
        ],
        out_specs=pl.BlockSpec((1, tq, w), lambda b, h, i: (b, i, h)),
        out_shape=jax.ShapeDtypeStruct((B, S, HEADS * HEAD_W), BF16),
        scratch_shapes=([pltpu.SMEM((nh, S // tk), F32)] + [pltpu.SMEM((nh, S // tq), F32)] * 2
                        + [pltpu.SMEM((nh,), F32)]
                        + per_head * nh),
        compiler_params=_cparams(("parallel", "parallel", "arbitrary"), VMEM_LIMIT_ATTN),
        name="fox_attn",
    )(z, z, z, f3)


def _retention_kernel(q_ref, k_ref, v_ref, cg_ref, cos_ref, sin_ref, dm_ref, g_ref, o_ref, r_scr,
                      *, L):
    @pl.when(pl.program_id(1) == 0)
    def _():
        r_scr[...] = jnp.zeros(r_scr.shape, F32)

    cosf = cos_ref[...]
    sinf = sin_ref[...]
    pos = lax.broadcasted_iota(jnp.int32, (L, 1), 0).astype(F32)

    for h in range(HEADS):
        lg = math.log(1.0 - 2.0 ** (-5.0 - h))
        qh = q_ref[0, :, h * HEAD_W:(h + 1) * HEAD_W].astype(F32)
        kh = k_ref[0, :, h * HEAD_W:(h + 1) * HEAD_W].astype(F32)
        qr = qh * cosf + pltpu.roll(qh, HEAD_W // 2, 1) * sinf
        kr = kh * cosf + pltpu.roll(kh, HEAD_W // 2, 1) * sinf
        v = v_ref[0, :, h * C_DV:(h + 1) * C_DV]
        s = lax.dot_general(qr.astype(BF16), kr.astype(BF16), _NT,
                            preferred_element_type=F32) * dm_ref[h]
        inner = jnp.dot(s.astype(BF16), v, preferred_element_type=F32)
        r = r_scr[h]
        q_dec = qr * jnp.exp(lg * (pos + 1.0))
        cross = jnp.dot(q_dec.astype(BF16), r.astype(BF16), preferred_element_type=F32)
        k_dec = kr * jnp.exp(lg * (L - 1.0 - pos))
        r_scr[h] = math.exp(lg * L) * r + jnp.dot(k_dec.T.astype(BF16), v, preferred_element_type=F32)
        o = inner + cross
        mu = jnp.mean(o, axis=1, keepdims=True)
        var = jnp.mean(jnp.square(o - mu), axis=1, keepdims=True)
        y = (o - mu) * lax.rsqrt(var + LN_EPS) * g_ref[:, h * C_DV:(h + 1) * C_DV]
        cg = cg_ref[0, :, h * C_DV:(h + 1) * C_DV].astype(F32)
        o_ref[0, :, h * C_DV:(h + 1) * C_DV] = (y * (cg * jax.nn.sigmoid(cg))).astype(BF16)


def _retention(z, cos_t, sin_t, g_ret):
    B, S, _ = z.shape
    L = min(L_RET, S)
    qk_w, v_w = HEADS * HEAD_W, HEADS * C_DV
    dist = np.arange(L)[:, None] - np.arange(L)[None, :]
    gam = 1.0 - 2.0 ** (-5.0 - np.arange(HEADS, dtype=np.float64))
    dm = jnp.asarray(np.where(dist >= 0, gam[:, None, None] ** np.maximum(dist, 0), 0.0), F32)
    return pl.pallas_call(
        functools.partial(_retention_kernel, L=L),
        grid=(B, S // L),
        in_specs=[
            pl.BlockSpec((1, L, qk_w), lambda b, n: (b, n, Z_CQ // qk_w)),
            pl.BlockSpec((1, L, qk_w), lambda b, n: (b, n, Z_CK // qk_w)),
            pl.BlockSpec((1, L, v_w), lambda b, n: (b, n, Z_CV // v_w)),
            pl.BlockSpec((1, L, v_w), lambda b, n: (b, n, Z_CG // v_w)),
            pl.BlockSpec((L, HEAD_W), lambda b, n: (n, 0)),
            pl.BlockSpec((L, HEAD_W), lambda b, n: (n, 0)),
            _const_spec(dm.shape),
            pl.BlockSpec((1, v_w), lambda b, n: (0, 0)),
        ],
        out_specs=pl.BlockSpec((1, L, v_w), lambda b, n: (b, n, 0)),
        out_shape=jax.ShapeDtypeStruct((B, S, v_w), BF16),
        scratch_shapes=[pltpu.VMEM((HEADS, HEAD_W, C_DV), F32)],
        compiler_params=_cparams(("parallel", "arbitrary")),
        name="retention",
    )(z, z, z, z, cos_t, sin_t, dm, g_ret)


def _rope_tables(S):
    half = HEAD_W // 2
    inv = 1.0 / (ROPE_BASE ** np.linspace(0.0, 1.0, half))
    ang = np.arange(S, dtype=np.float64)[:, None] * inv[None, :]
    cos, sin = np.cos(ang), np.sin(ang)
    return (jnp.asarray(np.concatenate([cos, cos], axis=1), F32),
            jnp.asarray(np.concatenate([-sin, sin], axis=1), F32))


def _layernorm(r, g, b):
    mu = jnp.mean(r, axis=1, keepdims=True)
    var = jnp.mean(jnp.square(r - mu), axis=1, keepdims=True)
    return (r - mu) * lax.rsqrt(var + LN_EPS) * g + b


def _merge_kernel(ya_ref, yb_ref, yc_ref, ga_ref, gb_ref, gc_ref, x_ref, gt_ref, sh2_ref, sc2_ref,
                  wpa_ref, wpb_ref, wpc_ref, wout_ref, lng_ref, lnb_ref, xo_ref, ho_ref, *, alpha):
    hs = x_ref.shape[1] // 2
    halves = [slice(0, hs), slice(hs, 2 * hs)]

    def branch(rows, y_ref, g_ref, w_ref):
        gate = jax.nn.sigmoid(g_ref[0, rows].astype(F32))
        return gate * jnp.dot(y_ref[0, rows], w_ref[...], preferred_element_type=F32)

    ms = [branch(r, ya_ref, ga_ref, wpa_ref) + branch(r, yb_ref, gb_ref, wpb_ref)
          + branch(r, yc_ref, gc_ref, wpc_ref) for r in halves]
    ys = [jnp.dot(m.astype(BF16), wout_ref[...], preferred_element_type=F32) for m in ms]
    for r, y in zip(halves, ys):
        xn = _layernorm(alpha * x_ref[0, r] + gt_ref[0] * y, lng_ref[...], lnb_ref[...])
        xo_ref[0, r] = xn
        ho_ref[0, r] = (xn * (1.0 + sc2_ref[0]) + sh2_ref[0]).astype(BF16)


def _const_spec(shape):
    return pl.BlockSpec(shape, lambda b, i: (0,) * len(shape), pipeline_mode=pl.Buffered(1))


def _layer_spec(stacked, l):
    shape = stacked.shape[1:]
    return pl.BlockSpec((None,) + shape, lambda b, i: (l,) + (0,) * len(shape),
                        pipeline_mode=pl.Buffered(1))


def _merge(ya, yb, yc, z, x, mod_l, wpa, wpb, wpc, wout, lng, lnb, alpha, l):
    B, S, D = x.shape
    tm = min(TM_MERGE, S)
    gblk = Z_GATES // D
    tok = lambda w, col: pl.BlockSpec((1, tm, w), lambda b, i: (b, i, col))
    modv = lambda col: pl.BlockSpec((1, 1, D), lambda b, i: (b, 0, col))
    return pl.pallas_call(
        functools.partial(_merge_kernel, alpha=alpha),
        grid=(B, S // tm),
        in_specs=[
            tok(ya.shape[2], 0), tok(yb.shape[2], 0), tok(yc.shape[2], 0),
            tok(D, gblk), tok(D, gblk + 1), tok(D, gblk + 2),
            tok(D, 0),
            modv(2), modv(3), modv(4),
            _layer_spec(wpa, l), _layer_spec(wpb, l), _layer_spec(wpc, l),
            _layer_spec(wout, l), _const_spec(lng.shape), _const_spec(lnb.shape),
        ],
        out_specs=[tok(D, 0), tok(D, 0)],
        out_shape=[jax.ShapeDtypeStruct((B, S, D), F32), jax.ShapeDtypeStruct((B, S, D), BF16)],
        compiler_params=_cparams(("parallel", "parallel")),
        name="merge_out_ln",
    )(ya, yb, yc, z, z, z, x, mod_l, mod_l, mod_l, wpa, wpb, wpc, wout, lng, lnb)


def _ffn_kernel(h_ref, x_ref, gt_ref, wup_ref, wconv_ref, bconv_ref, wdown_ref, lng_ref, lnb_ref,
                xo_ref, a_scr, tail_scr, *, alpha, cw):
    @pl.when(pl.program_id(1) == 0)
    def _():
        tail_scr[...] = jnp.zeros(tail_scr.shape, F32)

    h = h_ref[0]
    tm = h.shape[0]
    dff = a_scr.shape[1]
    row = lax.broadcasted_iota(jnp.int32, (tm, cw), 0)
    for ci in range(dff // cw):
        lo = ci * cw
        u = jnp.dot(h, wup_ref[:, lo:lo + cw], preferred_element_type=F32)
        g = jnp.dot(h, wup_ref[:, dff + lo:dff + lo + cw], preferred_element_type=F32)
        tail = tail_scr[:, lo:lo + cw]
        p1, p2 = tail[7:8], tail[6:7]
        um1 = jnp.where(row == 0, p1, pltpu.roll(u, 1, 0))
        um2 = jnp.where(row == 0, p2, jnp.where(row == 1, p1, pltpu.roll(u, 2, 0)))
        tail_scr[:, lo:lo + cw] = u[tm - 8:]
        conv = bconv_ref[:, lo:lo + cw] + um2 * wconv_ref[0:1, lo:lo + cw]
        conv = conv + um1 * wconv_ref[1:2, lo:lo + cw]
        conv = conv + u * wconv_ref[2:3, lo:lo + cw]
        act = 0.5 * conv * (1.0 + lax.erf(conv * (2.0 ** -0.5)))
        a_scr[:, lo:lo + cw] = (act * g).astype(BF16)
    halves = [slice(0, tm // 2), slice(tm // 2, tm)]
    ys = [jnp.dot(a_scr[r, :], wdown_ref[...], preferred_element_type=F32) for r in halves]
    for r, y in zip(halves, ys):
        xo_ref[0, r] = _layernorm(alpha * x_ref[0, r] + gt_ref[0] * y, lng_ref[...], lnb_ref[...])


def _ffn(h, x, mod_l, wup, wconv, bconv, wdown, lng, lnb, alpha, l):
    B, S, D = x.shape
    dff = wdown.shape[1]
    tm = min(TM_FFN, S)
    tok = pl.BlockSpec((1, tm, D), lambda b, i: (b, i, 0))
    return pl.pallas_call(
        functools.partial(_ffn_kernel, alpha=alpha, cw=CW_FFN),
        grid=(B, S // tm),
        in_specs=[
            tok, tok,
            pl.BlockSpec((1, 1, D), lambda b, i: (b, 0, 5)),
            _layer_spec(wup, l), _const_spec(wconv.shape), _const_spec(bconv.shape),
            _layer_spec(wdown, l), _const_spec(lng.shape), _const_spec(lnb.shape),
        ],
        out_specs=tok,
        out_shape=jax.ShapeDtypeStruct((B, S, D), F32),
        scratch_shapes=[pltpu.VMEM((tm, dff), BF16), pltpu.VMEM((8, dff), F32)],
        compiler_params=_cparams(("parallel", "arbitrary")),
        name="conv_ffn_ln",
    )(h, x, mod_l, wup, wconv, bconv, wdown, lng, lnb)


def _prep_in_proj(w, b):
    scale = np.ones((1, Z_WIDTH), np.float32)
    scale[:, Z_AQ:Z_AQ + 512] = A_DQK ** -0.5 * LOG2E
    scale[:, Z_BQ:Z_BQ + 512] = HEAD_W ** -0.5 * LOG2E
    scale[:, Z_CK:Z_CK + 512] = HEAD_W ** -0.5
    L, D, _ = w.shape
    tr = TR_PREP
    wv = jnp.transpose(w, (2, 0, 1))
    wm = pl.pallas_call(
        functools.partial(_win_prep_kernel, first_shifted=BF_OFF // tr, shift=HEADS),
        grid=(Z_WIDTH // tr,),
        in_specs=[
            pl.BlockSpec((tr, L, D), lambda j: (j, 0, 0)),
            pl.BlockSpec((8, L, D), lambda j: ((j + 1) * (tr // 8), 0, 0)),
            pl.BlockSpec((tr, 1), lambda j: (j, 0)),
        ],
        out_specs=pl.BlockSpec((L, tr, D), lambda j: (0, j, 0)),
        out_shape=jax.ShapeDtypeStruct((L, Z_WIDTH, D), BF16),
        compiler_params=_cparams(("parallel",)),
        name="w_in_prep",
    )(wv, wv, jnp.asarray(scale.reshape(-1, 1)))
    bm = jnp.concatenate([b[..., :BF_OFF], b[..., BF_OFF + HEADS:]], axis=-1) * scale
    pad = BF_PAD - HEADS
    wf = jnp.pad(w[..., BF_OFF:BF_OFF + HEADS], ((0, 0), (0, 0), (0, pad)))
    bf = jnp.pad(b[..., BF_OFF:BF_OFF + HEADS], ((0, 0), (0, pad)))
    return wm, bm[:, None, :], wf.astype(BF16), bf[:, None, :]


def _win_prep_kernel(w_ref, wnext_ref, s_ref, o_ref, *, first_shifted, shift):
    j = pl.program_id(0)

    def emit(rows):
        o_ref[...] = (jnp.swapaxes(rows, 0, 1) * s_ref[...]).astype(BF16)

    @pl.when(j < first_shifted)
    def _():
        emit(w_ref[...])

    @pl.when(j >= first_shifted)
    def _():
        tr = w_ref.shape[0]
        emit(jnp.concatenate([w_ref[...], wnext_ref[...]], axis=0)[shift:shift + tr])


def kernel(x, c, w_ada, b_ada, w_in, b_in, lam_q1, lam_k1, lam_q2, lam_k2, g_diff, g_ret, w_pa, w_pb, w_pc, w_out, ln_g, ln_b, w_up, w_conv, b_conv, w_down):
    B, S, D = x.shape
    depth = w_ada.shape[0]
    alpha = (2 * depth) ** 0.25
    chunk = 64
    mod = _ada(c, w_ada, b_ada)
    cos_t, sin_t = _rope_tables(S)
    wm, bm, wf, bf = _prep_in_proj(w_in, b_in)
    wpa, wpb, wpc, wout = (w.astype(BF16) for w in (w_pa, w_pb, w_pc, w_out))
    wup, wdown = w_up.astype(BF16), w_down.astype(BF16)
    for l in range(depth):
        mod_l = mod[l].reshape(B, 1, 6 * D)
        lam_init = jnp.full((1, 1), 0.8 - 0.6 * math.exp(-0.3 * l), F32)
        z, lf = _inproj(x, mod_l, wm, bm, wf, bf, l)
        f = _cumsum(lf)
        ya = _diff_attn(z, lam_q1[l][None], lam_k1[l][None], lam_q2[l][None], lam_k2[l][None],
                        lam_init, g_diff[l][None], chunk)
        yb = _fox_attn(z, f)
        yc = _retention(z, cos_t, sin_t, g_ret[l][None])
        x, h2 = _merge(ya, yb, yc, z, x, mod_l, wpa, wpb, wpc, wout,
                       ln_g[l, 0][None], ln_b[l, 0][None], alpha, l)
        x = _ffn(h2, x, mod_l, wup, w_conv[l], b_conv[l][None], wdown,
                 ln_g[l, 1][None], ln_b[l, 1][None], alpha, l)
    return x
```

```python
import functools
import math

import numpy as np
import jax
import jax.numpy as jnp
from jax import lax
from jax.experimental import pallas as pl
from jax.experimental.pallas import tpu as pltpu

F32 = jnp.float32
BF16 = jnp.bfloat16

HEADS = 4
A_DQK = 64
HEAD_W = 128
C_DV = 256
ROPE_BASE = 10000.0
LN_EPS = 1e-5

Z_AQ, Z_AK, Z_AV = 0, 512, 1024
Z_BQ, Z_BK, Z_BV = 1536, 2048, 2560
Z_CQ, Z_CK, Z_CV, Z_CG = 3072, 3584, 4096, 5120
Z_GATES = 6144
Z_WIDTH = 9216
BF_OFF = 3072
BF_PAD = 128

NEG = -1e30
LOG2E = 1.4426950408889634
VT_ROWS = 144

TN_ADA = 2048
TM_IN, TN_IN = 1024, 3072
TR_PREP = 512
TQ_A = 512
TK_B = 512
NH_A = 4
NH_B = 4
PROLOGUE_CHUNK = 512
SKIP_MARGIN = 160.0
L_RET = 256
TM_MERGE = 1024
TM_FFN = 1024
CW_FFN = 256

VMEM_LIMIT = 56 * 1024 * 1024
VMEM_LIMIT_ATTN = 60 * 1024 * 1024


def _cparams(sem, vmem_limit=VMEM_LIMIT):
    return pltpu.CompilerParams(dimension_semantics=sem, vmem_limit_bytes=vmem_limit)


def _ada_kernel(ct_ref, w_ref, b_ref, o_ref):
    w = w_ref[0]
    rows = []
    for b in range(ct_ref.shape[0]):
        cb = ct_ref[b]
        ca = cb * jax.nn.sigmoid(cb)
        rows.append(jnp.sum(ca * w, axis=0, keepdims=True))
    o_ref[0] = jnp.concatenate(rows, axis=0) + b_ref[0]


def _ada(c, w_ada, b_ada):
    B, D = c.shape
    L, _, N = w_ada.shape
    tn = min(TN_ADA, N)
    return pl.pallas_call(
        _ada_kernel,
        grid=(L, N // tn),
        in_specs=[
            pl.BlockSpec((B, D, 1), lambda l, j: (0, 0, 0)),
            pl.BlockSpec((1, D, tn), lambda l, j: (l, 0, j)),
            pl.BlockSpec((1, 1, tn), lambda l, j: (l, 0, j)),
        ],
        out_specs=pl.BlockSpec((1, B, tn), lambda l, j: (l, 0, j)),
        out_shape=jax.ShapeDtypeStruct((L, B, N), F32),
        compiler_params=_cparams(("parallel", "parallel")),
        name="ada_mod",
    )(c.reshape(B, D, 1), w_ada, b_ada.reshape(L, 1, N))


def _inproj_kernel(x_ref, sh_ref, sc_ref, w_ref, b_ref, wf_ref, bf_ref, z_ref, lf_ref, h_scr):
    @pl.when(pl.program_id(2) == 0)
    def _():
        h = x_ref[0] * (1.0 + sc_ref[0]) + sh_ref[0]
        hb = h.astype(BF16)
        h_scr[...] = hb
        zf = jnp.dot(hb, wf_ref[...], preferred_element_type=F32) + bf_ref[...]
        lt = zf.T[:8]
        lf_ref[0] = jnp.minimum(lt, 0.0) - jnp.log1p(jnp.exp(-jnp.abs(lt)))

    acc = lax.dot_general(h_scr[...], w_ref[...], _NT, preferred_element_type=F32)
    z_ref[0] = (acc + b_ref[...]).astype(BF16)


def _inproj(x, mod_l, w, b, wf, bf, l):
    B, S, D = x.shape
    N = w.shape[1]
    tm, tn = min(TM_IN, S), min(TN_IN, N)
    return pl.pallas_call(
        _inproj_kernel,
        grid=(B, S // tm, N // tn),
        in_specs=[
            pl.BlockSpec((1, tm, D), lambda b, i, j: (b, i, 0)),
            pl.BlockSpec((1, 1, D), lambda b, i, j: (b, 0, 0)),
            pl.BlockSpec((1, 1, D), lambda b, i, j: (b, 0, 1)),
            pl.BlockSpec((None, tn, D), lambda b, i, j: (l, j, 0)),
            pl.BlockSpec((None, 1, tn), lambda b, i, j: (l, 0, j)),
            pl.BlockSpec((None, D, BF_PAD), lambda b, i, j: (l, 0, 0)),
            pl.BlockSpec((None, 1, BF_PAD), lambda b, i, j: (l, 0, 0)),
        ],
        out_specs=[
            pl.BlockSpec((1, tm, tn), lambda b, i, j: (b, i, j)),
            pl.BlockSpec((1, 8, tm), lambda b, i, j: (b, 0, i)),
        ],
        out_shape=[
            jax.ShapeDtypeStruct((B, S, N), BF16),
            jax.ShapeDtypeStruct((B, 8, S), F32),
        ],
        scratch_shapes=[pltpu.VMEM((tm, D), BF16)],
        compiler_params=_cparams(("parallel", "parallel", "arbitrary")),
        name="in_proj",
    )(x, mod_l, mod_l, w, b, wf, bf)


def _cumsum_kernel(lf_ref, f_ref):
    x = lf_ref[0]
    lane = lax.broadcasted_iota(jnp.int32, x.shape, 1)
    k = 1
    while k < x.shape[1]:
        x = x + jnp.where(lane >= k, pltpu.roll(x, k, 1), 0.0)
        k *= 2
    f_ref[0] = x


def _cumsum(lf):
    B, R, S = lf.shape
    return pl.pallas_call(
        _cumsum_kernel,
        grid=(B,),
        in_specs=[pl.BlockSpec((1, R, S), lambda b: (b, 0, 0))],
        out_specs=pl.BlockSpec((1, R, S), lambda b: (b, 0, 0)),
        out_shape=jax.ShapeDtypeStruct((B, R, S), F32),
        compiler_params=_cparams(("parallel",)),
        name="forget_cumsum",
    )(lf)


def _build_vt(v_ref, vt_scr, tk):
    ones_row = (lax.broadcasted_iota(jnp.int32, (VT_ROWS - HEAD_W, tk), 0) == 0).astype(BF16)
    for c in range(v_ref.shape[1] // tk):
        vb = v_ref[0, c * tk:(c + 1) * tk, :].astype(F32)
        vt_scr[c, 0:HEAD_W, :] = vb.T.astype(BF16)
        vt_scr[c, HEAD_W:VT_ROWS, :] = ones_row


def _softmax_step_t(s, vt, m_scr, acc_scr, colmax=None):
    m_prev = m_scr[...]
    if colmax is None:
        colmax = jnp.max(s, axis=0, keepdims=True)
    m_new = jnp.maximum(m_prev, colmax)
    p = jnp.exp2(s - m_new).astype(BF16)
    alpha = jnp.exp2(m_prev - m_new)
    acc_scr[...] = alpha * acc_scr[...] + jnp.dot(vt, p, preferred_element_type=F32)
    m_scr[...] = m_new


def _put_scores(st, buf, mx, j):
    s = st["scores"](j)
    st[buf][...] = s
    st[mx][...] = jnp.max(s, axis=0, keepdims=True)


def _init_streams(streams, first_tile):
    for st in streams:
        st["m"][...] = jnp.full(st["m"].shape, NEG, F32)
        st["acc"][...] = jnp.zeros(st["acc"].shape, F32)
        _put_scores(st, "sa", "ma", first_tile)


def _pair_body(streams):
    def pair(t, carry):
        j = 2 * t
        for st in streams:
            _put_scores(st, "sb", "mb", j + 1)
            _softmax_step_t(st["sa"][...], st["vt"][j], st["m"], st["acc"], st["ma"][...])
        for st in streams:
            _put_scores(st, "sa", "ma", j + 2)
            _softmax_step_t(st["sb"][...], st["vt"][j + 1], st["m"], st["acc"], st["mb"][...])
        return carry
    return pair


def _sweep(qi, streams):
    _init_streams(streams, 0)
    lax.fori_loop(0, qi // 2, _pair_body(streams), 0)

    @pl.when(qi % 2 == 0)
    def _():
        for st in streams:
            _softmax_step_t(st["mask"](st["sa"][...]), st["vt"][qi], st["m"], st["acc"])

    @pl.when(qi % 2 == 1)
    def _():
        for st in streams:
            st["sb"][...] = st["scores"](qi)
            _softmax_step_t(st["sa"][...], st["vt"][qi - 1], st["m"], st["acc"], st["ma"][...])
        for st in streams:
            _softmax_step_t(st["mask"](st["sb"][...]), st["vt"][qi], st["m"], st["acc"])


def _sweep_wide(qi, j0, streams):
    for st in streams:
        st["m"][...] = jnp.full(st["m"].shape, NEG, F32)
        st["acc"][...] = jnp.zeros(st["acc"].shape, F32)

    @pl.when(j0 % 2 == 1)
    def _():
        for st in streams:
            _put_scores(st, "sb", "mb", j0)
        for st in streams:
            _put_scores(st, "sa", "ma", j0 + 1)
            _softmax_step_t(st["sb"][...], st["vt"][j0], st["m"], st["acc"], st["mb"][...])

    @pl.when(j0 % 2 == 0)
    def _():
        for st in streams:
            _put_scores(st, "sa", "ma", j0)

    lax.fori_loop((j0 + 1) // 2, qi, _pair_body(streams), 0)
    for st in streams:
        st["sl"][...] = st["scores_late"](2 * qi + 1)
        _softmax_step_t(st["mask"](st["sa"][...]), st["vt"][2 * qi], st["m"], st["acc"])
    for st in streams:
        nl = st["sl"].shape[1]
        _softmax_step_t(st["mask"](st["sl"][...]), st["vt"][2 * qi + 1],
                        st["m"].at[:, nl:], st["acc"].at[:, nl:])


_NT = (((1,), (1,)), ((), ()))


def _attn_scratch(S, tk, nq):
    return [pltpu.VMEM((S // tk, VT_ROWS, tk), BF16),
            pltpu.VMEM((1, nq), F32),
            pltpu.VMEM((VT_ROWS, nq), F32),
            pltpu.VMEM((tk, nq), F32), pltpu.VMEM((tk, nq), F32),
            pltpu.VMEM((1, nq), F32), pltpu.VMEM((1, nq), F32)]


def _head_cols(hh):
    return slice(hh * HEAD_W, (hh + 1) * HEAD_W)


def _diff_attn_kernel(q_ref, k_ref, v_ref, lq1_ref, lk1_ref, lq2_ref, lk2_ref, li_ref, g_ref,
                      o_ref, *scr, tq, chunk, nh):
    qi = pl.program_id(2)
    per = len(scr) // nh
    names = ("vt", "m", "acc", "sa", "sb", "ma", "mb")
    streams = [dict(zip(names, scr[hh * per:(hh + 1) * per])) for hh in range(nh)]

    @pl.when(qi == 0)
    def _():
        for hh, st in enumerate(streams):
            _build_vt(v_ref.at[:, :, _head_cols(hh)], st["vt"], tq)

    def mask_fn(s):
        key = lax.broadcasted_iota(jnp.int32, s.shape, 0)
        col = lax.broadcasted_iota(jnp.int32, s.shape, 1)
        qry = jnp.where(col >= tq, col - tq, col)
        return jnp.where(key // chunk <= qry // chunk, s, NEG)

    for hh, st in enumerate(streams):
        q = q_ref[0, :, _head_cols(hh)]
        lane = lax.broadcasted_iota(jnp.int32, q.shape, 1)
        zero = jnp.zeros_like(q)
        qs = jnp.concatenate([jnp.where(lane < A_DQK, q, zero),
                              jnp.where(lane >= A_DQK, q, zero)], axis=0)

        def scores(j, qs=qs, hh=hh):
            off = pl.multiple_of(j * tq, tq)
            return lax.dot_general(k_ref[0, pl.ds(off, tq), _head_cols(hh)], qs, _NT,
                                   preferred_element_type=F32)

        st["scores"], st["mask"] = scores, mask_fn

    _sweep(qi, streams)

    lam_init = li_ref[...]
    lam = (jnp.exp(jnp.sum(lq1_ref[...] * lk1_ref[...], axis=1, keepdims=True))
           - jnp.exp(jnp.sum(lq2_ref[...] * lk2_ref[...], axis=1, keepdims=True)) + lam_init)
    for hh, st in enumerate(streams):
        acc = st["acc"][...]
        ot_all = acc[0:HEAD_W] / acc[HEAD_W:HEAD_W + 1]
        o = (ot_all[:, :tq] - lam * ot_all[:, tq:]).T
        o = o * lax.rsqrt(jnp.mean(o * o, axis=1, keepdims=True) + LN_EPS)
        o_ref[0, :, _head_cols(hh)] = (o * g_ref[:, _head_cols(hh)] * (1.0 - lam_init)).astype(BF16)


def _resident(shape, index_map):
    return pl.BlockSpec(shape, index_map, pipeline_mode=pl.Buffered(1))


def _diff_attn(z, lq1, lk1, lq2, lk2, lam_init, g_diff, chunk):
    B, S, _ = z.shape
    tq = min(TQ_A, S)
    nh = NH_A
    w = nh * HEAD_W
    qb, kb, vb = Z_AQ // w, Z_AK // w, Z_AV // w
    small = pl.BlockSpec((1, A_DQK), lambda b, h, i: (0, 0))
    return pl.pallas_call(
        functools.partial(_diff_attn_kernel, tq=tq, chunk=chunk, nh=nh),
        grid=(B, HEADS // nh, S // tq),
        in_specs=[
            pl.BlockSpec((1, tq, w), lambda b, h, i: (b, i, qb + h)),
            pl.BlockSpec((1, S, w), lambda b, h, i: (b, 0, kb + h)),
            _resident((1, S, w), lambda b, h, i: (b, 0, vb + h)),
            small, small, small, small,
            pl.BlockSpec((1, 1), lambda b, h, i: (0, 0)),
            pl.BlockSpec((1, w), lambda b, h, i: (0, h)),
        ],
        out_specs=pl.BlockSpec((1, tq, w), lambda b, h, i: (b, i, h)),
        out_shape=jax.ShapeDtypeStruct((B, S, HEADS * HEAD_W), BF16),
        scratch_shapes=_attn_scratch(S, tq, 2 * tq) * nh,
        compiler_params=_cparams(("parallel", "parallel", "arbitrary"), VMEM_LIMIT_ATTN),
        name="diff_attn",
    )(z, z, z, lq1, lk1, lq2, lk2, lam_init, g_diff)


def _row_norm_max(x):
    xf = x.astype(F32)
    return jnp.sqrt(jnp.max(jnp.sum(xf * xf, axis=1, keepdims=True)))


def _fox_attn_kernel(q_ref, k_ref, v_ref, f_ref, o_ref, fmin_s, fmax_s, c0_s, kn_s, *scr, tk, nh):
    qi = pl.program_id(2)
    S = k_ref.shape[1]
    tq = 2 * tk
    n_blk = S // tq
    pc = min(PROLOGUE_CHUNK, S)
    per = len(scr) // nh
    names = ("vt", "m", "acc", "sa", "sb", "ma", "mb", "sl")
    streams = [dict(zip(names, scr[hh * per:(hh + 1) * per])) for hh in range(nh)]

    @pl.when(qi == 0)
    def _():
        for hh, st in enumerate(streams):
            _build_vt(v_ref.at[:, :, _head_cols(hh)], st["vt"], tk)
            kn = jnp.float32(0.0)
            for c in range(S // pc):
                kn = jnp.maximum(kn, _row_norm_max(k_ref[0, c * pc:(c + 1) * pc, _head_cols(hh)]))
            kn_s[hh] = kn
            for t in range(n_blk):
                fb = f_ref[hh, 2 * t:2 * t + 2, :] * LOG2E
                fmax_s[hh, t] = jnp.max(fb)
                c0_s[hh, t] = jnp.max(fb[0:1, 0:1])
            for j in range(2 * n_blk):
                fmin_s[hh, j] = jnp.min(f_ref[hh, j:j + 1, :] * LOG2E)

    def mask_fn(s):
        key = lax.broadcasted_iota(jnp.int32, s.shape, 0)
        qry = lax.broadcasted_iota(jnp.int32, s.shape, 1)
        return jnp.where(key <= qry, s, NEG)

    j0 = 2 * qi
    for hh, st in enumerate(streams):
        q = q_ref[0, :, _head_cols(hh)]
        c0 = c0_s[hh, qi]

        qk = _row_norm_max(q) * kn_s[hh] * 1.01 + 1.0
        thr = 2.0 * qk + (fmax_s[hh, qi] - c0_s[hh, qi]) + SKIP_MARGIN
        j0h = jnp.int32(0)
        for j in range(2 * n_blk - 2):
            skip = (j < 2 * qi) & (j0h == j) & (fmin_s[hh, j] - c0_s[hh, qi] >= thr)
            j0h = j0h + skip.astype(jnp.int32)
        j0 = jnp.minimum(j0, j0h)

        def biased(j, qq, st=st, hh=hh, c0=c0):
            off = pl.multiple_of(j * tk, tk)
            frow = f_ref[hh, pl.ds(j, 1), :] * LOG2E - c0
            fsub = jnp.broadcast_to(frow, (HEAD_W, tk)).T
            s = lax.dot_general(k_ref[0, pl.ds(off, tk), _head_cols(hh)], qq, _NT,
                                preferred_element_type=F32)
            return s - jnp.tile(fsub, (1, qq.shape[0] // HEAD_W))

        st["scores"] = functools.partial(biased, qq=q)
        st["scores_late"] = functools.partial(biased, qq=q[tk:])
        st["mask"] = mask_fn

    _sweep_wide(qi, j0, streams)

    for hh, st in enumerate(streams):
        acc = st["acc"][...]
        o_ref[0, :, _head_cols(hh)] = (acc[0:HEAD_W] / acc[HEAD_W:HEAD_W + 1]).T.astype(BF16)


def _fox_attn(z, f):
    B, S, _ = z.shape
    tk = min(TK_B, S // 2)
    f3 = f.reshape(B * f.shape[1], S // tk, tk)
    tq = 2 * tk
    nh = NH_B
    w = nh * HEAD_W
    rows = f3.shape[0] // B
    qb, kb, vb = Z_BQ // w, Z_BK // w, Z_BV // w
    per_head = _attn_scratch(S, tk, tq) + [pltpu.VMEM((tk, tq // 2), F32)]
    return pl.pallas_call(
        functools.partial(_fox_attn_kernel, tk=tk, nh=nh),
        grid=(B, HEADS // nh, S // tq),
        in_specs=[
            pl.BlockSpec((1, tq, w), lambda b, h, i: (b, i, qb + h)),
            _resident((1, S, w), lambda b, h, i: (b, 0, kb + h)),
            _resident((1, S, w), lambda b, h, i: (b, 0, vb + h)),
            pl.BlockSpec((nh, S // tk, tk), lambda b, h, i: (b * (rows // nh) + h, 0, 0)),
        ],
        out_specs=pl.BlockSpec((1, tq, w), lambda b, h, i: (b, i, h)),
        out_shape=jax.ShapeDtypeStruct((B, S, HEADS * HEAD_W), BF16),
        scratch_shapes=([pltpu.SMEM((nh, S // tk), F32)] + [pltpu.SMEM((nh, S // tq), F32)] * 2
                        + [pltpu.SMEM((nh,), F32)]
                        + per_head * nh),
        compiler_params=_cparams(("parallel", "parallel", "arbitrary"), VMEM_LIMIT_ATTN),
        name="fox_attn",
    )(z, z, z, f3)


def _retention_kernel(q_ref, k_ref, v_ref, cg_ref, cos_ref, sin_ref, dm_ref, g_ref, o_ref, r_scr,
                      *, L):
    @pl.when(pl.program_id(1) == 0)
    def _():
        r_scr[...] = jnp.zeros(r_scr.shape, F32)

    cosf = cos_ref[...]
    sinf = sin_ref[...]
    pos = lax.broadcasted_iota(jnp.int32, (L, 1), 0).astype(F32)

    for h in range(HEADS):
        lg = math.log(1.0 - 2.0 ** (-5.0 - h))
        qh = q_ref[0, :, h * HEAD_W:(h + 1) * HEAD_W].astype(F32)
        kh = k_ref[0, :, h * HEAD_W:(h + 1) * HEAD_W].astype(F32)
        qr = qh * cosf + pltpu.roll(qh, HEAD_W // 2, 1) * sinf
        kr = kh * cosf + pltpu.roll(kh, HEAD_W // 2, 1) * sinf
        v = v_ref[0, :, h * C_DV:(h + 1) * C_DV]
        q_dec = (qr * jnp.exp(lg * (pos + 1.0))).astype(BF16)
        k_decf = kr * jnp.exp(lg * (L - 1.0 - pos))
        s = lax.dot_general(q_dec, k_decf.astype(BF16), _NT,
                            preferred_element_type=F32) * dm_ref[h]
        r = r_scr[h]
        o = (jnp.dot(s.astype(BF16), v, preferred_element_type=F32)
             + jnp.dot(q_dec, r.astype(BF16), preferred_element_type=F32))
        r_scr[h] = math.exp(lg * L) * r + jnp.dot(k_decf.T.astype(BF16), v, preferred_element_type=F32)
        mu = jnp.mean(o, axis=1, keepdims=True)
        var = jnp.mean(jnp.square(o - mu), axis=1, keepdims=True)
        y = (o - mu) * lax.rsqrt(var + LN_EPS) * g_ref[:, h * C_DV:(h + 1) * C_DV]
        cg = cg_ref[0, :, h * C_DV:(h + 1) * C_DV].astype(F32)
        o_ref[0, :, h * C_DV:(h + 1) * C_DV] = (y * (cg * jax.nn.sigmoid(cg))).astype(BF16)


def _retention(z, cos_t, sin_t, g_ret):
    B, S, _ = z.shape
    L = min(L_RET, S)
    qk_w, v_w = HEADS * HEAD_W, HEADS * C_DV
    dist = np.arange(L)[:, None] - np.arange(L)[None, :]
    gam = 1.0 - 2.0 ** (-5.0 - np.arange(HEADS, dtype=np.float64))
    dm = jnp.asarray(np.where(dist >= 0, gam[:, None, None] ** float(-L), 0.0) + 0.0 * dist, F32)
    return pl.pallas_call(
        functools.partial(_retention_kernel, L=L),
        grid=(B, S // L),
        in_specs=[
            pl.BlockSpec((1, L, qk_w), lambda b, n: (b, n, Z_CQ // qk_w)),
            pl.BlockSpec((1, L, qk_w), lambda b, n: (b, n, Z_CK // qk_w)),
            pl.BlockSpec((1, L, v_w), lambda b, n: (b, n, Z_CV // v_w)),
            pl.BlockSpec((1, L, v_w), lambda b, n: (b, n, Z_CG // v_w)),
            pl.BlockSpec((L, HEAD_W), lambda b, n: (n, 0)),
            pl.BlockSpec((L, HEAD_W), lambda b, n: (n, 0)),
            _const_spec(dm.shape),
            pl.BlockSpec((1, v_w), lambda b, n: (0, 0)),
        ],
        out_specs=pl.BlockSpec((1, L, v_w), lambda b, n: (b, n, 0)),
        out_shape=jax.ShapeDtypeStruct((B, S, v_w), BF16),
        scratch_shapes=[pltpu.VMEM((HEADS, HEAD_W, C_DV), F32)],
        compiler_params=_cparams(("parallel", "arbitrary")),
        name="retention",
    )(z, z, z, z, cos_t, sin_t, dm, g_ret)


def _rope_tables(S):
    half = HEAD_W // 2
    inv = 1.0 / (ROPE_BASE ** np.linspace(0.0, 1.0, half))
    ang = np.arange(S, dtype=np.float64)[:, None] * inv[None, :]
    cos, sin = np.cos(ang), np.sin(ang)
    return (jnp.asarray(np.concatenate([cos, cos], axis=1), F32),
            jnp.asarray(np.concatenate([-sin, sin], axis=1), F32))


def _layernorm(r, g, b):
    mu = jnp.mean(r, axis=1, keepdims=True)
    var = jnp.mean(jnp.square(r - mu), axis=1, keepdims=True)
    return (r - mu) * lax.rsqrt(var + LN_EPS) * g + b


def _merge_kernel(ya_ref, yb_ref, yc_ref, ga_ref, gb_ref, gc_ref, x_ref, gt_ref, sh2_ref, sc2_ref,
                  wpa_ref, wpb_ref, wpc_ref, wout_ref, lng_ref, lnb_ref, xo_ref, ho_ref, *, alpha):
    hs = x_ref.shape[1] // 2
    halves = [slice(0, hs), slice(hs, 2 * hs)]

    def branch(rows, y_ref, g_ref, w_ref):
        gate = jax.nn.sigmoid(g_ref[0, rows].astype(F32))
        return gate * jnp.dot(y_ref[0, rows], w_ref[...], preferred_element_type=F32)

    ms = [branch(r, ya_ref, ga_ref, wpa_ref) + branch(r, yb_ref, gb_ref, wpb_ref)
          + branch(r, yc_ref, gc_ref, wpc_ref) for r in halves]
    ys = [jnp.dot(m.astype(BF16), wout_ref[...], preferred_element_type=F32) for m in ms]
    for r, y in zip(halves, ys):
        xn = _layernorm(alpha * x_ref[0, r] + gt_ref[0] * y, lng_ref[...], lnb_ref[...])
        xo_ref[0, r] = xn
        ho_ref[0, r] = (xn * (1.0 + sc2_ref[0]) + sh2_ref[0]).astype(BF16)


def _const_spec(shape):
    return pl.BlockSpec(shape, lambda b, i: (0,) * len(shape), pipeline_mode=pl.Buffered(1))


def _layer_spec(stacked, l):
    shape = stacked.shape[1:]
    return pl.BlockSpec((None,) + shape, lambda b, i: (l,) + (0,) * len(shape),
                        pipeline_mode=pl.Buffered(1))


def _merge(ya, yb, yc, z, x, mod_l, wpa, wpb, wpc, wout, lng, lnb, alpha, l):
    B, S, D = x.shape
    tm = min(TM_MERGE, S)
    gblk = Z_GATES // D
    tok = lambda w, col: pl.BlockSpec((1, tm, w), lambda b, i: (b, i, col))
    modv = lambda col: pl.BlockSpec((1, 1, D), lambda b, i: (b, 0, col))
    return pl.pallas_call(
        functools.partial(_merge_kernel, alpha=alpha),
        grid=(B, S // tm),
        in_specs=[
            tok(ya.shape[2], 0), tok(yb.shape[2], 0), tok(yc.shape[2], 0),
            tok(D, gblk), tok(D, gblk + 1), tok(D, gblk + 2),
            tok(D, 0),
            modv(2), modv(3), modv(4),
            _layer_spec(wpa, l), _layer_spec(wpb, l), _layer_spec(wpc, l),
            _layer_spec(wout, l), _const_spec(lng.shape), _const_spec(lnb.shape),
        ],
        out_specs=[tok(D, 0), tok(D, 0)],
        out_shape=[jax.ShapeDtypeStruct((B, S, D), F32), jax.ShapeDtypeStruct((B, S, D), BF16)],
        compiler_params=_cparams(("parallel", "parallel")),
        name="merge_out_ln",
    )(ya, yb, yc, z, z, z, x, mod_l, mod_l, mod_l, wpa, wpb, wpc, wout, lng, lnb)


def _ffn_kernel(h_ref, x_ref, gt_ref, wup_ref, wconv_ref, bconv_ref, wdown_ref, lng_ref, lnb_ref,
                xo_ref, a_scr, tail_scr, *, alpha, cw):
    @pl.when(pl.program_id(1) == 0)
    def _():
        tail_scr[...] = jnp.zeros(tail_scr.shape, F32)

    h = h_ref[0]
    tm = h.shape[0]
    dff = a_scr.shape[1]
    row = lax.broadcasted_iota(jnp.int32, (tm, cw), 0)
    for ci in range(dff // cw):
        lo = ci * cw
        u = jnp.dot(h, wup_ref[:, lo:lo + cw], preferred_element_type=F32)
        g = jnp.dot(h, wup_ref[:, dff + lo:dff + lo + cw], preferred_element_type=F32)
        tail = tail_scr[:, lo:lo + cw]
        p1, p2 = tail[7:8], tail[6:7]
        um1 = jnp.where(row == 0, p1, pltpu.roll(u, 1, 0))
        um2 = jnp.where(row == 0, p2, jnp.where(row == 1, p1, pltpu.roll(u, 2, 0)))
        tail_scr[:, lo:lo + cw] = u[tm - 8:]
        conv = bconv_ref[:, lo:lo + cw] + um2 * wconv_ref[0:1, lo:lo + cw]
        conv = conv + um1 * wconv_ref[1:2, lo:lo + cw]
        conv = conv + u * wconv_ref[2:3, lo:lo + cw]
        act = 0.5 * conv * (1.0 + lax.erf(conv * (2.0 ** -0.5)))
        a_scr[:, lo:lo + cw] = (act * g).astype(BF16)
    halves = [slice(0, tm // 2), slice(tm // 2, tm)]
    ys = [jnp.dot(a_scr[r, :], wdown_ref[...], preferred_element_type=F32) for r in halves]
    for r, y in zip(halves, ys):
        xo_ref[0, r] = _layernorm(alpha * x_ref[0, r] + gt_ref[0] * y, lng_ref[...], lnb_ref[...])


def _ffn(h, x, mod_l, wup, wconv, bconv, wdown, lng, lnb, alpha, l):
    B, S, D = x.shape
    dff = wdown.shape[1]
    tm = min(TM_FFN, S)
    tok = pl.BlockSpec((1, tm, D), lambda b, i: (b, i, 0))
    return pl.pallas_call(
        functools.partial(_ffn_kernel, alpha=alpha, cw=CW_FFN),
        grid=(B, S // tm),
        in_specs=[
            tok, tok,
            pl.BlockSpec((1, 1, D), lambda b, i: (b, 0, 5)),
            _layer_spec(wup, l), _const_spec(wconv.shape), _const_spec(bconv.shape),
            _layer_spec(wdown, l), _const_spec(lng.shape), _const_spec(lnb.shape),
        ],
        out_specs=tok,
        out_shape=jax.ShapeDtypeStruct((B, S, D), F32),
        scratch_shapes=[pltpu.VMEM((tm, dff), BF16), pltpu.VMEM((8, dff), F32)],
        compiler_params=_cparams(("parallel", "arbitrary")),
        name="conv_ffn_ln",
    )(h, x, mod_l, wup, wconv, bconv, wdown, lng, lnb)


def _prep_in_proj(w, b):
    scale = np.ones((1, Z_WIDTH), np.float32)
    scale[:, Z_AQ:Z_AQ + 512] = A_DQK ** -0.5 * LOG2E
    scale[:, Z_BQ:Z_BQ + 512] = HEAD_W ** -0.5 * LOG2E
    scale[:, Z_CK:Z_CK + 512] = HEAD_W ** -0.5
    L, D, _ = w.shape
    tr = TR_PREP
    wv = jnp.transpose(w, (2, 0, 1))
    wm = pl.pallas_call(
        functools.partial(_win_prep_kernel, first_shifted=BF_OFF // tr, shift=HEADS),
        grid=(Z_WIDTH // tr,),
        in_specs=[
            pl.BlockSpec((tr, L, D), lambda j: (j, 0, 0)),
            pl.BlockSpec((8, L, D), lambda j: ((j + 1) * (tr // 8), 0, 0)),
            pl.BlockSpec((tr, 1), lambda j: (j, 0)),
        ],
        out_specs=pl.BlockSpec((L, tr, D), lambda j: (0, j, 0)),
        out_shape=jax.ShapeDtypeStruct((L, Z_WIDTH, D), BF16),
        compiler_params=_cparams(("parallel",)),
        name="w_in_prep",
    )(wv, wv, jnp.asarray(scale.reshape(-1, 1)))
    bm = jnp.concatenate([b[..., :BF_OFF], b[..., BF_OFF + HEADS:]], axis=-1) * scale
    pad = BF_PAD - HEADS
    wf = jnp.pad(w[..., BF_OFF:BF_OFF + HEADS], ((0, 0), (0, 0), (0, pad)))
    bf = jnp.pad(b[..., BF_OFF:BF_OFF + HEADS], ((0, 0), (0, pad)))
    return wm, bm[:, None, :], wf.astype(BF16), bf[:, None, :]


def _win_prep_kernel(w_ref, wnext_ref, s_ref, o_ref, *, first_shifted, shift):
    j = pl.program_id(0)

    def emit(rows):
        o_ref[...] = (jnp.swapaxes(rows, 0, 1) * s_ref[...]).astype(BF16)

    @pl.when(j < first_shifted)
    def _():
        emit(w_ref[...])

    @pl.when(j >= first_shifted)
    def _():
        tr = w_ref.shape[0]
        emit(jnp.concatenate([w_ref[...], wnext_ref[...]], axis=0)[shift:shift + tr])


def kernel(x, c, w_ada, b_ada, w_in, b_in, lam_q1, lam_k1, lam_q2, lam_k2, g_diff, g_ret, w_pa, w_pb, w_pc, w_out, ln_g, ln_b, w_up, w_conv, b_conv, w_down):
    B, S, D = x.shape
    depth = w_ada.shape[0]
    alpha = (2 * depth) ** 0.25
    chunk = 64
    mod = _ada(c, w_ada, b_ada)
    cos_t, sin_t = _rope_tables(S)
    wm, bm, wf, bf = _prep_in_proj(w_in, b_in)
    wpa, wpb, wpc, wout = (w.astype(BF16) for w in (w_pa, w_pb, w_pc, w_out))
    wup, wdown = w_up.astype(BF16), w_down.astype(BF16)
    for l in range(depth):
        mod_l = mod[l].reshape(B, 1, 6 * D)
        lam_init = jnp.full((1, 1), 0.8 - 0.6 * math.exp(-0.3 * l), F32)
        z, lf = _inproj(x, mod_l, wm, bm, wf, bf, l)
        f = _cumsum(lf)
        ya = _diff_attn(z, lam_q1[l][None], lam_k1[l][None], lam_q2[l][None], lam_k2[l][None],
                        lam_init, g_diff[l][None], chunk)
        yb = _fox_attn(z, f)
        yc = _retention(z, cos_t, sin_t, g_ret[l][None])
        x, h2 = _merge(ya, yb, yc, z, x, mod_l, wpa, wpb, wpc, wout,
                       ln_g[l, 0][None], ln_b[l, 0][None], alpha, l)
        x = _ffn(h2, x, mod_l, wup, w_conv[l], b_conv[l][None], wdown,
                 ln_g[l, 1][None], ln_b[l, 1][None], alpha, l)
    return x
```
